```python
import jax, jax.numpy as jnp
from jax import lax
import numpy as np

D_MODEL = 2048
BATCH = 16
SEQ = 256
DEPTH = 2
DEC_BATCH = 8
DEC_SEQ = 4096
PAST_LEN = 512

GRID_W = 64
N_EVEN = (DEPTH + 1) // 2
N_ODD = DEPTH // 2
N_MOD = 9
D_FF = 5632
EPS = 1e-6
Q_BLOCK = 128
H_A = 8
HD_A = 128
D_A = H_A * HD_A
WIN_R = 8
WIN_C = 16
QB_C = 16
KB_C = QB_C + WIN_C
D_RNN = 1024
H_B = 8
BD_B = D_RNN // H_B
CONV_W = 4
LRU_C = 8.0
H_C = 8
DK_C = 256
DV_C = 256
D_C = H_C * DV_C
CHUNK = 128
ROPE_BASE = 10000.0
IN_AB = 3 * D_A + 2 * D_RNN
IN_C = 2 * H_C * DK_C + 2 * D_C + 4 * H_C

kernel_name = 'hybrid_na_rglru_mlstm_diffusion_step'


def rms_norm(x, g):
    xf = x.astype(jnp.float32)
    y = xf * lax.rsqrt(jnp.mean(xf * xf, axis=-1, keepdims=True) + EPS)
    return (y * g.astype(jnp.float32)).astype(x.dtype)


def modulation(cond, w, b):
    m = jax.nn.silu(cond) @ w + b
    return m.reshape(cond.shape[0], N_MOD, -1)


def adaln(x, g, mod, j):
    return rms_norm(x, g) * (1.0 + mod[:, 3 * j + 1][:, None]) + mod[:, 3 * j][:, None]


def macaron_ffn(x, mod, j, g, w_gate, w_up, w_down):
    h = adaln(x, g, mod, j)
    y = (jax.nn.silu(h @ w_gate) * (h @ w_up)) @ w_down
    return x + 0.5 * mod[:, 3 * j + 2][:, None] * y


def ab_project(h, w_in, qg, kg):
    b, t, _ = h.shape
    z = h @ w_in
    qa, ka, va, xb, gb = jnp.split(z, [D_A, 2 * D_A, 3 * D_A, 3 * D_A + D_RNN], axis=-1)
    qa = rms_norm(qa.reshape(b, t, H_A, HD_A), qg)
    ka = rms_norm(ka.reshape(b, t, H_A, HD_A), kg)
    va = va.reshape(b, t, H_A, HD_A)
    return qa, ka, va, xb, gb


def ctx_attention(q, k, v):
    b, l, h, hd = q.shape
    qb = jnp.moveaxis(q.reshape(b, l // Q_BLOCK, Q_BLOCK, h, hd), 1, 0)

    def blk(qq):
        s = jnp.einsum('bqhd,bkhd->bhqk', qq, k).astype(jnp.float32) * (hd ** -0.5)
        p = jax.nn.softmax(s, axis=-1).astype(v.dtype)
        return jnp.einsum('bhqk,bkhd->bqhd', p, v)

    o = lax.map(blk, qb)
    return jnp.moveaxis(o, 0, 1).reshape(b, l, h * hd)


def neighbourhood_attention(q, k, v, ck, cv, rpb):
    b, t, h, hd = q.shape
    rows = t // GRID_W
    kr = min(WIN_R, rows)
    ncb = GRID_W // QB_C
    scale = hd ** -0.5
    starts = np.clip(np.arange(ncb) * QB_C - WIN_C // 2, 0, GRID_W - KB_C)
    qcol = np.arange(GRID_W).reshape(ncb, QB_C)
    c0 = np.clip(qcol - WIN_C // 2, 0, GRID_W - WIN_C)
    kcol = starts[:, None] + np.arange(KB_C)[None, :]
    col_ok = (kcol[:, None, :] >= c0[:, :, None]) & (kcol[:, None, :] < c0[:, :, None] + WIN_C)
    dc_idx = np.clip(kcol[:, None, :] - qcol[:, :, None] + WIN_C - 1, 0, 2 * WIN_C - 2)
    mask = col_ok[None, None, :, :, None, :]
    kg = k.reshape(b, rows, GRID_W, h, hd)
    vg = v.reshape(b, rows, GRID_W, h, hd)
    qg = jnp.moveaxis(q.reshape(b, rows, ncb, QB_C, h, hd), 1, 0)

    def one_row(args):
        r, qr = args
        r0 = jnp.clip(r - WIN_R // 2, 0, rows - kr)
        kw = lax.dynamic_slice_in_dim(kg, r0, kr, axis=1)[:, :, kcol]
        vw = lax.dynamic_slice_in_dim(vg, r0, kr, axis=1)[:, :, kcol]
        s_win = jnp.einsum('bjqhd,brjkhd->bhjqrk', qr, kw).astype(jnp.float32) * scale
        dr_idx = r0 + jnp.arange(kr) - r + (WIN_R - 1)
        bias = jnp.transpose(rpb[:, dr_idx][:, :, dc_idx], (0, 2, 3, 1, 4))
        s_win = jnp.where(mask, s_win + bias[None].astype(jnp.float32), -jnp.inf)
        s_win = s_win.reshape(b, h, ncb, QB_C, kr * KB_C)
        s_ctx = jnp.einsum('bjqhd,blhd->bhjql', qr, ck).astype(jnp.float32) * scale
        p = jax.nn.softmax(jnp.concatenate([s_win, s_ctx], axis=-1), axis=-1).astype(v.dtype)
        p_win = p[..., :kr * KB_C].reshape(b, h, ncb, QB_C, kr, KB_C)
        p_ctx = p[..., kr * KB_C:]
        o = jnp.einsum('bhjqrk,brjkhd->bjqhd', p_win, vw) + jnp.einsum('bhjql,blhd->bjqhd', p_ctx, cv)
        return o.reshape(b, GRID_W, h, hd)

    out = lax.map(one_row, (jnp.arange(rows), qg))
    return jnp.moveaxis(out, 0, 1).reshape(b, t, h * hd)


def centred_dwconv(x, w, bias):
    t = x.shape[1]
    left = (CONV_W - 1) // 2
    xp = jnp.pad(x, ((0, 0), (left, CONV_W - 1 - left), (0, 0)))
    return sum(xp[:, i:i + t] * w[i] for i in range(CONV_W)) + bias


def blockdiag(x, w, bias):
    xb = x.reshape(x.shape[0], x.shape[1], H_B, BD_B)
    return jnp.einsum('bthi,hij->bthj', xb, w).reshape(x.shape) + bias


def _lin_combine(left, right):
    a1, b1 = left
    a2, b2 = right
    return a1 * a2, a2 * b1 + b2


def rglru_scan(x, wa, ba, wx, bx, lam, h0, reverse):
    r = jax.nn.sigmoid(blockdiag(x, wa, ba).astype(jnp.float32))
    i = jax.nn.sigmoid(blockdiag(x, wx, bx).astype(jnp.float32))
    log_a = -LRU_C * r * jax.nn.softplus(-lam.astype(jnp.float32))
    a = jnp.exp(log_a)
    u = jnp.sqrt(-jnp.expm1(2.0 * log_a)) * (i * x)
    t0 = -1 if reverse else 0
    u = u.at[:, t0].add(a[:, t0] * h0.astype(jnp.float32))
    _, hs = lax.associative_scan(_lin_combine, (a, u), axis=1, reverse=reverse)
    return hs, hs[:, 0 if reverse else -1]


def rglru_branch(xb, gb, conv_w, conv_b, wa, ba, wx, bx, lam, h0):
    xc = centred_dwconv(xb, conv_w, conv_b).astype(jnp.float32)
    hf, fin_f = rglru_scan(xc, wa[0], ba[0], wx[0], bx[0], lam[0], h0[:, 0], False)
    hb, fin_b = rglru_scan(xc, wa[1], ba[1], wx[1], bx[1], lam[1], h0[:, 1], True)
    out = (hf + hb) * jax.nn.gelu(gb.astype(jnp.float32))
    return out.astype(xb.dtype), jnp.stack([fin_f, fin_b], axis=1)


def rope_2d(x):
    t, dh = x.shape[1], x.shape[-1]
    half = dh // 2
    nf = half // 2
    pos = jnp.arange(t)
    freqs = ROPE_BASE ** (-jnp.arange(nf, dtype=jnp.float32) / nf)

    def rot(xa, p):
        ang = p.astype(jnp.float32)[:, None] * freqs[None, :]
        cos = jnp.cos(ang)[None, :, None, :]
        sin = jnp.sin(ang)[None, :, None, :]
        x1, x2 = xa[..., :nf], xa[..., nf:]
        return jnp.concatenate([x1 * cos - x2 * sin, x1 * sin + x2 * cos], axis=-1)

    xf = x.astype(jnp.float32)
    return jnp.concatenate([rot(xf[..., :half], pos // GRID_W), rot(xf[..., half:], pos % GRID_W)], axis=-1)


def mlstm_scan(q, k, v, ig, lf, c0, n0, m0):
    b, t, h, _ = q.shape
    nc = t // CHUNK

    def to_chunks(a):
        return jnp.moveaxis(a.reshape(b, nc, CHUNK, *a.shape[2:]), 1, 0)

    causal = np.tril(np.ones((CHUNK, CHUNK), dtype=bool))[None, :, :, None]

    def step(carry, inp):
        cm, nv, m = carry
        qq, kk, vv, ii, ff = inp
        bc = jnp.cumsum(ff, axis=1)
        dmat = bc[:, :, None, :] - bc[:, None, :, :] + ii[:, None, :, :]
        dmat = jnp.where(causal, dmat, -jnp.inf)
        inter = bc + m[:, None, :]
        m_t = jnp.maximum(inter, jnp.max(dmat, axis=2))
        w = jnp.exp(dmat - m_t[:, :, None, :])
        g = jnp.exp(inter - m_t)
        s = jnp.einsum('bthd,bshd->btsh', qq, kk) * w
        num = jnp.einsum('btsh,bshe->bthe', s, vv) + g[..., None] * jnp.einsum('bthd,bhde->bthe', qq, cm)
        den = jnp.sum(s, axis=2) + g * jnp.einsum('bthd,bhd->bth', qq, nv)
        hout = num / jnp.maximum(jnp.abs(den), jnp.exp(-m_t))[..., None]
        bk = bc[:, -1]
        dend = bk[:, None, :] - bc + ii
        m_new = jnp.maximum(bk + m, jnp.max(dend, axis=1))
        we = jnp.exp(dend - m_new[:, None, :])
        ge = jnp.exp(bk + m - m_new)
        c_new = ge[..., None, None] * cm + jnp.einsum('bsh,bshd,bshe->bhde', we, kk, vv)
        n_new = ge[..., None] * nv + jnp.einsum('bsh,bshd->bhd', we, kk)
        return (c_new, n_new, m_new), hout

    (cf, nf, mf), hs = lax.scan(step, (c0, n0, m0), (to_chunks(q), to_chunks(k), to_chunks(v), to_chunks(ig), to_chunks(lf)))
    return jnp.moveaxis(hs, 0, 1).reshape(b, t, h, v.shape[-1]), cf, nf, mf


def mlstm_mixer(hin, w_in, b_gate, mh_g, w_out, c0, n0, m0, rotary):
    b, t, _ = hin.shape
    f32 = jnp.float32
    qk = H_C * DK_C
    z = hin @ w_in
    q, k, v, o, gates = jnp.split(z, [qk, 2 * qk, 2 * qk + D_C, 2 * qk + 2 * D_C], axis=-1)
    q = q.reshape(b, t, H_C, DK_C).astype(f32)
    k = k.reshape(b, t, H_C, DK_C).astype(f32) * (DK_C ** -0.5)
    v = v.reshape(b, t, H_C, DV_C).astype(f32)
    if rotary:
        q = rope_2d(q)
        k = rope_2d(k)
    gates = gates.reshape(b, t, 4, H_C).astype(f32) + b_gate.astype(f32)
    ig_f, lf_f = gates[:, :, 0], jax.nn.log_sigmoid(gates[:, :, 1])
    ig_b, lf_b = gates[:, :, 2], jax.nn.log_sigmoid(gates[:, :, 3])
    c0 = c0.astype(f32)
    n0 = n0.astype(f32)
    m0 = m0.astype(f32)
    hf, cf, nf, mf = mlstm_scan(q, k, v, ig_f, lf_f, c0[:, 0], n0[:, 0], m0[:, 0])
    fl = lambda a: jnp.flip(a, axis=1)
    hb, cb, nb, mb = mlstm_scan(fl(q), fl(k), fl(v), fl(ig_b), fl(lf_b), c0[:, 1], n0[:, 1], m0[:, 1])
    hsum = hf + fl(hb)
    hn = rms_norm(hsum, mh_g) * jax.nn.sigmoid(o.reshape(b, t, H_C, DV_C).astype(f32))
    y = hn.reshape(b, t, D_C).astype(hin.dtype) @ w_out
    return y, jnp.stack([cf, cb], axis=1), jnp.stack([nf, nb], axis=1), jnp.stack([mf, mb], axis=1)


def setup_inputs(seed: int = 0) -> dict:
    key = jax.random.key(seed)
    ks = iter(jax.random.split(key, 40))
    f32 = jnp.float32

    def nrm(shape, s):
        return jax.random.normal(next(ks), shape, f32) * s

    x_prompt = nrm((BATCH, SEQ, D_MODEL), 1.0)
    x_sample = nrm((DEC_BATCH, DEC_SEQ, D_MODEL), 1.0)
    cache_k = nrm((DEC_BATCH, N_EVEN, PAST_LEN, H_A, HD_A), 1.0)
    cache_v = nrm((DEC_BATCH, N_EVEN, PAST_LEN, H_A, HD_A), 1.0)
    state_lru = nrm((DEC_BATCH, N_EVEN, 2, D_RNN), 0.5)
    state_mlstm_c = nrm((DEC_BATCH, N_ODD, 2, H_C, DK_C, DV_C), 0.5)
    state_mlstm_n = nrm((DEC_BATCH, N_ODD, 2, H_C, DK_C), 0.5)
    state_mlstm_m = nrm((DEC_BATCH, N_ODD, 2, H_C), 1.0)
    c = nrm((DEC_BATCH, D_MODEL), 1.0)
    c_ctx = nrm((D_MODEL,), 1.0)
    w_mod = nrm((DEPTH, D_MODEL, N_MOD * D_MODEL), 0.5 * D_MODEL ** -0.5)
    b_mod = nrm((DEPTH, N_MOD * D_MODEL), 0.02)
    norm_g = 1.0 + nrm((DEPTH, 3, D_MODEL), 0.02)
    w_ffn_gate = nrm((DEPTH, 2, D_MODEL, D_FF), D_MODEL ** -0.5)
    w_ffn_up = nrm((DEPTH, 2, D_MODEL, D_FF), D_MODEL ** -0.5)
    w_ffn_down = nrm((DEPTH, 2, D_FF, D_MODEL), D_FF ** -0.5)
    w_in_ab = nrm((N_EVEN, D_MODEL, IN_AB), D_MODEL ** -0.5)
    w_out_ab = nrm((N_EVEN, D_A + D_RNN, D_MODEL), (D_A + D_RNN) ** -0.5)
    qn_g = 1.0 + nrm((N_EVEN, HD_A), 0.02)
    kn_g = 1.0 + nrm((N_EVEN, HD_A), 0.02)
    rpb = nrm((N_EVEN, H_A, 2 * WIN_R - 1, 2 * WIN_C - 1), 0.1)
    conv_w = nrm((N_EVEN, CONV_W, D_RNN), CONV_W ** -0.5)
    conv_b = nrm((N_EVEN, D_RNN), 0.02)
    lru_wa = nrm((N_EVEN, 2, H_B, BD_B, BD_B), BD_B ** -0.5)
    lru_ba = nrm((N_EVEN, 2, D_RNN), 0.02)
    lru_wx = nrm((N_EVEN, 2, H_B, BD_B, BD_B), BD_B ** -0.5)
    lru_bx = nrm((N_EVEN, 2, D_RNN), 0.02)
    u = jax.random.uniform(next(ks), (N_EVEN, 2, D_RNN), f32, 0.9, 0.999)
    a = u ** (1.0 / LRU_C)
    lru_lam = jnp.log(a) - jnp.log1p(-a)
    w_in_c = nrm((N_ODD, D_MODEL, IN_C), D_MODEL ** -0.5)
    b_gate_c = nrm((N_ODD, 4, H_C), 0.1) + jnp.array([-1.0, 3.0, -1.0, 3.0], f32)[None, :, None]
    mh_norm_g = 1.0 + nrm((N_ODD, H_C, DV_C), 0.02)
    w_out_c = nrm((N_ODD, D_C, D_MODEL), D_C ** -0.5)
    return {'x_prompt': x_prompt, 'x_sample': x_sample, 'cache_k': cache_k, 'cache_v': cache_v,
            'state_lru': state_lru, 'state_mlstm_c': state_mlstm_c, 'state_mlstm_n': state_mlstm_n,
            'state_mlstm_m': state_mlstm_m, 'c': c, 'c_ctx': c_ctx, 'w_mod': w_mod, 'b_mod': b_mod,
            'norm_g': norm_g, 'w_ffn_gate': w_ffn_gate, 'w_ffn_up': w_ffn_up, 'w_ffn_down': w_ffn_down,
            'w_in_ab': w_in_ab, 'w_out_ab': w_out_ab, 'qn_g': qn_g, 'kn_g': kn_g, 'rpb': rpb,
            'conv_w': conv_w, 'conv_b': conv_b, 'lru_wa': lru_wa, 'lru_ba': lru_ba, 'lru_wx': lru_wx,
            'lru_bx': lru_bx, 'lru_lam': lru_lam, 'w_in_c': w_in_c, 'b_gate_c': b_gate_c,
            'mh_norm_g': mh_norm_g, 'w_out_c': w_out_c}


def reference(x_prompt, x_sample, cache_k, cache_v, state_lru, state_mlstm_c, state_mlstm_n, state_mlstm_m, c, c_ctx, w_mod, b_mod, norm_g, w_ffn_gate, w_ffn_up, w_ffn_down, w_in_ab, w_out_ab, qn_g, kn_g, rpb, conv_w, conv_b, lru_wa, lru_ba, lru_wx, lru_bx, lru_lam, w_in_c, b_gate_c, mh_norm_g, w_out_c):
    f32 = jnp.float32
    xp, xs = x_prompt, x_sample
    bp = xp.shape[0]
    new_k, new_v, new_lru, new_c, new_n, new_m = [], [], [], [], [], []
    for l in range(DEPTH):
        j = l // 2
        mp = modulation(c_ctx[None, :], w_mod[l], b_mod[l])
        ms = modulation(c, w_mod[l], b_mod[l])
        xp = macaron_ffn(xp, mp, 0, norm_g[l, 0], w_ffn_gate[l, 0], w_ffn_up[l, 0], w_ffn_down[l, 0])
        xs = macaron_ffn(xs, ms, 0, norm_g[l, 0], w_ffn_gate[l, 0], w_ffn_up[l, 0], w_ffn_down[l, 0])
        hp = adaln(xp, norm_g[l, 1], mp, 1)
        hs = adaln(xs, norm_g[l, 1], ms, 1)
        if l % 2 == 0:
            lru_p = (conv_w[j], conv_b[j], lru_wa[j], lru_ba[j], lru_wx[j], lru_bx[j], lru_lam[j])
            qa, ka, va, xb, gb = ab_project(hp, w_in_ab[j], qn_g[j], kn_g[j])
            oa = ctx_attention(qa, ka, va)
            ob, h_fin = rglru_branch(xb, gb, *lru_p, jnp.zeros((bp, 2, D_RNN), f32))
            yp = jnp.concatenate([oa, ob], axis=-1) @ w_out_ab[j]
            new_k.append(ka)
            new_v.append(va)
            new_lru.append(h_fin.astype(xp.dtype))
            qa, ka, va, xb, gb = ab_project(hs, w_in_ab[j], qn_g[j], kn_g[j])
            oa = neighbourhood_attention(qa, ka, va, cache_k[:, j], cache_v[:, j], rpb[j])
            ob, _ = rglru_branch(xb, gb, *lru_p, state_lru[:, j])
            ys = jnp.concatenate([oa, ob], axis=-1) @ w_out_ab[j]
        else:
            zc = jnp.zeros((bp, 2, H_C, DK_C, DV_C), f32)
            zn = jnp.zeros((bp, 2, H_C, DK_C), f32)
            zm = jnp.zeros((bp, 2, H_C), f32)
            yp, cfin, nfin, mfin = mlstm_mixer(hp, w_in_c[j], b_gate_c[j], mh_norm_g[j], w_out_c[j], zc, zn, zm, False)
            new_c.append(cfin.astype(xp.dtype))
            new_n.append(nfin.astype(xp.dtype))
            new_m.append(mfin.astype(xp.dtype))
            ys, _, _, _ = mlstm_mixer(hs, w_in_c[j], b_gate_c[j], mh_norm_g[j], w_out_c[j], state_mlstm_c[:, j], state_mlstm_n[:, j], state_mlstm_m[:, j], True)
        xp = xp + mp[:, 5][:, None] * yp
        xs = xs + ms[:, 5][:, None] * ys
        xp = macaron_ffn(xp, mp, 2, norm_g[l, 2], w_ffn_gate[l, 1], w_ffn_up[l, 1], w_ffn_down[l, 1])
        xs = macaron_ffn(xs, ms, 2, norm_g[l, 2], w_ffn_gate[l, 1], w_ffn_up[l, 1], w_ffn_down[l, 1])
    return (xp, xs, jnp.stack(new_k, axis=1), jnp.stack(new_v, axis=1), jnp.stack(new_lru, axis=1), jnp.stack(new_c, axis=1), jnp.stack(new_n, axis=1), jnp.stack(new_m, axis=1))
```

```python
import functools

import numpy as np
import jax
import jax.numpy as jnp
from jax import lax
from jax.experimental import pallas as pl
from jax.experimental.pallas import tpu as pltpu

D_MODEL = 2048
DEPTH = 2
GRID_W = 64
N_MOD = 9
D_FF = 5632
EPS = 1e-6
H_A = 8
HD_A = 128
D_A = H_A * HD_A
WIN_R = 8
WIN_C = 16
D_RNN = 1024
H_B = 8
BD_B = D_RNN // H_B
CONV_W = 4
LRU_C = 8.0
H_C = 8
DK_C = 256
DV_C = 256
D_C = H_C * DV_C
CHUNK = 128
ROPE_BASE = 10000.0

BF16 = jnp.bfloat16
F32 = jnp.float32

V7X_VMEM_BYTES = 64 * 1024 * 1024
VMEM_LIMIT = V7X_VMEM_BYTES - 8 * 1024 * 1024

TOKEN_TILE = 512
FF_TILE = 512
NA_ROWS = 4
NA_KEY_ROWS = NA_ROWS + WIN_R
LRU_ROWS = 8


def _params(*sem):
    return pltpu.CompilerParams(dimension_semantics=sem, vmem_limit_bytes=VMEM_LIMIT)


def _dot(a, b):
    return jnp.dot(a, b, preferred_element_type=F32)


def _dot_nt(a, b):
    return lax.dot_general(a, b, (((1,), (1,)), ((), ())), preferred_element_type=F32)


def _dot_tn(a, b):
    return lax.dot_general(a, b, (((0,), (0,)), ((), ())), preferred_element_type=F32)


def _rms(x, g):
    return x * lax.rsqrt(jnp.mean(x * x, axis=-1, keepdims=True) + EPS) * g


def _adaln(x, g, mod_ref, j):
    return _rms(x, g) * (1.0 + mod_ref[0, 3 * j + 1:3 * j + 2, :]) + mod_ref[0, 3 * j:3 * j + 1, :]


def _softplus(x):
    return jnp.maximum(x, 0.0) + jnp.log1p(jnp.exp(-jnp.abs(x)))


def _log_sigmoid(x):
    return -_softplus(-x)


def _mod_kernel(c_ref, w_ref, b_ref, o_ref):
    c = c_ref[...]
    s = (c * jax.nn.sigmoid(c)).astype(BF16)
    o_ref[0] = _dot(s, w_ref[0].astype(BF16)) + b_ref[0]


def _modulation(cond, w_mod, b_mod):
    r = cond.shape[0]
    n = N_MOD * D_MODEL
    tn = 1024
    return pl.pallas_call(
        _mod_kernel,
        grid=(DEPTH, n // tn),
        in_specs=[
            pl.BlockSpec((r, D_MODEL), lambda l, j: (0, 0)),
            pl.BlockSpec((1, D_MODEL, tn), lambda l, j: (l, 0, j)),
            pl.BlockSpec((1, 1, tn), lambda l, j: (l, 0, j)),
        ],
        out_specs=pl.BlockSpec((1, r, tn), lambda l, j: (l, 0, j)),
        out_shape=jax.ShapeDtypeStruct((DEPTH, r, n), F32),
        compiler_params=_params("arbitrary", "arbitrary"),
        name="modulation",
    )(cond, w_mod, b_mod.reshape(DEPTH, 1, n))


def _ffn_kernel(x_ref, mod_ref, g_ref, wg_ref, wu_ref, wd_ref, o_ref, h_scr, acc_scr, *, j):
    f = pl.program_id(2)

    @pl.when(f == 0)
    def _():
        h_scr[...] = _adaln(x_ref[0], g_ref[...], mod_ref, j).astype(BF16)
        acc_scr[...] = jnp.zeros_like(acc_scr)

    h = h_scr[...]
    a = _dot(h, wg_ref[...])
    u = _dot(h, wu_ref[...])
    act = (a * jax.nn.sigmoid(a)) * u
    acc_scr[...] += _dot(act.astype(BF16), wd_ref[...])

    @pl.when(f == pl.num_programs(2) - 1)
    def _():
        o_ref[0] = x_ref[0] + 0.5 * mod_ref[0, 3 * j + 2:3 * j + 3, :] * acc_scr[...]


def _ffn(x, mod, g, wg, wu, wd, l, s, j):
    grp, t, d = x.shape
    tm, tf = TOKEN_TILE, FF_TILE
    return pl.pallas_call(
        functools.partial(_ffn_kernel, j=j),
        grid=(grp, t // tm, D_FF // tf),
        in_specs=[
            pl.BlockSpec((1, tm, d), lambda b, i, f: (b, i, 0)),
            pl.BlockSpec((1, N_MOD, d), lambda b, i, f: (b, 0, 0)),
            pl.BlockSpec((1, d), lambda b, i, f: (0, 0)),
            pl.BlockSpec((None, None, d, tf), lambda b, i, f: (l, s, 0, f)),
            pl.BlockSpec((None, None, d, tf), lambda b, i, f: (l, s, 0, f)),
            pl.BlockSpec((None, None, tf, d), lambda b, i, f: (l, s, f, 0)),
        ],
        out_specs=pl.BlockSpec((1, tm, d), lambda b, i, f: (b, i, 0)),
        out_shape=jax.ShapeDtypeStruct(x.shape, F32),
        scratch_shapes=[pltpu.VMEM((tm, d), BF16), pltpu.VMEM((tm, d), F32)],
        compiler_params=_params("arbitrary", "arbitrary", "arbitrary"),
        name="macaron_ffn",
    )(x, mod, g.reshape(1, d), wg, wu, wd)


def _inproj_kernel(x_ref, mod_ref, g_ref, w_ref, o_ref, h_scr):
    @pl.when(pl.program_id(2) == 0)
    def _():
        h_scr[...] = _adaln(x_ref[0], g_ref[...], mod_ref, 1).astype(BF16)

    o_ref[0] = _dot(h_scr[...], w_ref[...])


def _inproj(x, mod, g, w, tn):
    grp, t, d = x.shape
    n = w.shape[1]
    tm = TOKEN_TILE
    return pl.pallas_call(
        _inproj_kernel,
        grid=(grp, t // tm, n // tn),
        in_specs=[
            pl.BlockSpec((1, tm, d), lambda b, i, k: (b, i, 0)),
            pl.BlockSpec((1, N_MOD, d), lambda b, i, k: (b, 0, 0)),
            pl.BlockSpec((1, d), lambda b, i, k: (0, 0)),
            pl.BlockSpec((d, tn), lambda b, i, k: (0, k)),
        ],
        out_specs=pl.BlockSpec((1, tm, tn), lambda b, i, k: (b, i, k)),
        out_shape=jax.ShapeDtypeStruct((grp, t, n), F32),
        scratch_shapes=[pltpu.VMEM((tm, d), BF16)],
        compiler_params=_params("arbitrary", "arbitrary", "arbitrary"),
        name="mixer_in_proj",
    )(x, mod, g.reshape(1, d), w)


def _ctx_attn_kernel(q_ref, k_ref, v_ref, qg_ref, kg_ref, o_ref, kn_ref, vo_ref):
    qn = _rms(q_ref[0], qg_ref[...])
    kn = _rms(k_ref[0], kg_ref[...])
    v = v_ref[0]
    kn_ref[0] = kn
    vo_ref[0] = v
    s = _dot_nt(qn.astype(BF16), kn.astype(BF16)) * (HD_A ** -0.5)
    p = jnp.exp(s - jnp.max(s, axis=-1, keepdims=True))
    o = _dot(p.astype(BF16), v.astype(BF16)) / jnp.sum(p, axis=-1, keepdims=True)
    o_ref[0] = o.astype(BF16)


def _ctx_attention(z, qg, kg):
    b, l, _ = z.shape
    blk = lambda off: pl.BlockSpec((1, l, HD_A), lambda i, h: (i, 0, off + h))
    gain = pl.BlockSpec((1, HD_A), lambda i, h: (0, 0))
    out = pl.BlockSpec((1, l, HD_A), lambda i, h: (i, 0, h))
    return pl.pallas_call(
        _ctx_attn_kernel,
        grid=(b, H_A),
        in_specs=[blk(0), blk(H_A), blk(2 * H_A), gain, gain],
        out_specs=[out, out, out],
        out_shape=[jax.ShapeDtypeStruct((b, l, D_A), BF16),
                   jax.ShapeDtypeStruct((b, l, D_A), F32),
                   jax.ShapeDtypeStruct((b, l, D_A), F32)],
        compiler_params=_params("arbitrary", "arbitrary"),
        name="ctx_attention",
    )(z, z, z, qg.reshape(1, HD_A), kg.reshape(1, HD_A))


def _na_bias_tables(rpb, rows):
    nblk = rows // NA_ROWS
    tables = []
    for i in (0, 1, nblk - 1):
        ks = int(np.clip(NA_ROWS * i - WIN_R // 2, 0, rows - NA_KEY_ROWS))
        r = (NA_ROWS * i + np.arange(NA_ROWS))[:, None, None, None]
        c = np.arange(GRID_W)[None, :, None, None]
        kr = (ks + np.arange(NA_KEY_ROWS))[None, None, :, None]
        kc = np.arange(GRID_W)[None, None, None, :]
        r0 = np.clip(r - WIN_R // 2, 0, rows - WIN_R)
        c0 = np.clip(c - WIN_C // 2, 0, GRID_W - WIN_C)
        ok = (kr >= r0) & (kr < r0 + WIN_R) & (kc >= c0) & (kc < c0 + WIN_C)
        dr = np.clip(kr - r + WIN_R - 1, 0, 2 * WIN_R - 2)
        dc = np.clip(kc - c + WIN_C - 1, 0, 2 * WIN_C - 2)
        shape = (NA_ROWS * GRID_W, NA_KEY_ROWS * GRID_W)
        full = (NA_ROWS, GRID_W, NA_KEY_ROWS, GRID_W)
        dr = np.broadcast_to(dr, full).reshape(shape)
        dc = np.broadcast_to(dc, full).reshape(shape)
        ok = np.broadcast_to(ok, full).reshape(shape)
        tables.append(jnp.where(ok[None], rpb[:, dr, dc], -jnp.inf))
    return jnp.stack(tables, axis=1)


def _na_kernel(q_ref, k_ref, v_ref, ck_ref, cv_ref, bias_ref, qg_ref, kg_ref, o_ref,
               kn_scr, vb_scr, ckb_scr, cvb_scr, *, rows):
    i = pl.program_id(2)
    nblk = rows // NA_ROWS

    @pl.when(i == 0)
    def _():
        kn_scr[...] = _rms(k_ref[0], kg_ref[...]).astype(BF16)
        vb_scr[...] = v_ref[0].astype(BF16)
        ckb_scr[...] = ck_ref[0].astype(BF16)
        cvb_scr[...] = cv_ref[0].astype(BF16)

    scale = HD_A ** -0.5
    qn = _rms(q_ref[0], qg_ref[...]).astype(BF16)
    ks = jnp.clip(NA_ROWS * i - WIN_R // 2, 0, rows - NA_KEY_ROWS)
    start = pl.multiple_of(ks * GRID_W, GRID_W)
    nkey = NA_KEY_ROWS * GRID_W
    kw = kn_scr[pl.ds(start, nkey), :]
    vw = vb_scr[pl.ds(start, nkey), :]
    kind = jnp.where(i == 0, 0, jnp.where(i == nblk - 1, 2, 1))
    s_w = _dot_nt(qn, kw) * scale + bias_ref[0, kind]
    s_c = _dot_nt(qn, ckb_scr[...]) * scale
    m = jnp.maximum(jnp.max(s_w, axis=-1, keepdims=True), jnp.max(s_c, axis=-1, keepdims=True))
    p_w = jnp.exp(s_w - m)
    p_c = jnp.exp(s_c - m)
    den = jnp.sum(p_w, axis=-1, keepdims=True) + jnp.sum(p_c, axis=-1, keepdims=True)
    o = _dot(p_w.astype(BF16), vw) + _dot(p_c.astype(BF16), cvb_scr[...])
    o_ref[0] = (o / den).astype(BF16)


def _neighbourhood_attention(z, ck, cv, rpb, qg, kg):
    b, t, _ = z.shape
    l = ck.shape[1]
    rows = t // GRID_W
    nblk = rows // NA_ROWS
    tq = NA_ROWS * GRID_W
    bias = _na_bias_tables(rpb, rows)
    full = lambda off: pl.BlockSpec((1, t, HD_A), lambda h, n, i: (n, 0, off + h))
    ctx = pl.BlockSpec((1, l, HD_A), lambda h, n, i: (n, 0, h))
    gain = pl.BlockSpec((1, HD_A), lambda h, n, i: (0, 0))
    return pl.pallas_call(
        functools.partial(_na_kernel, rows=rows),
        grid=(H_A, b, nblk),
        in_specs=[
            pl.BlockSpec((1, tq, HD_A), lambda h, n, i: (n, i, h)),
            full(H_A), full(2 * H_A), ctx, ctx,
            pl.BlockSpec((1, 3, tq, NA_KEY_ROWS * GRID_W), lambda h, n, i: (h, 0, 0, 0)),
            gain, gain,
        ],
        out_specs=pl.BlockSpec((1, tq, HD_A), lambda h, n, i: (n, i, h)),
        out_shape=jax.ShapeDtypeStruct((b, t, D_A), BF16),
        scratch_shapes=[pltpu.VMEM((t, HD_A), BF16), pltpu.VMEM((t, HD_A), BF16),
                        pltpu.VMEM((l, HD_A), BF16), pltpu.VMEM((l, HD_A), BF16)],
        compiler_params=_params("arbitrary", "arbitrary", "arbitrary"),
        name="neighbourhood_attention",
    )(z, z, z, ck, cv, bias, qg.reshape(1, HD_A), kg.reshape(1, HD_A))


def _lru_kernel(xb_ref, gb_ref, cw_ref, cb_ref, w4_ref, b4_ref, lam_ref, h0_ref, o_ref, fin_ref,
                xpad_scr, af_scr, uf_scr, ab_scr, ub_scr, *, t, chunk):
    pad = LRU_ROWS
    zeros = jnp.zeros((pad, BD_B), F32)
    xpad_scr[0:pad, :] = zeros
    xpad_scr[pad + t:2 * pad + t, :] = zeros
    xpad_scr[pad:pad + t, :] = xb_ref[0]

    sp = _softplus(-lam_ref[...])
    w4 = w4_ref[0]
    left = (CONV_W - 1) // 2

    for c in range(t // chunk):
        base = c * chunk
        xc = cb_ref[...]
        for k in range(CONV_W):
            xc = xc + xpad_scr[pl.ds(base + pad - left + k, chunk), :] * cw_ref[k:k + 1, :]
        ri = jax.nn.sigmoid(_dot(xc.astype(BF16), w4) + b4_ref[0])
        for d, (a_scr, u_scr) in enumerate(((af_scr, uf_scr), (ab_scr, ub_scr))):
            r = ri[:, 2 * d * BD_B:(2 * d + 1) * BD_B]
            g = ri[:, (2 * d + 1) * BD_B:(2 * d + 2) * BD_B]
            log_a = -LRU_C * r * sp[d:d + 1, :]
            a = jnp.exp(log_a)
            a_scr[pl.ds(base, chunk), :] = a
            u_scr[pl.ds(base, chunk), :] = jnp.sqrt(-jnp.tanh(log_a) * (a * a + 1.0)) * (g * xc)

    row = lax.broadcasted_iota(jnp.int32, (LRU_ROWS, BD_B), 0)
    ntile = t // LRU_ROWS

    def tile_scan(a, u, carry, reverse):
        for k in (1, 2, 4):
            if reverse:
                keep = row < LRU_ROWS - k
                shift = LRU_ROWS - k
            else:
                keep = row >= k
                shift = k
            a_s = jnp.where(keep, pltpu.roll(a, shift, 0), 1.0)
            u_s = jnp.where(keep, pltpu.roll(u, shift, 0), 0.0)
            u = a * u_s + u
            a = a * a_s
        return a * carry + u

    def scan(n, carry):
        cf, cb = carry
        bf = pl.multiple_of(n * LRU_ROWS, LRU_ROWS)
        bb = pl.multiple_of((ntile - 1 - n) * LRU_ROWS, LRU_ROWS)
        hf = tile_scan(af_scr[pl.ds(bf, LRU_ROWS), :], uf_scr[pl.ds(bf, LRU_ROWS), :], cf, False)
        hb = tile_scan(ab_scr[pl.ds(bb, LRU_ROWS), :], ub_scr[pl.ds(bb, LRU_ROWS), :], cb, True)
        uf_scr[pl.ds(bf, LRU_ROWS), :] = hf
        ub_scr[pl.ds(bb, LRU_ROWS), :] = hb
        return hf[LRU_ROWS - 1:LRU_ROWS, :], hb[0:1, :]

    cf, cb = lax.fori_loop(0, ntile, scan, (h0_ref[0, 0:1, :], h0_ref[0, 1:2, :]))
    fin_ref[0, 0:1, :] = cf
    fin_ref[0, 1:2, :] = cb

    for c in range(t // chunk):
        sl = pl.ds(c * chunk, chunk)
        hs = uf_scr[sl, :] + ub_scr[sl, :]
        o_ref[0, sl, :] = (hs * jax.nn.gelu(gb_ref[0, sl, :])).astype(BF16)


def _rglru(z, conv_w, conv_b, w4, b4, lam, h0):
    b, t, _ = z.shape
    chunk = min(t, 512)
    xoff = 3 * D_A // BD_B
    goff = xoff + H_B
    col = lambda off: pl.BlockSpec((1, t, BD_B), lambda n, h: (n, 0, off + h))
    return pl.pallas_call(
        functools.partial(_lru_kernel, t=t, chunk=chunk),
        grid=(b, H_B),
        in_specs=[
            col(xoff), col(goff),
            pl.BlockSpec((CONV_W, BD_B), lambda n, h: (0, h)),
            pl.BlockSpec((1, BD_B), lambda n, h: (0, h)),
            pl.BlockSpec((1, BD_B, 4 * BD_B), lambda n, h: (h, 0, 0)),
            pl.BlockSpec((1, 1, 4 * BD_B), lambda n, h: (h, 0, 0)),
            pl.BlockSpec((2, BD_B), lambda n, h: (0, h)),
            pl.BlockSpec((1, 2, BD_B), lambda n, h: (n, 0, h)),
        ],
        out_specs=[pl.BlockSpec((1, t, BD_B), lambda n, h: (n, 0, h)),
                   pl.BlockSpec((1, 2, BD_B), lambda n, h: (n, 0, h))],
        out_shape=[jax.ShapeDtypeStruct((b, t, D_RNN), BF16),
                   jax.ShapeDtypeStruct((b, 2, D_RNN), F32)],
        scratch_shapes=[pltpu.VMEM((t + 2 * LRU_ROWS, BD_B), F32)] + [pltpu.VMEM((t, BD_B), F32)] * 4,
        compiler_params=_params("arbitrary", "arbitrary"),
        name="rglru_branch",
    )(z, z, conv_w, conv_b.reshape(1, D_RNN), w4, b4, lam, h0)


def _outproj_ab_kernel(x_ref, mod_ref, oa_ref, ob_ref, w_ref, o_ref):
    y = _dot(oa_ref[0], w_ref[0:D_A, :]) + _dot(ob_ref[0], w_ref[D_A:D_A + D_RNN, :])
    o_ref[0] = x_ref[0] + mod_ref[0, 5:6, :] * y


def _outproj_ab(x, mod, oa, ob, w):
    grp, t, d = x.shape
    tm = TOKEN_TILE
    tok = lambda width: pl.BlockSpec((1, tm, width), lambda b, i: (b, i, 0))
    return pl.pallas_call(
        _outproj_ab_kernel,
        grid=(grp, t // tm),
        in_specs=[
            tok(d),
            pl.BlockSpec((1, N_MOD, d), lambda b, i: (b, 0, 0)),
            tok(D_A), tok(D_RNN),
            pl.BlockSpec((D_A + D_RNN, d), lambda b, i: (0, 0)),
        ],
        out_specs=tok(d),
        out_shape=jax.ShapeDtypeStruct(x.shape, F32),
        compiler_params=_params("arbitrary", "arbitrary"),
        name="mixer_ab_out_proj",
    )(x, mod, oa, ob, w)


def _rope(x, cos, sin):
    half = DK_C // 2
    out = []
    for p in range(2):
        sl = slice(p * half, (p + 1) * half)
        xs = x[:, sl]
        out.append(xs * cos[:, sl] + pltpu.roll(xs, half // 2, 1) * sin[:, sl])
    return jnp.concatenate(out, axis=-1)


def _mlstm_chunk(q, k, v, ig_row, ig_col, lf_row, lf_col, c_ref, n_ref, m_ref, reverse):
    ti = lax.broadcasted_iota(jnp.int32, (CHUNK, CHUNK), 0)
    si = lax.broadcasted_iota(jnp.int32, (CHUNK, CHUNK), 1)
    seen = (si >= ti) if reverse else (si <= ti)
    seen_t = (ti >= si) if reverse else (ti <= si)
    bc_col = jnp.sum(jnp.where(seen, lf_row, 0.0), axis=1, keepdims=True)
    bc_row = jnp.sum(jnp.where(seen_t, lf_col, 0.0), axis=0, keepdims=True)
    bk = jnp.sum(lf_row, axis=1, keepdims=True)
    m = m_ref[0:1, 0:1]
    dmat = jnp.where(seen, bc_col - bc_row + ig_row, -jnp.inf)
    inter = bc_col + m
    m_t = jnp.maximum(inter, jnp.max(dmat, axis=1, keepdims=True))
    w = jnp.exp(dmat - m_t)
    g = jnp.exp(inter - m_t)
    kscale = DK_C ** -0.5
    qb = q.astype(BF16)
    s = _dot_nt(qb, k.astype(BF16)) * (w * kscale)
    cm = c_ref[...]
    nv = n_ref[...]
    num = _dot(s.astype(BF16), v.astype(BF16)) + g * _dot(qb, cm.astype(BF16))
    den = jnp.sum(s, axis=1, keepdims=True) + g * jnp.sum(q * nv, axis=1, keepdims=True)
    hout = num / jnp.maximum(jnp.abs(den), jnp.exp(-m_t))
    dend = bk - bc_col + ig_col
    m_new = jnp.maximum(bk + m, jnp.max(dend, axis=0, keepdims=True))
    we = jnp.exp(dend - m_new)
    ge = jnp.exp(bk + m - m_new)
    kw = (we * kscale) * k
    c_ref[...] = ge * cm + _dot_tn(kw.astype(BF16), v.astype(BF16))
    n_ref[...] = ge * nv + jnp.sum(kw, axis=0, keepdims=True)
    m_ref[...] = jnp.broadcast_to(m_new, m_ref.shape)
    return hout


def _mlstm_kernel(*refs, rotary):
    bg_ref, m0_ref = refs[0], refs[1]
    (qf_ref, kf_ref, vf_ref, qb_ref, kb_ref, vb_ref, grf_ref, gcf_ref, grb_ref, gcb_ref,
     c0_ref, n0_ref) = refs[2:14]
    pos = 14
    if rotary:
        cosf_ref, sinf_ref, cosb_ref, sinb_ref = refs[pos:pos + 4]
        pos += 4
    hf_ref, hb_ref, cfin_ref, nfin_ref, mfin_ref = refs[pos:pos + 5]
    c_scr, n_scr, m_scr = refs[pos + 5:pos + 8]

    n, h, c = pl.program_id(0), pl.program_id(1), pl.program_id(2)

    @pl.when(c == 0)
    def _():
        for d in range(2):
            c_scr[d] = c0_ref[0, d, 0]
            n_scr[d] = n0_ref[0, d, 0]
            m_scr[d] = jnp.full(m_scr.shape[1:], m0_ref[n, d * H_C + h], F32)

    dirs = ((qf_ref, kf_ref, vf_ref, grf_ref, gcf_ref, hf_ref), (qb_ref, kb_ref, vb_ref, grb_ref, gcb_ref, hb_ref))
    for d, (q_ref, k_ref, v_ref, gr_ref, gc_ref, h_ref) in enumerate(dirs):
        q, k, v = q_ref[0], k_ref[0], v_ref[0]
        if rotary:
            cos_ref, sin_ref = (cosf_ref, sinf_ref) if d == 0 else (cosb_ref, sinb_ref)
            q = _rope(q, cos_ref[...], sin_ref[...])
            k = _rope(k, cos_ref[...], sin_ref[...])
        bi = bg_ref[2 * d, h]
        bf = bg_ref[2 * d + 1, h]
        ig_row = gr_ref[0, 0, 2 * d:2 * d + 1, :] + bi
        lf_row = _log_sigmoid(gr_ref[0, 0, 2 * d + 1:2 * d + 2, :] + bf)
        ig_col = gc_ref[0, 0, :, 2 * d:2 * d + 1] + bi
        lf_col = _log_sigmoid(gc_ref[0, 0, :, 2 * d + 1:2 * d + 2] + bf)
        h_ref[0] = _mlstm_chunk(q, k, v, ig_row, ig_col, lf_row, lf_col,
                                c_scr.at[d], n_scr.at[d], m_scr.at[d], reverse=(d == 1))

    @pl.when(c == pl.num_programs(2) - 1)
    def _():
        for d in range(2):
            cfin_ref[0, d, 0] = c_scr[d]
            nfin_ref[0, d, 0] = n_scr[d]
            mfin_ref[0, d, 0] = m_scr[d, 0:1, :]


def _rope_tables(t):
    half = DK_C // 2
    nf = half // 2
    pos = jnp.arange(t)
    freqs = ROPE_BASE ** (-jnp.arange(nf, dtype=F32) / nf)
    cos, sin = [], []
    for p in (pos // GRID_W, pos % GRID_W):
        ang = p.astype(F32)[:, None] * freqs[None, :]
        cos += [jnp.cos(ang), jnp.cos(ang)]
        sin += [-jnp.sin(ang), jnp.sin(ang)]
    return jnp.concatenate(cos, axis=-1), jnp.concatenate(sin, axis=-1)


def _mlstm(z, zg, b_gate, c0, n0, m0, rotary):
    b, t, _ = z.shape
    nc = t // CHUNK
    g4 = zg.reshape(b, t, 4, H_C)
    g_row = jnp.transpose(g4, (0, 3, 2, 1))
    g_col = jnp.transpose(g4, (0, 3, 1, 2))
    fwd = lambda n, h, c: c
    bwd = lambda n, h, c: nc - 1 - c

    def tok(off, order):
        return pl.BlockSpec((1, CHUNK, DK_C), lambda n, h, c: (n, order(n, h, c), off + h))

    def g_row_spec(order):
        return pl.BlockSpec((1, 1, 4, CHUNK), lambda n, h, c: (n, h, 0, order(n, h, c)))

    def g_col_spec(order):
        return pl.BlockSpec((1, 1, CHUNK, 4), lambda n, h, c: (n, h, order(n, h, c), 0))

    smem = pl.BlockSpec(memory_space=pltpu.SMEM)
    state = lambda *tail: pl.BlockSpec((1, 2, 1) + tail, lambda n, h, c: (n, 0, h) + (0,) * len(tail))
    in_specs = [smem, smem,
                tok(0, fwd), tok(H_C, fwd), tok(2 * H_C, fwd),
                tok(0, bwd), tok(H_C, bwd), tok(2 * H_C, bwd),
                g_row_spec(fwd), g_col_spec(fwd), g_row_spec(bwd), g_col_spec(bwd),
                state(DK_C, DV_C), state(1, DK_C)]
    args = [b_gate, m0.reshape(b, 2 * H_C), z, z, z, z, z, z, g_row, g_col, g_row, g_col,
            c0, n0.reshape(b, 2, H_C, 1, DK_C)]
    if rotary:
        cos, sin = _rope_tables(t)
        for order in (fwd, bwd):
            spec = pl.BlockSpec((CHUNK, DK_C), lambda n, h, c, order=order: (order(n, h, c), 0))
            in_specs += [spec, spec]
            args += [cos, sin]
    hspec = lambda order: pl.BlockSpec((1, CHUNK, DV_C), lambda n, h, c: (n, order(n, h, c), h))
    hf, hb, cfin, nfin, mfin = pl.pallas_call(
        functools.partial(_mlstm_kernel, rotary=rotary),
        grid=(b, H_C, nc),
        in_specs=in_specs,
        out_specs=[hspec(fwd), hspec(bwd), state(DK_C, DV_C), state(1, DK_C), state(1, 128)],
        out_shape=[jax.ShapeDtypeStruct((b, t, D_C), F32), jax.ShapeDtypeStruct((b, t, D_C), F32),
                   jax.ShapeDtypeStruct((b, 2, H_C, DK_C, DV_C), F32),
                   jax.ShapeDtypeStruct((b, 2, H_C, 1, DK_C), F32),
                   jax.ShapeDtypeStruct((b, 2, H_C, 1, 128), F32)],
        scratch_shapes=[pltpu.VMEM((2, DK_C, DV_C), F32), pltpu.VMEM((2, 1, DK_C), F32),
                        pltpu.VMEM((2, 8, 128), F32)],
        compiler_params=_params("arbitrary", "arbitrary", "arbitrary"),
        name="mlstm_scan",
    )(*args)
    return hf, hb, cfin, nfin.reshape(b, 2, H_C, DK_C), mfin[:, :, :, 0, 0]


def _outproj_c_kernel(x_ref, mod_ref, hf_ref, hb_ref, og_ref, g_ref, w_ref, o_ref, hn_scr):
    for h in range(H_C):
        sl = slice(h * DV_C, (h + 1) * DV_C)
        hs = hf_ref[0, :, sl] + hb_ref[0, :, sl]
        hn_scr[:, sl] = (_rms(hs, g_ref[:, sl]) * jax.nn.sigmoid(og_ref[0, :, sl])).astype(BF16)
    o_ref[0] = x_ref[0] + mod_ref[0, 5:6, :] * _dot(hn_scr[...], w_ref[...])


def _outproj_c(x, mod, hf, hb, z, mh_g, w):
    grp, t, d = x.shape
    tm = TOKEN_TILE // 2
    tok = pl.BlockSpec((1, tm, d), lambda b, i: (b, i, 0))
    return pl.pallas_call(
        _outproj_c_kernel,
        grid=(grp, t // tm),
        in_specs=[
            tok,
            pl.BlockSpec((1, N_MOD, d), lambda b, i: (b, 0, 0)),
            tok, tok,
            pl.BlockSpec((1, tm, D_C), lambda b, i: (b, i, 3)),
            pl.BlockSpec((1, D_C), lambda b, i: (0, 0)),
            pl.BlockSpec((D_C, d), lambda b, i: (0, 0)),
        ],
        out_specs=tok,
        out_shape=jax.ShapeDtypeStruct(x.shape, F32),
        scratch_shapes=[pltpu.VMEM((tm, D_C), BF16)],
        compiler_params=_params("arbitrary", "arbitrary"),
        name="mixer_c_out_proj",
    )(x, mod, hf, hb, z, mh_g.reshape(1, D_C), w)


def kernel(x_prompt, x_sample, cache_k, cache_v, state_lru, state_mlstm_c, state_mlstm_n, state_mlstm_m, c, c_ctx, w_mod, b_mod, norm_g, w_ffn_gate, w_ffn_up, w_ffn_down, w_in_ab, w_out_ab, qn_g, kn_g, rpb, conv_w, conv_b, lru_wa, lru_ba, lru_wx, lru_bx, lru_lam, w_in_c, b_gate_c, mh_norm_g, w_out_c):
    bp, lp, d = x_prompt.shape
    bs, ts, _ = x_sample.shape
    past = cache_k.shape[2]

    cond = jnp.concatenate([c, c_ctx[None, :]], axis=0)
    cond = jnp.pad(cond, ((0, -(bs + 1) % 8), (0, 0)))
    mod = _modulation(cond, w_mod, b_mod)[:, :bs + 1].reshape(DEPTH, bs + 1, N_MOD, d)

    wg = w_ffn_gate.astype(BF16)
    wu = w_ffn_up.astype(BF16)
    wd = w_ffn_down.astype(BF16)

    xp = x_prompt.reshape(1, bp * lp, d)
    xs = x_sample
    new_k, new_v, new_lru, new_c, new_n, new_m = [], [], [], [], [], []
    for l in range(DEPTH):
        j = l // 2
        ms = mod[l, :bs]
        mp = mod[l, bs:]
        xp = _ffn(xp, mp, norm_g[l, 0], wg, wu, wd, l, 0, 0)
        xs = _ffn(xs, ms, norm_g[l, 0], wg, wu, wd, l, 0, 0)
        if l % 2 == 0:
            w_in = w_in_ab[j].astype(BF16)
            w_out = w_out_ab[j].astype(BF16)
            w4 = jnp.concatenate([lru_wa[j, 0], lru_wx[j, 0], lru_wa[j, 1], lru_wx[j, 1]], axis=-1).astype(BF16)
            b4 = jnp.concatenate([lru_ba[j, 0].reshape(H_B, 1, BD_B), lru_bx[j, 0].reshape(H_B, 1, BD_B),
                                  lru_ba[j, 1].reshape(H_B, 1, BD_B), lru_bx[j, 1].reshape(H_B, 1, BD_B)], axis=-1)
            zp = _inproj(xp, mp, norm_g[l, 1], w_in, 1024).reshape(bp, lp, -1)
            oa, ka, va = _ctx_attention(zp, qn_g[j], kn_g[j])
            ob, h_fin = _rglru(zp, conv_w[j], conv_b[j], w4, b4, lru_lam[j], jnp.zeros((bp, 2, D_RNN), F32))
            xp = _outproj_ab(xp, mp, oa.reshape(1, bp * lp, D_A), ob.reshape(1, bp * lp, D_RNN), w_out)
            new_k.append(ka.reshape(bp, lp, H_A, HD_A))
            new_v.append(va.reshape(bp, lp, H_A, HD_A))
            new_lru.append(h_fin)
            zs = _inproj(xs, ms, norm_g[l, 1], w_in, 1024)
            oa = _neighbourhood_attention(zs, cache_k[:, j].reshape(bs, past, D_A),
                                          cache_v[:, j].reshape(bs, past, D_A), rpb[j], qn_g[j], kn_g[j])
            ob, _ = _rglru(zs, conv_w[j], conv_b[j], w4, b4, lru_lam[j], state_lru[:, j])
            xs = _outproj_ab(xs, ms, oa, ob, w_out)
        else:
            w_main = w_in_c[j, :, :4 * D_C].astype(BF16)
            w_gate = jnp.pad(w_in_c[j, :, 4 * D_C:], ((0, 0), (0, 128 - 4 * H_C))).astype(BF16)
            w_out = w_out_c[j].astype(BF16)
            zp = _inproj(xp, mp, norm_g[l, 1], w_main, 1024)
            zgp = _inproj(xp, mp, norm_g[l, 1], w_gate, 128)[..., :4 * H_C]
            hf, hb, cfin, nfin, mfin = _mlstm(
                zp.reshape(bp, lp, -1), zgp.reshape(bp, lp, -1), b_gate_c[j],
                jnp.zeros((bp, 2, H_C, DK_C, DV_C), F32), jnp.zeros((bp, 2, H_C, DK_C), F32),
                jnp.zeros((bp, 2, H_C), F32), False)
            xp = _outproj_c(xp, mp, hf.reshape(1, bp * lp, D_C), hb.reshape(1, bp * lp, D_C), zp,
                            mh_norm_g[j], w_out)
            new_c.append(cfin)
            new_n.append(nfin)
            new_m.append(mfin)
            zs = _inproj(xs, ms, norm_g[l, 1], w_main, 1024)
            zgs = _inproj(xs, ms, norm_g[l, 1], w_gate, 128)[..., :4 * H_C]
            hf, hb, _, _, _ = _mlstm(zs, zgs, b_gate_c[j], state_mlstm_c[:, j], state_mlstm_n[:, j],
                                     state_mlstm_m[:, j], True)
            xs = _outproj_c(xs, ms, hf, hb, zs, mh_norm_g[j], w_out)
        xp = _ffn(xp, mp, norm_g[l, 2], wg, wu, wd, l, 1, 2)
        xs = _ffn(xs, ms, norm_g[l, 2], wg, wu, wd, l, 1, 2)
    return (xp.reshape(bp, lp, d), xs, jnp.stack(new_k, axis=1), jnp.stack(new_v, axis=1),
            jnp.stack(new_lru, axis=1), jnp.stack(new_c, axis=1), jnp.stack(new_n, axis=1),
            jnp.stack(new_m, axis=1))
```

```python
import functools

import numpy as np
import jax
import jax.numpy as jnp
from jax import lax
from jax.experimental import pallas as pl
from jax.experimental.pallas import tpu as pltpu

D_MODEL = 2048
DEPTH = 2
GRID_W = 64
N_MOD = 9
D_FF = 5632
EPS = 1e-6
H_A = 8
HD_A = 128
D_A = H_A * HD_A
WIN_R = 8
WIN_C = 16
D_RNN = 1024
H_B = 8
BD_B = D_RNN // H_B
CONV_W = 4
LRU_C = 8.0
H_C = 8
DK_C = 256
DV_C = 256
D_C = H_C * DV_C
CHUNK = 128
ROPE_BASE = 10000.0

BF16 = jnp.bfloat16
F32 = jnp.float32

V7X_VMEM_BYTES = 64 * 1024 * 1024
VMEM_LIMIT = V7X_VMEM_BYTES - 8 * 1024 * 1024
LANES = 128
SUBLANES = 8

TOKEN_TILE = 512
FF_TILE = 512
PROJ_TILE = 1024
NA_ROWS = 4
NA_KEY_ROWS = NA_ROWS + WIN_R
LRU_SEGS = SUBLANES
MLSTM_HEADS = 2


def _params(*sem):
    return pltpu.CompilerParams(dimension_semantics=sem, vmem_limit_bytes=VMEM_LIMIT)


def _dot(a, b):
    return jnp.dot(a, b, preferred_element_type=F32)


def _dot_nt(a, b):
    return lax.dot_general(a, b, (((1,), (1,)), ((), ())), preferred_element_type=F32)


def _dot_tn(a, b):
    return lax.dot_general(a, b, (((0,), (0,)), ((), ())), preferred_element_type=F32)


def _rms(x, g):
    return x * lax.rsqrt(jnp.mean(x * x, axis=-1, keepdims=True) + EPS) * g


def _adaln(x, g, mod_ref, j):
    return _rms(x, g) * (1.0 + mod_ref[0, 3 * j + 1:3 * j + 2, :]) + mod_ref[0, 3 * j:3 * j + 1, :]


def _softplus(x):
    return jnp.maximum(x, 0.0) + jnp.log1p(jnp.exp(-jnp.abs(x)))


def _log_sigmoid(x):
    return -_softplus(-x)


def _sigmoid(x):
    return 0.5 * jnp.tanh(0.5 * x) + 0.5


def _mod_kernel(c_ref, w_ref, b_ref, o_ref):
    c = c_ref[...]
    s = (c * jax.nn.sigmoid(c)).astype(BF16)
    o_ref[0] = _dot(s, w_ref[0].astype(BF16)) + b_ref[0]


def _modulation(cond, w_mod, b_mod):
    r = cond.shape[0]
    n = N_MOD * D_MODEL
    tn = 1024
    return pl.pallas_call(
        _mod_kernel,
        grid=(DEPTH, n // tn),
        in_specs=[
            pl.BlockSpec((r, D_MODEL), lambda l, j: (0, 0)),
            pl.BlockSpec((1, D_MODEL, tn), lambda l, j: (l, 0, j)),
            pl.BlockSpec((1, 1, tn), lambda l, j: (l, 0, j)),
        ],
        out_specs=pl.BlockSpec((1, r, tn), lambda l, j: (l, 0, j)),
        out_shape=jax.ShapeDtypeStruct((DEPTH, r, n), F32),
        compiler_params=_params("arbitrary", "arbitrary"),
        name="modulation",
    )(cond, w_mod, b_mod.reshape(DEPTH, 1, n))


def _ffn_kernel(x_ref, mod_ref, g_ref, wg_ref, wu_ref, wd_ref, o_ref, h_scr, acc_scr, *, j):
    f = pl.program_id(2)

    @pl.when(f == 0)
    def _():
        h_scr[...] = _adaln(x_ref[0], g_ref[...], mod_ref, j).astype(BF16)
        acc_scr[...] = jnp.zeros_like(acc_scr)

    h = h_scr[...]
    a = _dot(h, wg_ref[...])
    u = _dot(h, wu_ref[...])
    act = (a * jax.nn.sigmoid(a)) * u
    acc_scr[...] += _dot(act.astype(BF16), wd_ref[...])

    @pl.when(f == pl.num_programs(2) - 1)
    def _():
        o_ref[0] = x_ref[0] + 0.5 * mod_ref[0, 3 * j + 2:3 * j + 3, :] * acc_scr[...]


def _ffn(x, mod, g, wg, wu, wd, l, s, j):
    grp, t, d = x.shape
    tm, tf = TOKEN_TILE, FF_TILE
    return pl.pallas_call(
        functools.partial(_ffn_kernel, j=j),
        grid=(grp, t // tm, D_FF // tf),
        in_specs=[
            pl.BlockSpec((1, tm, d), lambda b, i, f: (b, i, 0)),
            pl.BlockSpec((1, N_MOD, d), lambda b, i, f: (b, 0, 0)),
            pl.BlockSpec((1, d), lambda b, i, f: (0, 0)),
            pl.BlockSpec((None, None, d, tf), lambda b, i, f: (l, s, 0, f)),
            pl.BlockSpec((None, None, d, tf), lambda b, i, f: (l, s, 0, f)),
            pl.BlockSpec((None, None, tf, d), lambda b, i, f: (l, s, f, 0)),
        ],
        out_specs=pl.BlockSpec((1, tm, d), lambda b, i, f: (b, i, 0)),
        out_shape=jax.ShapeDtypeStruct(x.shape, F32),
        scratch_shapes=[pltpu.VMEM((tm, d), BF16), pltpu.VMEM((tm, d), F32)],
        compiler_params=_params("arbitrary", "arbitrary", "arbitrary"),
        name="macaron_ffn",
    )(x, mod, g.reshape(1, d), wg, wu, wd)


def _rope(x, cos, sin):
    half = DK_C // 2
    out = []
    for p in range(2):
        sl = slice(p * half, (p + 1) * half)
        xs = x[:, sl]
        out.append(xs * cos[:, sl] + pltpu.roll(xs, half // 2, 1) * sin[:, sl])
    return jnp.concatenate(out, axis=-1)


def _rope_t(x, cos, sin):
    q = DK_C // 4
    swapped = jnp.concatenate([x[q:2 * q], x[0:q], x[3 * q:4 * q], x[2 * q:3 * q]], axis=0)
    return x * cos + swapped * sin


def _inproj_ab_kernel(x_ref, mod_ref, g_ref, w_ref, gain_ref, oa_ref, ob_ref, h_scr, *, n_a, n_norm):
    k = pl.program_id(2)

    @pl.when(k == 0)
    def _():
        h_scr[...] = _adaln(x_ref[0], g_ref[...], mod_ref, 1).astype(BF16)

    z = _dot(h_scr[...], w_ref[...])

    @pl.when(k >= n_a)
    def _():
        ob_ref[0] = z

    @pl.when(k < n_a)
    def _():
        for hh in range(w_ref.shape[1] // HD_A):
            sl = slice(hh * HD_A, (hh + 1) * HD_A)
            zs = z[:, sl]
            r = lax.rsqrt(jnp.mean(zs * zs, axis=-1, keepdims=True) + EPS)
            r = jnp.where(k < n_norm, r, 1.0)
            oa_ref[0, :, sl] = (zs * r * gain_ref[0, :, sl]).astype(oa_ref.dtype)


def _inproj_ab(x, mod, g, w, gains, dtype_a):
    grp, t, d = x.shape
    tm, tn = TOKEN_TILE, PROJ_TILE
    n_a = 3 * D_A // tn
    nt = w.shape[1] // tn
    return pl.pallas_call(
        functools.partial(_inproj_ab_kernel, n_a=n_a, n_norm=2 * D_A // tn),
        grid=(grp, t // tm, nt),
        in_specs=[
            pl.BlockSpec((1, tm, d), lambda b, i, k: (b, i, 0)),
            pl.BlockSpec((1, N_MOD, d), lambda b, i, k: (b, 0, 0)),
            pl.BlockSpec((1, d), lambda b, i, k: (0, 0)),
            pl.BlockSpec((d, tn), lambda b, i, k: (0, k)),
            pl.BlockSpec((1, 1, tn), lambda b, i, k: (jnp.minimum(k, n_a - 1), 0, 0)),
        ],
        out_specs=[pl.BlockSpec((1, tm, tn), lambda b, i, k: (b, i, jnp.minimum(k, n_a - 1))),
                   pl.BlockSpec((1, tm, tn), lambda b, i, k: (b, i, jnp.maximum(k - n_a, 0)))],
        out_shape=[jax.ShapeDtypeStruct((grp, t, n_a * tn), dtype_a),
                   jax.ShapeDtypeStruct((grp, t, (nt - n_a) * tn), F32)],
        scratch_shapes=[pltpu.VMEM((tm, d), BF16)],
        compiler_params=_params("arbitrary", "arbitrary", "arbitrary"),
        name="mixer_ab_in_proj",
    )(x, mod, g.reshape(1, d), w, gains)


def _inproj_c_kernel(*refs, rotary, n_q, n_qv, n_main):
    x_ref, mod_ref, g_ref, w_ref, wkt_ref, wgate_ref = refs[:6]
    pos = 6
    if rotary:
        cos_ref, sin_ref, cost_ref, sint_ref = refs[pos:pos + 4]
        pos += 4
    oa_ref, ob_ref, okt_ref, og_ref, h_scr = refs[pos:pos + 5]
    k = pl.program_id(2)
    tn = w_ref.shape[1]

    @pl.when(k == 0)
    def _():
        h = _adaln(x_ref[0], g_ref[...], mod_ref, 1).astype(BF16)
        h_scr[...] = h
        og_ref[0] = _dot(h, wgate_ref[...])

    @pl.when(k < n_q)
    def _():
        z = _dot(h_scr[...], w_ref[...])
        if rotary:
            for hh in range(tn // DK_C):
                sl = slice(hh * DK_C, (hh + 1) * DK_C)
                oa_ref[0, :, sl] = _rope(z[:, sl], cos_ref[...], sin_ref[...]).astype(BF16)
        else:
            oa_ref[0] = z.astype(BF16)

    @pl.when((k >= n_q) & (k < n_qv))
    def _():
        oa_ref[0] = _dot(h_scr[...], w_ref[...]).astype(BF16)

    @pl.when((k >= n_qv) & (k < n_main))
    def _():
        ob_ref[0] = _dot(h_scr[...], w_ref[...])

    @pl.when(k >= n_main)
    def _():
        zt = _dot_nt(wkt_ref[...], h_scr[...])
        if rotary:
            for hh in range(tn // DK_C):
                sl = slice(hh * DK_C, (hh + 1) * DK_C)
                okt_ref[0, sl, :] = _rope_t(zt[sl, :], cost_ref[...], sint_ref[...]).astype(BF16)
        else:
            okt_ref[0] = zt.astype(BF16)


def _inproj_c(x, mod, g, w_qvo, wkt, w_gate, rope_tables):
    grp, t, d = x.shape
    tm, tn = TOKEN_TILE, PROJ_TILE
    n_q, n_qv, n_main = D_C // tn, 2 * D_C // tn, 3 * D_C // tn
    nt = n_main + D_C // tn
    rotary = rope_tables is not None
    in_specs = [
        pl.BlockSpec((1, tm, d), lambda b, i, k: (b, i, 0)),
        pl.BlockSpec((1, N_MOD, d), lambda b, i, k: (b, 0, 0)),
        pl.BlockSpec((1, d), lambda b, i, k: (0, 0)),
        pl.BlockSpec((d, tn), lambda b, i, k: (0, jnp.minimum(k, n_main - 1))),
        pl.BlockSpec((tn, d), lambda b, i, k: (jnp.maximum(k - n_main, 0), 0)),
        pl.BlockSpec((d, LANES), lambda b, i, k: (0, 0)),
    ]
    args = [x, mod, g.reshape(1, d), w_qvo, wkt, w_gate]
    if rotary:
        cos, sin = rope_tables
        in_specs += [pl.BlockSpec((tm, DK_C), lambda b, i, k: (i, 0))] * 2
        in_specs += [pl.BlockSpec((DK_C, tm), lambda b, i, k: (0, i))] * 2
        args += [cos, sin, cos.T, sin.T]
    return pl.pallas_call(
        functools.partial(_inproj_c_kernel, rotary=rotary, n_q=n_q, n_qv=n_qv, n_main=n_main),
        grid=(grp, t // tm, nt),
        in_specs=in_specs,
        out_specs=[pl.BlockSpec((1, tm, tn), lambda b, i, k: (b, i, jnp.minimum(k, n_qv - 1))),
                   pl.BlockSpec((1, tm, tn), lambda b, i, k: (b, i, jnp.clip(k - n_qv, 0, n_main - n_qv - 1))),
                   pl.BlockSpec((1, tn, tm), lambda b, i, k: (b, jnp.maximum(k - n_main, 0), i)),
                   pl.BlockSpec((1, tm, LANES), lambda b, i, k: (b, i, 0))],
        out_shape=[jax.ShapeDtypeStruct((grp, t, 2 * D_C), BF16),
                   jax.ShapeDtypeStruct((grp, t, D_C), F32),
                   jax.ShapeDtypeStruct((grp, D_C, t), BF16),
                   jax.ShapeDtypeStruct((grp, t, LANES), F32)],
        scratch_shapes=[pltpu.VMEM((tm, d), BF16)],
        compiler_params=_params("arbitrary", "arbitrary", "arbitrary"),
        name="mixer_c_in_proj",
    )(*args)


def _ctx_attn_kernel(q_ref, k_ref, v_ref, o_ref):
    s = _dot_nt(q_ref[0].astype(BF16), k_ref[0].astype(BF16))
    p = jnp.exp(s - jnp.max(s, axis=-1, keepdims=True))
    o = _dot(p.astype(BF16), v_ref[0].astype(BF16)) / jnp.sum(p, axis=-1, keepdims=True)
    o_ref[0] = o.astype(BF16)


def _ctx_attention(qkv):
    b, l, _ = qkv.shape
    blk = lambda off: pl.BlockSpec((1, l, HD_A), lambda i, h: (i, 0, off + h))
    return pl.pallas_call(
        _ctx_attn_kernel,
        grid=(b, H_A),
        in_specs=[blk(0), blk(H_A), blk(2 * H_A)],
        out_specs=pl.BlockSpec((1, l, HD_A), lambda i, h: (i, 0, h)),
        out_shape=jax.ShapeDtypeStruct((b, l, D_A), BF16),
        compiler_params=_params("arbitrary", "arbitrary"),
        name="ctx_attention",
    )(qkv, qkv, qkv)


def _na_bias_tables(rpb, rows):
    nblk = rows // NA_ROWS
    n_dr, n_dc = 2 * WIN_R - 1, 2 * WIN_C - 1
    c = np.arange(GRID_W)[:, None]
    kc = np.arange(GRID_W)[None, :]
    dc = np.clip(kc - c + WIN_C - 1, 0, n_dc - 1)
    pick_c = (dc[None] == np.arange(n_dc)[:, None, None]).astype(np.float32)
    c0 = np.clip(c - WIN_C // 2, 0, GRID_W - WIN_C)
    ok_c = (kc >= c0) & (kc < c0 + WIN_C)
    by_col = jnp.einsum("hrd,dck->hrck", rpb, pick_c, precision=lax.Precision.HIGHEST)
    tables = []
    for i in (0, 1, nblk - 1):
        ks = int(np.clip(NA_ROWS * i - WIN_R // 2, 0, rows - NA_KEY_ROWS))
        r = (NA_ROWS * i + np.arange(NA_ROWS))[:, None]
        kr = (ks + np.arange(NA_KEY_ROWS))[None, :]
        r0 = np.clip(r - WIN_R // 2, 0, rows - WIN_R)
        ok_r = (kr >= r0) & (kr < r0 + WIN_R)
        dr = np.clip(kr - r + WIN_R - 1, 0, n_dr - 1)
        pick_r = (dr[None] == np.arange(n_dr)[:, None, None]).astype(np.float32)
        bias = jnp.einsum("hrck,rqs->hqcsk", by_col, pick_r, precision=lax.Precision.HIGHEST)
        ok = ok_r[None, :, None, :, None] & ok_c[None, None, :, None, :]
        bias = jnp.where(ok, bias, -jnp.inf)
        tables.append(bias.reshape(H_A, NA_ROWS * GRID_W, NA_KEY_ROWS * GRID_W))
    return jnp.stack(tables, axis=1)


def _na_kernel(q_ref, k_ref, v_ref, ck_ref, cv_ref, bias_ref, o_ref, *, rows):
    i = pl.program_id(2)
    nblk = rows // NA_ROWS
    q = q_ref[0]
    ks = jnp.clip(NA_ROWS * i - WIN_R // 2, 0, rows - NA_KEY_ROWS)
    start = pl.multiple_of(ks * GRID_W, GRID_W)
    nkey = NA_KEY_ROWS * GRID_W
    kw = k_ref[0, pl.ds(start, nkey), :]
    vw = v_ref[0, pl.ds(start, nkey), :]
    kind = jnp.where(i == 0, 0, jnp.where(i == nblk - 1, 2, 1))
    s_w = _dot_nt(q, kw) + bias_ref[0, kind]
    s_c = _dot_nt(q, ck_ref[0])
    m = jnp.maximum(jnp.max(s_w, axis=-1, keepdims=True), jnp.max(s_c, axis=-1, keepdims=True))
    p_w = jnp.exp(s_w - m)
    p_c = jnp.exp(s_c - m)
    den = jnp.sum(p_w, axis=-1, keepdims=True) + jnp.sum(p_c, axis=-1, keepdims=True)
    o = _dot(p_w.astype(BF16), vw) + _dot(p_c.astype(BF16), cv_ref[0])
    o_ref[0] = (o / den).astype(BF16)


def _neighbourhood_attention(qkv, ck, cv, rpb):
    b, t, _ = qkv.shape
    l = ck.shape[1]
    rows = t // GRID_W
    nblk = rows // NA_ROWS
    tq = NA_ROWS * GRID_W
    bias = _na_bias_tables(rpb, rows)
    full = lambda off: pl.BlockSpec((1, t, HD_A), lambda h, n, i: (n, 0, off + h))
    ctx = pl.BlockSpec((1, l, HD_A), lambda h, n, i: (n, 0, h))
    return pl.pallas_call(
        functools.partial(_na_kernel, rows=rows),
        grid=(H_A, b, nblk),
        in_specs=[
            pl.BlockSpec((1, tq, HD_A), lambda h, n, i: (n, i, h)),
            full(H_A), full(2 * H_A), ctx, ctx,
            pl.BlockSpec((1, 3, tq, NA_KEY_ROWS * GRID_W), lambda h, n, i: (h, 0, 0, 0)),
        ],
        out_specs=pl.BlockSpec((1, tq, HD_A), lambda h, n, i: (n, i, h)),
        out_shape=jax.ShapeDtypeStruct((b, t, D_A), BF16),
        compiler_params=_params("arbitrary", "arbitrary", "arbitrary"),
        name="neighbourhood_attention",
    )(qkv, qkv, qkv, ck, cv, bias)


def _lru_kernel(xb_ref, gb_ref, cw_ref, cb_ref, w4_ref, b4_ref, lam_ref, h0_ref, o_ref, fin_ref,
                xpad_scr, af_scr, uf_scr, ab_scr, ub_scr, *, t, chunk):
    seg = t // LRU_SEGS
    pitch = seg + SUBLANES
    pad = SUBLANES
    zeros = jnp.zeros((pad, BD_B), F32)
    xpad_scr[0:pad, :] = zeros
    xpad_scr[pad + t:2 * pad + t, :] = zeros
    xpad_scr[pad:pad + t, :] = xb_ref[0]

    sp = _softplus(-lam_ref[...])
    w4 = w4_ref[0]
    left = (CONV_W - 1) // 2
    for c in range(t // chunk):
        base = c * chunk
        xc = cb_ref[...]
        for k in range(CONV_W):
            xc = xc + xpad_scr[pl.ds(base + pad - left + k, chunk), :] * cw_ref[k:k + 1, :]
        ri = _sigmoid(_dot(xc.astype(BF16), w4) + b4_ref[0])
        for d, (a_scr, u_scr) in enumerate(((af_scr, uf_scr), (ab_scr, ub_scr))):
            r = ri[:, 2 * d * BD_B:(2 * d + 1) * BD_B]
            g = ri[:, (2 * d + 1) * BD_B:(2 * d + 2) * BD_B]
            log_a = -LRU_C * r * sp[d:d + 1, :]
            a = jnp.exp(log_a)
            u = jnp.sqrt(-jnp.tanh(log_a) * (a * a + 1.0)) * (g * xc)
            for p in range(chunk // seg):
                s = (base + p * seg) // seg
                a_scr[s * pitch:s * pitch + seg, :] = a[p * seg:(p + 1) * seg, :]
                u_scr[s * pitch:s * pitch + seg, :] = u[p * seg:(p + 1) * seg, :]

    def scan(j, carry):
        hf, pf, hb, pb = carry
        rf = pl.ds(j, LRU_SEGS, stride=pitch)
        rb = pl.ds(seg - 1 - j, LRU_SEGS, stride=pitch)
        a = af_scr[rf, :]
        hf = a * hf + uf_scr[rf, :]
        pf = a * pf
        uf_scr[rf, :] = hf
        af_scr[rf, :] = pf
        a = ab_scr[rb, :]
        hb = a * hb + ub_scr[rb, :]
        pb = a * pb
        ub_scr[rb, :] = hb
        ab_scr[rb, :] = pb
        return hf, pf, hb, pb

    zero = jnp.zeros((LRU_SEGS, BD_B), F32)
    one = jnp.ones((LRU_SEGS, BD_B), F32)
    hf, pf, hb, pb = lax.fori_loop(0, seg, scan, (zero, one, zero, one), unroll=4)

    cf = h0_ref[0, 0:1, :]
    cin_f = []
    for s in range(LRU_SEGS):
        cin_f.append(cf)
        cf = hf[s:s + 1, :] + pf[s:s + 1, :] * cf
    cb = h0_ref[0, 1:2, :]
    cin_b = [None] * LRU_SEGS
    for s in reversed(range(LRU_SEGS)):
        cin_b[s] = cb
        cb = hb[s:s + 1, :] + pb[s:s + 1, :] * cb
    fin_ref[0, 0:1, :] = cf
    fin_ref[0, 1:2, :] = cb

    for s in range(LRU_SEGS):
        rows = slice(s * pitch, s * pitch + seg)
        hs = (uf_scr[rows, :] + af_scr[rows, :] * cin_f[s]) + (ub_scr[rows, :] + ab_scr[rows, :] * cin_b[s])
        nat = slice(s * seg, (s + 1) * seg)
        o_ref[0, nat, :] = (hs * jax.nn.gelu(gb_ref[0, nat, :])).astype(BF16)


def _rglru(xg, conv_w, conv_b, w4, b4, lam, h0):
    b, t, _ = xg.shape
    chunk = min(t, 512)
    seg = t // LRU_SEGS
    col = lambda off: pl.BlockSpec((1, t, BD_B), lambda n, h: (n, 0, off + h))
    return pl.pallas_call(
        functools.partial(_lru_kernel, t=t, chunk=chunk),
        grid=(b, H_B),
        in_specs=[
            col(0), col(H_B),
            pl.BlockSpec((CONV_W, BD_B), lambda n, h: (0, h)),
            pl.BlockSpec((1, BD_B), lambda n, h: (0, h)),
            pl.BlockSpec((1, BD_B, 4 * BD_B), lambda n, h: (h, 0, 0)),
            pl.BlockSpec((1, 1, 4 * BD_B), lambda n, h: (h, 0, 0)),
            pl.BlockSpec((2, BD_B), lambda n, h: (0, h)),
            pl.BlockSpec((1, 2, BD_B), lambda n, h: (n, 0, h)),
        ],
        out_specs=[pl.BlockSpec((1, t, BD_B), lambda n, h: (n, 0, h)),
                   pl.BlockSpec((1, 2, BD_B), lambda n, h: (n, 0, h))],
        out_shape=[jax.ShapeDtypeStruct((b, t, D_RNN), BF16),
                   jax.ShapeDtypeStruct((b, 2, D_RNN), F32)],
        scratch_shapes=[pltpu.VMEM((t + 2 * SUBLANES, BD_B), F32)]
        + [pltpu.VMEM((LRU_SEGS * (seg + SUBLANES), BD_B), F32)] * 4,
        compiler_params=_params("arbitrary", "arbitrary"),
        name="rglru_branch",
    )(xg, xg, conv_w, conv_b.reshape(1, D_RNN), w4, b4, lam, h0)


def _outproj_ab_kernel(x_ref, mod_ref, oa_ref, ob_ref, w_ref, o_ref):
    y = _dot(oa_ref[0], w_ref[0:D_A, :]) + _dot(ob_ref[0], w_ref[D_A:D_A + D_RNN, :])
    o_ref[0] = x_ref[0] + mod_ref[0, 5:6, :] * y


def _outproj_ab(x, mod, oa, ob, w):
    grp, t, d = x.shape
    tm = TOKEN_TILE
    tok = lambda width: pl.BlockSpec((1, tm, width), lambda b, i: (b, i, 0))
    return pl.pallas_call(
        _outproj_ab_kernel,
        grid=(grp, t // tm),
        in_specs=[
            tok(d),
            pl.BlockSpec((1, N_MOD, d), lambda b, i: (b, 0, 0)),
            tok(D_A), tok(D_RNN),
            pl.BlockSpec((D_A + D_RNN, d), lambda b, i: (0, 0)),
        ],
        out_specs=tok(d),
        out_shape=jax.ShapeDtypeStruct(x.shape, F32),
        compiler_params=_params("arbitrary", "arbitrary"),
        name="mixer_ab_out_proj",
    )(x, mod, oa, ob, w)


def _gate_prep_kernel(g_ref, b_ref, o_ref):
    t = g_ref.shape[2]
    g = g_ref[0] + b_ref[...]
    lane = lax.broadcasted_iota(jnp.int32, (H_C, t), 1) & (CHUNK - 1)
    cum = _log_sigmoid(g[H_C:2 * H_C])
    suf = _log_sigmoid(g[3 * H_C:4 * H_C])
    sh = 1
    while sh < CHUNK:
        cum = cum + jnp.where(lane >= sh, pltpu.roll(cum, sh, 1), 0.0)
        suf = suf + jnp.where(lane < CHUNK - sh, pltpu.roll(suf, t - sh, 1), 0.0)
        sh *= 2
    o_ref[0, 0:H_C] = g[0:H_C]
    o_ref[0, H_C:2 * H_C] = cum
    o_ref[0, 2 * H_C:3 * H_C] = g[2 * H_C:3 * H_C]
    o_ref[0, 3 * H_C:4 * H_C] = suf


def _gate_prep(g_rows, b_gate):
    b, r, t = g_rows.shape
    return pl.pallas_call(
        _gate_prep_kernel,
        grid=(b,),
        in_specs=[pl.BlockSpec((1, r, t), lambda n: (n, 0, 0)), pl.BlockSpec((r, 1), lambda n: (0, 0))],
        out_specs=pl.BlockSpec((1, r, t), lambda n: (n, 0, 0)),
        out_shape=jax.ShapeDtypeStruct(g_rows.shape, F32),
        compiler_params=_params("arbitrary"),
        name="mlstm_gate_prep",
    )(g_rows, b_gate.reshape(r, 1))


def _mlstm_chunk(q, kt, v_aug, ig_row, bc_row, bc_col, cm, m, reverse):
    ti = lax.broadcasted_iota(jnp.int32, (CHUNK, CHUNK), 0)
    si = lax.broadcasted_iota(jnp.int32, (CHUNK, CHUNK), 1)
    seen = (si >= ti) if reverse else (si <= ti)
    bk = bc_row[:, 0:1] if reverse else bc_row[:, CHUNK - 1:CHUNK]
    dmat = jnp.where(seen, bc_col - bc_row + ig_row, -jnp.inf)
    inter = bc_col + m
    m_t = jnp.maximum(inter, jnp.max(dmat, axis=1, keepdims=True))
    w = jnp.exp(dmat - m_t)
    g = jnp.exp(inter - m_t)
    kscale = DK_C ** -0.5
    s = _dot(q, kt) * (w * kscale)
    qc = _dot(q, cm.astype(BF16))
    num = _dot(s.astype(BF16), v_aug[:, :DV_C]) + g * qc[:, :DV_C]
    den = jnp.sum(s, axis=1, keepdims=True) + g * qc[:, DV_C:DV_C + 1]
    hout = num / jnp.maximum(jnp.abs(den), jnp.exp(-m_t))
    dend = bk - bc_row + ig_row
    m_new = jnp.maximum(bk + m, jnp.max(dend, axis=1, keepdims=True))
    we = jnp.exp(dend - m_new)
    ge = jnp.exp(bk + m - m_new)
    kwt = kt.astype(F32) * (we * kscale)
    c_new = ge * cm + _dot(kwt.astype(BF16), v_aug)
    return hout, c_new, m_new


def _mlstm_kernel(m0_ref, qvf_ref, vvf_ref, ktf_ref, qvb_ref, vvb_ref, ktb_ref, grf_ref, gcf_ref, grb_ref, gcb_ref,
                  c0_ref, n0_ref, hf_ref, hb_ref, cfin_ref, nfin_ref, mfin_ref, c_scr, m_scr):
    n, hp, c = pl.program_id(0), pl.program_id(1), pl.program_id(2)
    eye = (lax.broadcasted_iota(jnp.int32, (DK_C, DK_C), 0) == lax.broadcasted_iota(jnp.int32, (DK_C, DK_C), 1))
    lane0 = lax.broadcasted_iota(jnp.int32, (DK_C, LANES), 1) == 0

    @pl.when(c == 0)
    def _():
        for d in range(2):
            for j in range(MLSTM_HEADS):
                c_scr[d, j, :, 0:DV_C] = c0_ref[0, d, j]
                n_col = jnp.sum(jnp.where(eye, n0_ref[0, d, j], 0.0), axis=1, keepdims=True)
                c_scr[d, j, :, DV_C:DV_C + LANES] = jnp.where(lane0, n_col, 0.0)
                m_scr[d, j] = jnp.full(m_scr.shape[2:], m0_ref[n, d * H_C + hp * MLSTM_HEADS + j], F32)

    ones_col = jnp.where(lax.broadcasted_iota(jnp.int32, (CHUNK, LANES), 1) == 0, 1.0, 0.0).astype(BF16)
    dirs = ((qvf_ref, vvf_ref, ktf_ref, grf_ref, gcf_ref, hf_ref), (qvb_ref, vvb_ref, ktb_ref, grb_ref, gcb_ref, hb_ref))
    chains = []
    for d, (q_ref, v_ref, kt_ref, gr_ref, gc_ref, h_ref) in enumerate(dirs):
        for j in range(MLSTM_HEADS):
            sl = slice(j * DK_C, (j + 1) * DK_C)
            v_aug = jnp.concatenate([v_ref[0, :, sl], ones_col], axis=1)
            chains.append((d, j, h_ref, sl, _mlstm_chunk(
                q_ref[0, :, sl], kt_ref[0, sl, :], v_aug,
                gr_ref[0, j, 2 * d:2 * d + 1, :], gr_ref[0, j, 2 * d + 1:2 * d + 2, :],
                gc_ref[0, j, :, 2 * d + 1:2 * d + 2],
                c_scr[d, j], m_scr[d, j, 0:1, 0:1], reverse=(d == 1))))
    for d, j, h_ref, sl, (hout, c_new, m_new) in chains:
        h_ref[0, :, sl] = hout
        c_scr[d, j] = c_new
        m_scr[d, j] = jnp.broadcast_to(m_new, m_scr.shape[2:])

    @pl.when(c == pl.num_programs(2) - 1)
    def _():
        for d in range(2):
            for j in range(MLSTM_HEADS):
                cfin_ref[0, d, j] = c_scr[d, j, :, 0:DV_C]
                n_col = c_scr[d, j, :, DV_C:DV_C + 1]
                nfin_ref[0, d, j] = jnp.sum(jnp.where(eye, n_col, 0.0), axis=0, keepdims=True)
                mfin_ref[0, d, j] = m_scr[d, j, 0:1, :]


def _rope_tables(t):
    half = DK_C // 2
    nf = half // 2
    pos = jnp.arange(t)
    freqs = ROPE_BASE ** (-jnp.arange(nf, dtype=F32) / nf)
    cos, sin = [], []
    for p in (pos // GRID_W, pos % GRID_W):
        ang = p.astype(F32)[:, None] * freqs[None, :]
        cos += [jnp.cos(ang), jnp.cos(ang)]
        sin += [-jnp.sin(ang), jnp.sin(ang)]
    return jnp.concatenate(cos, axis=-1), jnp.concatenate(sin, axis=-1)


def _mlstm(qv, kt, gates, c0, n0, m0):
    b, t, _ = qv.shape
    nc = t // CHUNK
    hps = MLSTM_HEADS
    ngrp = H_C // hps
    g4 = gates.reshape(b, 4, H_C, t)
    g_row = jnp.transpose(g4, (0, 2, 1, 3))
    g_col = jnp.transpose(g4, (0, 2, 3, 1))
    fwd = lambda c: c
    bwd = lambda c: nc - 1 - c

    def tok(off, order):
        return pl.BlockSpec((1, CHUNK, hps * DK_C), lambda n, h, c: (n, order(c), off + h))

    def kt_spec(order):
        return pl.BlockSpec((1, hps * DK_C, CHUNK), lambda n, h, c: (n, h, order(c)))

    def g_row_spec(order):
        return pl.BlockSpec((1, hps, 4, CHUNK), lambda n, h, c: (n, h, 0, order(c)))

    def g_col_spec(order):
        return pl.BlockSpec((1, hps, CHUNK, 4), lambda n, h, c: (n, h, order(c), 0))

    state = lambda *tail: pl.BlockSpec((1, 2, hps) + tail, lambda n, h, c: (n, 0, h) + (0,) * len(tail))
    hspec = lambda order: pl.BlockSpec((1, CHUNK, hps * DV_C), lambda n, h, c: (n, order(c), h))
    hf, hb, cfin, nfin, mfin = pl.pallas_call(
        _mlstm_kernel,
        grid=(b, ngrp, nc),
        in_specs=[pl.BlockSpec(memory_space=pltpu.SMEM),
                  tok(0, fwd), tok(ngrp, fwd), kt_spec(fwd),
                  tok(0, bwd), tok(ngrp, bwd), kt_spec(bwd),
                  g_row_spec(fwd), g_col_spec(fwd), g_row_spec(bwd), g_col_spec(bwd),
                  state(DK_C, DV_C), state(1, DK_C)],
        out_specs=[hspec(fwd), hspec(bwd), state(DK_C, DV_C), state(1, DK_C), state(1, LANES)],
        out_shape=[jax.ShapeDtypeStruct((b, t, D_C), F32), jax.ShapeDtypeStruct((b, t, D_C), F32),
                   jax.ShapeDtypeStruct((b, 2, H_C, DK_C, DV_C), F32),
                   jax.ShapeDtypeStruct((b, 2, H_C, 1, DK_C), F32),
                   jax.ShapeDtypeStruct((b, 2, H_C, 1, LANES), F32)],
        scratch_shapes=[pltpu.VMEM((2, hps, DK_C, DV_C + LANES), F32),
                        pltpu.VMEM((2, hps, SUBLANES, LANES), F32)],
        compiler_params=_params("arbitrary", "arbitrary", "arbitrary"),
        name="mlstm_scan",
    )(m0.reshape(b, 2 * H_C), qv, qv, kt, qv, qv, kt, g_row, g_col, g_row, g_col,
      c0, n0.reshape(b, 2, H_C, 1, DK_C))
    return hf, hb, cfin, nfin.reshape(b, 2, H_C, DK_C), mfin[:, :, :, 0, 0]


def _outproj_c_kernel(x_ref, mod_ref, hf_ref, hb_ref, og_ref, g_ref, w_ref, o_ref, hn_scr):
    for h in range(H_C):
        sl = slice(h * DV_C, (h + 1) * DV_C)
        hs = hf_ref[0, :, sl] + hb_ref[0, :, sl]
        hn_scr[:, sl] = (_rms(hs, g_ref[:, sl]) * jax.nn.sigmoid(og_ref[0, :, sl])).astype(BF16)
    o_ref[0] = x_ref[0] + mod_ref[0, 5:6, :] * _dot(hn_scr[...], w_ref[...])


def _outproj_c(x, mod, hf, hb, og, mh_g, w):
    grp, t, d = x.shape
    tm = TOKEN_TILE // 2
    tok = pl.BlockSpec((1, tm, d), lambda b, i: (b, i, 0))
    return pl.pallas_call(
        _outproj_c_kernel,
        grid=(grp, t // tm),
        in_specs=[
            tok,
            pl.BlockSpec((1, N_MOD, d), lambda b, i: (b, 0, 0)),
            tok, tok,
            pl.BlockSpec((1, tm, D_C), lambda b, i: (b, i, 0)),
            pl.BlockSpec((1, D_C), lambda b, i: (0, 0)),
            pl.BlockSpec((D_C, d), lambda b, i: (0, 0)),
        ],
        out_specs=tok,
        out_shape=jax.ShapeDtypeStruct(x.shape, F32),
        scratch_shapes=[pltpu.VMEM((tm, D_C), BF16)],
        compiler_params=_params("arbitrary", "arbitrary"),
        name="mixer_c_out_proj",
    )(x, mod, hf, hb, og, mh_g.reshape(1, D_C), w)


def _mixer_ab(j, xp, xs, mp, ms, g, p):
    bp, lp, bs = p["bp"], p["lp"], xs.shape[0]
    w_in = p["w_in_ab"][j].astype(BF16)
    w_out = p["w_out_ab"][j].astype(BF16)
    gains = jnp.stack([jnp.tile(p["qn_g"][j] * (HD_A ** -0.5), H_A), jnp.tile(p["kn_g"][j], H_A),
                       jnp.ones((D_A,), F32)]).reshape(3, 1, D_A)
    wa, wx, ba, bx = p["lru_wa"][j], p["lru_wx"][j], p["lru_ba"][j], p["lru_bx"][j]
    w4 = jnp.concatenate([wa[0], wx[0], wa[1], wx[1]], axis=-1).astype(BF16)
    b4 = jnp.concatenate([v.reshape(H_B, 1, BD_B) for v in (ba[0], bx[0], ba[1], bx[1])], axis=-1)
    lru = (p["conv_w"][j], p["conv_b"][j], w4, b4, p["lru_lam"][j])

    qkv, xg = _inproj_ab(xp, mp, g, w_in, gains, F32)
    qkv = qkv.reshape(bp, lp, 3 * D_A)
    oa = _ctx_attention(qkv)
    ob, h_fin = _rglru(xg.reshape(bp, lp, 2 * D_RNN), *lru, jnp.zeros((bp, 2, D_RNN), F32))
    xp = _outproj_ab(xp, mp, oa.reshape(1, bp * lp, D_A), ob.reshape(1, bp * lp, D_RNN), w_out)
    new = (qkv[:, :, D_A:2 * D_A].reshape(bp, lp, H_A, HD_A), qkv[:, :, 2 * D_A:].reshape(bp, lp, H_A, HD_A), h_fin)

    past = p["cache_k"].shape[2]
    ck = p["cache_k"][:, j].reshape(bs, past, D_A).astype(BF16)
    cv = p["cache_v"][:, j].reshape(bs, past, D_A).astype(BF16)
    qkv, xg = _inproj_ab(xs, ms, g, w_in, gains, BF16)
    oa = _neighbourhood_attention(qkv, ck, cv, p["rpb"][j])
    ob, _ = _rglru(xg, *lru, p["state_lru"][:, j])
    xs = _outproj_ab(xs, ms, oa, ob, w_out)
    return xp, xs, new


def _mixer_c(j, xp, xs, mp, ms, g, p):
    bp, lp, bs, ts = p["bp"], p["lp"], xs.shape[0], xs.shape[1]
    w = p["w_in_c"][j]
    w_qvo = jnp.concatenate([w[:, :D_C], w[:, 2 * D_C:4 * D_C]], axis=1).astype(BF16)
    wkt = w[:, D_C:2 * D_C].T.astype(BF16)
    w_gate = jnp.pad(w[:, 4 * D_C:], ((0, 0), (0, LANES - 4 * H_C))).astype(BF16)
    w_out = p["w_out_c"][j].astype(BF16)

    def gates_of(zg, b, t):
        g_rows = jnp.transpose(zg.reshape(b, t, LANES)[:, :, :4 * H_C], (0, 2, 1))
        return _gate_prep(g_rows, p["b_gate_c"][j])

    qv, og, kt, zg = _inproj_c(xp, mp, g, w_qvo, wkt, w_gate, None)
    kt = jnp.transpose(kt.reshape(D_C, bp, lp), (1, 0, 2))
    hf, hb, cfin, nfin, mfin = _mlstm(
        qv.reshape(bp, lp, 2 * D_C), kt, gates_of(zg, bp, lp),
        jnp.zeros((bp, 2, H_C, DK_C, DV_C), F32), jnp.zeros((bp, 2, H_C, DK_C), F32), jnp.zeros((bp, 2, H_C), F32))
    xp = _outproj_c(xp, mp, hf.reshape(1, bp * lp, D_C), hb.reshape(1, bp * lp, D_C), og, p["mh_norm_g"][j], w_out)

    qv, og, kt, zg = _inproj_c(xs, ms, g, w_qvo, wkt, w_gate, _rope_tables(ts))
    hf, hb, _, _, _ = _mlstm(qv, kt, gates_of(zg, bs, ts), p["state_mlstm_c"][:, j], p["state_mlstm_n"][:, j],
                             p["state_mlstm_m"][:, j])
    xs = _outproj_c(xs, ms, hf, hb, og, p["mh_norm_g"][j], w_out)
    return xp, xs, (cfin, nfin, mfin)


def kernel(x_prompt, x_sample, cache_k, cache_v, state_lru, state_mlstm_c, state_mlstm_n, state_mlstm_m, c, c_ctx, w_mod, b_mod, norm_g, w_ffn_gate, w_ffn_up, w_ffn_down, w_in_ab, w_out_ab, qn_g, kn_g, rpb, conv_w, conv_b, lru_wa, lru_ba, lru_wx, lru_bx, lru_lam, w_in_c, b_gate_c, mh_norm_g, w_out_c):
    bp, lp, d = x_prompt.shape
    bs = x_sample.shape[0]
    p = dict(bp=bp, lp=lp, cache_k=cache_k, cache_v=cache_v, state_lru=state_lru, state_mlstm_c=state_mlstm_c,
             state_mlstm_n=state_mlstm_n, state_mlstm_m=state_mlstm_m, w_in_ab=w_in_ab, w_out_ab=w_out_ab,
             qn_g=qn_g, kn_g=kn_g, rpb=rpb, conv_w=conv_w, conv_b=conv_b, lru_wa=lru_wa, lru_ba=lru_ba,
             lru_wx=lru_wx, lru_bx=lru_bx, lru_lam=lru_lam, w_in_c=w_in_c, b_gate_c=b_gate_c,
             mh_norm_g=mh_norm_g, w_out_c=w_out_c)

    cond = jnp.concatenate([c, c_ctx[None, :]], axis=0)
    cond = jnp.pad(cond, ((0, -(bs + 1) % SUBLANES), (0, 0)))
    mod = _modulation(cond, w_mod, b_mod)[:, :bs + 1].reshape(DEPTH, bs + 1, N_MOD, d)

    wg = w_ffn_gate.astype(BF16)
    wu = w_ffn_up.astype(BF16)
    wd = w_ffn_down.astype(BF16)

    xp = x_prompt.reshape(1, bp * lp, d)
    xs = x_sample
    new_ab, new_c = [], []
    for l in range(DEPTH):
        j = l // 2
        ms = mod[l, :bs]
        mp = mod[l, bs:]
        xp = _ffn(xp, mp, norm_g[l, 0], wg, wu, wd, l, 0, 0)
        xs = _ffn(xs, ms, norm_g[l, 0], wg, wu, wd, l, 0, 0)
        if l % 2 == 0:
            xp, xs, new = _mixer_ab(j, xp, xs, mp, ms, norm_g[l, 1], p)
            new_ab.append(new)
        else:
            xp, xs, new = _mixer_c(j, xp, xs, mp, ms, norm_g[l, 1], p)
            new_c.append(new)
        xp = _ffn(xp, mp, norm_g[l, 2], wg, wu, wd, l, 1, 2)
        xs = _ffn(xs, ms, norm_g[l, 2], wg, wu, wd, l, 1, 2)
    stack = lambda items, i: jnp.stack([it[i] for it in items], axis=1)
    return (xp.reshape(bp, lp, d), xs, stack(new_ab, 0), stack(new_ab, 1), stack(new_ab, 2),
            stack(new_c, 0), stack(new_c, 1), stack(new_c, 2))
```

```python
import functools

import numpy as np
import jax
import jax.numpy as jnp
from jax import lax
from jax.experimental import pallas as pl
from jax.experimental.pallas import tpu as pltpu

D_MODEL = 2048
DEPTH = 2
GRID_W = 64
N_MOD = 9
D_FF = 5632
EPS = 1e-6
H_A = 8
HD_A = 128
D_A = H_A * HD_A
WIN_R = 8
WIN_C = 16
D_RNN = 1024
H_B = 8
BD_B = D_RNN // H_B
CONV_W = 4
LRU_C = 8.0
H_C = 8
DK_C = 256
DV_C = 256
D_C = H_C * DV_C
CHUNK = 128
ROPE_BASE = 10000.0

BF16 = jnp.bfloat16
F32 = jnp.float32

V7X_VMEM_BYTES = 64 * 1024 * 1024
VMEM_LIMIT = V7X_VMEM_BYTES - 8 * 1024 * 1024
LANES = 128
SUBLANES = 8

TOKEN_TILE = 512
FF_TILE = 512
FFN_NORM_ROWS = 64
PROJ_TOKEN_TILE = 1024
PROJ_TILE = 512
NA_ROWS = 4
NA_KEY_ROWS = NA_ROWS + WIN_R
LRU_SEGS = SUBLANES
MLSTM_HEADS = 2


def _params(*sem):
    return pltpu.CompilerParams(dimension_semantics=sem, vmem_limit_bytes=VMEM_LIMIT)


def _dot(a, b):
    return jnp.dot(a, b, preferred_element_type=F32)


def _dot_nt(a, b):
    return lax.dot_general(a, b, (((1,), (1,)), ((), ())), preferred_element_type=F32)


def _dot_tn(a, b):
    return lax.dot_general(a, b, (((0,), (0,)), ((), ())), preferred_element_type=F32)


def _rms(x, g):
    return x * lax.rsqrt(jnp.mean(x * x, axis=-1, keepdims=True) + EPS) * g


def _adaln(x, g, mod_ref, j):
    return _rms(x, g) * (1.0 + mod_ref[0, 3 * j + 1:3 * j + 2, :]) + mod_ref[0, 3 * j:3 * j + 1, :]


def _softplus(x):
    return jnp.maximum(x, 0.0) + jnp.log1p(jnp.exp(-jnp.abs(x)))


def _log_sigmoid(x):
    return -_softplus(-x)


def _sigmoid(x):
    return 0.5 * jnp.tanh(0.5 * x) + 0.5


def _mod_kernel(c_ref, w_ref, b_ref, o_ref):
    c = c_ref[...]
    s = (c * jax.nn.sigmoid(c)).astype(BF16)
    o_ref[0] = _dot(s, w_ref[0].astype(BF16)) + b_ref[0]


def _modulation(cond, w_mod, b_mod):
    r = cond.shape[0]
    n = N_MOD * D_MODEL
    tn = 1024
    return pl.pallas_call(
        _mod_kernel,
        grid=(DEPTH, n // tn),
        in_specs=[
            pl.BlockSpec((r, D_MODEL), lambda l, j: (0, 0)),
            pl.BlockSpec((1, D_MODEL, tn), lambda l, j: (l, 0, j)),
            pl.BlockSpec((1, 1, tn), lambda l, j: (l, 0, j)),
        ],
        out_specs=pl.BlockSpec((1, r, tn), lambda l, j: (l, 0, j)),
        out_shape=jax.ShapeDtypeStruct((DEPTH, r, n), F32),
        compiler_params=_params("arbitrary", "arbitrary"),
        name="modulation",
    )(cond, w_mod, b_mod.reshape(DEPTH, 1, n))


def _ffn_kernel(x_ref, xn_ref, mod_ref, modn_ref, g_ref, wg_ref, wu_ref, wd_ref, o_ref,
                ha_scr, hb_scr, acc_scr, *, j, nf):
    p, f2 = pl.program_id(0), pl.program_id(1)
    rows = FFN_NORM_ROWS
    tm = ha_scr.shape[0]

    @pl.when((p == 0) & (f2 == 0))
    def _():
        ha_scr[...] = _adaln(x_ref[0], g_ref[...], mod_ref, j).astype(BF16)

    @pl.when((f2 == 0) | (f2 == nf))
    def _():
        acc_scr[...] = jnp.zeros_like(acc_scr)

    def body(h_cur, h_nxt, f):
        r0 = pl.multiple_of(jnp.minimum(f, tm // rows - 1) * rows, rows)
        h_nxt[pl.ds(r0, rows), :] = _adaln(xn_ref[0, pl.ds(r0, rows), :], g_ref[...], modn_ref, j).astype(BF16)
        h = h_cur[...]
        a = _dot(h, wg_ref[...])
        u = _dot(h, wu_ref[...])
        act = (a * jax.nn.sigmoid(a)) * u
        acc_scr[...] += _dot(act.astype(BF16), wd_ref[...])

    @pl.when(f2 < nf)
    def _():
        body(ha_scr, hb_scr, f2)

    @pl.when(f2 >= nf)
    def _():
        body(hb_scr, ha_scr, f2 - nf)

    @pl.when((f2 == nf - 1) | (f2 == 2 * nf - 1))
    def _():
        o_ref[0] = x_ref[0] + 0.5 * mod_ref[0, 3 * j + 2:3 * j + 3, :] * acc_scr[...]


def _ffn(x, mod, g, wg, wu, wd, l, s, j):
    grp, t, d = x.shape
    tm, tf = TOKEN_TILE, FF_TILE
    nf = D_FF // tf
    per_group = t // tm
    ntile = grp * per_group
    assert ntile % 2 == 0 and nf >= tm // FFN_NORM_ROWS
    tile = lambda p, f2: 2 * p + f2 // nf
    nxt = lambda p, f2: jnp.minimum(tile(p, f2) + 1, ntile - 1)
    out = pl.pallas_call(
        functools.partial(_ffn_kernel, j=j, nf=nf),
        grid=(ntile // 2, 2 * nf),
        in_specs=[
            pl.BlockSpec((1, tm, d), lambda p, f2: (0, tile(p, f2), 0)),
            pl.BlockSpec((1, tm, d), lambda p, f2: (0, nxt(p, f2), 0)),
            pl.BlockSpec((1, N_MOD, d), lambda p, f2: (tile(p, f2) // per_group, 0, 0)),
            pl.BlockSpec((1, N_MOD, d), lambda p, f2: (nxt(p, f2) // per_group, 0, 0)),
            pl.BlockSpec((1, d), lambda p, f2: (0, 0)),
            pl.BlockSpec((None, None, d, tf), lambda p, f2: (l, s, 0, f2 % nf)),
            pl.BlockSpec((None, None, d, tf), lambda p, f2: (l, s, 0, f2 % nf)),
            pl.BlockSpec((None, None, tf, d), lambda p, f2: (l, s, f2 % nf, 0)),
        ],
        out_specs=pl.BlockSpec((1, tm, d), lambda p, f2: (0, tile(p, f2), 0)),
        out_shape=jax.ShapeDtypeStruct((1, grp * t, d), F32),
        scratch_shapes=[pltpu.VMEM((tm, d), BF16), pltpu.VMEM((tm, d), BF16), pltpu.VMEM((tm, d), F32)],
        compiler_params=_params("arbitrary", "arbitrary"),
        name="macaron_ffn",
    )(x.reshape(1, grp * t, d), x.reshape(1, grp * t, d), mod, mod, g.reshape(1, d), wg, wu, wd)
    return out.reshape(x.shape)


def _rope(x, cos, sin):
    half = DK_C // 2
    out = []
    for p in range(2):
        sl = slice(p * half, (p + 1) * half)
        xs = x[:, sl]
        out.append(xs * cos[:, sl] + pltpu.roll(xs, half // 2, 1) * sin[:, sl])
    return jnp.concatenate(out, axis=-1)


def _rope_t(x, cos, sin):
    q = DK_C // 4
    swapped = jnp.concatenate([x[q:2 * q], x[0:q], x[3 * q:4 * q], x[2 * q:3 * q]], axis=0)
    return x * cos + swapped * sin


def _inproj_ab_kernel(x_ref, mod_ref, g_ref, w_ref, gain_ref, oa_ref, ob_ref, h_scr, *, n_a, n_norm):
    k = pl.program_id(2)

    @pl.when(k == 0)
    def _():
        h_scr[...] = _adaln(x_ref[0], g_ref[...], mod_ref, 1).astype(BF16)

    z = _dot(h_scr[...], w_ref[...])

    @pl.when(k >= n_a)
    def _():
        ob_ref[0] = z

    @pl.when(k < n_a)
    def _():
        for hh in range(w_ref.shape[1] // HD_A):
            sl = slice(hh * HD_A, (hh + 1) * HD_A)
            zs = z[:, sl]
            r = lax.rsqrt(jnp.mean(zs * zs, axis=-1, keepdims=True) + EPS)
            r = jnp.where(k < n_norm, r, 1.0)
            oa_ref[0, :, sl] = (zs * r * gain_ref[0, :, sl]).astype(oa_ref.dtype)


def _inproj_ab(x, mod, g, w, gains, dtype_a):
    grp, t, d = x.shape
    tm, tn = PROJ_TOKEN_TILE, PROJ_TILE
    n_a = 3 * D_A // tn
    nt = w.shape[1] // tn
    return pl.pallas_call(
        functools.partial(_inproj_ab_kernel, n_a=n_a, n_norm=2 * D_A // tn),
        grid=(grp, t // tm, nt),
        in_specs=[
            pl.BlockSpec((1, tm, d), lambda b, i, k: (b, i, 0)),
            pl.BlockSpec((1, N_MOD, d), lambda b, i, k: (b, 0, 0)),
            pl.BlockSpec((1, d), lambda b, i, k: (0, 0)),
            pl.BlockSpec((d, tn), lambda b, i, k: (0, k)),
            pl.BlockSpec((1, 1, tn), lambda b, i, k: (jnp.minimum(k, n_a - 1), 0, 0)),
        ],
        out_specs=[pl.BlockSpec((1, tm, tn), lambda b, i, k: (b, i, jnp.minimum(k, n_a - 1))),
                   pl.BlockSpec((1, tm, tn), lambda b, i, k: (b, i, jnp.maximum(k - n_a, 0)))],
        out_shape=[jax.ShapeDtypeStruct((grp, t, n_a * tn), dtype_a),
                   jax.ShapeDtypeStruct((grp, t, (nt - n_a) * tn), F32)],
        scratch_shapes=[pltpu.VMEM((tm, d), BF16)],
        compiler_params=_params("arbitrary", "arbitrary", "arbitrary"),
        name="mixer_ab_in_proj",
    )(x, mod, g.reshape(1, d), w, gains.reshape(n_a, 1, tn))


def _inproj_c_kernel(*refs, rotary, n_q, n_qv, n_main):
    x_ref, mod_ref, g_ref, w_ref, wkt_ref, wgate_ref = refs[:6]
    pos = 6
    if rotary:
        cos_ref, sin_ref, cost_ref, sint_ref = refs[pos:pos + 4]
        pos += 4
    oa_ref, ob_ref, okt_ref, og_ref, h_scr = refs[pos:pos + 5]
    k = pl.program_id(2)
    tn = w_ref.shape[1]

    @pl.when(k == 0)
    def _():
        h = _adaln(x_ref[0], g_ref[...], mod_ref, 1).astype(BF16)
        h_scr[...] = h
        og_ref[0] = _dot(h, wgate_ref[...])

    @pl.when(k < n_q)
    def _():
        z = _dot(h_scr[...], w_ref[...])
        if rotary:
            for hh in range(tn // DK_C):
                sl = slice(hh * DK_C, (hh + 1) * DK_C)
                oa_ref[0, :, sl] = _rope(z[:, sl], cos_ref[...], sin_ref[...]).astype(BF16)
        else:
            oa_ref[0] = z.astype(BF16)

    @pl.when((k >= n_q) & (k < n_qv))
    def _():
        oa_ref[0] = _dot(h_scr[...], w_ref[...]).astype(BF16)

    @pl.when((k >= n_qv) & (k < n_main))
    def _():
        ob_ref[0] = _dot(h_scr[...], w_ref[...])

    @pl.when(k >= n_main)
    def _():
        zt = _dot_nt(wkt_ref[...], h_scr[...])
        if rotary:
            for hh in range(tn // DK_C):
                sl = slice(hh * DK_C, (hh + 1) * DK_C)
                okt_ref[0, sl, :] = _rope_t(zt[sl, :], cost_ref[...], sint_ref[...]).astype(BF16)
        else:
            okt_ref[0] = zt.astype(BF16)


def _inproj_c(x, mod, g, w_qvo, wkt, w_gate, rope_tables):
    grp, t, d = x.shape
    tm, tn = PROJ_TOKEN_TILE, PROJ_TILE
    n_q, n_qv, n_main = D_C // tn, 2 * D_C // tn, 3 * D_C // tn
    nt = n_main + D_C // tn
    rotary = rope_tables is not None
    in_specs = [
        pl.BlockSpec((1, tm, d), lambda b, i, k: (b, i, 0)),
        pl.BlockSpec((1, N_MOD, d), lambda b, i, k: (b, 0, 0)),
        pl.BlockSpec((1, d), lambda b, i, k: (0, 0)),
        pl.BlockSpec((d, tn), lambda b, i, k: (0, jnp.minimum(k, n_main - 1))),
        pl.BlockSpec((tn, d), lambda b, i, k: (jnp.maximum(k - n_main, 0), 0)),
        pl.BlockSpec((d, LANES), lambda b, i, k: (0, 0)),
    ]
    args = [x, mod, g.reshape(1, d), w_qvo, wkt, w_gate]
    if rotary:
        cos, sin = rope_tables
        in_specs += [pl.BlockSpec((tm, DK_C), lambda b, i, k: (i, 0))] * 2
        in_specs += [pl.BlockSpec((DK_C, tm), lambda b, i, k: (0, i))] * 2
        args += [cos, sin, cos.T, sin.T]
    return pl.pallas_call(
        functools.partial(_inproj_c_kernel, rotary=rotary, n_q=n_q, n_qv=n_qv, n_main=n_main),
        grid=(grp, t // tm, nt),
        in_specs=in_specs,
        out_specs=[pl.BlockSpec((1, tm, tn), lambda b, i, k: (b, i, jnp.minimum(k, n_qv - 1))),
                   pl.BlockSpec((1, tm, tn), lambda b, i, k: (b, i, jnp.clip(k - n_qv, 0, n_main - n_qv - 1))),
                   pl.BlockSpec((1, tn, tm), lambda b, i, k: (b, jnp.maximum(k - n_main, 0), i)),
                   pl.BlockSpec((1, tm, LANES), lambda b, i, k: (b, i, 0))],
        out_shape=[jax.ShapeDtypeStruct((grp, t, 2 * D_C), BF16),
                   jax.ShapeDtypeStruct((grp, t, D_C), F32),
                   jax.ShapeDtypeStruct((grp, D_C, t), BF16),
                   jax.ShapeDtypeStruct((grp, t, LANES), F32)],
        scratch_shapes=[pltpu.VMEM((tm, d), BF16)],
        compiler_params=_params("arbitrary", "arbitrary", "arbitrary"),
        name="mixer_c_in_proj",
    )(*args)


def _ctx_attn_kernel(q_ref, k_ref, v_ref, o_ref):
    s = _dot_nt(q_ref[0].astype(BF16), k_ref[0].astype(BF16))
    p = jnp.exp(s - jnp.max(s, axis=-1, keepdims=True))
    o = _dot(p.astype(BF16), v_ref[0].astype(BF16)) / jnp.sum(p, axis=-1, keepdims=True)
    o_ref[0] = o.astype(BF16)


def _ctx_attention(qkv):
    b, l, _ = qkv.shape
    blk = lambda off: pl.BlockSpec((1, l, HD_A), lambda i, h: (i, 0, off + h))
    return pl.pallas_call(
        _ctx_attn_kernel,
        grid=(b, H_A),
        in_specs=[blk(0), blk(H_A), blk(2 * H_A)],
        out_specs=pl.BlockSpec((1, l, HD_A), lambda i, h: (i, 0, h)),
        out_shape=jax.ShapeDtypeStruct((b, l, D_A), BF16),
        compiler_params=_params("arbitrary", "arbitrary"),
        name="ctx_attention",
    )(qkv, qkv, qkv)


def _na_bias_tables(rpb, rows):
    nblk = rows // NA_ROWS
    n_dr, n_dc = 2 * WIN_R - 1, 2 * WIN_C - 1
    c = np.arange(GRID_W)[:, None]
    kc = np.arange(GRID_W)[None, :]
    dc = np.clip(kc - c + WIN_C - 1, 0, n_dc - 1)
    pick_c = (dc[None] == np.arange(n_dc)[:, None, None]).astype(np.float32)
    c0 = np.clip(c - WIN_C // 2, 0, GRID_W - WIN_C)
    ok_c = (kc >= c0) & (kc < c0 + WIN_C)
    by_col = jnp.einsum("hrd,dck->hrck", rpb, pick_c, precision=lax.Precision.HIGHEST)
    tables = []
    for i in (0, 1, nblk - 1):
        ks = int(np.clip(NA_ROWS * i - WIN_R // 2, 0, rows - NA_KEY_ROWS))
        r = (NA_ROWS * i + np.arange(NA_ROWS))[:, None]
        kr = (ks + np.arange(NA_KEY_ROWS))[None, :]
        r0 = np.clip(r - WIN_R // 2, 0, rows - WIN_R)
        ok_r = (kr >= r0) & (kr < r0 + WIN_R)
        dr = np.clip(kr - r + WIN_R - 1, 0, n_dr - 1)
        pick_r = (dr[None] == np.arange(n_dr)[:, None, None]).astype(np.float32)
        bias = jnp.einsum("hrck,rqs->hqcsk", by_col, pick_r, precision=lax.Precision.HIGHEST)
        ok = ok_r[None, :, None, :, None] & ok_c[None, None, :, None, :]
        bias = jnp.where(ok, bias, -jnp.inf)
        tables.append(bias.reshape(H_A, NA_ROWS * GRID_W, NA_KEY_ROWS * GRID_W))
    return jnp.stack(tables, axis=1)


def _na_kernel(q_ref, k_ref, v_ref, ck_ref, cv_ref, bias_ref, o_ref, *, rows):
    i = pl.program_id(2)
    nblk = rows // NA_ROWS
    q = q_ref[0]
    ks = jnp.clip(NA_ROWS * i - WIN_R // 2, 0, rows - NA_KEY_ROWS)
    start = pl.multiple_of(ks * GRID_W, GRID_W)
    nkey = NA_KEY_ROWS * GRID_W
    kw = k_ref[0, pl.ds(start, nkey), :]
    vw = v_ref[0, pl.ds(start, nkey), :]
    kind = jnp.where(i == 0, 0, jnp.where(i == nblk - 1, 2, 1))
    s_w = _dot_nt(q, kw) + bias_ref[0, kind]
    s_c = _dot_nt(q, ck_ref[0])
    m = jnp.maximum(jnp.max(s_w, axis=-1, keepdims=True), jnp.max(s_c, axis=-1, keepdims=True))
    p_w = jnp.exp(s_w - m)
    p_c = jnp.exp(s_c - m)
    den = jnp.sum(p_w, axis=-1, keepdims=True) + jnp.sum(p_c, axis=-1, keepdims=True)
    o = _dot(p_w.astype(BF16), vw) + _dot(p_c.astype(BF16), cv_ref[0])
    o_ref[0] = (o / den).astype(BF16)


def _neighbourhood_attention(qkv, ck, cv, rpb):
    b, t, _ = qkv.shape
    l = ck.shape[1]
    rows = t // GRID_W
    nblk = rows // NA_ROWS
    tq = NA_ROWS * GRID_W
    bias = _na_bias_tables(rpb, rows)
    full = lambda off: pl.BlockSpec((1, t, HD_A), lambda h, n, i: (n, 0, off + h))
    ctx = pl.BlockSpec((1, l, HD_A), lambda h, n, i: (n, 0, h))
    return pl.pallas_call(
        functools.partial(_na_kernel, rows=rows),
        grid=(H_A, b, nblk),
        in_specs=[
            pl.BlockSpec((1, tq, HD_A), lambda h, n, i: (n, i, h)),
            full(H_A), full(2 * H_A), ctx, ctx,
            pl.BlockSpec((1, 3, tq, NA_KEY_ROWS * GRID_W), lambda h, n, i: (h, 0, 0, 0)),
        ],
        out_specs=pl.BlockSpec((1, tq, HD_A), lambda h, n, i: (n, i, h)),
        out_shape=jax.ShapeDtypeStruct((b, t, D_A), BF16),
        compiler_params=_params("arbitrary", "arbitrary", "arbitrary"),
        name="neighbourhood_attention",
    )(qkv, qkv, qkv, ck, cv, bias)


def _lru_kernel(xb_ref, gb_ref, cw_ref, cb_ref, w4_ref, b4_ref, lam_ref, h0_ref, o_ref, fin_ref,
                xpad_scr, af_scr, uf_scr, ab_scr, ub_scr, *, t, chunk):
    seg = t // LRU_SEGS
    pitch = seg + SUBLANES
    pad = SUBLANES
    zeros = jnp.zeros((pad, BD_B), F32)
    xpad_scr[0:pad, :] = zeros
    xpad_scr[pad + t:2 * pad + t, :] = zeros
    xpad_scr[pad:pad + t, :] = xb_ref[0]

    sp = _softplus(-lam_ref[...])
    w4 = w4_ref[0]
    left = (CONV_W - 1) // 2
    for c in range(t // chunk):
        base = c * chunk
        xc = cb_ref[...]
        for k in range(CONV_W):
            xc = xc + xpad_scr[pl.ds(base + pad - left + k, chunk), :] * cw_ref[k:k + 1, :]
        ri = _sigmoid(_dot(xc.astype(BF16), w4) + b4_ref[0])
        for d, (a_scr, u_scr) in enumerate(((af_scr, uf_scr), (ab_scr, ub_scr))):
            r = ri[:, 2 * d * BD_B:(2 * d + 1) * BD_B]
            g = ri[:, (2 * d + 1) * BD_B:(2 * d + 2) * BD_B]
            log_a = -LRU_C * r * sp[d:d + 1, :]
            a = jnp.exp(log_a)
            u = jnp.sqrt(-jnp.tanh(log_a) * (a * a + 1.0)) * (g * xc)
            for p in range(chunk // seg):
                s = (base + p * seg) // seg
                a_scr[s * pitch:s * pitch + seg, :] = a[p * seg:(p + 1) * seg, :]
                u_scr[s * pitch:s * pitch + seg, :] = u[p * seg:(p + 1) * seg, :]

    def scan(j, carry):
        hf, pf, hb, pb = carry
        rf = pl.ds(j, LRU_SEGS, stride=pitch)
        rb = pl.ds(seg - 1 - j, LRU_SEGS, stride=pitch)
        a = af_scr[rf, :]
        hf = a * hf + uf_scr[rf, :]
        pf = a * pf
        uf_scr[rf, :] = hf
        af_scr[rf, :] = pf
        a = ab_scr[rb, :]
        hb = a * hb + ub_scr[rb, :]
        pb = a * pb
        ub_scr[rb, :] = hb
        ab_scr[rb, :] = pb
        return hf, pf, hb, pb

    zero = jnp.zeros((LRU_SEGS, BD_B), F32)
    one = jnp.ones((LRU_SEGS, BD_B), F32)
    hf, pf, hb, pb = lax.fori_loop(0, seg, scan, (zero, one, zero, one), unroll=4)

    cf = h0_ref[0, 0:1, :]
    cin_f = []
    for s in range(LRU_SEGS):
        cin_f.append(cf)
        cf = hf[s:s + 1, :] + pf[s:s + 1, :] * cf
    cb = h0_ref[0, 1:2, :]
    cin_b = [None] * LRU_SEGS
    for s in reversed(range(LRU_SEGS)):
        cin_b[s] = cb
        cb = hb[s:s + 1, :] + pb[s:s + 1, :] * cb
    fin_ref[0, 0:1, :] = cf
    fin_ref[0, 1:2, :] = cb

    for s in range(LRU_SEGS):
        rows = slice(s * pitch, s * pitch + seg)
        hs = (uf_scr[rows, :] + af_scr[rows, :] * cin_f[s]) + (ub_scr[rows, :] + ab_scr[rows, :] * cin_b[s])
        nat = slice(s * seg, (s + 1) * seg)
        o_ref[0, nat, :] = (hs * jax.nn.gelu(gb_ref[0, nat, :])).astype(BF16)


def _rglru(xg, conv_w, conv_b, w4, b4, lam, h0):
    b, t, _ = xg.shape
    chunk = min(t, 512)
    seg = t // LRU_SEGS
    col = lambda off: pl.BlockSpec((1, t, BD_B), lambda n, h: (n, 0, off + h))
    return pl.pallas_call(
        functools.partial(_lru_kernel, t=t, chunk=chunk),
        grid=(b, H_B),
        in_specs=[
            col(0), col(H_B),
            pl.BlockSpec((CONV_W, BD_B), lambda n, h: (0, h)),
            pl.BlockSpec((1, BD_B), lambda n, h: (0, h)),
            pl.BlockSpec((1, BD_B, 4 * BD_B), lambda n, h: (h, 0, 0)),
            pl.BlockSpec((1, 1, 4 * BD_B), lambda n, h: (h, 0, 0)),
            pl.BlockSpec((2, BD_B), lambda n, h: (0, h)),
            pl.BlockSpec((1, 2, BD_B), lambda n, h: (n, 0, h)),
        ],
        out_specs=[pl.BlockSpec((1, t, BD_B), lambda n, h: (n, 0, h)),
                   pl.BlockSpec((1, 2, BD_B), lambda n, h: (n, 0, h))],
        out_shape=[jax.ShapeDtypeStruct((b, t, D_RNN), BF16),
                   jax.ShapeDtypeStruct((b, 2, D_RNN), F32)],
        scratch_shapes=[pltpu.VMEM((t + 2 * SUBLANES, BD_B), F32)]
        + [pltpu.VMEM((LRU_SEGS * (seg + SUBLANES), BD_B), F32)] * 4,
        compiler_params=_params("arbitrary", "arbitrary"),
        name="rglru_branch",
    )(xg, xg, conv_w, conv_b.reshape(1, D_RNN), w4, b4, lam, h0)


def _outproj_ab_kernel(x_ref, mod_ref, oa_ref, ob_ref, w_ref, o_ref):
    y = _dot(oa_ref[0], w_ref[0:D_A, :]) + _dot(ob_ref[0], w_ref[D_A:D_A + D_RNN, :])
    o_ref[0] = x_ref[0] + mod_ref[0, 5:6, :] * y


def _outproj_ab(x, mod, oa, ob, w):
    grp, t, d = x.shape
    tm = TOKEN_TILE
    tok = lambda width: pl.BlockSpec((1, tm, width), lambda b, i: (b, i, 0))
    return pl.pallas_call(
        _outproj_ab_kernel,
        grid=(grp, t // tm),
        in_specs=[
            tok(d),
            pl.BlockSpec((1, N_MOD, d), lambda b, i: (b, 0, 0)),
            tok(D_A), tok(D_RNN),
            pl.BlockSpec((D_A + D_RNN, d), lambda b, i: (0, 0)),
        ],
        out_specs=tok(d),
        out_shape=jax.ShapeDtypeStruct(x.shape, F32),
        compiler_params=_params("arbitrary", "arbitrary"),
        name="mixer_ab_out_proj",
    )(x, mod, oa, ob, w)


def _gate_prep_kernel(g_ref, b_ref, o_ref):
    t = g_ref.shape[2]
    g = g_ref[0] + b_ref[...]
    lane = lax.broadcasted_iota(jnp.int32, (H_C, t), 1) & (CHUNK - 1)
    cum = _log_sigmoid(g[H_C:2 * H_C])
    suf = _log_sigmoid(g[3 * H_C:4 * H_C])
    sh = 1
    while sh < CHUNK:
        cum = cum + jnp.where(lane >= sh, pltpu.roll(cum, sh, 1), 0.0)
        suf = suf + jnp.where(lane < CHUNK - sh, pltpu.roll(suf, t - sh, 1), 0.0)
        sh *= 2
    o_ref[0, 0:H_C] = g[0:H_C]
    o_ref[0, H_C:2 * H_C] = cum
    o_ref[0, 2 * H_C:3 * H_C] = g[2 * H_C:3 * H_C]
    o_ref[0, 3 * H_C:4 * H_C] = suf


def _gate_prep(g_rows, b_gate):
    b, r, t = g_rows.shape
    return pl.pallas_call(
        _gate_prep_kernel,
        grid=(b,),
        in_specs=[pl.BlockSpec((1, r, t), lambda n: (n, 0, 0)), pl.BlockSpec((r, 1), lambda n: (0, 0))],
        out_specs=pl.BlockSpec((1, r, t), lambda n: (n, 0, 0)),
        out_shape=jax.ShapeDtypeStruct(g_rows.shape, F32),
        compiler_params=_params("arbitrary"),
        name="mlstm_gate_prep",
    )(g_rows, b_gate.reshape(r, 1))


def _mlstm_chunk(q, kt, v_aug, ig_row, bc_row, bc_col, cm, m, reverse):
    ti = lax.broadcasted_iota(jnp.int32, (CHUNK, CHUNK), 0)
    si = lax.broadcasted_iota(jnp.int32, (CHUNK, CHUNK), 1)
    seen = (si >= ti) if reverse else (si <= ti)
    bk = bc_row[:, 0:1] if reverse else bc_row[:, CHUNK - 1:CHUNK]
    dmat = jnp.where(seen, bc_col - bc_row + ig_row, -jnp.inf)
    inter = bc_col + m
    m_t = jnp.maximum(inter, jnp.max(dmat, axis=1, keepdims=True))
    w = jnp.exp(dmat - m_t)
    g = jnp.exp(inter - m_t)
    kscale = DK_C ** -0.5
    s = _dot(q, kt) * (w * kscale)
    qc = _dot(q, cm.astype(BF16))
    num = _dot(s.astype(BF16), v_aug[:, :DV_C]) + g * qc[:, :DV_C]
    den = jnp.sum(s, axis=1, keepdims=True) + g * qc[:, DV_C:DV_C + 1]
    hout = num / jnp.maximum(jnp.abs(den), jnp.exp(-m_t))
    dend = bk - bc_row + ig_row
    m_new = jnp.maximum(bk + m, jnp.max(dend, axis=1, keepdims=True))
    we = jnp.exp(dend - m_new)
    ge = jnp.exp(bk + m - m_new)
    kwt = kt.astype(F32) * (we * kscale)
    c_new = ge * cm + _dot(kwt.astype(BF16), v_aug)
    return hout, c_new, m_new


def _mlstm_kernel(m0_ref, qvf_ref, vvf_ref, ktf_ref, qvb_ref, vvb_ref, ktb_ref, grf_ref, gcf_ref, grb_ref, gcb_ref,
                  c0_ref, n0_ref, hf_ref, hb_ref, cfin_ref, nfin_ref, mfin_ref, c_scr, m_scr):
    n, hp, c = pl.program_id(0), pl.program_id(1), pl.program_id(2)
    eye = (lax.broadcasted_iota(jnp.int32, (DK_C, DK_C), 0) == lax.broadcasted_iota(jnp.int32, (DK_C, DK_C), 1))
    lane0 = lax.broadcasted_iota(jnp.int32, (DK_C, LANES), 1) == 0

    @pl.when(c == 0)
    def _():
        for d in range(2):
            for j in range(MLSTM_HEADS):
                c_scr[d, j, :, 0:DV_C] = c0_ref[0, d, j]
                n_col = jnp.sum(jnp.where(eye, n0_ref[0, d, j], 0.0), axis=1, keepdims=True)
                c_scr[d, j, :, DV_C:DV_C + LANES] = jnp.where(lane0, n_col, 0.0)
                m_scr[d, j] = jnp.full(m_scr.shape[2:], m0_ref[n, d * H_C + hp * MLSTM_HEADS + j], F32)

    ones_col = jnp.where(lax.broadcasted_iota(jnp.int32, (CHUNK, LANES), 1) == 0, 1.0, 0.0).astype(BF16)
    dirs = ((qvf_ref, vvf_ref, ktf_ref, grf_ref, gcf_ref, hf_ref), (qvb_ref, vvb_ref, ktb_ref, grb_ref, gcb_ref, hb_ref))
    chains = []
    for d, (q_ref, v_ref, kt_ref, gr_ref, gc_ref, h_ref) in enumerate(dirs):
        for j in range(MLSTM_HEADS):
            sl = slice(j * DK_C, (j + 1) * DK_C)
            v_aug = jnp.concatenate([v_ref[0, :, sl], ones_col], axis=1)
            chains.append((d, j, h_ref, sl, _mlstm_chunk(
                q_ref[0, :, sl], kt_ref[0, sl, :], v_aug,
                gr_ref[0, j, 2 * d:2 * d + 1, :], gr_ref[0, j, 2 * d + 1:2 * d + 2, :],
                gc_ref[0, j, :, 2 * d + 1:2 * d + 2],
                c_scr[d, j], m_scr[d, j, 0:1, 0:1], reverse=(d == 1))))
    for d, j, h_ref, sl, (hout, c_new, m_new) in chains:
        h_ref[0, :, sl] = hout
        c_scr[d, j] = c_new
        m_scr[d, j] = jnp.broadcast_to(m_new, m_scr.shape[2:])

    @pl.when(c == pl.num_programs(2) - 1)
    def _():
        for d in range(2):
            for j in range(MLSTM_HEADS):
                cfin_ref[0, d, j] = c_scr[d, j, :, 0:DV_C]
                n_col = c_scr[d, j, :, DV_C:DV_C + 1]
                nfin_ref[0, d, j] = jnp.sum(jnp.where(eye, n_col, 0.0), axis=0, keepdims=True)
                mfin_ref[0, d, j] = m_scr[d, j, 0:1, :]


def _rope_tables(t):
    half = DK_C // 2
    nf = half // 2
    pos = jnp.arange(t)
    freqs = ROPE_BASE ** (-jnp.arange(nf, dtype=F32) / nf)
    cos, sin = [], []
    for p in (pos // GRID_W, pos % GRID_W):
        ang = p.astype(F32)[:, None] * freqs[None, :]
        cos += [jnp.cos(ang), jnp.cos(ang)]
        sin += [-jnp.sin(ang), jnp.sin(ang)]
    return jnp.concatenate(cos, axis=-1), jnp.concatenate(sin, axis=-1)


def _mlstm(qv, kt, gates, c0, n0, m0):
    b, t, _ = qv.shape
    nc = t // CHUNK
    hps = MLSTM_HEADS
    ngrp = H_C // hps
    g4 = gates.reshape(b, 4, H_C, t)
    g_row = jnp.transpose(g4, (0, 2, 1, 3))
    g_col = jnp.transpose(g4, (0, 2, 3, 1))
    fwd = lambda c: c
    bwd = lambda c: nc - 1 - c

    def tok(off, order):
        return pl.BlockSpec((1, CHUNK, hps * DK_C), lambda n, h, c: (n, order(c), off + h))

    def kt_spec(order):
        return pl.BlockSpec((1, hps * DK_C, CHUNK), lambda n, h, c: (n, h, order(c)))

    def g_row_spec(order):
        return pl.BlockSpec((1, hps, 4, CHUNK), lambda n, h, c: (n, h, 0, order(c)))

    def g_col_spec(order):
        return pl.BlockSpec((1, hps, CHUNK, 4), lambda n, h, c: (n, h, order(c), 0))

    state = lambda *tail: pl.BlockSpec((1, 2, hps) + tail, lambda n, h, c: (n, 0, h) + (0,) * len(tail))
    hspec = lambda order: pl.BlockSpec((1, CHUNK, hps * DV_C), lambda n, h, c: (n, order(c), h))
    hf, hb, cfin, nfin, mfin = pl.pallas_call(
        _mlstm_kernel,
        grid=(b, ngrp, nc),
        in_specs=[pl.BlockSpec(memory_space=pltpu.SMEM),
                  tok(0, fwd), tok(ngrp, fwd), kt_spec(fwd),
                  tok(0, bwd), tok(ngrp, bwd), kt_spec(bwd),
                  g_row_spec(fwd), g_col_spec(fwd), g_row_spec(bwd), g_col_spec(bwd),
                  state(DK_C, DV_C), state(1, DK_C)],
        out_specs=[hspec(fwd), hspec(bwd), state(DK_C, DV_C), state(1, DK_C), state(1, LANES)],
        out_shape=[jax.ShapeDtypeStruct((b, t, D_C), F32), jax.ShapeDtypeStruct((b, t, D_C), F32),
                   jax.ShapeDtypeStruct((b, 2, H_C, DK_C, DV_C), F32),
                   jax.ShapeDtypeStruct((b, 2, H_C, 1, DK_C), F32),
                   jax.ShapeDtypeStruct((b, 2, H_C, 1, LANES), F32)],
        scratch_shapes=[pltpu.VMEM((2, hps, DK_C, DV_C + LANES), F32),
                        pltpu.VMEM((2, hps, SUBLANES, LANES), F32)],
        compiler_params=_params("arbitrary", "arbitrary", "arbitrary"),
        name="mlstm_scan",
    )(m0.reshape(b, 2 * H_C), qv, qv, kt, qv, qv, kt, g_row, g_col, g_row, g_col,
      c0, n0.reshape(b, 2, H_C, 1, DK_C))
    return hf, hb, cfin, nfin.reshape(b, 2, H_C, DK_C), mfin[:, :, :, 0, 0]


def _outproj_c_kernel(x_ref, mod_ref, hf_ref, hb_ref, og_ref, g_ref, w_ref, o_ref, hn_scr):
    for h in range(H_C):
        sl = slice(h * DV_C, (h + 1) * DV_C)
        hs = hf_ref[0, :, sl] + hb_ref[0, :, sl]
        hn_scr[:, sl] = (_rms(hs, g_ref[:, sl]) * jax.nn.sigmoid(og_ref[0, :, sl])).astype(BF16)
    o_ref[0] = x_ref[0] + mod_ref[0, 5:6, :] * _dot(hn_scr[...], w_ref[...])


def _outproj_c(x, mod, hf, hb, og, mh_g, w):
    grp, t, d = x.shape
    tm = TOKEN_TILE // 2
    tok = pl.BlockSpec((1, tm, d), lambda b, i: (b, i, 0))
    return pl.pallas_call(
        _outproj_c_kernel,
        grid=(grp, t // tm),
        in_specs=[
            tok,
            pl.BlockSpec((1, N_MOD, d), lambda b, i: (b, 0, 0)),
            tok, tok,
            pl.BlockSpec((1, tm, D_C), lambda b, i: (b, i, 0)),
            pl.BlockSpec((1, D_C), lambda b, i: (0, 0)),
            pl.BlockSpec((D_C, d), lambda b, i: (0, 0)),
        ],
        out_specs=tok,
        out_shape=jax.ShapeDtypeStruct(x.shape, F32),
        scratch_shapes=[pltpu.VMEM((tm, D_C), BF16)],
        compiler_params=_params("arbitrary", "arbitrary"),
        name="mixer_c_out_proj",
    )(x, mod, hf, hb, og, mh_g.reshape(1, D_C), w)


def _mixer_ab(j, xp, xs, mp, ms, g, p):
    bp, lp, bs = p["bp"], p["lp"], xs.shape[0]
    w_in = p["w_in_ab"][j].astype(BF16)
    w_out = p["w_out_ab"][j].astype(BF16)
    gains = jnp.stack([jnp.tile(p["qn_g"][j] * (HD_A ** -0.5), H_A), jnp.tile(p["kn_g"][j], H_A),
                       jnp.ones((D_A,), F32)]).reshape(3, 1, D_A)
    wa, wx, ba, bx = p["lru_wa"][j], p["lru_wx"][j], p["lru_ba"][j], p["lru_bx"][j]
    w4 = jnp.concatenate([wa[0], wx[0], wa[1], wx[1]], axis=-1).astype(BF16)
    b4 = jnp.concatenate([v.reshape(H_B, 1, BD_B) for v in (ba[0], bx[0], ba[1], bx[1])], axis=-1)
    lru = (p["conv_w"][j], p["conv_b"][j], w4, b4, p["lru_lam"][j])

    qkv, xg = _inproj_ab(xp, mp, g, w_in, gains, F32)
    qkv = qkv.reshape(bp, lp, 3 * D_A)
    oa = _ctx_attention(qkv)
    ob, h_fin = _rglru(xg.reshape(bp, lp, 2 * D_RNN), *lru, jnp.zeros((bp, 2, D_RNN), F32))
    xp = _outproj_ab(xp, mp, oa.reshape(1, bp * lp, D_A), ob.reshape(1, bp * lp, D_RNN), w_out)
    new = (qkv[:, :, D_A:2 * D_A].reshape(bp, lp, H_A, HD_A), qkv[:, :, 2 * D_A:].reshape(bp, lp, H_A, HD_A), h_fin)

    past = p["cache_k"].shape[2]
    ck = p["cache_k"][:, j].reshape(bs, past, D_A).astype(BF16)
    cv = p["cache_v"][:, j].reshape(bs, past, D_A).astype(BF16)
    qkv, xg = _inproj_ab(xs, ms, g, w_in, gains, BF16)
    oa = _neighbourhood_attention(qkv, ck, cv, p["rpb"][j])
    ob, _ = _rglru(xg, *lru, p["state_lru"][:, j])
    xs = _outproj_ab(xs, ms, oa, ob, w_out)
    return xp, xs, new


def _mixer_c(j, xp, xs, mp, ms, g, p):
    bp, lp, bs, ts = p["bp"], p["lp"], xs.shape[0], xs.shape[1]
    w = p["w_in_c"][j]
    w_qvo = jnp.concatenate([w[:, :D_C], w[:, 2 * D_C:4 * D_C]], axis=1).astype(BF16)
    wkt = w[:, D_C:2 * D_C].T.astype(BF16)
    w_gate = jnp.pad(w[:, 4 * D_C:], ((0, 0), (0, LANES - 4 * H_C))).astype(BF16)
    w_out = p["w_out_c"][j].astype(BF16)

    def gates_of(zg, b, t):
        g_rows = jnp.transpose(zg.reshape(b, t, LANES)[:, :, :4 * H_C], (0, 2, 1))
        return _gate_prep(g_rows, p["b_gate_c"][j])

    qv, og, kt, zg = _inproj_c(xp, mp, g, w_qvo, wkt, w_gate, None)
    kt = jnp.transpose(kt.reshape(D_C, bp, lp), (1, 0, 2))
    hf, hb, cfin, nfin, mfin = _mlstm(
        qv.reshape(bp, lp, 2 * D_C), kt, gates_of(zg, bp, lp),
        jnp.zeros((bp, 2, H_C, DK_C, DV_C), F32), jnp.zeros((bp, 2, H_C, DK_C), F32), jnp.zeros((bp, 2, H_C), F32))
    xp = _outproj_c(xp, mp, hf.reshape(1, bp * lp, D_C), hb.reshape(1, bp * lp, D_C), og, p["mh_norm_g"][j], w_out)

    qv, og, kt, zg = _inproj_c(xs, ms, g, w_qvo, wkt, w_gate, _rope_tables(ts))
    hf, hb, _, _, _ = _mlstm(qv, kt, gates_of(zg, bs, ts), p["state_mlstm_c"][:, j], p["state_mlstm_n"][:, j],
                             p["state_mlstm_m"][:, j])
    xs = _outproj_c(xs, ms, hf, hb, og, p["mh_norm_g"][j], w_out)
    return xp, xs, (cfin, nfin, mfin)


def kernel(x_prompt, x_sample, cache_k, cache_v, state_lru, state_mlstm_c, state_mlstm_n, state_mlstm_m, c, c_ctx, w_mod, b_mod, norm_g, w_ffn_gate, w_ffn_up, w_ffn_down, w_in_ab, w_out_ab, qn_g, kn_g, rpb, conv_w, conv_b, lru_wa, lru_ba, lru_wx, lru_bx, lru_lam, w_in_c, b_gate_c, mh_norm_g, w_out_c):
    bp, lp, d = x_prompt.shape
    bs = x_sample.shape[0]
    p = dict(bp=bp, lp=lp, cache_k=cache_k, cache_v=cache_v, state_lru=state_lru, state_mlstm_c=state_mlstm_c,
             state_mlstm_n=state_mlstm_n, state_mlstm_m=state_mlstm_m, w_in_ab=w_in_ab, w_out_ab=w_out_ab,
             qn_g=qn_g, kn_g=kn_g, rpb=rpb, conv_w=conv_w, conv_b=conv_b, lru_wa=lru_wa, lru_ba=lru_ba,
             lru_wx=lru_wx, lru_bx=lru_bx, lru_lam=lru_lam, w_in_c=w_in_c, b_gate_c=b_gate_c,
             mh_norm_g=mh_norm_g, w_out_c=w_out_c)

    cond = jnp.concatenate([c, c_ctx[None, :]], axis=0)
    cond = jnp.pad(cond, ((0, -(bs + 1) % SUBLANES), (0, 0)))
    mod = _modulation(cond, w_mod, b_mod)[:, :bs + 1].reshape(DEPTH, bs + 1, N_MOD, d)

    wg = w_ffn_gate.astype(BF16)
    wu = w_ffn_up.astype(BF16)
    wd = w_ffn_down.astype(BF16)

    xp = x_prompt.reshape(1, bp * lp, d)
    xs = x_sample
    new_ab, new_c = [], []
    for l in range(DEPTH):
        j = l // 2
        ms = mod[l, :bs]
        mp = mod[l, bs:]
        xp = _ffn(xp, mp, norm_g[l, 0], wg, wu, wd, l, 0, 0)
        xs = _ffn(xs, ms, norm_g[l, 0], wg, wu, wd, l, 0, 0)
        if l % 2 == 0:
            xp, xs, new = _mixer_ab(j, xp, xs, mp, ms, norm_g[l, 1], p)
            new_ab.append(new)
        else:
            xp, xs, new = _mixer_c(j, xp, xs, mp, ms, norm_g[l, 1], p)
            new_c.append(new)
        xp = _ffn(xp, mp, norm_g[l, 2], wg, wu, wd, l, 1, 2)
        xs = _ffn(xs, ms, norm_g[l, 2], wg, wu, wd, l, 1, 2)
    stack = lambda items, i: jnp.stack([it[i] for it in items], axis=1)
    return (xp.reshape(bp, lp, d), xs, stack(new_ab, 0), stack(new_ab, 1), stack(new_ab, 2),
            stack(new_c, 0), stack(new_c, 1), stack(new_c, 2))
```

```python
import functools

import numpy as np
import jax
import jax.numpy as jnp
from jax import lax
from jax.experimental import pallas as pl
from jax.experimental.pallas import tpu as pltpu

D_MODEL = 2048
DEPTH = 2
GRID_W = 64
N_MOD = 9
D_FF = 5632
EPS = 1e-6
H_A = 8
HD_A = 128
D_A = H_A * HD_A
WIN_R = 8
WIN_C = 16
D_RNN = 1024
H_B = 8
BD_B = D_RNN // H_B
CONV_W = 4
LRU_C = 8.0
H_C = 8
DK_C = 256
DV_C = 256
D_C = H_C * DV_C
CHUNK = 128
ROPE_BASE = 10000.0

BF16 = jnp.bfloat16
F32 = jnp.float32

V7X_VMEM_BYTES = 64 * 1024 * 1024
VMEM_LIMIT = V7X_VMEM_BYTES - 8 * 1024 * 1024
LANES = 128
SUBLANES = 8

TOKEN_TILE = 512
FF_TILE = 512
FFN_NORM_ROWS = 64
PROJ_TOKEN_TILE = 1024
PROJ_TILE = 512
NA_ROWS = 4
NA_KEY_ROWS = NA_ROWS + WIN_R
LRU_SEGS = SUBLANES
MLSTM_HEADS = 2


def _params(*sem):
    return pltpu.CompilerParams(dimension_semantics=sem, vmem_limit_bytes=VMEM_LIMIT)


def _dot(a, b):
    return jnp.dot(a, b, preferred_element_type=F32)


def _dot_nt(a, b):
    return lax.dot_general(a, b, (((1,), (1,)), ((), ())), preferred_element_type=F32)


def _dot_tn(a, b):
    return lax.dot_general(a, b, (((0,), (0,)), ((), ())), preferred_element_type=F32)


def _rms(x, g):
    return x * lax.rsqrt(jnp.mean(x * x, axis=-1, keepdims=True) + EPS) * g


def _adaln(x, g, mod_ref, j):
    return _rms(x, g) * (1.0 + mod_ref[0, 3 * j + 1:3 * j + 2, :]) + mod_ref[0, 3 * j:3 * j + 1, :]


def _softplus(x):
    return jnp.maximum(x, 0.0) + jnp.log1p(jnp.exp(-jnp.abs(x)))


def _log_sigmoid(x):
    return -_softplus(-x)


def _sigmoid(x):
    return 0.5 * jnp.tanh(0.5 * x) + 0.5


def _mod_kernel(c_ref, w_ref, b_ref, o_ref):
    c = c_ref[...]
    s = (c * jax.nn.sigmoid(c)).astype(BF16)
    o_ref[0] = _dot(s, w_ref[0].astype(BF16)) + b_ref[0]


def _modulation(cond, w_mod, b_mod):
    r = cond.shape[0]
    n = N_MOD * D_MODEL
    tn = 1024
    return pl.pallas_call(
        _mod_kernel,
        grid=(DEPTH, n // tn),
        in_specs=[
            pl.BlockSpec((r, D_MODEL), lambda l, j: (0, 0)),
            pl.BlockSpec((1, D_MODEL, tn), lambda l, j: (l, 0, j)),
            pl.BlockSpec((1, 1, tn), lambda l, j: (l, 0, j)),
        ],
        out_specs=pl.BlockSpec((1, r, tn), lambda l, j: (l, 0, j)),
        out_shape=jax.ShapeDtypeStruct((DEPTH, r, n), F32),
        compiler_params=_params("arbitrary", "arbitrary"),
        name="modulation",
    )(cond, w_mod, b_mod.reshape(DEPTH, 1, n))


def _ffn_kernel(x0_ref, xn_ref, mod_ref, modn_ref, g_ref, wg_ref, wu_ref, wd_ref, o_ref,
                ha_scr, hb_scr, xa_scr, xb_scr, acc_scr, *, j, nf):
    p, f2 = pl.program_id(0), pl.program_id(1)
    rows = xn_ref.shape[1]
    tm = ha_scr.shape[0]

    @pl.when((p == 0) & (f2 == 0))
    def _():
        xa_scr[...] = x0_ref[0]
        ha_scr[...] = _adaln(x0_ref[0], g_ref[...], mod_ref, j).astype(BF16)

    @pl.when((f2 == 0) | (f2 == nf))
    def _():
        acc_scr[...] = jnp.zeros_like(acc_scr)

    def body(h_cur, h_nxt, x_nxt, f):
        r0 = pl.multiple_of(jnp.minimum(f, tm // rows - 1) * rows, rows)
        xs = xn_ref[0]
        x_nxt[pl.ds(r0, rows), :] = xs
        h_nxt[pl.ds(r0, rows), :] = _adaln(xs, g_ref[...], modn_ref, j).astype(BF16)
        h = h_cur[...]
        a = _dot(h, wg_ref[...])
        u = _dot(h, wu_ref[...])
        act = (a * jax.nn.sigmoid(a)) * u
        acc_scr[...] += _dot(act.astype(BF16), wd_ref[...])

    @pl.when(f2 < nf)
    def _():
        body(ha_scr, hb_scr, xb_scr, f2)

    @pl.when(f2 >= nf)
    def _():
        body(hb_scr, ha_scr, xa_scr, f2 - nf)

    for last, x_cur in ((nf - 1, xa_scr), (2 * nf - 1, xb_scr)):
        @pl.when(f2 == last)
        def _():
            o_ref[0] = x_cur[...] + 0.5 * mod_ref[0, 3 * j + 2:3 * j + 3, :] * acc_scr[...]


def _ffn(x, mod, g, wg, wu, wd, l, s, j):
    grp, t, d = x.shape
    tm, tf = TOKEN_TILE, FF_TILE
    nf = D_FF // tf
    per_group = t // tm
    ntile = grp * per_group
    rows = FFN_NORM_ROWS
    nsl = tm // rows
    assert ntile % 2 == 0 and nf >= nsl
    tile = lambda p, f2: 2 * p + f2 // nf
    nxt = lambda p, f2: jnp.minimum(tile(p, f2) + 1, ntile - 1)
    out = pl.pallas_call(
        functools.partial(_ffn_kernel, j=j, nf=nf),
        grid=(ntile // 2, 2 * nf),
        in_specs=[
            pl.BlockSpec((1, tm, d), lambda p, f2: (0, 0, 0)),
            pl.BlockSpec((1, rows, d), lambda p, f2: (0, nxt(p, f2) * nsl + jnp.minimum(f2 % nf, nsl - 1), 0)),
            pl.BlockSpec((1, N_MOD, d), lambda p, f2: (tile(p, f2) // per_group, 0, 0)),
            pl.BlockSpec((1, N_MOD, d), lambda p, f2: (nxt(p, f2) // per_group, 0, 0)),
            pl.BlockSpec((1, d), lambda p, f2: (0, 0)),
            pl.BlockSpec((None, None, d, tf), lambda p, f2: (l, s, 0, f2 % nf)),
            pl.BlockSpec((None, None, d, tf), lambda p, f2: (l, s, 0, f2 % nf)),
            pl.BlockSpec((None, None, tf, d), lambda p, f2: (l, s, f2 % nf, 0)),
        ],
        out_specs=pl.BlockSpec((1, tm, d), lambda p, f2: (0, tile(p, f2), 0)),
        out_shape=jax.ShapeDtypeStruct((1, grp * t, d), F32),
        scratch_shapes=[pltpu.VMEM((tm, d), BF16), pltpu.VMEM((tm, d), BF16),
                        pltpu.VMEM((tm, d), F32), pltpu.VMEM((tm, d), F32), pltpu.VMEM((tm, d), F32)],
        compiler_params=_params("arbitrary", "arbitrary"),
        name="macaron_ffn",
    )(x.reshape(1, grp * t, d), x.reshape(1, grp * t, d), mod, mod, g.reshape(1, d), wg, wu, wd)
    return out.reshape(x.shape)


def _rope(x, cos, sin):
    half = DK_C // 2
    out = []
    for p in range(2):
        sl = slice(p * half, (p + 1) * half)
        xs = x[:, sl]
        out.append(xs * cos[:, sl] + pltpu.roll(xs, half // 2, 1) * sin[:, sl])
    return jnp.concatenate(out, axis=-1)


def _rope_t(x, cos, sin):
    q = DK_C // 4
    swapped = jnp.concatenate([x[q:2 * q], x[0:q], x[3 * q:4 * q], x[2 * q:3 * q]], axis=0)
    return x * cos + swapped * sin


def _inproj_ab_kernel(x_ref, mod_ref, g_ref, w_ref, gain_ref, oa_ref, ob_ref, h_scr, *, n_a, n_norm):
    k = pl.program_id(2)

    @pl.when(k == 0)
    def _():
        h_scr[...] = _adaln(x_ref[0], g_ref[...], mod_ref, 1).astype(BF16)

    z = _dot(h_scr[...], w_ref[...])

    @pl.when(k >= n_a)
    def _():
        ob_ref[0] = z

    @pl.when(k < n_a)
    def _():
        for hh in range(w_ref.shape[1] // HD_A):
            sl = slice(hh * HD_A, (hh + 1) * HD_A)
            zs = z[:, sl]
            r = lax.rsqrt(jnp.mean(zs * zs, axis=-1, keepdims=True) + EPS)
            r = jnp.where(k < n_norm, r, 1.0)
            oa_ref[0, :, sl] = (zs * r * gain_ref[0, :, sl]).astype(oa_ref.dtype)


def _inproj_ab(x, mod, g, w, gains, dtype_a):
    grp, t, d = x.shape
    tm, tn = PROJ_TOKEN_TILE, PROJ_TILE
    n_a = 3 * D_A // tn
    nt = w.shape[1] // tn
    return pl.pallas_call(
        functools.partial(_inproj_ab_kernel, n_a=n_a, n_norm=2 * D_A // tn),
        grid=(grp, t // tm, nt),
        in_specs=[
            pl.BlockSpec((1, tm, d), lambda b, i, k: (b, i, 0)),
            pl.BlockSpec((1, N_MOD, d), lambda b, i, k: (b, 0, 0)),
            pl.BlockSpec((1, d), lambda b, i, k: (0, 0)),
            pl.BlockSpec((d, tn), lambda b, i, k: (0, k)),
            pl.BlockSpec((1, 1, tn), lambda b, i, k: (jnp.minimum(k, n_a - 1), 0, 0)),
        ],
        out_specs=[pl.BlockSpec((1, tm, tn), lambda b, i, k: (b, i, jnp.minimum(k, n_a - 1))),
                   pl.BlockSpec((1, tm, tn), lambda b, i, k: (b, i, jnp.maximum(k - n_a, 0)))],
        out_shape=[jax.ShapeDtypeStruct((grp, t, n_a * tn), dtype_a),
                   jax.ShapeDtypeStruct((grp, t, (nt - n_a) * tn), F32)],
        scratch_shapes=[pltpu.VMEM((tm, d), BF16)],
        compiler_params=_params("arbitrary", "arbitrary", "arbitrary"),
        name="mixer_ab_in_proj",
    )(x, mod, g.reshape(1, d), w, gains.reshape(n_a, 1, tn))


def _inproj_c_kernel(*refs, rotary, n_q, n_qv, n_main):
    x_ref, mod_ref, g_ref, w_ref, wkt_ref, wgate_ref = refs[:6]
    pos = 6
    if rotary:
        cos_ref, sin_ref, cost_ref, sint_ref = refs[pos:pos + 4]
        pos += 4
    oa_ref, ob_ref, okt_ref, og_ref, h_scr = refs[pos:pos + 5]
    k = pl.program_id(2)
    tn = w_ref.shape[1]

    @pl.when(k == 0)
    def _():
        h = _adaln(x_ref[0], g_ref[...], mod_ref, 1).astype(BF16)
        h_scr[...] = h
        og_ref[0] = _dot(h, wgate_ref[...])

    @pl.when(k < n_q)
    def _():
        z = _dot(h_scr[...], w_ref[...])
        if rotary:
            for hh in range(tn // DK_C):
                sl = slice(hh * DK_C, (hh + 1) * DK_C)
                oa_ref[0, :, sl] = _rope(z[:, sl], cos_ref[...], sin_ref[...]).astype(BF16)
        else:
            oa_ref[0] = z.astype(BF16)

    @pl.when((k >= n_q) & (k < n_qv))
    def _():
        oa_ref[0] = _dot(h_scr[...], w_ref[...]).astype(BF16)

    @pl.when((k >= n_qv) & (k < n_main))
    def _():
        ob_ref[0] = _dot(h_scr[...], w_ref[...])

    @pl.when(k >= n_main)
    def _():
        zt = _dot_nt(wkt_ref[...], h_scr[...])
        if rotary:
            for hh in range(tn // DK_C):
                sl = slice(hh * DK_C, (hh + 1) * DK_C)
                okt_ref[0, sl, :] = _rope_t(zt[sl, :], cost_ref[...], sint_ref[...]).astype(BF16)
        else:
            okt_ref[0] = zt.astype(BF16)


def _inproj_c(x, mod, g, w_qvo, wkt, w_gate, rope_tables):
    grp, t, d = x.shape
    tm, tn = PROJ_TOKEN_TILE, PROJ_TILE
    n_q, n_qv, n_main = D_C // tn, 2 * D_C // tn, 3 * D_C // tn
    nt = n_main + D_C // tn
    rotary = rope_tables is not None
    in_specs = [
        pl.BlockSpec((1, tm, d), lambda b, i, k: (b, i, 0)),
        pl.BlockSpec((1, N_MOD, d), lambda b, i, k: (b, 0, 0)),
        pl.BlockSpec((1, d), lambda b, i, k: (0, 0)),
        pl.BlockSpec((d, tn), lambda b, i, k: (0, jnp.minimum(k, n_main - 1))),
        pl.BlockSpec((tn, d), lambda b, i, k: (jnp.maximum(k - n_main, 0), 0)),
        pl.BlockSpec((d, LANES), lambda b, i, k: (0, 0)),
    ]
    args = [x, mod, g.reshape(1, d), w_qvo, wkt, w_gate]
    if rotary:
        cos, sin = rope_tables
        in_specs += [pl.BlockSpec((tm, DK_C), lambda b, i, k: (i, 0))] * 2
        in_specs += [pl.BlockSpec((DK_C, tm), lambda b, i, k: (0, i))] * 2
        args += [cos, sin, cos.T, sin.T]
    return pl.pallas_call(
        functools.partial(_inproj_c_kernel, rotary=rotary, n_q=n_q, n_qv=n_qv, n_main=n_main),
        grid=(grp, t // tm, nt),
        in_specs=in_specs,
        out_specs=[pl.BlockSpec((1, tm, tn), lambda b, i, k: (b, i, jnp.minimum(k, n_qv - 1))),
                   pl.BlockSpec((1, tm, tn), lambda b, i, k: (b, i, jnp.clip(k - n_qv, 0, n_main - n_qv - 1))),
                   pl.BlockSpec((1, tn, tm), lambda b, i, k: (b, jnp.maximum(k - n_main, 0), i)),
                   pl.BlockSpec((1, tm, LANES), lambda b, i, k: (b, i, 0))],
        out_shape=[jax.ShapeDtypeStruct((grp, t, 2 * D_C), BF16),
                   jax.ShapeDtypeStruct((grp, t, D_C), F32),
                   jax.ShapeDtypeStruct((grp, D_C, t), BF16),
                   jax.ShapeDtypeStruct((grp, t, LANES), F32)],
        scratch_shapes=[pltpu.VMEM((tm, d), BF16)],
        compiler_params=_params("arbitrary", "arbitrary", "arbitrary"),
        name="mixer_c_in_proj",
    )(*args)


def _ctx_attn_kernel(q_ref, k_ref, v_ref, o_ref):
    s = _dot_nt(q_ref[0].astype(BF16), k_ref[0].astype(BF16))
    p = jnp.exp(s - jnp.max(s, axis=-1, keepdims=True))
    o = _dot(p.astype(BF16), v_ref[0].astype(BF16)) / jnp.sum(p, axis=-1, keepdims=True)
    o_ref[0] = o.astype(BF16)


def _ctx_attention(qkv):
    b, l, _ = qkv.shape
    blk = lambda off: pl.BlockSpec((1, l, HD_A), lambda i, h: (i, 0, off + h))
    return pl.pallas_call(
        _ctx_attn_kernel,
        grid=(b, H_A),
        in_specs=[blk(0), blk(H_A), blk(2 * H_A)],
        out_specs=pl.BlockSpec((1, l, HD_A), lambda i, h: (i, 0, h)),
        out_shape=jax.ShapeDtypeStruct((b, l, D_A), BF16),
        compiler_params=_params("arbitrary", "arbitrary"),
        name="ctx_attention",
    )(qkv, qkv, qkv)


def _na_bias_tables(rpb, rows):
    nblk = rows // NA_ROWS
    n_dr, n_dc = 2 * WIN_R - 1, 2 * WIN_C - 1
    c = np.arange(GRID_W)[:, None]
    kc = np.arange(GRID_W)[None, :]
    dc = np.clip(kc - c + WIN_C - 1, 0, n_dc - 1)
    pick_c = (dc[None] == np.arange(n_dc)[:, None, None]).astype(np.float32)
    c0 = np.clip(c - WIN_C // 2, 0, GRID_W - WIN_C)
    ok_c = (kc >= c0) & (kc < c0 + WIN_C)
    by_col = jnp.einsum("hrd,dck->hrck", rpb, pick_c, precision=lax.Precision.HIGHEST)
    tables = []
    for i in (0, 1, nblk - 1):
        ks = int(np.clip(NA_ROWS * i - WIN_R // 2, 0, rows - NA_KEY_ROWS))
        r = (NA_ROWS * i + np.arange(NA_ROWS))[:, None]
        kr = (ks + np.arange(NA_KEY_ROWS))[None, :]
        r0 = np.clip(r - WIN_R // 2, 0, rows - WIN_R)
        ok_r = (kr >= r0) & (kr < r0 + WIN_R)
        dr = np.clip(kr - r + WIN_R - 1, 0, n_dr - 1)
        pick_r = (dr[None] == np.arange(n_dr)[:, None, None]).astype(np.float32)
        bias = jnp.einsum("hrck,rqs->hqcsk", by_col, pick_r, precision=lax.Precision.HIGHEST)
        ok = ok_r[None, :, None, :, None] & ok_c[None, None, :, None, :]
        bias = jnp.where(ok, bias, -jnp.inf)
        tables.append(bias.reshape(H_A, NA_ROWS * GRID_W, NA_KEY_ROWS * GRID_W))
    return jnp.stack(tables, axis=1)


def _na_kernel(q_ref, k_ref, v_ref, ck_ref, cv_ref, bias_ref, o_ref, *, rows):
    i = pl.program_id(2)
    nblk = rows // NA_ROWS
    q = q_ref[0]
    ks = jnp.clip(NA_ROWS * i - WIN_R // 2, 0, rows - NA_KEY_ROWS)
    start = pl.multiple_of(ks * GRID_W, GRID_W)
    nkey = NA_KEY_ROWS * GRID_W
    kw = k_ref[0, pl.ds(start, nkey), :]
    vw = v_ref[0, pl.ds(start, nkey), :]
    kind = jnp.where(i == 0, 0, jnp.where(i == nblk - 1, 2, 1))
    s_w = _dot_nt(q, kw) + bias_ref[0, kind]
    s_c = _dot_nt(q, ck_ref[0])
    m = jnp.maximum(jnp.max(s_w, axis=-1, keepdims=True), jnp.max(s_c, axis=-1, keepdims=True))
    p_w = jnp.exp(s_w - m)
    p_c = jnp.exp(s_c - m)
    den = jnp.sum(p_w, axis=-1, keepdims=True) + jnp.sum(p_c, axis=-1, keepdims=True)
    o = _dot(p_w.astype(BF16), vw) + _dot(p_c.astype(BF16), cv_ref[0])
    o_ref[0] = (o / den).astype(BF16)


def _neighbourhood_attention(qkv, ck, cv, rpb):
    b, t, _ = qkv.shape
    l = ck.shape[1]
    rows = t // GRID_W
    nblk = rows // NA_ROWS
    tq = NA_ROWS * GRID_W
    bias = _na_bias_tables(rpb, rows)
    full = lambda off: pl.BlockSpec((1, t, HD_A), lambda h, n, i: (n, 0, off + h))
    ctx = pl.BlockSpec((1, l, HD_A), lambda h, n, i: (n, 0, h))
    return pl.pallas_call(
        functools.partial(_na_kernel, rows=rows),
        grid=(H_A, b, nblk),
        in_specs=[
            pl.BlockSpec((1, tq, HD_A), lambda h, n, i: (n, i, h)),
            full(H_A), full(2 * H_A), ctx, ctx,
            pl.BlockSpec((1, 3, tq, NA_KEY_ROWS * GRID_W), lambda h, n, i: (h, 0, 0, 0)),
        ],
        out_specs=pl.BlockSpec((1, tq, HD_A), lambda h, n, i: (n, i, h)),
        out_shape=jax.ShapeDtypeStruct((b, t, D_A), BF16),
        compiler_params=_params("arbitrary", "arbitrary", "arbitrary"),
        name="neighbourhood_attention",
    )(qkv, qkv, qkv, ck, cv, bias)


def _lru_kernel(xb_ref, gb_ref, cw_ref, cb_ref, w4_ref, b4_ref, lam_ref, h0_ref, o_ref, fin_ref,
                xpad_scr, af_scr, uf_scr, ab_scr, ub_scr, *, t, chunk):
    seg = t // LRU_SEGS
    pitch = seg + SUBLANES
    pad = SUBLANES
    zeros = jnp.zeros((pad, BD_B), F32)
    xpad_scr[0:pad, :] = zeros
    xpad_scr[pad + t:2 * pad + t, :] = zeros
    xpad_scr[pad:pad + t, :] = xb_ref[0]

    sp = _softplus(-lam_ref[...])
    w4 = w4_ref[0]
    left = (CONV_W - 1) // 2
    for c in range(t // chunk):
        base = c * chunk
        xc = cb_ref[...]
        for k in range(CONV_W):
            xc = xc + xpad_scr[pl.ds(base + pad - left + k, chunk), :] * cw_ref[k:k + 1, :]
        ri = _sigmoid(_dot(xc.astype(BF16), w4) + b4_ref[0])
        for d, (a_scr, u_scr) in enumerate(((af_scr, uf_scr), (ab_scr, ub_scr))):
            r = ri[:, 2 * d * BD_B:(2 * d + 1) * BD_B]
            g = ri[:, (2 * d + 1) * BD_B:(2 * d + 2) * BD_B]
            log_a = -LRU_C * r * sp[d:d + 1, :]
            a = jnp.exp(log_a)
            u = jnp.sqrt(-jnp.tanh(log_a) * (a * a + 1.0)) * (g * xc)
            for p in range(chunk // seg):
                s = (base + p * seg) // seg
                a_scr[s * pitch:s * pitch + seg, :] = a[p * seg:(p + 1) * seg, :]
                u_scr[s * pitch:s * pitch + seg, :] = u[p * seg:(p + 1) * seg, :]

    def scan(j, carry):
        hf, pf, hb, pb = carry
        rf = pl.ds(j, LRU_SEGS, stride=pitch)
        rb = pl.ds(seg - 1 - j, LRU_SEGS, stride=pitch)
        a = af_scr[rf, :]
        hf = a * hf + uf_scr[rf, :]
        pf = a * pf
        uf_scr[rf, :] = hf
        af_scr[rf, :] = pf
        a = ab_scr[rb, :]
        hb = a * hb + ub_scr[rb, :]
        pb = a * pb
        ub_scr[rb, :] = hb
        ab_scr[rb, :] = pb
        return hf, pf, hb, pb

    zero = jnp.zeros((LRU_SEGS, BD_B), F32)
    one = jnp.ones((LRU_SEGS, BD_B), F32)
    hf, pf, hb, pb = lax.fori_loop(0, seg, scan, (zero, one, zero, one), unroll=4)

    cf = h0_ref[0, 0:1, :]
    cin_f = []
    for s in range(LRU_SEGS):
        cin_f.append(cf)
        cf = hf[s:s + 1, :] + pf[s:s + 1, :] * cf
    cb = h0_ref[0, 1:2, :]
    cin_b = [None] * LRU_SEGS
    for s in reversed(range(LRU_SEGS)):
        cin_b[s] = cb
        cb = hb[s:s + 1, :] + pb[s:s + 1, :] * cb
    fin_ref[0, 0:1, :] = cf
    fin_ref[0, 1:2, :] = cb

    for s in range(LRU_SEGS):
        rows = slice(s * pitch, s * pitch + seg)
        hs = (uf_scr[rows, :] + af_scr[rows, :] * cin_f[s]) + (ub_scr[rows, :] + ab_scr[rows, :] * cin_b[s])
        nat = slice(s * seg, (s + 1) * seg)
        o_ref[0, nat, :] = (hs * jax.nn.gelu(gb_ref[0, nat, :])).astype(BF16)


def _rglru(xg, conv_w, conv_b, w4, b4, lam, h0):
    b, t, _ = xg.shape
    chunk = min(t, 512)
    seg = t // LRU_SEGS
    col = lambda off: pl.BlockSpec((1, t, BD_B), lambda n, h: (n, 0, off + h))
    return pl.pallas_call(
        functools.partial(_lru_kernel, t=t, chunk=chunk),
        grid=(b, H_B),
        in_specs=[
            col(0), col(H_B),
            pl.BlockSpec((CONV_W, BD_B), lambda n, h: (0, h)),
            pl.BlockSpec((1, BD_B), lambda n, h: (0, h)),
            pl.BlockSpec((1, BD_B, 4 * BD_B), lambda n, h: (h, 0, 0)),
            pl.BlockSpec((1, 1, 4 * BD_B), lambda n, h: (h, 0, 0)),
            pl.BlockSpec((2, BD_B), lambda n, h: (0, h)),
            pl.BlockSpec((1, 2, BD_B), lambda n, h: (n, 0, h)),
        ],
        out_specs=[pl.BlockSpec((1, t, BD_B), lambda n, h: (n, 0, h)),
                   pl.BlockSpec((1, 2, BD_B), lambda n, h: (n, 0, h))],
        out_shape=[jax.ShapeDtypeStruct((b, t, D_RNN), BF16),
                   jax.ShapeDtypeStruct((b, 2, D_RNN), F32)],
        scratch_shapes=[pltpu.VMEM((t + 2 * SUBLANES, BD_B), F32)]
        + [pltpu.VMEM((LRU_SEGS * (seg + SUBLANES), BD_B), F32)] * 4,
        compiler_params=_params("arbitrary", "arbitrary"),
        name="rglru_branch",
    )(xg, xg, conv_w, conv_b.reshape(1, D_RNN), w4, b4, lam, h0)


def _outproj_ab_kernel(x_ref, mod_ref, oa_ref, ob_ref, w_ref, o_ref):
    y = _dot(oa_ref[0], w_ref[0:D_A, :]) + _dot(ob_ref[0], w_ref[D_A:D_A + D_RNN, :])
    o_ref[0] = x_ref[0] + mod_ref[0, 5:6, :] * y


def _outproj_ab(x, mod, oa, ob, w):
    grp, t, d = x.shape
    tm = TOKEN_TILE
    tok = lambda width: pl.BlockSpec((1, tm, width), lambda b, i: (b, i, 0))
    return pl.pallas_call(
        _outproj_ab_kernel,
        grid=(grp, t // tm),
        in_specs=[
            tok(d),
            pl.BlockSpec((1, N_MOD, d), lambda b, i: (b, 0, 0)),
            tok(D_A), tok(D_RNN),
            pl.BlockSpec((D_A + D_RNN, d), lambda b, i: (0, 0)),
        ],
        out_specs=tok(d),
        out_shape=jax.ShapeDtypeStruct(x.shape, F32),
        compiler_params=_params("arbitrary", "arbitrary"),
        name="mixer_ab_out_proj",
    )(x, mod, oa, ob, w)


def _gate_prep_kernel(g_ref, b_ref, o_ref):
    t = g_ref.shape[2]
    g = g_ref[0] + b_ref[...]
    lane = lax.broadcasted_iota(jnp.int32, (H_C, t), 1) & (CHUNK - 1)
    cum = _log_sigmoid(g[H_C:2 * H_C])
    suf = _log_sigmoid(g[3 * H_C:4 * H_C])
    sh = 1
    while sh < CHUNK:
        cum = cum + jnp.where(lane >= sh, pltpu.roll(cum, sh, 1), 0.0)
        suf = suf + jnp.where(lane < CHUNK - sh, pltpu.roll(suf, t - sh, 1), 0.0)
        sh *= 2
    o_ref[0, 0:H_C] = g[0:H_C]
    o_ref[0, H_C:2 * H_C] = cum
    o_ref[0, 2 * H_C:3 * H_C] = g[2 * H_C:3 * H_C]
    o_ref[0, 3 * H_C:4 * H_C] = suf


def _gate_prep(g_rows, b_gate):
    b, r, t = g_rows.shape
    return pl.pallas_call(
        _gate_prep_kernel,
        grid=(b,),
        in_specs=[pl.BlockSpec((1, r, t), lambda n: (n, 0, 0)), pl.BlockSpec((r, 1), lambda n: (0, 0))],
        out_specs=pl.BlockSpec((1, r, t), lambda n: (n, 0, 0)),
        out_shape=jax.ShapeDtypeStruct(g_rows.shape, F32),
        compiler_params=_params("arbitrary"),
        name="mlstm_gate_prep",
    )(g_rows, b_gate.reshape(r, 1))


def _mlstm_chunk(q, kt, v_aug, ig_row, bc_row, bc_col, cm, m, reverse):
    ti = lax.broadcasted_iota(jnp.int32, (CHUNK, CHUNK), 0)
    si = lax.broadcasted_iota(jnp.int32, (CHUNK, CHUNK), 1)
    seen = (si >= ti) if reverse else (si <= ti)
    bk = bc_row[:, 0:1] if reverse else bc_row[:, CHUNK - 1:CHUNK]
    dmat = jnp.where(seen, bc_col - bc_row + ig_row, -jnp.inf)
    inter = bc_col + m
    m_t = jnp.maximum(inter, jnp.max(dmat, axis=1, keepdims=True))
    w = jnp.exp(dmat - m_t)
    g = jnp.exp(inter - m_t)
    kscale = DK_C ** -0.5
    s = _dot(q, kt) * (w * kscale)
    qc = _dot(q, cm.astype(BF16))
    num = _dot(s.astype(BF16), v_aug[:, :DV_C]) + g * qc[:, :DV_C]
    den = jnp.sum(s, axis=1, keepdims=True) + g * qc[:, DV_C:DV_C + 1]
    hout = num / jnp.maximum(jnp.abs(den), jnp.exp(-m_t))
    dend = bk - bc_row + ig_row
    m_new = jnp.maximum(bk + m, jnp.max(dend, axis=1, keepdims=True))
    we = jnp.exp(dend - m_new)
    ge = jnp.exp(bk + m - m_new)
    kwt = kt.astype(F32) * (we * kscale)
    c_new = ge * cm + _dot(kwt.astype(BF16), v_aug)
    return hout, c_new, m_new


def _mlstm_kernel(m0_ref, qvf_ref, vvf_ref, ktf_ref, qvb_ref, vvb_ref, ktb_ref, grf_ref, gcf_ref, grb_ref, gcb_ref,
                  c0_ref, n0_ref, hf_ref, hb_ref, cfin_ref, nfin_ref, mfin_ref, c_scr, m_scr):
    n, hp, c = pl.program_id(0), pl.program_id(1), pl.program_id(2)
    eye = (lax.broadcasted_iota(jnp.int32, (DK_C, DK_C), 0) == lax.broadcasted_iota(jnp.int32, (DK_C, DK_C), 1))
    lane0 = lax.broadcasted_iota(jnp.int32, (DK_C, LANES), 1) == 0

    @pl.when(c == 0)
    def _():
        for d in range(2):
            for j in range(MLSTM_HEADS):
                c_scr[d, j, :, 0:DV_C] = c0_ref[0, d, j]
                n_col = jnp.sum(jnp.where(eye, n0_ref[0, d, j], 0.0), axis=1, keepdims=True)
                c_scr[d, j, :, DV_C:DV_C + LANES] = jnp.where(lane0, n_col, 0.0)
                m_scr[d, j] = jnp.full(m_scr.shape[2:], m0_ref[n, d * H_C + hp * MLSTM_HEADS + j], F32)

    ones_col = jnp.where(lax.broadcasted_iota(jnp.int32, (CHUNK, LANES), 1) == 0, 1.0, 0.0).astype(BF16)
    dirs = ((qvf_ref, vvf_ref, ktf_ref, grf_ref, gcf_ref, hf_ref), (qvb_ref, vvb_ref, ktb_ref, grb_ref, gcb_ref, hb_ref))
    chains = []
    for d, (q_ref, v_ref, kt_ref, gr_ref, gc_ref, h_ref) in enumerate(dirs):
        for j in range(MLSTM_HEADS):
            sl = slice(j * DK_C, (j + 1) * DK_C)
            v_aug = jnp.concatenate([v_ref[0, :, sl], ones_col], axis=1)
            chains.append((d, j, h_ref, sl, _mlstm_chunk(
                q_ref[0, :, sl], kt_ref[0, sl, :], v_aug,
                gr_ref[0, j, 2 * d:2 * d + 1, :], gr_ref[0, j, 2 * d + 1:2 * d + 2, :],
                gc_ref[0, j, :, 2 * d + 1:2 * d + 2],
                c_scr[d, j], m_scr[d, j, 0:1, 0:1], reverse=(d == 1))))
    for d, j, h_ref, sl, (hout, c_new, m_new) in chains:
        h_ref[0, :, sl] = hout
        c_scr[d, j] = c_new
        m_scr[d, j] = jnp.broadcast_to(m_new, m_scr.shape[2:])

    @pl.when(c == pl.num_programs(2) - 1)
    def _():
        for d in range(2):
            for j in range(MLSTM_HEADS):
                cfin_ref[0, d, j] = c_scr[d, j, :, 0:DV_C]
                n_col = c_scr[d, j, :, DV_C:DV_C + 1]
                nfin_ref[0, d, j] = jnp.sum(jnp.where(eye, n_col, 0.0), axis=0, keepdims=True)
                mfin_ref[0, d, j] = m_scr[d, j, 0:1, :]


def _rope_tables(t):
    half = DK_C // 2
    nf = half // 2
    pos = jnp.arange(t)
    freqs = ROPE_BASE ** (-jnp.arange(nf, dtype=F32) / nf)
    cos, sin = [], []
    for p in (pos // GRID_W, pos % GRID_W):
        ang = p.astype(F32)[:, None] * freqs[None, :]
        cos += [jnp.cos(ang), jnp.cos(ang)]
        sin += [-jnp.sin(ang), jnp.sin(ang)]
    return jnp.concatenate(cos, axis=-1), jnp.concatenate(sin, axis=-1)


def _mlstm(qv, kt, gates, c0, n0, m0):
    b, t, _ = qv.shape
    nc = t // CHUNK
    hps = MLSTM_HEADS
    ngrp = H_C // hps
    g4 = gates.reshape(b, 4, H_C, t)
    g_row = jnp.transpose(g4, (0, 2, 1, 3))
    g_col = jnp.transpose(g4, (0, 2, 3, 1))
    fwd = lambda c: c
    bwd = lambda c: nc - 1 - c

    def tok(off, order):
        return pl.BlockSpec((1, CHUNK, hps * DK_C), lambda n, h, c: (n, order(c), off + h))

    def kt_spec(order):
        return pl.BlockSpec((1, hps * DK_C, CHUNK), lambda n, h, c: (n, h, order(c)))

    def g_row_spec(order):
        return pl.BlockSpec((1, hps, 4, CHUNK), lambda n, h, c: (n, h, 0, order(c)))

    def g_col_spec(order):
        return pl.BlockSpec((1, hps, CHUNK, 4), lambda n, h, c: (n, h, order(c), 0))

    state = lambda *tail: pl.BlockSpec((1, 2, hps) + tail, lambda n, h, c: (n, 0, h) + (0,) * len(tail))
    hspec = lambda order: pl.BlockSpec((1, CHUNK, hps * DV_C), lambda n, h, c: (n, order(c), h))
    hf, hb, cfin, nfin, mfin = pl.pallas_call(
        _mlstm_kernel,
        grid=(b, ngrp, nc),
        in_specs=[pl.BlockSpec(memory_space=pltpu.SMEM),
                  tok(0, fwd), tok(ngrp, fwd), kt_spec(fwd),
                  tok(0, bwd), tok(ngrp, bwd), kt_spec(bwd),
                  g_row_spec(fwd), g_col_spec(fwd), g_row_spec(bwd), g_col_spec(bwd),
                  state(DK_C, DV_C), state(1, DK_C)],
        out_specs=[hspec(fwd), hspec(bwd), state(DK_C, DV_C), state(1, DK_C), state(1, LANES)],
        out_shape=[jax.ShapeDtypeStruct((b, t, D_C), F32), jax.ShapeDtypeStruct((b, t, D_C), F32),
                   jax.ShapeDtypeStruct((b, 2, H_C, DK_C, DV_C), F32),
                   jax.ShapeDtypeStruct((b, 2, H_C, 1, DK_C), F32),
                   jax.ShapeDtypeStruct((b, 2, H_C, 1, LANES), F32)],
        scratch_shapes=[pltpu.VMEM((2, hps, DK_C, DV_C + LANES), F32),
                        pltpu.VMEM((2, hps, SUBLANES, LANES), F32)],
        compiler_params=_params("arbitrary", "arbitrary", "arbitrary"),
        name="mlstm_scan",
    )(m0.reshape(b, 2 * H_C), qv, qv, kt, qv, qv, kt, g_row, g_col, g_row, g_col,
      c0, n0.reshape(b, 2, H_C, 1, DK_C))
    return hf, hb, cfin, nfin.reshape(b, 2, H_C, DK_C), mfin[:, :, :, 0, 0]


def _outproj_c_kernel(x_ref, mod_ref, hf_ref, hb_ref, og_ref, g_ref, w_ref, o_ref, hn_scr):
    for h in range(H_C):
        sl = slice(h * DV_C, (h + 1) * DV_C)
        hs = hf_ref[0, :, sl] + hb_ref[0, :, sl]
        hn_scr[:, sl] = (_rms(hs, g_ref[:, sl]) * jax.nn.sigmoid(og_ref[0, :, sl])).astype(BF16)
    o_ref[0] = x_ref[0] + mod_ref[0, 5:6, :] * _dot(hn_scr[...], w_ref[...])


def _outproj_c(x, mod, hf, hb, og, mh_g, w):
    grp, t, d = x.shape
    tm = TOKEN_TILE // 2
    tok = pl.BlockSpec((1, tm, d), lambda b, i: (b, i, 0))
    return pl.pallas_call(
        _outproj_c_kernel,
        grid=(grp, t // tm),
        in_specs=[
            tok,
            pl.BlockSpec((1, N_MOD, d), lambda b, i: (b, 0, 0)),
            tok, tok,
            pl.BlockSpec((1, tm, D_C), lambda b, i: (b, i, 0)),
            pl.BlockSpec((1, D_C), lambda b, i: (0, 0)),
            pl.BlockSpec((D_C, d), lambda b, i: (0, 0)),
        ],
        out_specs=tok,
        out_shape=jax.ShapeDtypeStruct(x.shape, F32),
        scratch_shapes=[pltpu.VMEM((tm, D_C), BF16)],
        compiler_params=_params("arbitrary", "arbitrary"),
        name="mixer_c_out_proj",
    )(x, mod, hf, hb, og, mh_g.reshape(1, D_C), w)


def _mixer_ab(j, xp, xs, mp, ms, g, p):
    bp, lp, bs = p["bp"], p["lp"], xs.shape[0]
    w_in = p["w_in_ab"][j].astype(BF16)
    w_out = p["w_out_ab"][j].astype(BF16)
    gains = jnp.stack([jnp.tile(p["qn_g"][j] * (HD_A ** -0.5), H_A), jnp.tile(p["kn_g"][j], H_A),
                       jnp.ones((D_A,), F32)]).reshape(3, 1, D_A)
    wa, wx, ba, bx = p["lru_wa"][j], p["lru_wx"][j], p["lru_ba"][j], p["lru_bx"][j]
    w4 = jnp.concatenate([wa[0], wx[0], wa[1], wx[1]], axis=-1).astype(BF16)
    b4 = jnp.concatenate([v.reshape(H_B, 1, BD_B) for v in (ba[0], bx[0], ba[1], bx[1])], axis=-1)
    lru = (p["conv_w"][j], p["conv_b"][j], w4, b4, p["lru_lam"][j])

    qkv, xg = _inproj_ab(xp, mp, g, w_in, gains, F32)
    qkv = qkv.reshape(bp, lp, 3 * D_A)
    oa = _ctx_attention(qkv)
    ob, h_fin = _rglru(xg.reshape(bp, lp, 2 * D_RNN), *lru, jnp.zeros((bp, 2, D_RNN), F32))
    xp = _outproj_ab(xp, mp, oa.reshape(1, bp * lp, D_A), ob.reshape(1, bp * lp, D_RNN), w_out)
    new = (qkv[:, :, D_A:2 * D_A].reshape(bp, lp, H_A, HD_A), qkv[:, :, 2 * D_A:].reshape(bp, lp, H_A, HD_A), h_fin)

    past = p["cache_k"].shape[2]
    ck = p["cache_k"][:, j].reshape(bs, past, D_A).astype(BF16)
    cv = p["cache_v"][:, j].reshape(bs, past, D_A).astype(BF16)
    qkv, xg = _inproj_ab(xs, ms, g, w_in, gains, BF16)
    oa = _neighbourhood_attention(qkv, ck, cv, p["rpb"][j])
    ob, _ = _rglru(xg, *lru, p["state_lru"][:, j])
    xs = _outproj_ab(xs, ms, oa, ob, w_out)
    return xp, xs, new


def _mixer_c(j, xp, xs, mp, ms, g, p):
    bp, lp, bs, ts = p["bp"], p["lp"], xs.shape[0], xs.shape[1]
    w = p["w_in_c"][j]
    w_qvo = jnp.concatenate([w[:, :D_C], w[:, 2 * D_C:4 * D_C]], axis=1).astype(BF16)
    wkt = w[:, D_C:2 * D_C].T.astype(BF16)
    w_gate = jnp.pad(w[:, 4 * D_C:], ((0, 0), (0, LANES - 4 * H_C))).astype(BF16)
    w_out = p["w_out_c"][j].astype(BF16)

    def gates_of(zg, b, t):
        g_rows = jnp.transpose(zg.reshape(b, t, LANES)[:, :, :4 * H_C], (0, 2, 1))
        return _gate_prep(g_rows, p["b_gate_c"][j])

    qv, og, kt, zg = _inproj_c(xp, mp, g, w_qvo, wkt, w_gate, None)
    kt = jnp.transpose(kt.reshape(D_C, bp, lp), (1, 0, 2))
    hf, hb, cfin, nfin, mfin = _mlstm(
        qv.reshape(bp, lp, 2 * D_C), kt, gates_of(zg, bp, lp),
        jnp.zeros((bp, 2, H_C, DK_C, DV_C), F32), jnp.zeros((bp, 2, H_C, DK_C), F32), jnp.zeros((bp, 2, H_C), F32))
    xp = _outproj_c(xp, mp, hf.reshape(1, bp * lp, D_C), hb.reshape(1, bp * lp, D_C), og, p["mh_norm_g"][j], w_out)

    qv, og, kt, zg = _inproj_c(xs, ms, g, w_qvo, wkt, w_gate, _rope_tables(ts))
    hf, hb, _, _, _ = _mlstm(qv, kt, gates_of(zg, bs, ts), p["state_mlstm_c"][:, j], p["state_mlstm_n"][:, j],
                             p["state_mlstm_m"][:, j])
    xs = _outproj_c(xs, ms, hf, hb, og, p["mh_norm_g"][j], w_out)
    return xp, xs, (cfin, nfin, mfin)


def kernel(x_prompt, x_sample, cache_k, cache_v, state_lru, state_mlstm_c, state_mlstm_n, state_mlstm_m, c, c_ctx, w_mod, b_mod, norm_g, w_ffn_gate, w_ffn_up, w_ffn_down, w_in_ab, w_out_ab, qn_g, kn_g, rpb, conv_w, conv_b, lru_wa, lru_ba, lru_wx, lru_bx, lru_lam, w_in_c, b_gate_c, mh_norm_g, w_out_c):
    bp, lp, d = x_prompt.shape
    bs = x_sample.shape[0]
    p = dict(bp=bp, lp=lp, cache_k=cache_k, cache_v=cache_v, state_lru=state_lru, state_mlstm_c=state_mlstm_c,
             state_mlstm_n=state_mlstm_n, state_mlstm_m=state_mlstm_m, w_in_ab=w_in_ab, w_out_ab=w_out_ab,
             qn_g=qn_g, kn_g=kn_g, rpb=rpb, conv_w=conv_w, conv_b=conv_b, lru_wa=lru_wa, lru_ba=lru_ba,
             lru_wx=lru_wx, lru_bx=lru_bx, lru_lam=lru_lam, w_in_c=w_in_c, b_gate_c=b_gate_c,
             mh_norm_g=mh_norm_g, w_out_c=w_out_c)

    cond = jnp.concatenate([c, c_ctx[None, :]], axis=0)
    cond = jnp.pad(cond, ((0, -(bs + 1) % SUBLANES), (0, 0)))
    mod = _modulation(cond, w_mod, b_mod)[:, :bs + 1].reshape(DEPTH, bs + 1, N_MOD, d)

    wg = w_ffn_gate.astype(BF16)
    wu = w_ffn_up.astype(BF16)
    wd = w_ffn_down.astype(BF16)

    xp = x_prompt.reshape(1, bp * lp, d)
    xs = x_sample
    new_ab, new_c = [], []
    for l in range(DEPTH):
        j = l // 2
        ms = mod[l, :bs]
        mp = mod[l, bs:]
        xp = _ffn(xp, mp, norm_g[l, 0], wg, wu, wd, l, 0, 0)
        xs = _ffn(xs, ms, norm_g[l, 0], wg, wu, wd, l, 0, 0)
        if l % 2 == 0:
            xp, xs, new = _mixer_ab(j, xp, xs, mp, ms, norm_g[l, 1], p)
            new_ab.append(new)
        else:
            xp, xs, new = _mixer_c(j, xp, xs, mp, ms, norm_g[l, 1], p)
            new_c.append(new)
        xp = _ffn(xp, mp, norm_g[l, 2], wg, wu, wd, l, 1, 2)
        xs = _ffn(xs, ms, norm_g[l, 2], wg, wu, wd, l, 1, 2)
    stack = lambda items, i: jnp.stack([it[i] for it in items], axis=1)
    return (xp.reshape(bp, lp, d), xs, stack(new_ab, 0), stack(new_ab, 1), stack(new_ab, 2),
            stack(new_c, 0), stack(new_c, 1), stack(new_c, 2))
```

```python
import functools

import numpy as np
import jax
import jax.numpy as jnp
from jax import lax
from jax.experimental import pallas as pl
from jax.experimental.pallas import tpu as pltpu

D_MODEL = 2048
DEPTH = 2
GRID_W = 64
N_MOD = 9
D_FF = 5632
EPS = 1e-6
H_A = 8
HD_A = 128
D_A = H_A * HD_A
WIN_R = 8
WIN_C = 16
D_RNN = 1024
H_B = 8
BD_B = D_RNN // H_B
CONV_W = 4
LRU_C = 8.0
H_C = 8
DK_C = 256
DV_C = 256
D_C = H_C * DV_C
CHUNK = 128
ROPE_BASE = 10000.0

BF16 = jnp.bfloat16
F32 = jnp.float32

V7X_VMEM_BYTES = 64 * 1024 * 1024
VMEM_LIMIT = V7X_VMEM_BYTES - 8 * 1024 * 1024
LANES = 128
SUBLANES = 8

TOKEN_TILE = 512
FF_TILE = 512
FFN_NORM_ROWS = 64
PROJ_TOKEN_TILE = 1024
PROJ_TILE = 512
NA_ROWS = 4
NA_KEY_ROWS = NA_ROWS + WIN_R
LRU_SEGS = SUBLANES
NA_HEADS = 2
MLSTM_HEADS = 2


def _params(*sem):
    return pltpu.CompilerParams(dimension_semantics=sem, vmem_limit_bytes=VMEM_LIMIT)


def _dot(a, b):
    return jnp.dot(a, b, preferred_element_type=F32)


def _dot_nt(a, b):
    return lax.dot_general(a, b, (((1,), (1,)), ((), ())), preferred_element_type=F32)


def _dot_tn(a, b):
    return lax.dot_general(a, b, (((0,), (0,)), ((), ())), preferred_element_type=F32)


def _rms(x, g):
    return x * lax.rsqrt(jnp.mean(x * x, axis=-1, keepdims=True) + EPS) * g


def _adaln(x, g, mod_ref, j):
    return _rms(x, g) * (1.0 + mod_ref[0, 3 * j + 1:3 * j + 2, :]) + mod_ref[0, 3 * j:3 * j + 1, :]


def _softplus(x):
    return jnp.maximum(x, 0.0) + jnp.log1p(jnp.exp(-jnp.abs(x)))


def _log_sigmoid(x):
    return -_softplus(-x)


def _mod_kernel(c_ref, w_ref, b_ref, o_ref):
    c = c_ref[...]
    s = (c * jax.nn.sigmoid(c)).astype(BF16)
    o_ref[0] = _dot(s, w_ref[0].astype(BF16)) + b_ref[0]


def _modulation(cond, w_mod, b_mod):
    r = cond.shape[0]
    n = N_MOD * D_MODEL
    tn = 1024
    return pl.pallas_call(
        _mod_kernel,
        grid=(DEPTH, n // tn),
        in_specs=[
            pl.BlockSpec((r, D_MODEL), lambda l, j: (0, 0)),
            pl.BlockSpec((1, D_MODEL, tn), lambda l, j: (l, 0, j)),
            pl.BlockSpec((1, 1, tn), lambda l, j: (l, 0, j)),
        ],
        out_specs=pl.BlockSpec((1, r, tn), lambda l, j: (l, 0, j)),
        out_shape=jax.ShapeDtypeStruct((DEPTH, r, n), F32),
        compiler_params=_params("arbitrary", "arbitrary"),
        name="modulation",
    )(cond, w_mod, b_mod.reshape(DEPTH, 1, n))


def _ffn_kernel(x0_ref, xn_ref, mod_ref, modn_ref, g_ref, wg_ref, wu_ref, wd_ref, o_ref,
                ha_scr, hb_scr, xa_scr, xb_scr, acc_scr, *, j, nf):
    p, f2 = pl.program_id(0), pl.program_id(1)
    rows = xn_ref.shape[1]
    tm = ha_scr.shape[0]

    @pl.when((p == 0) & (f2 == 0))
    def _():
        xa_scr[...] = x0_ref[0]
        ha_scr[...] = _adaln(x0_ref[0], g_ref[...], mod_ref, j).astype(BF16)

    @pl.when((f2 == 0) | (f2 == nf))
    def _():
        acc_scr[...] = jnp.zeros_like(acc_scr)

    def body(h_cur, h_nxt, x_nxt, f):
        r0 = pl.multiple_of(jnp.minimum(f, tm // rows - 1) * rows, rows)
        xs = xn_ref[0]
        x_nxt[pl.ds(r0, rows), :] = xs
        h_nxt[pl.ds(r0, rows), :] = _adaln(xs, g_ref[...], modn_ref, j).astype(BF16)
        h = h_cur[...]
        a = _dot(h, wg_ref[...])
        u = _dot(h, wu_ref[...])
        act = (a * jax.nn.sigmoid(a)) * u
        acc_scr[...] += _dot(act.astype(BF16), wd_ref[...])

    @pl.when(f2 < nf)
    def _():
        body(ha_scr, hb_scr, xb_scr, f2)

    @pl.when(f2 >= nf)
    def _():
        body(hb_scr, ha_scr, xa_scr, f2 - nf)

    for last, x_cur in ((nf - 1, xa_scr), (2 * nf - 1, xb_scr)):
        @pl.when(f2 == last)
        def _():
            o_ref[0] = x_cur[...] + 0.5 * mod_ref[0, 3 * j + 2:3 * j + 3, :] * acc_scr[...]


def _ffn(x, mod, g, wg, wu, wd, l, s, j):
    grp, t, d = x.shape
    tm, tf = TOKEN_TILE, FF_TILE
    nf = D_FF // tf
    per_group = t // tm
    ntile = grp * per_group
    rows = FFN_NORM_ROWS
    nsl = tm // rows
    assert ntile % 2 == 0 and nf >= nsl
    tile = lambda p, f2: 2 * p + f2 // nf
    nxt = lambda p, f2: jnp.minimum(tile(p, f2) + 1, ntile - 1)
    out = pl.pallas_call(
        functools.partial(_ffn_kernel, j=j, nf=nf),
        grid=(ntile // 2, 2 * nf),
        in_specs=[
            pl.BlockSpec((1, tm, d), lambda p, f2: (0, 0, 0)),
            pl.BlockSpec((1, rows, d), lambda p, f2: (0, nxt(p, f2) * nsl + jnp.minimum(f2 % nf, nsl - 1), 0)),
            pl.BlockSpec((1, N_MOD, d), lambda p, f2: (tile(p, f2) // per_group, 0, 0)),
            pl.BlockSpec((1, N_MOD, d), lambda p, f2: (nxt(p, f2) // per_group, 0, 0)),
            pl.BlockSpec((1, d), lambda p, f2: (0, 0)),
            pl.BlockSpec((None, None, d, tf), lambda p, f2: (l, s, 0, f2 % nf)),
            pl.BlockSpec((None, None, d, tf), lambda p, f2: (l, s, 0, f2 % nf)),
            pl.BlockSpec((None, None, tf, d), lambda p, f2: (l, s, f2 % nf, 0)),
        ],
        out_specs=pl.BlockSpec((1, tm, d), lambda p, f2: (0, tile(p, f2), 0)),
        out_shape=jax.ShapeDtypeStruct((1, grp * t, d), F32),
        scratch_shapes=[pltpu.VMEM((tm, d), BF16), pltpu.VMEM((tm, d), BF16),
                        pltpu.VMEM((tm, d), F32), pltpu.VMEM((tm, d), F32), pltpu.VMEM((tm, d), F32)],
        compiler_params=_params("arbitrary", "arbitrary"),
        name="macaron_ffn",
    )(x.reshape(1, grp * t, d), x.reshape(1, grp * t, d), mod, mod, g.reshape(1, d), wg, wu, wd)
    return out.reshape(x.shape)


def _rope(x, cos, sin):
    half = DK_C // 2
    out = []
    for p in range(2):
        sl = slice(p * half, (p + 1) * half)
        xs = x[:, sl]
        out.append(xs * cos[:, sl] + pltpu.roll(xs, half // 2, 1) * sin[:, sl])
    return jnp.concatenate(out, axis=-1)


def _rope_t(x, cos, sin):
    q = DK_C // 4
    swapped = jnp.concatenate([x[q:2 * q], x[0:q], x[3 * q:4 * q], x[2 * q:3 * q]], axis=0)
    return x * cos + swapped * sin


def _inproj_ab_kernel(x_ref, mod_ref, g_ref, w_ref, gain_ref, oa_ref, ob_ref, h_scr, *, n_a, n_norm):
    k = pl.program_id(2)

    @pl.when(k == 0)
    def _():
        h_scr[...] = _adaln(x_ref[0], g_ref[...], mod_ref, 1).astype(BF16)

    z = _dot(h_scr[...], w_ref[...])

    @pl.when(k >= n_a)
    def _():
        ob_ref[0] = z

    @pl.when(k < n_a)
    def _():
        for hh in range(w_ref.shape[1] // HD_A):
            sl = slice(hh * HD_A, (hh + 1) * HD_A)
            zs = z[:, sl]
            r = lax.rsqrt(jnp.mean(zs * zs, axis=-1, keepdims=True) + EPS)
            r = jnp.where(k < n_norm, r, 1.0)
            oa_ref[0, :, sl] = (zs * r * gain_ref[0, :, sl]).astype(oa_ref.dtype)


def _inproj_ab(x, mod, g, w, gains, dtype_a):
    grp, t, d = x.shape
    tm, tn = PROJ_TOKEN_TILE, PROJ_TILE
    n_a = 3 * D_A // tn
    nt = w.shape[1] // tn
    return pl.pallas_call(
        functools.partial(_inproj_ab_kernel, n_a=n_a, n_norm=2 * D_A // tn),
        grid=(grp, t // tm, nt),
        in_specs=[
            pl.BlockSpec((1, tm, d), lambda b, i, k: (b, i, 0)),
            pl.BlockSpec((1, N_MOD, d), lambda b, i, k: (b, 0, 0)),
            pl.BlockSpec((1, d), lambda b, i, k: (0, 0)),
            pl.BlockSpec((d, tn), lambda b, i, k: (0, k)),
            pl.BlockSpec((1, 1, tn), lambda b, i, k: (jnp.minimum(k, n_a - 1), 0, 0)),
        ],
        out_specs=[pl.BlockSpec((1, tm, tn), lambda b, i, k: (b, i, jnp.minimum(k, n_a - 1))),
                   pl.BlockSpec((1, tm, tn), lambda b, i, k: (b, i, jnp.maximum(k - n_a, 0)))],
        out_shape=[jax.ShapeDtypeStruct((grp, t, n_a * tn), dtype_a),
                   jax.ShapeDtypeStruct((grp, t, (nt - n_a) * tn), F32)],
        scratch_shapes=[pltpu.VMEM((tm, d), BF16)],
        compiler_params=_params("arbitrary", "arbitrary", "arbitrary"),
        name="mixer_ab_in_proj",
    )(x, mod, g.reshape(1, d), w, gains.reshape(n_a, 1, tn))


def _inproj_c_kernel(*refs, rotary, n_q, n_qv, n_main):
    x_ref, mod_ref, g_ref, w_ref, wkt_ref, wgate_ref = refs[:6]
    pos = 6
    if rotary:
        cos_ref, sin_ref, cost_ref, sint_ref = refs[pos:pos + 4]
        pos += 4
    oa_ref, ob_ref, okt_ref, og_ref, h_scr = refs[pos:pos + 5]
    k = pl.program_id(2)
    tn = w_ref.shape[1]

    @pl.when(k == 0)
    def _():
        h = _adaln(x_ref[0], g_ref[...], mod_ref, 1).astype(BF16)
        h_scr[...] = h
        og_ref[0] = _dot(h, wgate_ref[...])

    @pl.when(k < n_q)
    def _():
        z = _dot(h_scr[...], w_ref[...])
        if rotary:
            for hh in range(tn // DK_C):
                sl = slice(hh * DK_C, (hh + 1) * DK_C)
                oa_ref[0, :, sl] = _rope(z[:, sl], cos_ref[...], sin_ref[...]).astype(BF16)
        else:
            oa_ref[0] = z.astype(BF16)

    @pl.when((k >= n_q) & (k < n_qv))
    def _():
        oa_ref[0] = _dot(h_scr[...], w_ref[...]).astype(BF16)

    @pl.when((k >= n_qv) & (k < n_main))
    def _():
        ob_ref[0] = _dot(h_scr[...], w_ref[...])

    @pl.when(k >= n_main)
    def _():
        zt = _dot_nt(wkt_ref[...], h_scr[...])
        if rotary:
            for hh in range(tn // DK_C):
                sl = slice(hh * DK_C, (hh + 1) * DK_C)
                okt_ref[0, sl, :] = _rope_t(zt[sl, :], cost_ref[...], sint_ref[...]).astype(BF16)
        else:
            okt_ref[0] = zt.astype(BF16)


def _inproj_c(x, mod, g, w, wkt, w_gate, rope_tables):
    grp, t, d = x.shape
    tm, tn = PROJ_TOKEN_TILE, PROJ_TILE
    n_q, n_qv, n_main = D_C // tn, 2 * D_C // tn, 3 * D_C // tn
    nt = n_main + D_C // tn

    def w_col(k):
        kk = jnp.minimum(k, n_main - 1)
        return jnp.where(kk < n_q, kk, kk + n_q)

    rotary = rope_tables is not None
    in_specs = [
        pl.BlockSpec((1, tm, d), lambda b, i, k: (b, i, 0)),
        pl.BlockSpec((1, N_MOD, d), lambda b, i, k: (b, 0, 0)),
        pl.BlockSpec((1, d), lambda b, i, k: (0, 0)),
        pl.BlockSpec((d, tn), lambda b, i, k: (0, w_col(k))),
        pl.BlockSpec((tn, d), lambda b, i, k: (jnp.maximum(k - n_main, 0), 0)),
        pl.BlockSpec((d, LANES), lambda b, i, k: (0, 0)),
    ]
    args = [x, mod, g.reshape(1, d), w, wkt, w_gate]
    if rotary:
        cos, sin = rope_tables
        in_specs += [pl.BlockSpec((tm, DK_C), lambda b, i, k: (i, 0))] * 2
        in_specs += [pl.BlockSpec((DK_C, tm), lambda b, i, k: (0, i))] * 2
        args += [cos, sin, cos.T, sin.T]
    return pl.pallas_call(
        functools.partial(_inproj_c_kernel, rotary=rotary, n_q=n_q, n_qv=n_qv, n_main=n_main),
        grid=(grp, t // tm, nt),
        in_specs=in_specs,
        out_specs=[pl.BlockSpec((1, tm, tn), lambda b, i, k: (b, i, jnp.minimum(k, n_qv - 1))),
                   pl.BlockSpec((1, tm, tn), lambda b, i, k: (b, i, jnp.clip(k - n_qv, 0, n_main - n_qv - 1))),
                   pl.BlockSpec((1, tn, tm), lambda b, i, k: (b, jnp.maximum(k - n_main, 0), i)),
                   pl.BlockSpec((1, tm, LANES), lambda b, i, k: (b, i, 0))],
        out_shape=[jax.ShapeDtypeStruct((grp, t, 2 * D_C), BF16),
                   jax.ShapeDtypeStruct((grp, t, D_C), F32),
                   jax.ShapeDtypeStruct((grp, D_C, t), BF16),
                   jax.ShapeDtypeStruct((grp, t, LANES), F32)],
        scratch_shapes=[pltpu.VMEM((tm, d), BF16)],
        compiler_params=_params("arbitrary", "arbitrary", "arbitrary"),
        name="mixer_c_in_proj",
    )(*args)


def _ctx_attn_kernel(q_ref, k_ref, v_ref, o_ref):
    s = _dot_nt(q_ref[0].astype(BF16), k_ref[0].astype(BF16))
    p = jnp.exp(s - jnp.max(s, axis=-1, keepdims=True))
    o = _dot(p.astype(BF16), v_ref[0].astype(BF16)) / jnp.sum(p, axis=-1, keepdims=True)
    o_ref[0] = o.astype(BF16)


def _ctx_attention(qkv):
    b, l, _ = qkv.shape
    blk = lambda off: pl.BlockSpec((1, l, HD_A), lambda i, h: (i, 0, off + h))
    return pl.pallas_call(
        _ctx_attn_kernel,
        grid=(b, H_A),
        in_specs=[blk(0), blk(H_A), blk(2 * H_A)],
        out_specs=pl.BlockSpec((1, l, HD_A), lambda i, h: (i, 0, h)),
        out_shape=jax.ShapeDtypeStruct((b, l, D_A), BF16),
        compiler_params=_params("arbitrary", "arbitrary"),
        name="ctx_attention",
    )(qkv, qkv, qkv)


def _na_bias_tables(rpb, rows):
    nblk = rows // NA_ROWS
    n_dr, n_dc = 2 * WIN_R - 1, 2 * WIN_C - 1
    c = np.arange(GRID_W)[:, None]
    kc = np.arange(GRID_W)[None, :]
    dc = np.clip(kc - c + WIN_C - 1, 0, n_dc - 1)
    pick_c = (dc[None] == np.arange(n_dc)[:, None, None]).astype(np.float32)
    c0 = np.clip(c - WIN_C // 2, 0, GRID_W - WIN_C)
    ok_c = (kc >= c0) & (kc < c0 + WIN_C)
    by_col = jnp.einsum("hrd,dck->hrck", rpb, pick_c, precision=lax.Precision.HIGHEST)
    tables = []
    for i in (0, 1, nblk - 1):
        ks = int(np.clip(NA_ROWS * i - WIN_R // 2, 0, rows - NA_KEY_ROWS))
        r = (NA_ROWS * i + np.arange(NA_ROWS))[:, None]
        kr = (ks + np.arange(NA_KEY_ROWS))[None, :]
        r0 = np.clip(r - WIN_R // 2, 0, rows - WIN_R)
        ok_r = (kr >= r0) & (kr < r0 + WIN_R)
        dr = np.clip(kr - r + WIN_R - 1, 0, n_dr - 1)
        pick_r = (dr[None] == np.arange(n_dr)[:, None, None]).astype(np.float32)
        bias = jnp.einsum("hrck,rqs->hqcsk", by_col, pick_r, precision=lax.Precision.HIGHEST)
        ok = ok_r[None, :, None, :, None] & ok_c[None, None, :, None, :]
        bias = jnp.where(ok, bias, -jnp.inf)
        tables.append(bias.reshape(H_A, NA_ROWS * GRID_W, NA_KEY_ROWS * GRID_W))
    return jnp.stack(tables, axis=1)


def _na_kernel(q_ref, k_ref, v_ref, ck_ref, cv_ref, bias_ref, o_ref, *, rows):
    i = pl.program_id(2)
    nblk = rows // NA_ROWS
    ks = jnp.clip(NA_ROWS * i - WIN_R // 2, 0, rows - NA_KEY_ROWS)
    start = pl.multiple_of(ks * GRID_W, GRID_W)
    nkey = NA_KEY_ROWS * GRID_W
    kind = jnp.where(i == 0, 0, jnp.where(i == nblk - 1, 2, 1))
    ones_w = jnp.ones((nkey, HD_A), BF16)
    ones_c = jnp.ones((ck_ref.shape[1], HD_A), BF16)
    for hh in range(NA_HEADS):
        sl = slice(hh * HD_A, (hh + 1) * HD_A)
        q = q_ref[0, :, sl]
        kw = k_ref[0, pl.ds(start, nkey), sl]
        vw = v_ref[0, pl.ds(start, nkey), sl]
        s_w = _dot_nt(q, kw) + bias_ref[hh, kind]
        s_c = _dot_nt(q, ck_ref[0, :, sl])
        m = jnp.maximum(jnp.max(s_w, axis=-1, keepdims=True), jnp.max(s_c, axis=-1, keepdims=True))
        p_w = jnp.exp(s_w - m).astype(BF16)
        p_c = jnp.exp(s_c - m).astype(BF16)
        den = _dot(p_w, ones_w) + _dot(p_c, ones_c)
        o = _dot(p_w, vw) + _dot(p_c, cv_ref[0, :, sl])
        o_ref[0, :, sl] = (o / den).astype(BF16)


def _neighbourhood_attention(qkv, ck, cv, rpb):
    b, t, _ = qkv.shape
    l = ck.shape[1]
    rows = t // GRID_W
    nblk = rows // NA_ROWS
    tq = NA_ROWS * GRID_W
    bias = _na_bias_tables(rpb, rows)
    hw = NA_HEADS * HD_A
    ngrp = H_A // NA_HEADS
    full = lambda off: pl.BlockSpec((1, t, hw), lambda h, n, i: (n, 0, off + h))
    ctx = pl.BlockSpec((1, l, hw), lambda h, n, i: (n, 0, h))
    return pl.pallas_call(
        functools.partial(_na_kernel, rows=rows),
        grid=(ngrp, b, nblk),
        in_specs=[
            pl.BlockSpec((1, tq, hw), lambda h, n, i: (n, i, h)),
            full(ngrp), full(2 * ngrp), ctx, ctx,
            pl.BlockSpec((NA_HEADS, 3, tq, NA_KEY_ROWS * GRID_W), lambda h, n, i: (h, 0, 0, 0)),
        ],
        out_specs=pl.BlockSpec((1, tq, hw), lambda h, n, i: (n, i, h)),
        out_shape=jax.ShapeDtypeStruct((b, t, D_A), BF16),
        compiler_params=_params("arbitrary", "arbitrary", "arbitrary"),
        name="neighbourhood_attention",
    )(qkv, qkv, qkv, ck, cv, bias)


def _lru_kernel(xb_ref, gb_ref, cw_ref, cb_ref, w4h_ref, b4_ref, lam_ref, h0_ref, o_ref, fin_ref,
                xpad_scr, af_scr, uf_scr, ab_scr, ub_scr, *, t, chunk):
    seg = t // LRU_SEGS
    pitch = seg + SUBLANES
    pad = SUBLANES
    zeros = jnp.zeros((pad, BD_B), F32)
    xpad_scr[0:pad, :] = zeros
    xpad_scr[pad + t:2 * pad + t, :] = zeros
    xpad_scr[pad:pad + t, :] = xb_ref[0]

    c1 = (-0.5 * LRU_C) * _softplus(-lam_ref[...])
    w4h = w4h_ref[0]
    b4h = 0.5 * b4_ref[0]
    left = (CONV_W - 1) // 2
    for c in range(t // chunk):
        base = c * chunk
        xc = cb_ref[...]
        for k in range(CONV_W):
            xc = xc + xpad_scr[pl.ds(base + pad - left + k, chunk), :] * cw_ref[k:k + 1, :]
        th = jnp.tanh(_dot(xc.astype(BF16), w4h) + b4h)
        xh = 0.5 * xc
        for d, (a_scr, u_scr) in enumerate(((af_scr, uf_scr), (ab_scr, ub_scr))):
            tr = th[:, 2 * d * BD_B:(2 * d + 1) * BD_B]
            ti = th[:, (2 * d + 1) * BD_B:(2 * d + 2) * BD_B]
            log_a = c1[d:d + 1, :] * tr + c1[d:d + 1, :]
            a = jnp.exp(log_a)
            u = jnp.sqrt(-jnp.tanh(log_a) * (a * a + 1.0)) * (xh * ti + xh)
            for p in range(chunk // seg):
                s = (base + p * seg) // seg
                a_scr[s * pitch:s * pitch + seg, :] = a[p * seg:(p + 1) * seg, :]
                u_scr[s * pitch:s * pitch + seg, :] = u[p * seg:(p + 1) * seg, :]

    def scan(j, carry):
        hf, pf, hb, pb = carry
        rf = pl.ds(j, LRU_SEGS, stride=pitch)
        rb = pl.ds(seg - 1 - j, LRU_SEGS, stride=pitch)
        a = af_scr[rf, :]
        hf = a * hf + uf_scr[rf, :]
        pf = a * pf
        uf_scr[rf, :] = hf
        af_scr[rf, :] = pf
        a = ab_scr[rb, :]
        hb = a * hb + ub_scr[rb, :]
        pb = a * pb
        ub_scr[rb, :] = hb
        ab_scr[rb, :] = pb
        return hf, pf, hb, pb

    zero = jnp.zeros((LRU_SEGS, BD_B), F32)
    one = jnp.ones((LRU_SEGS, BD_B), F32)
    hf, pf, hb, pb = lax.fori_loop(0, seg, scan, (zero, one, zero, one), unroll=4)

    cf = h0_ref[0, 0:1, :]
    cin_f = []
    for s in range(LRU_SEGS):
        cin_f.append(cf)
        cf = hf[s:s + 1, :] + pf[s:s + 1, :] * cf
    cb = h0_ref[0, 1:2, :]
    cin_b = [None] * LRU_SEGS
    for s in reversed(range(LRU_SEGS)):
        cin_b[s] = cb
        cb = hb[s:s + 1, :] + pb[s:s + 1, :] * cb
    fin_ref[0, 0:1, :] = cf
    fin_ref[0, 1:2, :] = cb

    for s in range(LRU_SEGS):
        rows = slice(s * pitch, s * pitch + seg)
        hs = (uf_scr[rows, :] + af_scr[rows, :] * cin_f[s]) + (ub_scr[rows, :] + ab_scr[rows, :] * cin_b[s])
        nat = slice(s * seg, (s + 1) * seg)
        o_ref[0, nat, :] = (hs * jax.nn.gelu(gb_ref[0, nat, :])).astype(BF16)


def _rglru(xg, conv_w, conv_b, w4h, b4, lam, h0):
    b, t, _ = xg.shape
    chunk = min(t, 512)
    seg = t // LRU_SEGS
    col = lambda off: pl.BlockSpec((1, t, BD_B), lambda n, h: (n, 0, off + h))
    return pl.pallas_call(
        functools.partial(_lru_kernel, t=t, chunk=chunk),
        grid=(b, H_B),
        in_specs=[
            col(0), col(H_B),
            pl.BlockSpec((CONV_W, BD_B), lambda n, h: (0, h)),
            pl.BlockSpec((1, BD_B), lambda n, h: (0, h)),
            pl.BlockSpec((1, BD_B, 4 * BD_B), lambda n, h: (h, 0, 0)),
            pl.BlockSpec((1, 1, 4 * BD_B), lambda n, h: (h, 0, 0)),
            pl.BlockSpec((2, BD_B), lambda n, h: (0, h)),
            pl.BlockSpec((1, 2, BD_B), lambda n, h: (n, 0, h)),
        ],
        out_specs=[pl.BlockSpec((1, t, BD_B), lambda n, h: (n, 0, h)),
                   pl.BlockSpec((1, 2, BD_B), lambda n, h: (n, 0, h))],
        out_shape=[jax.ShapeDtypeStruct((b, t, D_RNN), BF16),
                   jax.ShapeDtypeStruct((b, 2, D_RNN), F32)],
        scratch_shapes=[pltpu.VMEM((t + 2 * SUBLANES, BD_B), F32)]
        + [pltpu.VMEM((LRU_SEGS * (seg + SUBLANES), BD_B), F32)] * 4,
        compiler_params=_params("arbitrary", "arbitrary"),
        name="rglru_branch",
    )(xg, xg, conv_w, conv_b.reshape(1, D_RNN), w4h, b4, lam, h0)


def _outproj_ab_kernel(x_ref, mod_ref, oa_ref, ob_ref, w_ref, o_ref):
    y = _dot(oa_ref[0], w_ref[0:D_A, :]) + _dot(ob_ref[0], w_ref[D_A:D_A + D_RNN, :])
    o_ref[0] = x_ref[0] + mod_ref[0, 5:6, :] * y


def _outproj_ab(x, mod, oa, ob, w):
    grp, t, d = x.shape
    tm = TOKEN_TILE
    tok = lambda width: pl.BlockSpec((1, tm, width), lambda b, i: (b, i, 0))
    return pl.pallas_call(
        _outproj_ab_kernel,
        grid=(grp, t // tm),
        in_specs=[
            tok(d),
            pl.BlockSpec((1, N_MOD, d), lambda b, i: (b, 0, 0)),
            tok(D_A), tok(D_RNN),
            pl.BlockSpec((D_A + D_RNN, d), lambda b, i: (0, 0)),
        ],
        out_specs=tok(d),
        out_shape=jax.ShapeDtypeStruct(x.shape, F32),
        compiler_params=_params("arbitrary", "arbitrary"),
        name="mixer_ab_out_proj",
    )(x, mod, oa, ob, w)


def _gate_prep_kernel(g_ref, b_ref, o_ref):
    t = g_ref.shape[2]
    g = g_ref[0] + b_ref[...]
    lane = lax.broadcasted_iota(jnp.int32, (H_C, t), 1) & (CHUNK - 1)
    cum = _log_sigmoid(g[H_C:2 * H_C])
    suf = _log_sigmoid(g[3 * H_C:4 * H_C])
    sh = 1
    while sh < CHUNK:
        cum = cum + jnp.where(lane >= sh, pltpu.roll(cum, sh, 1), 0.0)
        suf = suf + jnp.where(lane < CHUNK - sh, pltpu.roll(suf, t - sh, 1), 0.0)
        sh *= 2
    o_ref[0, 0:H_C] = g[0:H_C]
    o_ref[0, H_C:2 * H_C] = cum
    o_ref[0, 2 * H_C:3 * H_C] = g[2 * H_C:3 * H_C]
    o_ref[0, 3 * H_C:4 * H_C] = suf


def _gate_prep(g_rows, b_gate):
    b, r, t = g_rows.shape
    return pl.pallas_call(
        _gate_prep_kernel,
        grid=(b,),
        in_specs=[pl.BlockSpec((1, r, t), lambda n: (n, 0, 0)), pl.BlockSpec((r, 1), lambda n: (0, 0))],
        out_specs=pl.BlockSpec((1, r, t), lambda n: (n, 0, 0)),
        out_shape=jax.ShapeDtypeStruct(g_rows.shape, F32),
        compiler_params=_params("arbitrary"),
        name="mlstm_gate_prep",
    )(g_rows, b_gate.reshape(r, 1))


def _mlstm_chunk(q, kt, v_aug, ig_row, bc_row, bc_col, cm, m, reverse):
    ti = lax.broadcasted_iota(jnp.int32, (CHUNK, CHUNK), 0)
    si = lax.broadcasted_iota(jnp.int32, (CHUNK, CHUNK), 1)
    seen = (si >= ti) if reverse else (si <= ti)
    bk = bc_row[:, 0:1] if reverse else bc_row[:, CHUNK - 1:CHUNK]
    bc_rep = jnp.broadcast_to(bc_col, (CHUNK, CHUNK))
    dmat = jnp.where(seen, bc_rep - bc_row + ig_row, -jnp.inf)
    inter = bc_rep + m
    m_t = jnp.maximum(inter, jnp.broadcast_to(jnp.max(dmat, axis=1, keepdims=True), (CHUNK, CHUNK)))
    w = jnp.exp(dmat - m_t)
    g = jnp.exp(inter - m_t)
    kscale = DK_C ** -0.5
    s = _dot(q, kt) * (w * kscale)
    tot = _dot(s.astype(BF16), v_aug) + jnp.concatenate([g, g, g], axis=1) * _dot(q, cm.astype(BF16))
    inv = 1.0 / jnp.maximum(jnp.abs(tot[:, DV_C:]), jnp.exp(-m_t))
    hout = tot[:, :DV_C] * jnp.concatenate([inv, inv], axis=1)
    dend = bk - bc_row + ig_row
    m_new = jnp.maximum(bk + m, jnp.max(dend, axis=1, keepdims=True))
    we = jnp.exp(dend - m_new)
    ge = jnp.exp(bk + m - m_new)
    kwt = kt.astype(F32) * (we * kscale)
    c_new = ge * cm + _dot(kwt.astype(BF16), v_aug)
    return hout, c_new, m_new


def _mlstm_kernel(m0_ref, qvf_ref, vvf_ref, ktf_ref, qvb_ref, vvb_ref, ktb_ref, grf_ref, gcf_ref, grb_ref, gcb_ref,
                  c0_ref, n0_ref, hf_ref, hb_ref, cfin_ref, nfin_ref, mfin_ref, c_scr, m_scr):
    n, hp, c = pl.program_id(0), pl.program_id(1), pl.program_id(2)
    eye = (lax.broadcasted_iota(jnp.int32, (DK_C, DK_C), 0) == lax.broadcasted_iota(jnp.int32, (DK_C, DK_C), 1))

    @pl.when(c == 0)
    def _():
        for d in range(2):
            for j in range(MLSTM_HEADS):
                c_scr[d, j, :, 0:DV_C] = c0_ref[0, d, j]
                n_col = jnp.sum(jnp.where(eye, n0_ref[0, d, j], 0.0), axis=1, keepdims=True)
                c_scr[d, j, :, DV_C:DV_C + LANES] = jnp.broadcast_to(n_col, (DK_C, LANES))
                m_scr[d, j] = jnp.full(m_scr.shape[2:], m0_ref[n, d * H_C + hp * MLSTM_HEADS + j], F32)

    ones_col = jnp.ones((CHUNK, LANES), BF16)
    dirs = ((qvf_ref, vvf_ref, ktf_ref, grf_ref, gcf_ref, hf_ref), (qvb_ref, vvb_ref, ktb_ref, grb_ref, gcb_ref, hb_ref))
    chains = []
    for d, (q_ref, v_ref, kt_ref, gr_ref, gc_ref, h_ref) in enumerate(dirs):
        for j in range(MLSTM_HEADS):
            sl = slice(j * DK_C, (j + 1) * DK_C)
            v_aug = jnp.concatenate([v_ref[0, :, sl], ones_col], axis=1)
            chains.append((d, j, h_ref, sl, _mlstm_chunk(
                q_ref[0, :, sl], kt_ref[0, sl, :], v_aug,
                gr_ref[0, j, 2 * d:2 * d + 1, :], gr_ref[0, j, 2 * d + 1:2 * d + 2, :],
                gc_ref[0, j, :, 2 * d + 1:2 * d + 2],
                c_scr[d, j], m_scr[d, j, 0:1, 0:1], reverse=(d == 1))))
    for d, j, h_ref, sl, (hout, c_new, m_new) in chains:
        h_ref[0, :, sl] = hout
        c_scr[d, j] = c_new
        m_scr[d, j] = jnp.broadcast_to(m_new, m_scr.shape[2:])

    @pl.when(c == pl.num_programs(2) - 1)
    def _():
        for d in range(2):
            for j in range(MLSTM_HEADS):
                cfin_ref[0, d, j] = c_scr[d, j, :, 0:DV_C]
                n_col = c_scr[d, j, :, DV_C:DV_C + 1]
                nfin_ref[0, d, j] = jnp.sum(jnp.where(eye, n_col, 0.0), axis=0, keepdims=True)
                mfin_ref[0, d, j] = m_scr[d, j, 0:1, :]


def _rope_tables(t):
    half = DK_C // 2
    nf = half // 2
    pos = jnp.arange(t)
    freqs = ROPE_BASE ** (-jnp.arange(nf, dtype=F32) / nf)
    cos, sin = [], []
    for p in (pos // GRID_W, pos % GRID_W):
        ang = p.astype(F32)[:, None] * freqs[None, :]
        cos += [jnp.cos(ang), jnp.cos(ang)]
        sin += [-jnp.sin(ang), jnp.sin(ang)]
    return jnp.concatenate(cos, axis=-1), jnp.concatenate(sin, axis=-1)


def _mlstm(qv, kt, gates, c0, n0, m0):
    b, t, _ = qv.shape
    nc = t // CHUNK
    hps = MLSTM_HEADS
    ngrp = H_C // hps
    g4 = gates.reshape(b, 4, H_C, t)
    g_row = jnp.transpose(g4, (0, 2, 1, 3))
    g_col = jnp.transpose(g4, (0, 2, 3, 1))
    fwd = lambda c: c
    bwd = lambda c: nc - 1 - c

    def tok(off, order):
        return pl.BlockSpec((1, CHUNK, hps * DK_C), lambda n, h, c: (n, order(c), off + h))

    def kt_spec(order):
        return pl.BlockSpec((1, hps * DK_C, CHUNK), lambda n, h, c: (n, h, order(c)))

    def g_row_spec(order):
        return pl.BlockSpec((1, hps, 4, CHUNK), lambda n, h, c: (n, h, 0, order(c)))

    def g_col_spec(order):
        return pl.BlockSpec((1, hps, CHUNK, 4), lambda n, h, c: (n, h, order(c), 0))

    state = lambda *tail: pl.BlockSpec((1, 2, hps) + tail, lambda n, h, c: (n, 0, h) + (0,) * len(tail))
    hspec = lambda order: pl.BlockSpec((1, CHUNK, hps * DV_C), lambda n, h, c: (n, order(c), h))
    hf, hb, cfin, nfin, mfin = pl.pallas_call(
        _mlstm_kernel,
        grid=(b, ngrp, nc),
        in_specs=[pl.BlockSpec(memory_space=pltpu.SMEM),
                  tok(0, fwd), tok(ngrp, fwd), kt_spec(fwd),
                  tok(0, bwd), tok(ngrp, bwd), kt_spec(bwd),
                  g_row_spec(fwd), g_col_spec(fwd), g_row_spec(bwd), g_col_spec(bwd),
                  state(DK_C, DV_C), state(1, DK_C)],
        out_specs=[hspec(fwd), hspec(bwd), state(DK_C, DV_C), state(1, DK_C), state(1, LANES)],
        out_shape=[jax.ShapeDtypeStruct((b, t, D_C), F32), jax.ShapeDtypeStruct((b, t, D_C), F32),
                   jax.ShapeDtypeStruct((b, 2, H_C, DK_C, DV_C), F32),
                   jax.ShapeDtypeStruct((b, 2, H_C, 1, DK_C), F32),
                   jax.ShapeDtypeStruct((b, 2, H_C, 1, LANES), F32)],
        scratch_shapes=[pltpu.VMEM((2, hps, DK_C, DV_C + LANES), F32),
                        pltpu.VMEM((2, hps, SUBLANES, LANES), F32)],
        compiler_params=_params("arbitrary", "arbitrary", "arbitrary"),
        name="mlstm_scan",
    )(m0.reshape(b, 2 * H_C), qv, qv, kt, qv, qv, kt, g_row, g_col, g_row, g_col,
      c0, n0.reshape(b, 2, H_C, 1, DK_C))
    return hf, hb, cfin, nfin.reshape(b, 2, H_C, DK_C), mfin[:, :, :, 0, 0]


def _outproj_c_kernel(x_ref, mod_ref, hf_ref, hb_ref, og_ref, g_ref, w_ref, o_ref, hn_scr):
    for h in range(H_C):
        sl = slice(h * DV_C, (h + 1) * DV_C)
        hs = hf_ref[0, :, sl] + hb_ref[0, :, sl]
        hn_scr[:, sl] = (_rms(hs, g_ref[:, sl]) * jax.nn.sigmoid(og_ref[0, :, sl])).astype(BF16)
    o_ref[0] = x_ref[0] + mod_ref[0, 5:6, :] * _dot(hn_scr[...], w_ref[...])


def _outproj_c(x, mod, hf, hb, og, mh_g, w):
    grp, t, d = x.shape
    tm = TOKEN_TILE // 2
    tok = pl.BlockSpec((1, tm, d), lambda b, i: (b, i, 0))
    return pl.pallas_call(
        _outproj_c_kernel,
        grid=(grp, t // tm),
        in_specs=[
            tok,
            pl.BlockSpec((1, N_MOD, d), lambda b, i: (b, 0, 0)),
            tok, tok,
            pl.BlockSpec((1, tm, D_C), lambda b, i: (b, i, 0)),
            pl.BlockSpec((1, D_C), lambda b, i: (0, 0)),
            pl.BlockSpec((D_C, d), lambda b, i: (0, 0)),
        ],
        out_specs=tok,
        out_shape=jax.ShapeDtypeStruct(x.shape, F32),
        scratch_shapes=[pltpu.VMEM((tm, D_C), BF16)],
        compiler_params=_params("arbitrary", "arbitrary"),
        name="mixer_c_out_proj",
    )(x, mod, hf, hb, og, mh_g.reshape(1, D_C), w)


def _mixer_ab(j, xp, xs, mp, ms, g, p):
    bp, lp, bs = p["bp"], p["lp"], xs.shape[0]
    w_in = p["w_in_ab"][j].astype(BF16)
    w_out = p["w_out_ab"][j].astype(BF16)
    gains = jnp.stack([jnp.tile(p["qn_g"][j] * (HD_A ** -0.5), H_A), jnp.tile(p["kn_g"][j], H_A),
                       jnp.ones((D_A,), F32)]).reshape(3, 1, D_A)
    wa, wx, ba, bx = p["lru_wa"][j], p["lru_wx"][j], p["lru_ba"][j], p["lru_bx"][j]
    w4 = (0.5 * jnp.concatenate([wa[0], wx[0], wa[1], wx[1]], axis=-1)).astype(BF16)
    b4 = jnp.concatenate([v.reshape(H_B, 1, BD_B) for v in (ba[0], bx[0], ba[1], bx[1])], axis=-1)
    lru = (p["conv_w"][j], p["conv_b"][j], w4, b4, p["lru_lam"][j])

    qkv, xg = _inproj_ab(xp, mp, g, w_in, gains, F32)
    qkv = qkv.reshape(bp, lp, 3 * D_A)
    oa = _ctx_attention(qkv)
    ob, h_fin = _rglru(xg.reshape(bp, lp, 2 * D_RNN), *lru, jnp.zeros((bp, 2, D_RNN), F32))
    xp = _outproj_ab(xp, mp, oa.reshape(1, bp * lp, D_A), ob.reshape(1, bp * lp, D_RNN), w_out)
    new = (qkv[:, :, D_A:2 * D_A].reshape(bp, lp, H_A, HD_A), qkv[:, :, 2 * D_A:].reshape(bp, lp, H_A, HD_A), h_fin)

    past = p["cache_k"].shape[2]
    ck = p["cache_k"][:, j].reshape(bs, past, D_A).astype(BF16)
    cv = p["cache_v"][:, j].reshape(bs, past, D_A).astype(BF16)
    qkv, xg = _inproj_ab(xs, ms, g, w_in, gains, BF16)
    oa = _neighbourhood_attention(qkv, ck, cv, p["rpb"][j])
    ob, _ = _rglru(xg, *lru, p["state_lru"][:, j])
    xs = _outproj_ab(xs, ms, oa, ob, w_out)
    return xp, xs, new


def _mixer_c(j, xp, xs, mp, ms, g, p):
    bp, lp, bs, ts = p["bp"], p["lp"], xs.shape[0], xs.shape[1]
    w = p["w_in_c"][j].astype(BF16)
    wkt = w[:, D_C:2 * D_C].T
    w_gate = jnp.pad(w[:, 4 * D_C:], ((0, 0), (0, LANES - 4 * H_C)))
    w_out = p["w_out_c"][j].astype(BF16)

    def gates_of(zg, b, t):
        g_rows = jnp.transpose(zg.reshape(b, t, LANES)[:, :, :4 * H_C], (0, 2, 1))
        return _gate_prep(g_rows, p["b_gate_c"][j])

    qv, og, kt, zg = _inproj_c(xp, mp, g, w, wkt, w_gate, None)
    kt = jnp.transpose(kt.reshape(D_C, bp, lp), (1, 0, 2))
    hf, hb, cfin, nfin, mfin = _mlstm(
        qv.reshape(bp, lp, 2 * D_C), kt, gates_of(zg, bp, lp),
        jnp.zeros((bp, 2, H_C, DK_C, DV_C), F32), jnp.zeros((bp, 2, H_C, DK_C), F32), jnp.zeros((bp, 2, H_C), F32))
    xp = _outproj_c(xp, mp, hf.reshape(1, bp * lp, D_C), hb.reshape(1, bp * lp, D_C), og, p["mh_norm_g"][j], w_out)

    qv, og, kt, zg = _inproj_c(xs, ms, g, w, wkt, w_gate, _rope_tables(ts))
    hf, hb, _, _, _ = _mlstm(qv, kt, gates_of(zg, bs, ts), p["state_mlstm_c"][:, j], p["state_mlstm_n"][:, j],
                             p["state_mlstm_m"][:, j])
    xs = _outproj_c(xs, ms, hf, hb, og, p["mh_norm_g"][j], w_out)
    return xp, xs, (cfin, nfin, mfin)


def kernel(x_prompt, x_sample, cache_k, cache_v, state_lru, state_mlstm_c, state_mlstm_n, state_mlstm_m, c, c_ctx, w_mod, b_mod, norm_g, w_ffn_gate, w_ffn_up, w_ffn_down, w_in_ab, w_out_ab, qn_g, kn_g, rpb, conv_w, conv_b, lru_wa, lru_ba, lru_wx, lru_bx, lru_lam, w_in_c, b_gate_c, mh_norm_g, w_out_c):
    bp, lp, d = x_prompt.shape
    bs = x_sample.shape[0]
    p = dict(bp=bp, lp=lp, cache_k=cache_k, cache_v=cache_v, state_lru=state_lru, state_mlstm_c=state_mlstm_c,
             state_mlstm_n=state_mlstm_n, state_mlstm_m=state_mlstm_m, w_in_ab=w_in_ab, w_out_ab=w_out_ab,
             qn_g=qn_g, kn_g=kn_g, rpb=rpb, conv_w=conv_w, conv_b=conv_b, lru_wa=lru_wa, lru_ba=lru_ba,
             lru_wx=lru_wx, lru_bx=lru_bx, lru_lam=lru_lam, w_in_c=w_in_c, b_gate_c=b_gate_c,
             mh_norm_g=mh_norm_g, w_out_c=w_out_c)

    cond = jnp.concatenate([c, c_ctx[None, :]], axis=0)
    cond = jnp.pad(cond, ((0, -(bs + 1) % SUBLANES), (0, 0)))
    mod = _modulation(cond, w_mod, b_mod)[:, :bs + 1].reshape(DEPTH, bs + 1, N_MOD, d)

    wg = w_ffn_gate.astype(BF16)
    wu = w_ffn_up.astype(BF16)
    wd = w_ffn_down.astype(BF16)

    xp = x_prompt.reshape(1, bp * lp, d)
    xs = x_sample
    new_ab, new_c = [], []
    for l in range(DEPTH):
        j = l // 2
        ms = mod[l, :bs]
        mp = mod[l, bs:]
        xp = _ffn(xp, mp, norm_g[l, 0], wg, wu, wd, l, 0, 0)
        xs = _ffn(xs, ms, norm_g[l, 0], wg, wu, wd, l, 0, 0)
        if l % 2 == 0:
            xp, xs, new = _mixer_ab(j, xp, xs, mp, ms, norm_g[l, 1], p)
            new_ab.append(new)
        else:
            xp, xs, new = _mixer_c(j, xp, xs, mp, ms, norm_g[l, 1], p)
            new_c.append(new)
        xp = _ffn(xp, mp, norm_g[l, 2], wg, wu, wd, l, 1, 2)
        xs = _ffn(xs, ms, norm_g[l, 2], wg, wu, wd, l, 1, 2)
    stack = lambda items, i: jnp.stack([it[i] for it in items], axis=1)
    return (xp.reshape(bp, lp, d), xs, stack(new_ab, 0), stack(new_ab, 1), stack(new_ab, 2),
            stack(new_c, 0), stack(new_c, 1), stack(new_c, 2))
```

```python
import functools

import numpy as np
import jax
import jax.numpy as jnp
from jax import lax
from jax.experimental import pallas as pl
from jax.experimental.pallas import tpu as pltpu

D_MODEL = 2048
DEPTH = 2
GRID_W = 64
N_MOD = 9
D_FF = 5632
EPS = 1e-6
H_A = 8
HD_A = 128
D_A = H_A * HD_A
WIN_R = 8
WIN_C = 16
D_RNN = 1024
H_B = 8
BD_B = D_RNN // H_B
CONV_W = 4
LRU_C = 8.0
H_C = 8
DK_C = 256
DV_C = 256
D_C = H_C * DV_C
CHUNK = 128
ROPE_BASE = 10000.0

BF16 = jnp.bfloat16
F32 = jnp.float32

V7X_VMEM_BYTES = 64 * 1024 * 1024
VMEM_LIMIT = V7X_VMEM_BYTES - 8 * 1024 * 1024
LANES = 128
SUBLANES = 8

TOKEN_TILE = 512
FF_TILE = 512
FFN_NORM_ROWS = 64
PROJ_TOKEN_TILE = 1024
PROJ_TILE = 512
NA_ROWS = 4
NA_KEY_ROWS = NA_ROWS + WIN_R
LRU_SEGS = SUBLANES
NA_HEADS = 4
MLSTM_HEADS = 4


def _params(*sem):
    return pltpu.CompilerParams(dimension_semantics=sem, vmem_limit_bytes=VMEM_LIMIT)


def _dot(a, b):
    return jnp.dot(a, b, preferred_element_type=F32)


def _dot_nt(a, b):
    return lax.dot_general(a, b, (((1,), (1,)), ((), ())), preferred_element_type=F32)


def _dot_tn(a, b):
    return lax.dot_general(a, b, (((0,), (0,)), ((), ())), preferred_element_type=F32)


def _rms(x, g):
    return x * lax.rsqrt(jnp.mean(x * x, axis=-1, keepdims=True) + EPS) * g


def _adaln(x, g, mod_ref, j):
    return _rms(x, g) * (1.0 + mod_ref[0, 3 * j + 1:3 * j + 2, :]) + mod_ref[0, 3 * j:3 * j + 1, :]


def _softplus(x):
    return jnp.maximum(x, 0.0) + jnp.log1p(jnp.exp(-jnp.abs(x)))


def _log_sigmoid(x):
    return -_softplus(-x)


def _mod_kernel(c_ref, w_ref, b_ref, o_ref):
    c = c_ref[...]
    s = (c * jax.nn.sigmoid(c)).astype(BF16)
    o_ref[0] = _dot(s, w_ref[0].astype(BF16)) + b_ref[0]


def _modulation(cond, w_mod, b_mod):
    r = cond.shape[0]
    n = N_MOD * D_MODEL
    tn = 1024
    return pl.pallas_call(
        _mod_kernel,
        grid=(DEPTH, n // tn),
        in_specs=[
            pl.BlockSpec((r, D_MODEL), lambda l, j: (0, 0)),
            pl.BlockSpec((1, D_MODEL, tn), lambda l, j: (l, 0, j)),
            pl.BlockSpec((1, 1, tn), lambda l, j: (l, 0, j)),
        ],
        out_specs=pl.BlockSpec((1, r, tn), lambda l, j: (l, 0, j)),
        out_shape=jax.ShapeDtypeStruct((DEPTH, r, n), F32),
        compiler_params=_params("arbitrary", "arbitrary"),
        name="modulation",
    )(cond, w_mod, b_mod.reshape(DEPTH, 1, n))


def _ffn_kernel(x0_ref, xn_ref, mod_ref, modn_ref, g_ref, wg_ref, wu_ref, wd_ref, o_ref,
                ha_scr, hb_scr, xa_scr, xb_scr, acc_scr, *, j, nf):
    p, f2 = pl.program_id(0), pl.program_id(1)
    rows = xn_ref.shape[1]
    tm = ha_scr.shape[0]

    @pl.when((p == 0) & (f2 == 0))
    def _():
        xa_scr[...] = x0_ref[0]
        ha_scr[...] = _adaln(x0_ref[0], g_ref[...], mod_ref, j).astype(BF16)

    @pl.when((f2 == 0) | (f2 == nf))
    def _():
        acc_scr[...] = jnp.zeros_like(acc_scr)

    def body(h_cur, h_nxt, x_nxt, f):
        r0 = pl.multiple_of(jnp.minimum(f, tm // rows - 1) * rows, rows)
        xs = xn_ref[0]
        x_nxt[pl.ds(r0, rows), :] = xs
        h_nxt[pl.ds(r0, rows), :] = _adaln(xs, g_ref[...], modn_ref, j).astype(BF16)
        h = h_cur[...]
        a = _dot(h, wg_ref[...])
        u = _dot(h, wu_ref[...])
        act = (a * jax.nn.sigmoid(a)) * u
        acc_scr[...] += _dot(act.astype(BF16), wd_ref[...])

    @pl.when(f2 < nf)
    def _():
        body(ha_scr, hb_scr, xb_scr, f2)

    @pl.when(f2 >= nf)
    def _():
        body(hb_scr, ha_scr, xa_scr, f2 - nf)

    for last, x_cur in ((nf - 1, xa_scr), (2 * nf - 1, xb_scr)):
        @pl.when(f2 == last)
        def _():
            o_ref[0] = x_cur[...] + 0.5 * mod_ref[0, 3 * j + 2:3 * j + 3, :] * acc_scr[...]


def _ffn(x, mod, g, wg, wu, wd, l, s, j):
    grp, t, d = x.shape
    tm, tf = TOKEN_TILE, FF_TILE
    nf = D_FF // tf
    per_group = t // tm
    ntile = grp * per_group
    rows = FFN_NORM_ROWS
    nsl = tm // rows
    assert ntile % 2 == 0 and nf >= nsl
    tile = lambda p, f2: 2 * p + f2 // nf
    nxt = lambda p, f2: jnp.minimum(tile(p, f2) + 1, ntile - 1)
    out = pl.pallas_call(
        functools.partial(_ffn_kernel, j=j, nf=nf),
        grid=(ntile // 2, 2 * nf),
        in_specs=[
            pl.BlockSpec((1, tm, d), lambda p, f2: (0, 0, 0)),
            pl.BlockSpec((1, rows, d), lambda p, f2: (0, nxt(p, f2) * nsl + jnp.minimum(f2 % nf, nsl - 1), 0)),
            pl.BlockSpec((1, N_MOD, d), lambda p, f2: (tile(p, f2) // per_group, 0, 0)),
            pl.BlockSpec((1, N_MOD, d), lambda p, f2: (nxt(p, f2) // per_group, 0, 0)),
            pl.BlockSpec((1, d), lambda p, f2: (0, 0)),
            pl.BlockSpec((None, None, d, tf), lambda p, f2: (l, s, 0, f2 % nf)),
            pl.BlockSpec((None, None, d, tf), lambda p, f2: (l, s, 0, f2 % nf)),
            pl.BlockSpec((None, None, tf, d), lambda p, f2: (l, s, f2 % nf, 0)),
        ],
        out_specs=pl.BlockSpec((1, tm, d), lambda p, f2: (0, tile(p, f2), 0)),
        out_shape=jax.ShapeDtypeStruct((1, grp * t, d), F32),
        scratch_shapes=[pltpu.VMEM((tm, d), BF16), pltpu.VMEM((tm, d), BF16),
                        pltpu.VMEM((tm, d), F32), pltpu.VMEM((tm, d), F32), pltpu.VMEM((tm, d), F32)],
        compiler_params=_params("arbitrary", "arbitrary"),
        name="macaron_ffn",
    )(x.reshape(1, grp * t, d), x.reshape(1, grp * t, d), mod, mod, g.reshape(1, d), wg, wu, wd)
    return out.reshape(x.shape)


def _rope(x, cos, sin):
    half = DK_C // 2
    out = []
    for p in range(2):
        sl = slice(p * half, (p + 1) * half)
        xs = x[:, sl]
        out.append(xs * cos[:, sl] + pltpu.roll(xs, half // 2, 1) * sin[:, sl])
    return jnp.concatenate(out, axis=-1)


def _rope_t(x, cos, sin):
    q = DK_C // 4
    swapped = jnp.concatenate([x[q:2 * q], x[0:q], x[3 * q:4 * q], x[2 * q:3 * q]], axis=0)
    return x * cos + swapped * sin


def _inproj_ab_kernel(x_ref, mod_ref, g_ref, w_ref, gain_ref, oa_ref, ob_ref, h_scr, *, n_a, n_norm):
    k = pl.program_id(2)

    @pl.when(k == 0)
    def _():
        h_scr[...] = _adaln(x_ref[0], g_ref[...], mod_ref, 1).astype(BF16)

    @pl.when(k >= n_a)
    def _():
        ob_ref[0] = _dot(h_scr[...], w_ref[...])

    @pl.when(k < n_norm)
    def _():
        for pair in range(w_ref.shape[1] // (2 * HD_A)):
            z2 = _dot(h_scr[...], w_ref[:, 2 * pair * HD_A:2 * (pair + 1) * HD_A])
            for hh in range(2):
                zs = z2[:, hh * HD_A:(hh + 1) * HD_A]
                sl = slice((2 * pair + hh) * HD_A, (2 * pair + hh + 1) * HD_A)
                r = lax.rsqrt(jnp.mean(zs * zs, axis=-1, keepdims=True) + EPS)
                oa_ref[0, :, sl] = (zs * r * gain_ref[0, :, sl]).astype(oa_ref.dtype)

    @pl.when((k >= n_norm) & (k < n_a))
    def _():
        oa_ref[0] = (_dot(h_scr[...], w_ref[...]) * gain_ref[0]).astype(oa_ref.dtype)


def _inproj_ab(x, mod, g, w, gains, dtype_a):
    grp, t, d = x.shape
    tm, tn = PROJ_TOKEN_TILE, PROJ_TILE
    n_a = 3 * D_A // tn
    nt = w.shape[1] // tn
    return pl.pallas_call(
        functools.partial(_inproj_ab_kernel, n_a=n_a, n_norm=2 * D_A // tn),
        grid=(grp, t // tm, nt),
        in_specs=[
            pl.BlockSpec((1, tm, d), lambda b, i, k: (b, i, 0)),
            pl.BlockSpec((1, N_MOD, d), lambda b, i, k: (b, 0, 0)),
            pl.BlockSpec((1, d), lambda b, i, k: (0, 0)),
            pl.BlockSpec((d, tn), lambda b, i, k: (0, k)),
            pl.BlockSpec((1, 1, tn), lambda b, i, k: (jnp.minimum(k, n_a - 1), 0, 0)),
        ],
        out_specs=[pl.BlockSpec((1, tm, tn), lambda b, i, k: (b, i, jnp.minimum(k, n_a - 1))),
                   pl.BlockSpec((1, tm, tn), lambda b, i, k: (b, i, jnp.maximum(k - n_a, 0)))],
        out_shape=[jax.ShapeDtypeStruct((grp, t, n_a * tn), dtype_a),
                   jax.ShapeDtypeStruct((grp, t, (nt - n_a) * tn), F32)],
        scratch_shapes=[pltpu.VMEM((tm, d), BF16)],
        compiler_params=_params("arbitrary", "arbitrary", "arbitrary"),
        name="mixer_ab_in_proj",
    )(x, mod, g.reshape(1, d), w, gains.reshape(n_a, 1, tn))


def _inproj_c_kernel(*refs, rotary, n_q, n_qv, n_main):
    x_ref, mod_ref, g_ref, w_ref, wkt_ref, wgate_ref = refs[:6]
    pos = 6
    if rotary:
        cos_ref, sin_ref, cost_ref, sint_ref = refs[pos:pos + 4]
        pos += 4
    oa_ref, ob_ref, okt_ref, og_ref, h_scr = refs[pos:pos + 5]
    k = pl.program_id(2)
    tn = w_ref.shape[1]

    @pl.when(k == 0)
    def _():
        h = _adaln(x_ref[0], g_ref[...], mod_ref, 1).astype(BF16)
        h_scr[...] = h
        og_ref[0] = _dot(h, wgate_ref[...])

    @pl.when(k < n_q)
    def _():
        z = _dot(h_scr[...], w_ref[...])
        if rotary:
            for hh in range(tn // DK_C):
                sl = slice(hh * DK_C, (hh + 1) * DK_C)
                oa_ref[0, :, sl] = _rope(z[:, sl], cos_ref[...], sin_ref[...]).astype(BF16)
        else:
            oa_ref[0] = z.astype(BF16)

    @pl.when((k >= n_q) & (k < n_qv))
    def _():
        oa_ref[0] = _dot(h_scr[...], w_ref[...]).astype(BF16)

    @pl.when((k >= n_qv) & (k < n_main))
    def _():
        ob_ref[0] = _dot(h_scr[...], w_ref[...])

    @pl.when(k >= n_main)
    def _():
        zt = _dot_nt(wkt_ref[...], h_scr[...])
        if rotary:
            for hh in range(tn // DK_C):
                sl = slice(hh * DK_C, (hh + 1) * DK_C)
                okt_ref[0, sl, :] = _rope_t(zt[sl, :], cost_ref[...], sint_ref[...]).astype(BF16)
        else:
            okt_ref[0] = zt.astype(BF16)


def _inproj_c(x, mod, g, w, wkt, w_gate, rope_tables):
    grp, t, d = x.shape
    tm, tn = PROJ_TOKEN_TILE, PROJ_TILE
    n_q, n_qv, n_main = D_C // tn, 2 * D_C // tn, 3 * D_C // tn
    nt = n_main + D_C // tn

    def w_col(k):
        kk = jnp.minimum(k, n_main - 1)
        return jnp.where(kk < n_q, kk, kk + n_q)

    rotary = rope_tables is not None
    in_specs = [
        pl.BlockSpec((1, tm, d), lambda b, i, k: (b, i, 0)),
        pl.BlockSpec((1, N_MOD, d), lambda b, i, k: (b, 0, 0)),
        pl.BlockSpec((1, d), lambda b, i, k: (0, 0)),
        pl.BlockSpec((d, tn), lambda b, i, k: (0, w_col(k))),
        pl.BlockSpec((tn, d), lambda b, i, k: (jnp.maximum(k - n_main, 0), 0)),
        pl.BlockSpec((d, LANES), lambda b, i, k: (0, 0)),
    ]
    args = [x, mod, g.reshape(1, d), w, wkt, w_gate]
    if rotary:
        cos, sin = rope_tables
        in_specs += [pl.BlockSpec((tm, DK_C), lambda b, i, k: (i, 0))] * 2
        in_specs += [pl.BlockSpec((DK_C, tm), lambda b, i, k: (0, i))] * 2
        args += [cos, sin, cos.T, sin.T]
    return pl.pallas_call(
        functools.partial(_inproj_c_kernel, rotary=rotary, n_q=n_q, n_qv=n_qv, n_main=n_main),
        grid=(grp, t // tm, nt),
        in_specs=in_specs,
        out_specs=[pl.BlockSpec((1, tm, tn), lambda b, i, k: (b, i, jnp.minimum(k, n_qv - 1))),
                   pl.BlockSpec((1, tm, tn), lambda b, i, k: (b, i, jnp.clip(k - n_qv, 0, n_main - n_qv - 1))),
                   pl.BlockSpec((1, tn, tm), lambda b, i, k: (b, jnp.maximum(k - n_main, 0), i)),
                   pl.BlockSpec((1, tm, LANES), lambda b, i, k: (b, i, 0))],
        out_shape=[jax.ShapeDtypeStruct((grp, t, 2 * D_C), BF16),
                   jax.ShapeDtypeStruct((grp, t, D_C), F32),
                   jax.ShapeDtypeStruct((grp, D_C, t), BF16),
                   jax.ShapeDtypeStruct((grp, t, LANES), F32)],
        scratch_shapes=[pltpu.VMEM((tm, d), BF16)],
        compiler_params=_params("arbitrary", "arbitrary", "arbitrary"),
        name="mixer_c_in_proj",
    )(*args)


def _ctx_attn_kernel(q_ref, k_ref, v_ref, o_ref):
    s = _dot_nt(q_ref[0].astype(BF16), k_ref[0].astype(BF16))
    p = jnp.exp(s - jnp.max(s, axis=-1, keepdims=True))
    o = _dot(p.astype(BF16), v_ref[0].astype(BF16)) / jnp.sum(p, axis=-1, keepdims=True)
    o_ref[0] = o.astype(BF16)


def _ctx_attention(qkv):
    b, l, _ = qkv.shape
    blk = lambda off: pl.BlockSpec((1, l, HD_A), lambda i, h: (i, 0, off + h))
    return pl.pallas_call(
        _ctx_attn_kernel,
        grid=(b, H_A),
        in_specs=[blk(0), blk(H_A), blk(2 * H_A)],
        out_specs=pl.BlockSpec((1, l, HD_A), lambda i, h: (i, 0, h)),
        out_shape=jax.ShapeDtypeStruct((b, l, D_A), BF16),
        compiler_params=_params("arbitrary", "arbitrary"),
        name="ctx_attention",
    )(qkv, qkv, qkv)


def _na_bias_tables(rpb, rows):
    nblk = rows // NA_ROWS
    n_dr, n_dc = 2 * WIN_R - 1, 2 * WIN_C - 1
    c = np.arange(GRID_W)[:, None]
    kc = np.arange(GRID_W)[None, :]
    dc = np.clip(kc - c + WIN_C - 1, 0, n_dc - 1)
    pick_c = (dc[None] == np.arange(n_dc)[:, None, None]).astype(np.float32)
    c0 = np.clip(c - WIN_C // 2, 0, GRID_W - WIN_C)
    ok_c = (kc >= c0) & (kc < c0 + WIN_C)
    by_col = jnp.einsum("hrd,dck->hrck", rpb, pick_c, precision=lax.Precision.HIGHEST)
    tables = []
    for i in (0, 1, nblk - 1):
        ks = int(np.clip(NA_ROWS * i - WIN_R // 2, 0, rows - NA_KEY_ROWS))
        r = (NA_ROWS * i + np.arange(NA_ROWS))[:, None]
        kr = (ks + np.arange(NA_KEY_ROWS))[None, :]
        r0 = np.clip(r - WIN_R // 2, 0, rows - WIN_R)
        ok_r = (kr >= r0) & (kr < r0 + WIN_R)
        dr = np.clip(kr - r + WIN_R - 1, 0, n_dr - 1)
        pick_r = (dr[None] == np.arange(n_dr)[:, None, None]).astype(np.float32)
        bias = jnp.einsum("hrck,rqs->hqcsk", by_col, pick_r, precision=lax.Precision.HIGHEST)
        ok = ok_r[None, :, None, :, None] & ok_c[None, None, :, None, :]
        bias = jnp.where(ok, bias, -jnp.inf)
        tables.append(bias.reshape(H_A, NA_ROWS * GRID_W, NA_KEY_ROWS * GRID_W))
    return jnp.stack(tables, axis=1)


def _na_kernel(q_ref, k_ref, v_ref, ck_ref, cv_ref, bias_ref, o_ref, *, rows):
    i = pl.program_id(2)
    nblk = rows // NA_ROWS
    ks = jnp.clip(NA_ROWS * i - WIN_R // 2, 0, rows - NA_KEY_ROWS)
    start = pl.multiple_of(ks * GRID_W, GRID_W)
    nkey = NA_KEY_ROWS * GRID_W
    kind = jnp.where(i == 0, 0, jnp.where(i == nblk - 1, 2, 1))
    ones_w = jnp.ones((nkey, HD_A), BF16)
    ones_c = jnp.ones((ck_ref.shape[1], HD_A), BF16)
    for hh in range(NA_HEADS):
        sl = slice(hh * HD_A, (hh + 1) * HD_A)
        q = q_ref[0, :, sl]
        kw = k_ref[0, pl.ds(start, nkey), sl]
        vw = v_ref[0, pl.ds(start, nkey), sl]
        s_w = _dot_nt(q, kw) + bias_ref[hh, kind]
        s_c = _dot_nt(q, ck_ref[0, :, sl])
        m = jnp.maximum(jnp.max(s_w, axis=-1, keepdims=True), jnp.max(s_c, axis=-1, keepdims=True))
        p_w = jnp.exp(s_w - m).astype(BF16)
        p_c = jnp.exp(s_c - m).astype(BF16)
        den = _dot(p_w, ones_w) + _dot(p_c, ones_c)
        o = _dot(p_w, vw) + _dot(p_c, cv_ref[0, :, sl])
        o_ref[0, :, sl] = (o / den).astype(BF16)


def _neighbourhood_attention(qkv, ck, cv, rpb):
    b, t, _ = qkv.shape
    l = ck.shape[1]
    rows = t // GRID_W
    nblk = rows // NA_ROWS
    tq = NA_ROWS * GRID_W
    bias = _na_bias_tables(rpb, rows)
    hw = NA_HEADS * HD_A
    ngrp = H_A // NA_HEADS
    full = lambda off: pl.BlockSpec((1, t, hw), lambda h, n, i: (n, 0, off + h))
    ctx = pl.BlockSpec((1, l, hw), lambda h, n, i: (n, 0, h))
    return pl.pallas_call(
        functools.partial(_na_kernel, rows=rows),
        grid=(ngrp, b, nblk),
        in_specs=[
            pl.BlockSpec((1, tq, hw), lambda h, n, i: (n, i, h)),
            full(ngrp), full(2 * ngrp), ctx, ctx,
            pl.BlockSpec((NA_HEADS, 3, tq, NA_KEY_ROWS * GRID_W), lambda h, n, i: (h, 0, 0, 0)),
        ],
        out_specs=pl.BlockSpec((1, tq, hw), lambda h, n, i: (n, i, h)),
        out_shape=jax.ShapeDtypeStruct((b, t, D_A), BF16),
        compiler_params=_params("arbitrary", "arbitrary", "arbitrary"),
        name="neighbourhood_attention",
    )(qkv, qkv, qkv, ck, cv, bias)


def _lru_kernel(xb_ref, gb_ref, cw_ref, cb_ref, w4h_ref, b4_ref, lam_ref, h0_ref, o_ref, fin_ref,
                xpad_scr, af_scr, uf_scr, ab_scr, ub_scr, *, t, chunk):
    seg = t // LRU_SEGS
    pitch = seg + SUBLANES
    pad = SUBLANES
    zeros = jnp.zeros((pad, BD_B), F32)
    xpad_scr[0:pad, :] = zeros
    xpad_scr[pad + t:2 * pad + t, :] = zeros
    xpad_scr[pad:pad + t, :] = xb_ref[0]

    c1 = (-0.5 * LRU_C) * _softplus(-lam_ref[...])
    w4h = w4h_ref[0]
    b4h = 0.5 * b4_ref[0]
    left = (CONV_W - 1) // 2
    for c in range(t // chunk):
        base = c * chunk
        xc = cb_ref[...]
        for k in range(CONV_W):
            xc = xc + xpad_scr[pl.ds(base + pad - left + k, chunk), :] * cw_ref[k:k + 1, :]
        th = jnp.tanh(_dot(xc.astype(BF16), w4h) + b4h)
        xh = 0.5 * xc
        for d, (a_scr, u_scr) in enumerate(((af_scr, uf_scr), (ab_scr, ub_scr))):
            tr = th[:, 2 * d * BD_B:(2 * d + 1) * BD_B]
            ti = th[:, (2 * d + 1) * BD_B:(2 * d + 2) * BD_B]
            log_a = c1[d:d + 1, :] * tr + c1[d:d + 1, :]
            a = jnp.exp(log_a)
            u = jnp.sqrt(-jnp.tanh(log_a) * (a * a + 1.0)) * (xh * ti + xh)
            for p in range(chunk // seg):
                s = (base + p * seg) // seg
                a_scr[s * pitch:s * pitch + seg, :] = a[p * seg:(p + 1) * seg, :]
                u_scr[s * pitch:s * pitch + seg, :] = u[p * seg:(p + 1) * seg, :]

    def scan(j, carry):
        hf, pf, hb, pb = carry
        rf = pl.ds(j, LRU_SEGS, stride=pitch)
        rb = pl.ds(seg - 1 - j, LRU_SEGS, stride=pitch)
        a = af_scr[rf, :]
        hf = a * hf + uf_scr[rf, :]
        pf = a * pf
        uf_scr[rf, :] = hf
        af_scr[rf, :] = pf
        a = ab_scr[rb, :]
        hb = a * hb + ub_scr[rb, :]
        pb = a * pb
        ub_scr[rb, :] = hb
        ab_scr[rb, :] = pb
        return hf, pf, hb, pb

    zero = jnp.zeros((LRU_SEGS, BD_B), F32)
    one = jnp.ones((LRU_SEGS, BD_B), F32)
    hf, pf, hb, pb = lax.fori_loop(0, seg, scan, (zero, one, zero, one), unroll=4)

    cf = h0_ref[0, 0:1, :]
    cin_f = []
    for s in range(LRU_SEGS):
        cin_f.append(cf)
        cf = hf[s:s + 1, :] + pf[s:s + 1, :] * cf
    cb = h0_ref[0, 1:2, :]
    cin_b = [None] * LRU_SEGS
    for s in reversed(range(LRU_SEGS)):
        cin_b[s] = cb
        cb = hb[s:s + 1, :] + pb[s:s + 1, :] * cb
    fin_ref[0, 0:1, :] = cf
    fin_ref[0, 1:2, :] = cb

    for s in range(LRU_SEGS):
        rows = slice(s * pitch, s * pitch + seg)
        hs = (uf_scr[rows, :] + af_scr[rows, :] * cin_f[s]) + (ub_scr[rows, :] + ab_scr[rows, :] * cin_b[s])
        nat = slice(s * seg, (s + 1) * seg)
        o_ref[0, nat, :] = (hs * jax.nn.gelu(gb_ref[0, nat, :])).astype(BF16)


def _rglru(xg, conv_w, conv_b, w4h, b4, lam, h0):
    b, t, _ = xg.shape
    chunk = min(t, 512)
    seg = t // LRU_SEGS
    col = lambda off: pl.BlockSpec((1, t, BD_B), lambda n, h: (n, 0, off + h))
    return pl.pallas_call(
        functools.partial(_lru_kernel, t=t, chunk=chunk),
        grid=(b, H_B),
        in_specs=[
            col(0), col(H_B),
            pl.BlockSpec((CONV_W, BD_B), lambda n, h: (0, h)),
            pl.BlockSpec((1, BD_B), lambda n, h: (0, h)),
            pl.BlockSpec((1, BD_B, 4 * BD_B), lambda n, h: (h, 0, 0)),
            pl.BlockSpec((1, 1, 4 * BD_B), lambda n, h: (h, 0, 0)),
            pl.BlockSpec((2, BD_B), lambda n, h: (0, h)),
            pl.BlockSpec((1, 2, BD_B), lambda n, h: (n, 0, h)),
        ],
        out_specs=[pl.BlockSpec((1, t, BD_B), lambda n, h: (n, 0, h)),
                   pl.BlockSpec((1, 2, BD_B), lambda n, h: (n, 0, h))],
        out_shape=[jax.ShapeDtypeStruct((b, t, D_RNN), BF16),
                   jax.ShapeDtypeStruct((b, 2, D_RNN), F32)],
        scratch_shapes=[pltpu.VMEM((t + 2 * SUBLANES, BD_B), F32)]
        + [pltpu.VMEM((LRU_SEGS * (seg + SUBLANES), BD_B), F32)] * 4,
        compiler_params=_params("arbitrary", "arbitrary"),
        name="rglru_branch",
    )(xg, xg, conv_w, conv_b.reshape(1, D_RNN), w4h, b4, lam, h0)


def _outproj_ab_kernel(x_ref, mod_ref, oa_ref, ob_ref, w_ref, o_ref):
    y = _dot(oa_ref[0], w_ref[0:D_A, :]) + _dot(ob_ref[0], w_ref[D_A:D_A + D_RNN, :])
    o_ref[0] = x_ref[0] + mod_ref[0, 5:6, :] * y


def _outproj_ab(x, mod, oa, ob, w):
    grp, t, d = x.shape
    tm = TOKEN_TILE
    tok = lambda width: pl.BlockSpec((1, tm, width), lambda b, i: (b, i, 0))
    return pl.pallas_call(
        _outproj_ab_kernel,
        grid=(grp, t // tm),
        in_specs=[
            tok(d),
            pl.BlockSpec((1, N_MOD, d), lambda b, i: (b, 0, 0)),
            tok(D_A), tok(D_RNN),
            pl.BlockSpec((D_A + D_RNN, d), lambda b, i: (0, 0)),
        ],
        out_specs=tok(d),
        out_shape=jax.ShapeDtypeStruct(x.shape, F32),
        compiler_params=_params("arbitrary", "arbitrary"),
        name="mixer_ab_out_proj",
    )(x, mod, oa, ob, w)


def _gate_prep_kernel(g_ref, b_ref, o_ref):
    t = g_ref.shape[2]
    g = g_ref[0] + b_ref[...]
    lane = lax.broadcasted_iota(jnp.int32, (H_C, t), 1) & (CHUNK - 1)
    cum = _log_sigmoid(g[H_C:2 * H_C])
    suf = _log_sigmoid(g[3 * H_C:4 * H_C])
    sh = 1
    while sh < CHUNK:
        cum = cum + jnp.where(lane >= sh, pltpu.roll(cum, sh, 1), 0.0)
        suf = suf + jnp.where(lane < CHUNK - sh, pltpu.roll(suf, t - sh, 1), 0.0)
        sh *= 2
    o_ref[0, 0:H_C] = g[0:H_C]
    o_ref[0, H_C:2 * H_C] = cum
    o_ref[0, 2 * H_C:3 * H_C] = g[2 * H_C:3 * H_C]
    o_ref[0, 3 * H_C:4 * H_C] = suf


def _gate_prep(g_rows, b_gate):
    b, r, t = g_rows.shape
    return pl.pallas_call(
        _gate_prep_kernel,
        grid=(b,),
        in_specs=[pl.BlockSpec((1, r, t), lambda n: (n, 0, 0)), pl.BlockSpec((r, 1), lambda n: (0, 0))],
        out_specs=pl.BlockSpec((1, r, t), lambda n: (n, 0, 0)),
        out_shape=jax.ShapeDtypeStruct(g_rows.shape, F32),
        compiler_params=_params("arbitrary"),
        name="mlstm_gate_prep",
    )(g_rows, b_gate.reshape(r, 1))


def _mlstm_chunk(q, kt, v_aug, ig_row, bc_row, bc_col, cm, m, reverse):
    ti = lax.broadcasted_iota(jnp.int32, (CHUNK, CHUNK), 0)
    si = lax.broadcasted_iota(jnp.int32, (CHUNK, CHUNK), 1)
    seen = (si >= ti) if reverse else (si <= ti)
    bk = bc_row[:, 0:1] if reverse else bc_row[:, CHUNK - 1:CHUNK]
    bc_rep = jnp.broadcast_to(bc_col, (CHUNK, CHUNK))
    dmat = jnp.where(seen, bc_rep - bc_row + ig_row, -jnp.inf)
    inter = bc_rep + m
    m_t = jnp.maximum(inter, jnp.broadcast_to(jnp.max(dmat, axis=1, keepdims=True), (CHUNK, CHUNK)))
    w = jnp.exp(dmat - m_t)
    g = jnp.exp(inter - m_t)
    kscale = DK_C ** -0.5
    s = _dot(q, kt) * (w * kscale)
    tot = _dot(s.astype(BF16), v_aug) + jnp.concatenate([g, g, g], axis=1) * _dot(q, cm.astype(BF16))
    inv = 1.0 / jnp.maximum(jnp.abs(tot[:, DV_C:]), jnp.exp(-m_t))
    hout = tot[:, :DV_C] * jnp.concatenate([inv, inv], axis=1)
    dend = bk - bc_row + ig_row
    m_new = jnp.maximum(bk + m, jnp.max(dend, axis=1, keepdims=True))
    we = jnp.exp(dend - m_new)
    ge = jnp.exp(bk + m - m_new)
    kwt = kt.astype(F32) * (we * kscale)
    c_new = ge * cm + _dot(kwt.astype(BF16), v_aug)
    return hout, c_new, m_new


def _mlstm_kernel(*refs, zero_state):
    if zero_state:
        qvf_ref, vvf_ref, ktf_ref, qvb_ref, vvb_ref, ktb_ref, grf_ref, gcf_ref, grb_ref, gcb_ref = refs[:10]
        rest = refs[10:]
    else:
        m0_ref, qvf_ref, vvf_ref, ktf_ref, qvb_ref, vvb_ref, ktb_ref, grf_ref, gcf_ref, grb_ref, gcb_ref = refs[:11]
        c0_ref, n0_ref = refs[11:13]
        rest = refs[13:]
    hf_ref, hb_ref, cfin_ref, nfin_ref, mfin_ref, c_scr, m_scr = rest
    n, hp, c = pl.program_id(0), pl.program_id(1), pl.program_id(2)
    eye = (lax.broadcasted_iota(jnp.int32, (DK_C, DK_C), 0) == lax.broadcasted_iota(jnp.int32, (DK_C, DK_C), 1))

    @pl.when(c == 0)
    def _():
        if zero_state:
            c_scr[...] = jnp.zeros_like(c_scr)
            m_scr[...] = jnp.zeros_like(m_scr)
            return
        for d in range(2):
            for j in range(MLSTM_HEADS):
                c_scr[d, j, :, 0:DV_C] = c0_ref[0, d, j]
                n_col = jnp.sum(jnp.where(eye, n0_ref[0, d, j], 0.0), axis=1, keepdims=True)
                c_scr[d, j, :, DV_C:DV_C + LANES] = jnp.broadcast_to(n_col, (DK_C, LANES))
                m_scr[d, j] = jnp.full(m_scr.shape[2:], m0_ref[n, d * H_C + hp * MLSTM_HEADS + j], F32)

    ones_col = jnp.ones((CHUNK, LANES), BF16)
    dirs = ((qvf_ref, vvf_ref, ktf_ref, grf_ref, gcf_ref, hf_ref), (qvb_ref, vvb_ref, ktb_ref, grb_ref, gcb_ref, hb_ref))
    chains = []
    for d, (q_ref, v_ref, kt_ref, gr_ref, gc_ref, h_ref) in enumerate(dirs):
        for j in range(MLSTM_HEADS):
            sl = slice(j * DK_C, (j + 1) * DK_C)
            v_aug = jnp.concatenate([v_ref[0, :, sl], ones_col], axis=1)
            chains.append((d, j, h_ref, sl, _mlstm_chunk(
                q_ref[0, :, sl], kt_ref[0, sl, :], v_aug,
                gr_ref[0, j, 2 * d:2 * d + 1, :], gr_ref[0, j, 2 * d + 1:2 * d + 2, :],
                gc_ref[0, j, :, 2 * d + 1:2 * d + 2],
                c_scr[d, j], m_scr[d, j, 0:1, 0:1], reverse=(d == 1))))
    for d, j, h_ref, sl, (hout, c_new, m_new) in chains:
        h_ref[0, :, sl] = hout
        c_scr[d, j] = c_new
        m_scr[d, j] = jnp.broadcast_to(m_new, m_scr.shape[2:])

    @pl.when(c == pl.num_programs(2) - 1)
    def _():
        for d in range(2):
            for j in range(MLSTM_HEADS):
                cfin_ref[0, d, j] = c_scr[d, j, :, 0:DV_C]
                n_col = c_scr[d, j, :, DV_C:DV_C + 1]
                nfin_ref[0, d, j] = jnp.sum(jnp.where(eye, n_col, 0.0), axis=0, keepdims=True)
                mfin_ref[0, d, j] = m_scr[d, j, 0:1, :]


def _rope_tables(t):
    half = DK_C // 2
    nf = half // 2
    pos = jnp.arange(t)
    freqs = ROPE_BASE ** (-jnp.arange(nf, dtype=F32) / nf)
    cos, sin = [], []
    for p in (pos // GRID_W, pos % GRID_W):
        ang = p.astype(F32)[:, None] * freqs[None, :]
        cos += [jnp.cos(ang), jnp.cos(ang)]
        sin += [-jnp.sin(ang), jnp.sin(ang)]
    return jnp.concatenate(cos, axis=-1), jnp.concatenate(sin, axis=-1)


def _mlstm(qv, kt, gates, c0, n0, m0):
    b, t, _ = qv.shape
    nc = t // CHUNK
    hps = MLSTM_HEADS
    ngrp = H_C // hps
    g4 = gates.reshape(b, 4, H_C, t)
    g_row = jnp.transpose(g4, (0, 2, 1, 3))
    g_col = jnp.transpose(g4, (0, 2, 3, 1))
    fwd = lambda c: c
    bwd = lambda c: nc - 1 - c

    def tok(off, order):
        return pl.BlockSpec((1, CHUNK, hps * DK_C), lambda n, h, c: (n, order(c), off + h))

    def kt_spec(order):
        return pl.BlockSpec((1, hps * DK_C, CHUNK), lambda n, h, c: (n, h, order(c)))

    def g_row_spec(order):
        return pl.BlockSpec((1, hps, 4, CHUNK), lambda n, h, c: (n, h, 0, order(c)))

    def g_col_spec(order):
        return pl.BlockSpec((1, hps, CHUNK, 4), lambda n, h, c: (n, h, order(c), 0))

    state = lambda *tail: pl.BlockSpec((1, 2, hps) + tail, lambda n, h, c: (n, 0, h) + (0,) * len(tail))
    hspec = lambda order: pl.BlockSpec((1, CHUNK, hps * DV_C), lambda n, h, c: (n, order(c), h))
    in_specs = [tok(0, fwd), tok(ngrp, fwd), kt_spec(fwd),
                tok(0, bwd), tok(ngrp, bwd), kt_spec(bwd),
                g_row_spec(fwd), g_col_spec(fwd), g_row_spec(bwd), g_col_spec(bwd)]
    args = [qv, qv, kt, qv, qv, kt, g_row, g_col, g_row, g_col]
    zero_state = c0 is None
    if not zero_state:
        in_specs = [pl.BlockSpec(memory_space=pltpu.SMEM)] + in_specs + [state(DK_C, DV_C), state(1, DK_C)]
        args = [m0.reshape(b, 2 * H_C)] + args + [c0, n0.reshape(b, 2, H_C, 1, DK_C)]
    hf, hb, cfin, nfin, mfin = pl.pallas_call(
        functools.partial(_mlstm_kernel, zero_state=zero_state),
        grid=(b, ngrp, nc),
        in_specs=in_specs,
        out_specs=[hspec(fwd), hspec(bwd), state(DK_C, DV_C), state(1, DK_C), state(1, LANES)],
        out_shape=[jax.ShapeDtypeStruct((b, t, D_C), F32), jax.ShapeDtypeStruct((b, t, D_C), F32),
                   jax.ShapeDtypeStruct((b, 2, H_C, DK_C, DV_C), F32),
                   jax.ShapeDtypeStruct((b, 2, H_C, 1, DK_C), F32),
                   jax.ShapeDtypeStruct((b, 2, H_C, 1, LANES), F32)],
        scratch_shapes=[pltpu.VMEM((2, hps, DK_C, DV_C + LANES), F32),
                        pltpu.VMEM((2, hps, SUBLANES, LANES), F32)],
        compiler_params=_params("arbitrary", "arbitrary", "arbitrary"),
        name="mlstm_scan",
    )(*args)
    return hf, hb, cfin, nfin.reshape(b, 2, H_C, DK_C), mfin[:, :, :, 0, 0]


def _outproj_c_kernel(x_ref, mod_ref, hf_ref, hb_ref, og_ref, g_ref, w_ref, o_ref, hn_scr):
    for h in range(H_C):
        sl = slice(h * DV_C, (h + 1) * DV_C)
        hs = hf_ref[0, :, sl] + hb_ref[0, :, sl]
        hn_scr[:, sl] = (_rms(hs, g_ref[:, sl]) * jax.nn.sigmoid(og_ref[0, :, sl])).astype(BF16)
    o_ref[0] = x_ref[0] + mod_ref[0, 5:6, :] * _dot(hn_scr[...], w_ref[...])


def _outproj_c(x, mod, hf, hb, og, mh_g, w):
    grp, t, d = x.shape
    tm = TOKEN_TILE // 2
    tok = pl.BlockSpec((1, tm, d), lambda b, i: (b, i, 0))
    return pl.pallas_call(
        _outproj_c_kernel,
        grid=(grp, t // tm),
        in_specs=[
            tok,
            pl.BlockSpec((1, N_MOD, d), lambda b, i: (b, 0, 0)),
            tok, tok,
            pl.BlockSpec((1, tm, D_C), lambda b, i: (b, i, 0)),
            pl.BlockSpec((1, D_C), lambda b, i: (0, 0)),
            pl.BlockSpec((D_C, d), lambda b, i: (0, 0)),
        ],
        out_specs=tok,
        out_shape=jax.ShapeDtypeStruct(x.shape, F32),
        scratch_shapes=[pltpu.VMEM((tm, D_C), BF16)],
        compiler_params=_params("arbitrary", "arbitrary"),
        name="mixer_c_out_proj",
    )(x, mod, hf, hb, og, mh_g.reshape(1, D_C), w)


def _mixer_ab(j, xp, xs, mp, ms, g, p):
    bp, lp, bs = p["bp"], p["lp"], xs.shape[0]
    w_in = p["w_in_ab"][j].astype(BF16)
    w_out = p["w_out_ab"][j].astype(BF16)
    gains = jnp.stack([jnp.tile(p["qn_g"][j] * (HD_A ** -0.5), H_A), jnp.tile(p["kn_g"][j], H_A),
                       jnp.ones((D_A,), F32)]).reshape(3, 1, D_A)
    wa, wx, ba, bx = p["lru_wa"][j], p["lru_wx"][j], p["lru_ba"][j], p["lru_bx"][j]
    w4 = (0.5 * jnp.concatenate([wa[0], wx[0], wa[1], wx[1]], axis=-1)).astype(BF16)
    b4 = jnp.concatenate([v.reshape(H_B, 1, BD_B) for v in (ba[0], bx[0], ba[1], bx[1])], axis=-1)
    lru = (p["conv_w"][j], p["conv_b"][j], w4, b4, p["lru_lam"][j])

    qkv, xg = _inproj_ab(xp, mp, g, w_in, gains, F32)
    qkv = qkv.reshape(bp, lp, 3 * D_A)
    oa = _ctx_attention(qkv)
    ob, h_fin = _rglru(xg.reshape(bp, lp, 2 * D_RNN), *lru, jnp.zeros((bp, 2, D_RNN), F32))
    xp = _outproj_ab(xp, mp, oa.reshape(1, bp * lp, D_A), ob.reshape(1, bp * lp, D_RNN), w_out)
    new = (qkv[:, :, D_A:2 * D_A].reshape(bp, lp, H_A, HD_A), qkv[:, :, 2 * D_A:].reshape(bp, lp, H_A, HD_A), h_fin)

    past = p["cache_k"].shape[2]
    ck = p["cache_k"][:, j].reshape(bs, past, D_A).astype(BF16)
    cv = p["cache_v"][:, j].reshape(bs, past, D_A).astype(BF16)
    qkv, xg = _inproj_ab(xs, ms, g, w_in, gains, BF16)
    oa = _neighbourhood_attention(qkv, ck, cv, p["rpb"][j])
    ob, _ = _rglru(xg, *lru, p["state_lru"][:, j])
    xs = _outproj_ab(xs, ms, oa, ob, w_out)
    return xp, xs, new


def _mixer_c(j, xp, xs, mp, ms, g, p):
    bp, lp, bs, ts = p["bp"], p["lp"], xs.shape[0], xs.shape[1]
    w = p["w_in_c"][j].astype(BF16)
    wkt = w[:, D_C:2 * D_C].T
    w_gate = jnp.pad(w[:, 4 * D_C:], ((0, 0), (0, LANES - 4 * H_C)))
    w_out = p["w_out_c"][j].astype(BF16)

    def gates_of(zg, b, t):
        g_rows = jnp.transpose(zg.reshape(b, t, LANES)[:, :, :4 * H_C], (0, 2, 1))
        return _gate_prep(g_rows, p["b_gate_c"][j])

    qv, og, kt, zg = _inproj_c(xp, mp, g, w, wkt, w_gate, None)
    kt = jnp.transpose(kt.reshape(D_C, bp, lp), (1, 0, 2))
    hf, hb, cfin, nfin, mfin = _mlstm(
        qv.reshape(bp, lp, 2 * D_C), kt, gates_of(zg, bp, lp), None, None, None)
    xp = _outproj_c(xp, mp, hf.reshape(1, bp * lp, D_C), hb.reshape(1, bp * lp, D_C), og, p["mh_norm_g"][j], w_out)

    qv, og, kt, zg = _inproj_c(xs, ms, g, w, wkt, w_gate, _rope_tables(ts))
    hf, hb, _, _, _ = _mlstm(qv, kt, gates_of(zg, bs, ts), p["state_mlstm_c"][:, j], p["state_mlstm_n"][:, j],
                             p["state_mlstm_m"][:, j])
    xs = _outproj_c(xs, ms, hf, hb, og, p["mh_norm_g"][j], w_out)
    return xp, xs, (cfin, nfin, mfin)


def kernel(x_prompt, x_sample, cache_k, cache_v, state_lru, state_mlstm_c, state_mlstm_n, state_mlstm_m, c, c_ctx, w_mod, b_mod, norm_g, w_ffn_gate, w_ffn_up, w_ffn_down, w_in_ab, w_out_ab, qn_g, kn_g, rpb, conv_w, conv_b, lru_wa, lru_ba, lru_wx, lru_bx, lru_lam, w_in_c, b_gate_c, mh_norm_g, w_out_c):
    bp, lp, d = x_prompt.shape
    bs = x_sample.shape[0]
    p = dict(bp=bp, lp=lp, cache_k=cache_k, cache_v=cache_v, state_lru=state_lru, state_mlstm_c=state_mlstm_c,
             state_mlstm_n=state_mlstm_n, state_mlstm_m=state_mlstm_m, w_in_ab=w_in_ab, w_out_ab=w_out_ab,
             qn_g=qn_g, kn_g=kn_g, rpb=rpb, conv_w=conv_w, conv_b=conv_b, lru_wa=lru_wa, lru_ba=lru_ba,
             lru_wx=lru_wx, lru_bx=lru_bx, lru_lam=lru_lam, w_in_c=w_in_c, b_gate_c=b_gate_c,
             mh_norm_g=mh_norm_g, w_out_c=w_out_c)

    cond = jnp.concatenate([c, c_ctx[None, :]], axis=0)
    cond = jnp.pad(cond, ((0, -(bs + 1) % SUBLANES), (0, 0)))
    mod = _modulation(cond, w_mod, b_mod)[:, :bs + 1].reshape(DEPTH, bs + 1, N_MOD, d)

    wg = w_ffn_gate.astype(BF16)
    wu = w_ffn_up.astype(BF16)
    wd = w_ffn_down.astype(BF16)

    xp = x_prompt.reshape(1, bp * lp, d)
    xs = x_sample
    new_ab, new_c = [], []
    for l in range(DEPTH):
        j = l // 2
        ms = mod[l, :bs]
        mp = mod[l, bs:]
        xp = _ffn(xp, mp, norm_g[l, 0], wg, wu, wd, l, 0, 0)
        xs = _ffn(xs, ms, norm_g[l, 0], wg, wu, wd, l, 0, 0)
        if l % 2 == 0:
            xp, xs, new = _mixer_ab(j, xp, xs, mp, ms, norm_g[l, 1], p)
            new_ab.append(new)
        else:
            xp, xs, new = _mixer_c(j, xp, xs, mp, ms, norm_g[l, 1], p)
            new_c.append(new)
        xp = _ffn(xp, mp, norm_g[l, 2], wg, wu, wd, l, 1, 2)
        xs = _ffn(xs, ms, norm_g[l, 2], wg, wu, wd, l, 1, 2)
    stack = lambda items, i: jnp.concatenate([jnp.expand_dims(it[i], 1) for it in items], axis=1)
    return (xp.reshape(bp, lp, d), xs, stack(new_ab, 0), stack(new_ab, 1), stack(new_ab, 2),
            stack(new_c, 0), stack(new_c, 1), stack(new_c, 2))
```

```python
import functools

import numpy as np
import jax
import jax.numpy as jnp
from jax import lax
from jax.experimental import pallas as pl
from jax.experimental.pallas import tpu as pltpu

D_MODEL = 2048
DEPTH = 2
GRID_W = 64
N_MOD = 9
D_FF = 5632
EPS = 1e-6
H_A = 8
HD_A = 128
D_A = H_A * HD_A
WIN_R = 8
WIN_C = 16
D_RNN = 1024
H_B = 8
BD_B = D_RNN // H_B
CONV_W = 4
LRU_C = 8.0
H_C = 8
DK_C = 256
DV_C = 256
D_C = H_C * DV_C
CHUNK = 128
ROPE_BASE = 10000.0

BF16 = jnp.bfloat16
F32 = jnp.float32

V7X_VMEM_BYTES = 64 * 1024 * 1024
VMEM_LIMIT = V7X_VMEM_BYTES - 8 * 1024 * 1024
LANES = 128
SUBLANES = 8

TOKEN_TILE = 512
FF_TILE = 512
FFN_NORM_ROWS = 64
PROJ_TOKEN_TILE = 1024
PROJ_TILE = 512
NA_ROWS = 4
NA_KEY_ROWS = NA_ROWS + WIN_R
LRU_SEGS = SUBLANES
NA_HEADS = 4
MLSTM_HEADS = 4


def _params(*sem):
    return pltpu.CompilerParams(dimension_semantics=sem, vmem_limit_bytes=VMEM_LIMIT)


def _dot(a, b):
    return jnp.dot(a, b, preferred_element_type=F32)


def _dot_nt(a, b):
    return lax.dot_general(a, b, (((1,), (1,)), ((), ())), preferred_element_type=F32)


def _dot_tn(a, b):
    return lax.dot_general(a, b, (((0,), (0,)), ((), ())), preferred_element_type=F32)


def _rms(x, g):
    return x * lax.rsqrt(jnp.mean(x * x, axis=-1, keepdims=True) + EPS) * g


def _adaln(x, g, mod_ref, j):
    return _rms(x, g) * (1.0 + mod_ref[0, 3 * j + 1:3 * j + 2, :]) + mod_ref[0, 3 * j:3 * j + 1, :]


def _softplus(x):
    return jnp.maximum(x, 0.0) + jnp.log1p(jnp.exp(-jnp.abs(x)))


def _log_sigmoid(x):
    return -_softplus(-x)


def _mod_kernel(c_ref, w_ref, b_ref, o_ref):
    c = c_ref[...]
    s = (c * jax.nn.sigmoid(c)).astype(BF16)
    o_ref[0] = _dot(s, w_ref[0].astype(BF16)) + b_ref[0]


def _modulation(cond, w_mod, b_mod):
    r = cond.shape[0]
    n = N_MOD * D_MODEL
    tn = 1024
    return pl.pallas_call(
        _mod_kernel,
        grid=(DEPTH, n // tn),
        in_specs=[
            pl.BlockSpec((r, D_MODEL), lambda l, j: (0, 0)),
            pl.BlockSpec((1, D_MODEL, tn), lambda l, j: (l, 0, j)),
            pl.BlockSpec((1, 1, tn), lambda l, j: (l, 0, j)),
        ],
        out_specs=pl.BlockSpec((1, r, tn), lambda l, j: (l, 0, j)),
        out_shape=jax.ShapeDtypeStruct((DEPTH, r, n), F32),
        compiler_params=_params("arbitrary", "arbitrary"),
        name="modulation",
    )(cond, w_mod, b_mod.reshape(DEPTH, 1, n))


def _ffn_kernel(x0_ref, xn_ref, mod_ref, modn_ref, g_ref, wg_ref, wu_ref, wd_ref, o_ref,
                ha_scr, hb_scr, xa_scr, xb_scr, acc_scr, *, j, nf):
    p, f2 = pl.program_id(0), pl.program_id(1)
    rows = xn_ref.shape[1]
    tm = ha_scr.shape[0]

    @pl.when((p == 0) & (f2 == 0))
    def _():
        xa_scr[...] = x0_ref[0]
        ha_scr[...] = _adaln(x0_ref[0], g_ref[...], mod_ref, j).astype(BF16)

    @pl.when((f2 == 0) | (f2 == nf))
    def _():
        acc_scr[...] = jnp.zeros_like(acc_scr)

    def body(h_cur, h_nxt, x_nxt, f):
        r0 = pl.multiple_of(jnp.minimum(f, tm // rows - 1) * rows, rows)
        xs = xn_ref[0]
        x_nxt[pl.ds(r0, rows), :] = xs
        h_nxt[pl.ds(r0, rows), :] = _adaln(xs, g_ref[...], modn_ref, j).astype(BF16)
        h = h_cur[...]
        a = _dot(h, wg_ref[...])
        u = _dot(h, wu_ref[...])
        act = (a * jax.nn.sigmoid(a)) * u
        acc_scr[...] += _dot(act.astype(BF16), wd_ref[...])

    @pl.when(f2 < nf)
    def _():
        body(ha_scr, hb_scr, xb_scr, f2)

    @pl.when(f2 >= nf)
    def _():
        body(hb_scr, ha_scr, xa_scr, f2 - nf)

    for last, x_cur in ((nf - 1, xa_scr), (2 * nf - 1, xb_scr)):
        @pl.when(f2 == last)
        def _():
            o_ref[0] = x_cur[...] + 0.5 * mod_ref[0, 3 * j + 2:3 * j + 3, :] * acc_scr[...]


def _ffn(x, mod, g, wg, wu, wd, l, s, j):
    grp, t, d = x.shape
    tm, tf = TOKEN_TILE, FF_TILE
    nf = D_FF // tf
    per_group = t // tm
    ntile = grp * per_group
    rows = FFN_NORM_ROWS
    nsl = tm // rows
    assert ntile % 2 == 0 and nf >= nsl
    tile = lambda p, f2: 2 * p + f2 // nf
    nxt = lambda p, f2: jnp.minimum(tile(p, f2) + 1, ntile - 1)
    out = pl.pallas_call(
        functools.partial(_ffn_kernel, j=j, nf=nf),
        grid=(ntile // 2, 2 * nf),
        in_specs=[
            pl.BlockSpec((1, tm, d), lambda p, f2: (0, 0, 0)),
            pl.BlockSpec((1, rows, d), lambda p, f2: (0, nxt(p, f2) * nsl + jnp.minimum(f2 % nf, nsl - 1), 0)),
            pl.BlockSpec((1, N_MOD, d), lambda p, f2: (tile(p, f2) // per_group, 0, 0)),
            pl.BlockSpec((1, N_MOD, d), lambda p, f2: (nxt(p, f2) // per_group, 0, 0)),
            pl.BlockSpec((1, d), lambda p, f2: (0, 0)),
            pl.BlockSpec((None, None, d, tf), lambda p, f2: (l, s, 0, f2 % nf)),
            pl.BlockSpec((None, None, d, tf), lambda p, f2: (l, s, 0, f2 % nf)),
            pl.BlockSpec((None, None, tf, d), lambda p, f2: (l, s, f2 % nf, 0)),
        ],
        out_specs=pl.BlockSpec((1, tm, d), lambda p, f2: (0, tile(p, f2), 0)),
        out_shape=jax.ShapeDtypeStruct((1, grp * t, d), F32),
        scratch_shapes=[pltpu.VMEM((tm, d), BF16), pltpu.VMEM((tm, d), BF16),
                        pltpu.VMEM((tm, d), F32), pltpu.VMEM((tm, d), F32), pltpu.VMEM((tm, d), F32)],
        compiler_params=_params("arbitrary", "arbitrary"),
        name="macaron_ffn",
    )(x.reshape(1, grp * t, d), x.reshape(1, grp * t, d), mod, mod, g.reshape(1, d), wg, wu, wd)
    return out.reshape(x.shape)


def _rope(x, cos, sin):
    half = DK_C // 2
    out = []
    for p in range(2):
        sl = slice(p * half, (p + 1) * half)
        xs = x[:, sl]
        out.append(xs * cos[:, sl] + pltpu.roll(xs, half // 2, 1) * sin[:, sl])
    return jnp.concatenate(out, axis=-1)


def _rope_t(x, cos, sin):
    q = DK_C // 4
    swapped = jnp.concatenate([x[q:2 * q], x[0:q], x[3 * q:4 * q], x[2 * q:3 * q]], axis=0)
    return x * cos + swapped * sin


def _inproj_ab_kernel(x_ref, mod_ref, g_ref, w_ref, gain_ref, oa_ref, ob_ref, h_scr, *, n_a, n_norm):
    k = pl.program_id(2)

    @pl.when(k == 0)
    def _():
        h_scr[...] = _adaln(x_ref[0], g_ref[...], mod_ref, 1).astype(BF16)

    @pl.when(k >= n_a)
    def _():
        ob_ref[0] = _dot(h_scr[...], w_ref[...])

    @pl.when(k < n_norm)
    def _():
        for pair in range(w_ref.shape[1] // (2 * HD_A)):
            z2 = _dot(h_scr[...], w_ref[:, 2 * pair * HD_A:2 * (pair + 1) * HD_A])
            for hh in range(2):
                zs = z2[:, hh * HD_A:(hh + 1) * HD_A]
                sl = slice((2 * pair + hh) * HD_A, (2 * pair + hh + 1) * HD_A)
                r = lax.rsqrt(jnp.mean(zs * zs, axis=-1, keepdims=True) + EPS)
                oa_ref[0, :, sl] = (zs * r * gain_ref[0, :, sl]).astype(oa_ref.dtype)

    @pl.when((k >= n_norm) & (k < n_a))
    def _():
        oa_ref[0] = (_dot(h_scr[...], w_ref[...]) * gain_ref[0]).astype(oa_ref.dtype)


def _inproj_ab(x, mod, g, w, gains, dtype_a):
    grp, t, d = x.shape
    tm, tn = PROJ_TOKEN_TILE, PROJ_TILE
    n_a = 3 * D_A // tn
    nt = w.shape[1] // tn
    return pl.pallas_call(
        functools.partial(_inproj_ab_kernel, n_a=n_a, n_norm=2 * D_A // tn),
        grid=(grp, t // tm, nt),
        in_specs=[
            pl.BlockSpec((1, tm, d), lambda b, i, k: (b, i, 0)),
            pl.BlockSpec((1, N_MOD, d), lambda b, i, k: (b, 0, 0)),
            pl.BlockSpec((1, d), lambda b, i, k: (0, 0)),
            pl.BlockSpec((d, tn), lambda b, i, k: (0, k)),
            pl.BlockSpec((1, 1, tn), lambda b, i, k: (jnp.minimum(k, n_a - 1), 0, 0)),
        ],
        out_specs=[pl.BlockSpec((1, tm, tn), lambda b, i, k: (b, i, jnp.minimum(k, n_a - 1))),
                   pl.BlockSpec((1, tm, tn), lambda b, i, k: (b, i, jnp.maximum(k - n_a, 0)))],
        out_shape=[jax.ShapeDtypeStruct((grp, t, n_a * tn), dtype_a),
                   jax.ShapeDtypeStruct((grp, t, (nt - n_a) * tn), F32)],
        scratch_shapes=[pltpu.VMEM((tm, d), BF16)],
        compiler_params=_params("arbitrary", "arbitrary", "arbitrary"),
        name="mixer_ab_in_proj",
    )(x, mod, g.reshape(1, d), w, gains.reshape(n_a, 1, tn))


def _inproj_c_kernel(*refs, rotary, n_q, n_qv, n_main):
    x_ref, mod_ref, g_ref, w_ref, wkt_ref, wgate_ref = refs[:6]
    pos = 6
    if rotary:
        cos_ref, sin_ref, cost_ref, sint_ref = refs[pos:pos + 4]
        pos += 4
    oa_ref, ob_ref, okt_ref, og_ref, h_scr = refs[pos:pos + 5]
    k = pl.program_id(2)
    tn = w_ref.shape[1]

    @pl.when(k == 0)
    def _():
        h = _adaln(x_ref[0], g_ref[...], mod_ref, 1).astype(BF16)
        h_scr[...] = h
        og_ref[0] = _dot(h, wgate_ref[...])

    @pl.when(k < n_q)
    def _():
        z = _dot(h_scr[...], w_ref[...])
        if rotary:
            for hh in range(tn // DK_C):
                sl = slice(hh * DK_C, (hh + 1) * DK_C)
                oa_ref[0, :, sl] = _rope(z[:, sl], cos_ref[...], sin_ref[...]).astype(BF16)
        else:
            oa_ref[0] = z.astype(BF16)

    @pl.when((k >= n_q) & (k < n_qv))
    def _():
        oa_ref[0] = _dot(h_scr[...], w_ref[...]).astype(BF16)

    @pl.when((k >= n_qv) & (k < n_main))
    def _():
        ob_ref[0] = _dot(h_scr[...], w_ref[...])

    @pl.when(k >= n_main)
    def _():
        zt = _dot_nt(wkt_ref[...], h_scr[...])
        if rotary:
            for hh in range(tn // DK_C):
                sl = slice(hh * DK_C, (hh + 1) * DK_C)
                okt_ref[0, sl, :] = _rope_t(zt[sl, :], cost_ref[...], sint_ref[...]).astype(BF16)
        else:
            okt_ref[0] = zt.astype(BF16)


def _inproj_c(x, mod, g, w, wkt, w_gate, rope_tables):
    grp, t, d = x.shape
    tm, tn = PROJ_TOKEN_TILE, PROJ_TILE
    n_q, n_qv, n_main = D_C // tn, 2 * D_C // tn, 3 * D_C // tn
    nt = n_main + D_C // tn

    def w_col(k):
        kk = jnp.minimum(k, n_main - 1)
        return jnp.where(kk < n_q, kk, kk + n_q)

    rotary = rope_tables is not None
    in_specs = [
        pl.BlockSpec((1, tm, d), lambda b, i, k: (b, i, 0)),
        pl.BlockSpec((1, N_MOD, d), lambda b, i, k: (b, 0, 0)),
        pl.BlockSpec((1, d), lambda b, i, k: (0, 0)),
        pl.BlockSpec((d, tn), lambda b, i, k: (0, w_col(k))),
        pl.BlockSpec((tn, d), lambda b, i, k: (jnp.maximum(k - n_main, 0), 0)),
        pl.BlockSpec((d, LANES), lambda b, i, k: (0, 0)),
    ]
    args = [x, mod, g.reshape(1, d), w, wkt, w_gate]
    if rotary:
        cos, sin = rope_tables
        in_specs += [pl.BlockSpec((tm, DK_C), lambda b, i, k: (i, 0))] * 2
        in_specs += [pl.BlockSpec((DK_C, tm), lambda b, i, k: (0, i))] * 2
        args += [cos, sin, cos.T, sin.T]
    return pl.pallas_call(
        functools.partial(_inproj_c_kernel, rotary=rotary, n_q=n_q, n_qv=n_qv, n_main=n_main),
        grid=(grp, t // tm, nt),
        in_specs=in_specs,
        out_specs=[pl.BlockSpec((1, tm, tn), lambda b, i, k: (b, i, jnp.minimum(k, n_qv - 1))),
                   pl.BlockSpec((1, tm, tn), lambda b, i, k: (b, i, jnp.clip(k - n_qv, 0, n_main - n_qv - 1))),
                   pl.BlockSpec((1, tn, tm), lambda b, i, k: (b, jnp.maximum(k - n_main, 0), i)),
                   pl.BlockSpec((1, tm, LANES), lambda b, i, k: (b, i, 0))],
        out_shape=[jax.ShapeDtypeStruct((grp, t, 2 * D_C), BF16),
                   jax.ShapeDtypeStruct((grp, t, D_C), F32),
                   jax.ShapeDtypeStruct((grp, D_C, t), BF16),
                   jax.ShapeDtypeStruct((grp, t, LANES), F32)],
        scratch_shapes=[pltpu.VMEM((tm, d), BF16)],
        compiler_params=_params("arbitrary", "arbitrary", "arbitrary"),
        name="mixer_c_in_proj",
    )(*args)


def _ctx_attn_kernel(q_ref, k_ref, v_ref, o_ref):
    s = _dot_nt(q_ref[0].astype(BF16), k_ref[0].astype(BF16))
    p = jnp.exp(s - jnp.max(s, axis=-1, keepdims=True))
    o = _dot(p.astype(BF16), v_ref[0].astype(BF16)) / jnp.sum(p, axis=-1, keepdims=True)
    o_ref[0] = o.astype(BF16)


def _ctx_attention(qkv):
    b, l, _ = qkv.shape
    blk = lambda off: pl.BlockSpec((1, l, HD_A), lambda i, h: (i, 0, off + h))
    return pl.pallas_call(
        _ctx_attn_kernel,
        grid=(b, H_A),
        in_specs=[blk(0), blk(H_A), blk(2 * H_A)],
        out_specs=pl.BlockSpec((1, l, HD_A), lambda i, h: (i, 0, h)),
        out_shape=jax.ShapeDtypeStruct((b, l, D_A), BF16),
        compiler_params=_params("arbitrary", "arbitrary"),
        name="ctx_attention",
    )(qkv, qkv, qkv)


def _na_bias_tables(rpb, rows):
    nblk = rows // NA_ROWS
    n_dr, n_dc = 2 * WIN_R - 1, 2 * WIN_C - 1
    c = np.arange(GRID_W)[:, None]
    kc = np.arange(GRID_W)[None, :]
    dc = np.clip(kc - c + WIN_C - 1, 0, n_dc - 1)
    pick_c = (dc[None] == np.arange(n_dc)[:, None, None]).astype(np.float32)
    c0 = np.clip(c - WIN_C // 2, 0, GRID_W - WIN_C)
    ok_c = (kc >= c0) & (kc < c0 + WIN_C)
    by_col = jnp.einsum("hrd,dck->hrck", rpb, pick_c, precision=lax.Precision.HIGHEST)
    tables = []
    for i in (0, 1, nblk - 1):
        ks = int(np.clip(NA_ROWS * i - WIN_R // 2, 0, rows - NA_KEY_ROWS))
        r = (NA_ROWS * i + np.arange(NA_ROWS))[:, None]
        kr = (ks + np.arange(NA_KEY_ROWS))[None, :]
        r0 = np.clip(r - WIN_R // 2, 0, rows - WIN_R)
        ok_r = (kr >= r0) & (kr < r0 + WIN_R)
        dr = np.clip(kr - r + WIN_R - 1, 0, n_dr - 1)
        pick_r = (dr[None] == np.arange(n_dr)[:, None, None]).astype(np.float32)
        bias = jnp.einsum("hrck,rqs->hqcsk", by_col, pick_r, precision=lax.Precision.HIGHEST)
        ok = ok_r[None, :, None, :, None] & ok_c[None, None, :, None, :]
        bias = jnp.where(ok, bias, -jnp.inf)
        tables.append(bias.reshape(H_A, NA_ROWS * GRID_W, NA_KEY_ROWS * GRID_W))
    return jnp.stack(tables, axis=1)


def _na_kernel(q_ref, k_ref, v_ref, ck_ref, cv_ref, bias_ref, o_ref, *, rows):
    i = pl.program_id(2)
    nblk = rows // NA_ROWS
    ks = jnp.clip(NA_ROWS * i - WIN_R // 2, 0, rows - NA_KEY_ROWS)
    start = pl.multiple_of(ks * GRID_W, GRID_W)
    nkey = NA_KEY_ROWS * GRID_W
    kind = jnp.where(i == 0, 0, jnp.where(i == nblk - 1, 2, 1))
    ones_w = jnp.ones((nkey, HD_A), BF16)
    ones_c = jnp.ones((ck_ref.shape[1], HD_A), BF16)
    for hh in range(NA_HEADS):
        sl = slice(hh * HD_A, (hh + 1) * HD_A)
        q = q_ref[0, :, sl]
        kw = k_ref[0, pl.ds(start, nkey), sl]
        vw = jnp.concatenate([v_ref[0, pl.ds(start, nkey), sl], ones_w], axis=1)
        vc = jnp.concatenate([cv_ref[0, :, sl], ones_c], axis=1)
        s_w = _dot_nt(q, kw) + bias_ref[hh, kind]
        s_c = _dot_nt(q, ck_ref[0, :, sl])
        m = jnp.maximum(jnp.max(s_w, axis=-1, keepdims=True), jnp.max(s_c, axis=-1, keepdims=True))
        p_w = jnp.exp(s_w - m).astype(BF16)
        p_c = jnp.exp(s_c - m).astype(BF16)
        o = _dot(p_w, vw) + _dot(p_c, vc)
        o_ref[0, :, sl] = (o[:, :HD_A] / o[:, HD_A:]).astype(BF16)


def _neighbourhood_attention(qkv, ck, cv, rpb):
    b, t, _ = qkv.shape
    l = ck.shape[1]
    rows = t // GRID_W
    nblk = rows // NA_ROWS
    tq = NA_ROWS * GRID_W
    bias = _na_bias_tables(rpb, rows)
    hw = NA_HEADS * HD_A
    ngrp = H_A // NA_HEADS
    full = lambda off: pl.BlockSpec((1, t, hw), lambda h, n, i: (n, 0, off + h))
    ctx = pl.BlockSpec((1, l, hw), lambda h, n, i: (n, 0, h))
    return pl.pallas_call(
        functools.partial(_na_kernel, rows=rows),
        grid=(ngrp, b, nblk),
        in_specs=[
            pl.BlockSpec((1, tq, hw), lambda h, n, i: (n, i, h)),
            full(ngrp), full(2 * ngrp), ctx, ctx,
            pl.BlockSpec((NA_HEADS, 3, tq, NA_KEY_ROWS * GRID_W), lambda h, n, i: (h, 0, 0, 0)),
        ],
        out_specs=pl.BlockSpec((1, tq, hw), lambda h, n, i: (n, i, h)),
        out_shape=jax.ShapeDtypeStruct((b, t, D_A), BF16),
        compiler_params=_params("arbitrary", "arbitrary", "arbitrary"),
        name="neighbourhood_attention",
    )(qkv, qkv, qkv, ck, cv, bias)


def _lru_kernel(xb_ref, gb_ref, cw_ref, cb_ref, w4h_ref, b4_ref, lam_ref, h0_ref, o_ref, fin_ref,
                xpad_scr, af_scr, uf_scr, ab_scr, ub_scr, *, t, chunk):
    seg = t // LRU_SEGS
    pitch = seg + SUBLANES
    pad = SUBLANES
    zeros = jnp.zeros((pad, BD_B), F32)
    xpad_scr[0:pad, :] = zeros
    xpad_scr[pad + t:2 * pad + t, :] = zeros
    xpad_scr[pad:pad + t, :] = xb_ref[0]

    c1 = (-0.5 * LRU_C) * _softplus(-lam_ref[...])
    w4h = w4h_ref[0]
    b4h = 0.5 * b4_ref[0]
    left = (CONV_W - 1) // 2
    for c in range(t // chunk):
        base = c * chunk
        xc = cb_ref[...]
        for k in range(CONV_W):
            xc = xc + xpad_scr[pl.ds(base + pad - left + k, chunk), :] * cw_ref[k:k + 1, :]
        th = jnp.tanh(_dot(xc.astype(BF16), w4h) + b4h)
        xh = 0.5 * xc
        for d, (a_scr, u_scr) in enumerate(((af_scr, uf_scr), (ab_scr, ub_scr))):
            tr = th[:, 2 * d * BD_B:(2 * d + 1) * BD_B]
            ti = th[:, (2 * d + 1) * BD_B:(2 * d + 2) * BD_B]
            log_a = c1[d:d + 1, :] * tr + c1[d:d + 1, :]
            a = jnp.exp(log_a)
            u = jnp.sqrt(-jnp.tanh(log_a) * (a * a + 1.0)) * (xh * ti + xh)
            for p in range(chunk // seg):
                s = (base + p * seg) // seg
                a_scr[s * pitch:s * pitch + seg, :] = a[p * seg:(p + 1) * seg, :]
                u_scr[s * pitch:s * pitch + seg, :] = u[p * seg:(p + 1) * seg, :]

    def scan(j, carry):
        hf, pf, hb, pb = carry
        rf = pl.ds(j, LRU_SEGS, stride=pitch)
        rb = pl.ds(seg - 1 - j, LRU_SEGS, stride=pitch)
        a = af_scr[rf, :]
        hf = a * hf + uf_scr[rf, :]
        pf = a * pf
        uf_scr[rf, :] = hf
        af_scr[rf, :] = pf
        a = ab_scr[rb, :]
        hb = a * hb + ub_scr[rb, :]
        pb = a * pb
        ub_scr[rb, :] = hb
        ab_scr[rb, :] = pb
        return hf, pf, hb, pb

    zero = jnp.zeros((LRU_SEGS, BD_B), F32)
    one = jnp.ones((LRU_SEGS, BD_B), F32)
    hf, pf, hb, pb = lax.fori_loop(0, seg, scan, (zero, one, zero, one), unroll=4)

    cf = h0_ref[0, 0:1, :]
    cin_f = []
    for s in range(LRU_SEGS):
        cin_f.append(cf)
        cf = hf[s:s + 1, :] + pf[s:s + 1, :] * cf
    cb = h0_ref[0, 1:2, :]
    cin_b = [None] * LRU_SEGS
    for s in reversed(range(LRU_SEGS)):
        cin_b[s] = cb
        cb = hb[s:s + 1, :] + pb[s:s + 1, :] * cb
    fin_ref[0, 0:1, :] = cf
    fin_ref[0, 1:2, :] = cb

    for s in range(LRU_SEGS):
        rows = slice(s * pitch, s * pitch + seg)
        hs = (uf_scr[rows, :] + af_scr[rows, :] * cin_f[s]) + (ub_scr[rows, :] + ab_scr[rows, :] * cin_b[s])
        nat = slice(s * seg, (s + 1) * seg)
        o_ref[0, nat, :] = (hs * jax.nn.gelu(gb_ref[0, nat, :])).astype(BF16)


def _rglru(xg, conv_w, conv_b, w4h, b4, lam, h0):
    b, t, _ = xg.shape
    chunk = min(t, 512)
    seg = t // LRU_SEGS
    col = lambda off: pl.BlockSpec((1, t, BD_B), lambda n, h: (n, 0, off + h))
    return pl.pallas_call(
        functools.partial(_lru_kernel, t=t, chunk=chunk),
        grid=(b, H_B),
        in_specs=[
            col(0), col(H_B),
            pl.BlockSpec((CONV_W, BD_B), lambda n, h: (0, h)),
            pl.BlockSpec((1, BD_B), lambda n, h: (0, h)),
            pl.BlockSpec((1, BD_B, 4 * BD_B), lambda n, h: (h, 0, 0)),
            pl.BlockSpec((1, 1, 4 * BD_B), lambda n, h: (h, 0, 0)),
            pl.BlockSpec((2, BD_B), lambda n, h: (0, h)),
            pl.BlockSpec((1, 2, BD_B), lambda n, h: (n, 0, h)),
        ],
        out_specs=[pl.BlockSpec((1, t, BD_B), lambda n, h: (n, 0, h)),
                   pl.BlockSpec((1, 2, BD_B), lambda n, h: (n, 0, h))],
        out_shape=[jax.ShapeDtypeStruct((b, t, D_RNN), BF16),
                   jax.ShapeDtypeStruct((b, 2, D_RNN), F32)],
        scratch_shapes=[pltpu.VMEM((t + 2 * SUBLANES, BD_B), F32)]
        + [pltpu.VMEM((LRU_SEGS * (seg + SUBLANES), BD_B), F32)] * 4,
        compiler_params=_params("arbitrary", "arbitrary"),
        name="rglru_branch",
    )(xg, xg, conv_w, conv_b.reshape(1, D_RNN), w4h, b4, lam, h0)


def _outproj_ab_kernel(x_ref, mod_ref, oa_ref, ob_ref, w_ref, o_ref):
    y = _dot(oa_ref[0], w_ref[0:D_A, :]) + _dot(ob_ref[0], w_ref[D_A:D_A + D_RNN, :])
    o_ref[0] = x_ref[0] + mod_ref[0, 5:6, :] * y


def _outproj_ab(x, mod, oa, ob, w):
    grp, t, d = x.shape
    tm = TOKEN_TILE
    tok = lambda width: pl.BlockSpec((1, tm, width), lambda b, i: (b, i, 0))
    return pl.pallas_call(
        _outproj_ab_kernel,
        grid=(grp, t // tm),
        in_specs=[
            tok(d),
            pl.BlockSpec((1, N_MOD, d), lambda b, i: (b, 0, 0)),
            tok(D_A), tok(D_RNN),
            pl.BlockSpec((D_A + D_RNN, d), lambda b, i: (0, 0)),
        ],
        out_specs=tok(d),
        out_shape=jax.ShapeDtypeStruct(x.shape, F32),
        compiler_params=_params("arbitrary", "arbitrary"),
        name="mixer_ab_out_proj",
    )(x, mod, oa, ob, w)


def _gate_prep_kernel(g_ref, b_ref, o_ref):
    t = g_ref.shape[2]
    g = g_ref[0] + b_ref[...]
    lane = lax.broadcasted_iota(jnp.int32, (H_C, t), 1) & (CHUNK - 1)
    cum = _log_sigmoid(g[H_C:2 * H_C])
    suf = _log_sigmoid(g[3 * H_C:4 * H_C])
    sh = 1
    while sh < CHUNK:
        cum = cum + jnp.where(lane >= sh, pltpu.roll(cum, sh, 1), 0.0)
        suf = suf + jnp.where(lane < CHUNK - sh, pltpu.roll(suf, t - sh, 1), 0.0)
        sh *= 2
    o_ref[0, 0:H_C] = g[0:H_C]
    o_ref[0, H_C:2 * H_C] = cum
    o_ref[0, 2 * H_C:3 * H_C] = g[2 * H_C:3 * H_C]
    o_ref[0, 3 * H_C:4 * H_C] = suf


def _gate_prep(g_rows, b_gate):
    b, r, t = g_rows.shape
    return pl.pallas_call(
        _gate_prep_kernel,
        grid=(b,),
        in_specs=[pl.BlockSpec((1, r, t), lambda n: (n, 0, 0)), pl.BlockSpec((r, 1), lambda n: (0, 0))],
        out_specs=pl.BlockSpec((1, r, t), lambda n: (n, 0, 0)),
        out_shape=jax.ShapeDtypeStruct(g_rows.shape, F32),
        compiler_params=_params("arbitrary"),
        name="mlstm_gate_prep",
    )(g_rows, b_gate.reshape(r, 1))


def _mlstm_chunk(q, kt, v_aug, ig_row, bc_row, bc_col, cm, m, reverse):
    ti = lax.broadcasted_iota(jnp.int32, (CHUNK, CHUNK), 0)
    si = lax.broadcasted_iota(jnp.int32, (CHUNK, CHUNK), 1)
    seen = (si >= ti) if reverse else (si <= ti)
    bk = bc_row[:, 0:1] if reverse else bc_row[:, CHUNK - 1:CHUNK]
    bc_rep = jnp.broadcast_to(bc_col, (CHUNK, CHUNK))
    dmat = jnp.where(seen, bc_rep - bc_row + ig_row, -jnp.inf)
    inter = bc_rep + m
    m_t = jnp.maximum(inter, jnp.broadcast_to(jnp.max(dmat, axis=1, keepdims=True), (CHUNK, CHUNK)))
    w = jnp.exp(dmat - m_t)
    g = jnp.exp(inter - m_t)
    kscale = DK_C ** -0.5
    s = _dot(q, kt) * (w * kscale)
    tot = _dot(s.astype(BF16), v_aug) + jnp.concatenate([g, g, g], axis=1) * _dot(q, cm.astype(BF16))
    inv = 1.0 / jnp.maximum(jnp.abs(tot[:, DV_C:]), jnp.exp(-m_t))
    hout = tot[:, :DV_C] * jnp.concatenate([inv, inv], axis=1)
    dend = bk - bc_row + ig_row
    m_new = jnp.maximum(bk + m, jnp.max(dend, axis=1, keepdims=True))
    we = jnp.exp(dend - m_new)
    ge = jnp.exp(bk + m - m_new)
    kwt = kt.astype(F32) * (we * kscale)
    c_new = ge * cm + _dot(kwt.astype(BF16), v_aug)
    return hout, c_new, m_new


def _mlstm_kernel(*refs, zero_state):
    if zero_state:
        qvf_ref, vvf_ref, ktf_ref, qvb_ref, vvb_ref, ktb_ref, grf_ref, gcf_ref, grb_ref, gcb_ref = refs[:10]
        rest = refs[10:]
    else:
        m0_ref, qvf_ref, vvf_ref, ktf_ref, qvb_ref, vvb_ref, ktb_ref, grf_ref, gcf_ref, grb_ref, gcb_ref = refs[:11]
        c0_ref, n0_ref = refs[11:13]
        rest = refs[13:]
    hf_ref, hb_ref, cfin_ref, nfin_ref, mfin_ref, c_scr, m_scr = rest
    n, hp, c = pl.program_id(0), pl.program_id(1), pl.program_id(2)
    eye = (lax.broadcasted_iota(jnp.int32, (DK_C, DK_C), 0) == lax.broadcasted_iota(jnp.int32, (DK_C, DK_C), 1))

    @pl.when(c == 0)
    def _():
        if zero_state:
            c_scr[...] = jnp.zeros_like(c_scr)
            m_scr[...] = jnp.zeros_like(m_scr)
            return
        for d in range(2):
            for j in range(MLSTM_HEADS):
                c_scr[d, j, :, 0:DV_C] = c0_ref[0, d, j]
                n_col = jnp.sum(jnp.where(eye, n0_ref[0, d, j], 0.0), axis=1, keepdims=True)
                c_scr[d, j, :, DV_C:DV_C + LANES] = jnp.broadcast_to(n_col, (DK_C, LANES))
                m_scr[d, j] = jnp.full(m_scr.shape[2:], m0_ref[n, d * H_C + hp * MLSTM_HEADS + j], F32)

    ones_col = jnp.ones((CHUNK, LANES), BF16)
    dirs = ((qvf_ref, vvf_ref, ktf_ref, grf_ref, gcf_ref, hf_ref), (qvb_ref, vvb_ref, ktb_ref, grb_ref, gcb_ref, hb_ref))
    chains = []
    for d, (q_ref, v_ref, kt_ref, gr_ref, gc_ref, h_ref) in enumerate(dirs):
        for j in range(MLSTM_HEADS):
            sl = slice(j * DK_C, (j + 1) * DK_C)
            v_aug = jnp.concatenate([v_ref[0, :, sl], ones_col], axis=1)
            chains.append((d, j, h_ref, sl, _mlstm_chunk(
                q_ref[0, :, sl], kt_ref[0, sl, :], v_aug,
                gr_ref[0, j, 2 * d:2 * d + 1, :], gr_ref[0, j, 2 * d + 1:2 * d + 2, :],
                gc_ref[0, 0, :, (2 * d + 1) * MLSTM_HEADS + j:(2 * d + 1) * MLSTM_HEADS + j + 1],
                c_scr[d, j], m_scr[d, j, 0:1, 0:1], reverse=(d == 1))))
    for d, j, h_ref, sl, (hout, c_new, m_new) in chains:
        h_ref[0, :, sl] = hout
        c_scr[d, j] = c_new
        m_scr[d, j] = jnp.broadcast_to(m_new, m_scr.shape[2:])

    @pl.when(c == pl.num_programs(2) - 1)
    def _():
        for d in range(2):
            for j in range(MLSTM_HEADS):
                cfin_ref[0, d, j] = c_scr[d, j, :, 0:DV_C]
                n_col = c_scr[d, j, :, DV_C:DV_C + 1]
                nfin_ref[0, d, j] = jnp.sum(jnp.where(eye, n_col, 0.0), axis=0, keepdims=True)
                mfin_ref[0, d, j] = m_scr[d, j, 0:1, :]


def _rope_tables(t):
    half = DK_C // 2
    nf = half // 2
    pos = jnp.arange(t)
    freqs = ROPE_BASE ** (-jnp.arange(nf, dtype=F32) / nf)
    cos, sin = [], []
    for p in (pos // GRID_W, pos % GRID_W):
        ang = p.astype(F32)[:, None] * freqs[None, :]
        cos += [jnp.cos(ang), jnp.cos(ang)]
        sin += [-jnp.sin(ang), jnp.sin(ang)]
    return jnp.concatenate(cos, axis=-1), jnp.concatenate(sin, axis=-1)


def _mlstm(qv, kt, gates, c0, n0, m0):
    b, t, _ = qv.shape
    nc = t // CHUNK
    hps = MLSTM_HEADS
    ngrp = H_C // hps
    g4 = gates.reshape(b, 4, H_C, t)
    g_row = jnp.transpose(g4, (0, 2, 1, 3))
    g_col = jnp.transpose(gates.reshape(b, 4, ngrp, hps, t), (0, 2, 4, 1, 3)).reshape(b, ngrp, t, 4 * hps)
    fwd = lambda c: c
    bwd = lambda c: nc - 1 - c

    def tok(off, order):
        return pl.BlockSpec((1, CHUNK, hps * DK_C), lambda n, h, c: (n, order(c), off + h))

    def kt_spec(order):
        return pl.BlockSpec((1, hps * DK_C, CHUNK), lambda n, h, c: (n, h, order(c)))

    def g_row_spec(order):
        return pl.BlockSpec((1, hps, 4, CHUNK), lambda n, h, c: (n, h, 0, order(c)))

    def g_col_spec(order):
        return pl.BlockSpec((1, 1, CHUNK, 4 * hps), lambda n, h, c: (n, h, order(c), 0))

    state = lambda *tail: pl.BlockSpec((1, 2, hps) + tail, lambda n, h, c: (n, 0, h) + (0,) * len(tail))
    hspec = lambda order: pl.BlockSpec((1, CHUNK, hps * DV_C), lambda n, h, c: (n, order(c), h))
    in_specs = [tok(0, fwd), tok(ngrp, fwd), kt_spec(fwd),
                tok(0, bwd), tok(ngrp, bwd), kt_spec(bwd),
                g_row_spec(fwd), g_col_spec(fwd), g_row_spec(bwd), g_col_spec(bwd)]
    args = [qv, qv, kt, qv, qv, kt, g_row, g_col, g_row, g_col]
    zero_state = c0 is None
    if not zero_state:
        in_specs = [pl.BlockSpec(memory_space=pltpu.SMEM)] + in_specs + [state(DK_C, DV_C), state(1, DK_C)]
        args = [m0.reshape(b, 2 * H_C)] + args + [c0, n0.reshape(b, 2, H_C, 1, DK_C)]
    hf, hb, cfin, nfin, mfin = pl.pallas_call(
        functools.partial(_mlstm_kernel, zero_state=zero_state),
        grid=(b, ngrp, nc),
        in_specs=in_specs,
        out_specs=[hspec(fwd), hspec(bwd), state(DK_C, DV_C), state(1, DK_C), state(1, LANES)],
        out_shape=[jax.ShapeDtypeStruct((b, t, D_C), F32), jax.ShapeDtypeStruct((b, t, D_C), F32),
                   jax.ShapeDtypeStruct((b, 2, H_C, DK_C, DV_C), F32),
                   jax.ShapeDtypeStruct((b, 2, H_C, 1, DK_C), F32),
                   jax.ShapeDtypeStruct((b, 2, H_C, 1, LANES), F32)],
        scratch_shapes=[pltpu.VMEM((2, hps, DK_C, DV_C + LANES), F32),
                        pltpu.VMEM((2, hps, SUBLANES, LANES), F32)],
        compiler_params=_params("arbitrary", "arbitrary", "arbitrary"),
        name="mlstm_scan",
    )(*args)
    return hf, hb, cfin, nfin.reshape(b, 2, H_C, DK_C), mfin[:, :, :, 0, 0]


def _outproj_c_kernel(x_ref, mod_ref, hf_ref, hb_ref, og_ref, g_ref, w_ref, o_ref, hn_scr):
    for h in range(H_C):
        sl = slice(h * DV_C, (h + 1) * DV_C)
        hs = hf_ref[0, :, sl] + hb_ref[0, :, sl]
        hn_scr[:, sl] = (_rms(hs, g_ref[:, sl]) * jax.nn.sigmoid(og_ref[0, :, sl])).astype(BF16)
    o_ref[0] = x_ref[0] + mod_ref[0, 5:6, :] * _dot(hn_scr[...], w_ref[...])


def _outproj_c(x, mod, hf, hb, og, mh_g, w):
    grp, t, d = x.shape
    tm = TOKEN_TILE // 2
    tok = pl.BlockSpec((1, tm, d), lambda b, i: (b, i, 0))
    return pl.pallas_call(
        _outproj_c_kernel,
        grid=(grp, t // tm),
        in_specs=[
            tok,
            pl.BlockSpec((1, N_MOD, d), lambda b, i: (b, 0, 0)),
            tok, tok,
            pl.BlockSpec((1, tm, D_C), lambda b, i: (b, i, 0)),
            pl.BlockSpec((1, D_C), lambda b, i: (0, 0)),
            pl.BlockSpec((D_C, d), lambda b, i: (0, 0)),
        ],
        out_specs=tok,
        out_shape=jax.ShapeDtypeStruct(x.shape, F32),
        scratch_shapes=[pltpu.VMEM((tm, D_C), BF16)],
        compiler_params=_params("arbitrary", "arbitrary"),
        name="mixer_c_out_proj",
    )(x, mod, hf, hb, og, mh_g.reshape(1, D_C), w)


def _mixer_ab(j, xp, xs, mp, ms, g, p):
    bp, lp, bs = p["bp"], p["lp"], xs.shape[0]
    w_in = p["w_in_ab"][j].astype(BF16)
    w_out = p["w_out_ab"][j].astype(BF16)
    gains = jnp.stack([jnp.tile(p["qn_g"][j] * (HD_A ** -0.5), H_A), jnp.tile(p["kn_g"][j], H_A),
                       jnp.ones((D_A,), F32)]).reshape(3, 1, D_A)
    wa, wx, ba, bx = p["lru_wa"][j], p["lru_wx"][j], p["lru_ba"][j], p["lru_bx"][j]
    w4 = (0.5 * jnp.concatenate([wa[0], wx[0], wa[1], wx[1]], axis=-1)).astype(BF16)
    b4 = jnp.concatenate([v.reshape(H_B, 1, BD_B) for v in (ba[0], bx[0], ba[1], bx[1])], axis=-1)
    lru = (p["conv_w"][j], p["conv_b"][j], w4, b4, p["lru_lam"][j])

    qkv, xg = _inproj_ab(xp, mp, g, w_in, gains, F32)
    qkv = qkv.reshape(bp, lp, 3 * D_A)
    oa = _ctx_attention(qkv)
    ob, h_fin = _rglru(xg.reshape(bp, lp, 2 * D_RNN), *lru, jnp.zeros((bp, 2, D_RNN), F32))
    xp = _outproj_ab(xp, mp, oa.reshape(1, bp * lp, D_A), ob.reshape(1, bp * lp, D_RNN), w_out)
    new = (qkv[:, :, D_A:2 * D_A].reshape(bp, lp, H_A, HD_A), qkv[:, :, 2 * D_A:].reshape(bp, lp, H_A, HD_A), h_fin)

    past = p["cache_k"].shape[2]
    ck = p["cache_k"][:, j].reshape(bs, past, D_A).astype(BF16)
    cv = p["cache_v"][:, j].reshape(bs, past, D_A).astype(BF16)
    qkv, xg = _inproj_ab(xs, ms, g, w_in, gains, BF16)
    oa = _neighbourhood_attention(qkv, ck, cv, p["rpb"][j])
    ob, _ = _rglru(xg, *lru, p["state_lru"][:, j])
    xs = _outproj_ab(xs, ms, oa, ob, w_out)
    return xp, xs, new


def _mixer_c(j, xp, xs, mp, ms, g, p):
    bp, lp, bs, ts = p["bp"], p["lp"], xs.shape[0], xs.shape[1]
    w = p["w_in_c"][j].astype(BF16)
    wkt = w[:, D_C:2 * D_C].T
    w_gate = jnp.pad(w[:, 4 * D_C:], ((0, 0), (0, LANES - 4 * H_C)))
    w_out = p["w_out_c"][j].astype(BF16)

    def gates_of(zg, b, t):
        g_rows = jnp.transpose(zg.reshape(b, t, LANES)[:, :, :4 * H_C], (0, 2, 1))
        return _gate_prep(g_rows, p["b_gate_c"][j])

    qv, og, kt, zg = _inproj_c(xp, mp, g, w, wkt, w_gate, None)
    kt = jnp.transpose(kt.reshape(D_C, bp, lp), (1, 0, 2))
    hf, hb, cfin, nfin, mfin = _mlstm(
        qv.reshape(bp, lp, 2 * D_C), kt, gates_of(zg, bp, lp), None, None, None)
    xp = _outproj_c(xp, mp, hf.reshape(1, bp * lp, D_C), hb.reshape(1, bp * lp, D_C), og, p["mh_norm_g"][j], w_out)

    qv, og, kt, zg = _inproj_c(xs, ms, g, w, wkt, w_gate, _rope_tables(ts))
    hf, hb, _, _, _ = _mlstm(qv, kt, gates_of(zg, bs, ts), p["state_mlstm_c"][:, j], p["state_mlstm_n"][:, j],
                             p["state_mlstm_m"][:, j])
    xs = _outproj_c(xs, ms, hf, hb, og, p["mh_norm_g"][j], w_out)
    return xp, xs, (cfin, nfin, mfin)


def kernel(x_prompt, x_sample, cache_k, cache_v, state_lru, state_mlstm_c, state_mlstm_n, state_mlstm_m, c, c_ctx, w_mod, b_mod, norm_g, w_ffn_gate, w_ffn_up, w_ffn_down, w_in_ab, w_out_ab, qn_g, kn_g, rpb, conv_w, conv_b, lru_wa, lru_ba, lru_wx, lru_bx, lru_lam, w_in_c, b_gate_c, mh_norm_g, w_out_c):
    bp, lp, d = x_prompt.shape
    bs = x_sample.shape[0]
    p = dict(bp=bp, lp=lp, cache_k=cache_k, cache_v=cache_v, state_lru=state_lru, state_mlstm_c=state_mlstm_c,
             state_mlstm_n=state_mlstm_n, state_mlstm_m=state_mlstm_m, w_in_ab=w_in_ab, w_out_ab=w_out_ab,
             qn_g=qn_g, kn_g=kn_g, rpb=rpb, conv_w=conv_w, conv_b=conv_b, lru_wa=lru_wa, lru_ba=lru_ba,
             lru_wx=lru_wx, lru_bx=lru_bx, lru_lam=lru_lam, w_in_c=w_in_c, b_gate_c=b_gate_c,
             mh_norm_g=mh_norm_g, w_out_c=w_out_c)

    cond = jnp.concatenate([c, c_ctx[None, :]], axis=0)
    cond = jnp.pad(cond, ((0, -(bs + 1) % SUBLANES), (0, 0)))
    mod = _modulation(cond, w_mod, b_mod)[:, :bs + 1].reshape(DEPTH, bs + 1, N_MOD, d)

    wg = w_ffn_gate.astype(BF16)
    wu = w_ffn_up.astype(BF16)
    wd = w_ffn_down.astype(BF16)

    xp = x_prompt.reshape(1, bp * lp, d)
    xs = x_sample
    new_ab, new_c = [], []
    for l in range(DEPTH):
        j = l // 2
        ms = mod[l, :bs]
        mp = mod[l, bs:]
        xp = _ffn(xp, mp, norm_g[l, 0], wg, wu, wd, l, 0, 0)
        xs = _ffn(xs, ms, norm_g[l, 0], wg, wu, wd, l, 0, 0)
        if l % 2 == 0:
            xp, xs, new = _mixer_ab(j, xp, xs, mp, ms, norm_g[l, 1], p)
            new_ab.append(new)
        else:
            xp, xs, new = _mixer_c(j, xp, xs, mp, ms, norm_g[l, 1], p)
            new_c.append(new)
        xp = _ffn(xp, mp, norm_g[l, 2], wg, wu, wd, l, 1, 2)
        xs = _ffn(xs, ms, norm_g[l, 2], wg, wu, wd, l, 1, 2)
    stack = lambda items, i: jnp.concatenate([jnp.expand_dims(it[i], 1) for it in items], axis=1)
    return (xp.reshape(bp, lp, d), xs, stack(new_ab, 0), stack(new_ab, 1), stack(new_ab, 2),
            stack(new_c, 0), stack(new_c, 1), stack(new_c, 2))
```

```python
import functools

import numpy as np
import jax
import jax.numpy as jnp
from jax import lax
from jax.experimental import pallas as pl
from jax.experimental.pallas import tpu as pltpu

D_MODEL = 2048
DEPTH = 2
GRID_W = 64
N_MOD = 9
D_FF = 5632
EPS = 1e-6
H_A = 8
HD_A = 128
D_A = H_A * HD_A
WIN_R = 8
WIN_C = 16
D_RNN = 1024
H_B = 8
BD_B = D_RNN // H_B
CONV_W = 4
LRU_C = 8.0
H_C = 8
DK_C = 256
DV_C = 256
D_C = H_C * DV_C
CHUNK = 128
ROPE_BASE = 10000.0

BF16 = jnp.bfloat16
F32 = jnp.float32

V7X_VMEM_BYTES = 64 * 1024 * 1024
VMEM_LIMIT = V7X_VMEM_BYTES - 8 * 1024 * 1024
FFN_VMEM_LIMIT = V7X_VMEM_BYTES - 5 * 1024 * 1024
LANES = 128
SUBLANES = 8

TOKEN_TILE = 512
FF_TILE = 512
FFN_NORM_ROWS = 128
PROJ_TOKEN_TILE = 1024
PROJ_TILE = 512
NA_ROWS = 4
NA_KEY_ROWS = NA_ROWS + WIN_R
LRU_SEGS = SUBLANES
NA_HEADS = 4
MLSTM_HEADS = 4


def _params(*sem, vmem=VMEM_LIMIT):
    return pltpu.CompilerParams(dimension_semantics=sem, vmem_limit_bytes=vmem)


def _dot(a, b):
    return jnp.dot(a, b, preferred_element_type=F32)


def _dot_nt(a, b):
    return lax.dot_general(a, b, (((1,), (1,)), ((), ())), preferred_element_type=F32)


def _dot_tn(a, b):
    return lax.dot_general(a, b, (((0,), (0,)), ((), ())), preferred_element_type=F32)


def _rms(x, g):
    return x * lax.rsqrt(jnp.mean(x * x, axis=-1, keepdims=True) + EPS) * g


def _adaln(x, g, mod_ref, j):
    return _rms(x, g) * (1.0 + mod_ref[0, 3 * j + 1:3 * j + 2, :]) + mod_ref[0, 3 * j:3 * j + 1, :]


def _softplus(x):
    return jnp.maximum(x, 0.0) + jnp.log1p(jnp.exp(-jnp.abs(x)))


def _log_sigmoid(x):
    return -_softplus(-x)


def _mod_kernel(c_ref, w_ref, b_ref, o_ref):
    c = c_ref[...]
    s = (c * jax.nn.sigmoid(c)).astype(BF16)
    o_ref[0] = _dot(s, w_ref[0].astype(BF16)) + b_ref[0]


def _modulation(cond, w_mod, b_mod):
    r = cond.shape[0]
    n = N_MOD * D_MODEL
    tn = 1024
    return pl.pallas_call(
        _mod_kernel,
        grid=(DEPTH, n // tn),
        in_specs=[
            pl.BlockSpec((r, D_MODEL), lambda l, j: (0, 0)),
            pl.BlockSpec((1, D_MODEL, tn), lambda l, j: (l, 0, j)),
            pl.BlockSpec((1, 1, tn), lambda l, j: (l, 0, j)),
        ],
        out_specs=pl.BlockSpec((1, r, tn), lambda l, j: (l, 0, j)),
        out_shape=jax.ShapeDtypeStruct((DEPTH, r, n), F32),
        compiler_params=_params("arbitrary", "arbitrary"),
        name="modulation",
    )(cond, w_mod, b_mod.reshape(DEPTH, 1, n))


def _ffn_kernel(x0_ref, xn_ref, mod_ref, modn_ref, g_ref, wga_ref, wua_ref, wda_ref, wgb_ref, wub_ref, wdb_ref,
                o_ref, ha_scr, hb_scr, xa_scr, xb_scr, acc_scr, *, j, ns, single_last):
    p, f2 = pl.program_id(0), pl.program_id(1)
    rows = xn_ref.shape[1]
    tm = ha_scr.shape[0]

    @pl.when((p == 0) & (f2 == 0))
    def _():
        xa_scr[...] = x0_ref[0]
        ha_scr[...] = _adaln(x0_ref[0], g_ref[...], mod_ref, j).astype(BF16)

    @pl.when((f2 == 0) | (f2 == ns))
    def _():
        acc_scr[...] = jnp.zeros_like(acc_scr)

    def body(h_cur, h_nxt, x_nxt, f, both):
        r0 = pl.multiple_of(jnp.minimum(f, tm // rows - 1) * rows, rows)
        xs = xn_ref[0]
        x_nxt[pl.ds(r0, rows), :] = xs
        h_nxt[pl.ds(r0, rows), :] = _adaln(xs, g_ref[...], modn_ref, j).astype(BF16)
        h = h_cur[...]
        hidden = ((wga_ref, wua_ref, wda_ref), (wgb_ref, wub_ref, wdb_ref)) if both else ((wga_ref, wua_ref, wda_ref),)
        y = None
        for wg_ref, wu_ref, wd_ref in hidden:
            a = _dot(h, wg_ref[...])
            u = _dot(h, wu_ref[...])
            act = (a * jax.nn.sigmoid(a)) * u
            yd = _dot(act.astype(BF16), wd_ref[...])
            y = yd if y is None else y + yd
        acc_scr[...] += y

    n_both = ns - 1 if single_last else ns
    for half, (h_cur, h_nxt, x_nxt) in enumerate(((ha_scr, hb_scr, xb_scr), (hb_scr, ha_scr, xa_scr))):
        @pl.when((f2 >= half * ns) & (f2 < half * ns + n_both))
        def _():
            body(h_cur, h_nxt, x_nxt, f2 - half * ns, True)

        if single_last:
            @pl.when(f2 == half * ns + ns - 1)
            def _():
                body(h_cur, h_nxt, x_nxt, ns - 1, False)

    for last, x_cur in ((ns - 1, xa_scr), (2 * ns - 1, xb_scr)):
        @pl.when(f2 == last)
        def _():
            o_ref[0] = x_cur[...] + 0.5 * mod_ref[0, 3 * j + 2:3 * j + 3, :] * acc_scr[...]


def _ffn(x, mod, g, wg, wu, wd, l, s, j):
    grp, t, d = x.shape
    tm, tf = TOKEN_TILE, FF_TILE
    nf = D_FF // tf
    ns = (nf + 1) // 2
    per_group = t // tm
    ntile = grp * per_group
    rows = FFN_NORM_ROWS
    nsl = tm // rows
    assert ntile % 2 == 0 and ns >= nsl
    tile = lambda p, f2: 2 * p + f2 // ns
    nxt = lambda p, f2: jnp.minimum(tile(p, f2) + 1, ntile - 1)
    fa = lambda f2: jnp.minimum(2 * (f2 % ns), nf - 1)
    fb = lambda f2: jnp.minimum(2 * (f2 % ns) + 1, nf - 1)
    wcol = lambda fsel: pl.BlockSpec((None, None, d, tf), lambda p, f2: (l, s, 0, fsel(f2)))
    wrow = lambda fsel: pl.BlockSpec((None, None, tf, d), lambda p, f2: (l, s, fsel(f2), 0))
    out = pl.pallas_call(
        functools.partial(_ffn_kernel, j=j, ns=ns, single_last=nf % 2 == 1),
        grid=(ntile // 2, 2 * ns),
        in_specs=[
            pl.BlockSpec((1, tm, d), lambda p, f2: (0, 0, 0), pipeline_mode=pl.Buffered(1)),
            pl.BlockSpec((1, rows, d), lambda p, f2: (0, nxt(p, f2) * nsl + jnp.minimum(f2 % ns, nsl - 1), 0)),
            pl.BlockSpec((1, N_MOD, d), lambda p, f2: (tile(p, f2) // per_group, 0, 0)),
            pl.BlockSpec((1, N_MOD, d), lambda p, f2: (nxt(p, f2) // per_group, 0, 0)),
            pl.BlockSpec((1, d), lambda p, f2: (0, 0)),
            wcol(fa), wcol(fa), wrow(fa), wcol(fb), wcol(fb), wrow(fb),
        ],
        out_specs=pl.BlockSpec((1, tm, d), lambda p, f2: (0, tile(p, f2), 0)),
        out_shape=jax.ShapeDtypeStruct((1, grp * t, d), F32),
        scratch_shapes=[pltpu.VMEM((tm, d), BF16), pltpu.VMEM((tm, d), BF16),
                        pltpu.VMEM((tm, d), F32), pltpu.VMEM((tm, d), F32), pltpu.VMEM((tm, d), F32)],
        compiler_params=_params("arbitrary", "arbitrary", vmem=FFN_VMEM_LIMIT),
        name="macaron_ffn",
    )(x.reshape(1, grp * t, d), x.reshape(1, grp * t, d), mod, mod, g.reshape(1, d), wg, wu, wd, wg, wu, wd)
    return out.reshape(x.shape)


def _rope(x, cos, sin):
    half = DK_C // 2
    out = []
    for p in range(2):
        sl = slice(p * half, (p + 1) * half)
        xs = x[:, sl]
        out.append(xs * cos[:, sl] + pltpu.roll(xs, half // 2, 1) * sin[:, sl])
    return jnp.concatenate(out, axis=-1)


def _rope_t(x, cos, sin):
    q = DK_C // 4
    swapped = jnp.concatenate([x[q:2 * q], x[0:q], x[3 * q:4 * q], x[2 * q:3 * q]], axis=0)
    return x * cos + swapped * sin


def _inproj_ab_kernel(x_ref, mod_ref, g_ref, w_ref, gain_ref, oa_ref, ob_ref, h_scr, *, n_a, n_norm):
    k = pl.program_id(2)

    @pl.when(k == 0)
    def _():
        h_scr[...] = _adaln(x_ref[0], g_ref[...], mod_ref, 1).astype(BF16)

    @pl.when(k >= n_a)
    def _():
        ob_ref[0] = _dot(h_scr[...], w_ref[...])

    @pl.when(k < n_norm)
    def _():
        for pair in range(w_ref.shape[1] // (2 * HD_A)):
            z2 = _dot(h_scr[...], w_ref[:, 2 * pair * HD_A:2 * (pair + 1) * HD_A])
            for hh in range(2):
                zs = z2[:, hh * HD_A:(hh + 1) * HD_A]
                sl = slice((2 * pair + hh) * HD_A, (2 * pair + hh + 1) * HD_A)
                r = lax.rsqrt(jnp.mean(zs * zs, axis=-1, keepdims=True) + EPS)
                oa_ref[0, :, sl] = (zs * r * gain_ref[0, :, sl]).astype(oa_ref.dtype)

    @pl.when((k >= n_norm) & (k < n_a))
    def _():
        oa_ref[0] = (_dot(h_scr[...], w_ref[...]) * gain_ref[0]).astype(oa_ref.dtype)


def _inproj_ab(x, mod, g, w, gains, dtype_a):
    grp, t, d = x.shape
    tm, tn = PROJ_TOKEN_TILE, PROJ_TILE
    n_a = 3 * D_A // tn
    nt = w.shape[1] // tn
    return pl.pallas_call(
        functools.partial(_inproj_ab_kernel, n_a=n_a, n_norm=2 * D_A // tn),
        grid=(grp, t // tm, nt),
        in_specs=[
            pl.BlockSpec((1, tm, d), lambda b, i, k: (b, i, 0)),
            pl.BlockSpec((1, N_MOD, d), lambda b, i, k: (b, 0, 0)),
            pl.BlockSpec((1, d), lambda b, i, k: (0, 0)),
            pl.BlockSpec((d, tn), lambda b, i, k: (0, k)),
            pl.BlockSpec((1, 1, tn), lambda b, i, k: (jnp.minimum(k, n_a - 1), 0, 0)),
        ],
        out_specs=[pl.BlockSpec((1, tm, tn), lambda b, i, k: (b, i, jnp.minimum(k, n_a - 1))),
                   pl.BlockSpec((1, tm, tn), lambda b, i, k: (b, i, jnp.maximum(k - n_a, 0)))],
        out_shape=[jax.ShapeDtypeStruct((grp, t, n_a * tn), dtype_a),
                   jax.ShapeDtypeStruct((grp, t, (nt - n_a) * tn), F32)],
        scratch_shapes=[pltpu.VMEM((tm, d), BF16)],
        compiler_params=_params("arbitrary", "arbitrary", "arbitrary"),
        name="mixer_ab_in_proj",
    )(x, mod, g.reshape(1, d), w, gains.reshape(n_a, 1, tn))


def _inproj_c_kernel(*refs, rotary, n_q, n_qv, n_main):
    x_ref, mod_ref, g_ref, w_ref, wkt_ref, wgate_ref = refs[:6]
    pos = 6
    if rotary:
        cos_ref, sin_ref, cost_ref, sint_ref = refs[pos:pos + 4]
        pos += 4
    oa_ref, ob_ref, okt_ref, og_ref, h_scr = refs[pos:pos + 5]
    k = pl.program_id(2)
    tn = w_ref.shape[1]

    @pl.when(k == 0)
    def _():
        h = _adaln(x_ref[0], g_ref[...], mod_ref, 1).astype(BF16)
        h_scr[...] = h
        og_ref[0] = _dot(h, wgate_ref[...])

    @pl.when(k < n_q)
    def _():
        z = _dot(h_scr[...], w_ref[...])
        if rotary:
            for hh in range(tn // DK_C):
                sl = slice(hh * DK_C, (hh + 1) * DK_C)
                oa_ref[0, :, sl] = _rope(z[:, sl], cos_ref[...], sin_ref[...]).astype(BF16)
        else:
            oa_ref[0] = z.astype(BF16)

    @pl.when((k >= n_q) & (k < n_qv))
    def _():
        oa_ref[0] = _dot(h_scr[...], w_ref[...]).astype(BF16)

    @pl.when((k >= n_qv) & (k < n_main))
    def _():
        ob_ref[0] = _dot(h_scr[...], w_ref[...])

    @pl.when(k >= n_main)
    def _():
        zt = _dot_nt(wkt_ref[...], h_scr[...])
        if rotary:
            for hh in range(tn // DK_C):
                sl = slice(hh * DK_C, (hh + 1) * DK_C)
                okt_ref[0, sl, :] = _rope_t(zt[sl, :], cost_ref[...], sint_ref[...]).astype(BF16)
        else:
            okt_ref[0] = zt.astype(BF16)


def _inproj_c(x, mod, g, w, wkt, w_gate, rope_tables):
    grp, t, d = x.shape
    tm, tn = PROJ_TOKEN_TILE, PROJ_TILE
    n_q, n_qv, n_main = D_C // tn, 2 * D_C // tn, 3 * D_C // tn
    nt = n_main + D_C // tn

    def w_col(k):
        kk = jnp.minimum(k, n_main - 1)
        return jnp.where(kk < n_q, kk, kk + n_q)

    rotary = rope_tables is not None
    in_specs = [
        pl.BlockSpec((1, tm, d), lambda b, i, k: (b, i, 0)),
        pl.BlockSpec((1, N_MOD, d), lambda b, i, k: (b, 0, 0)),
        pl.BlockSpec((1, d), lambda b, i, k: (0, 0)),
        pl.BlockSpec((d, tn), lambda b, i, k: (0, w_col(k))),
        pl.BlockSpec((tn, d), lambda b, i, k: (jnp.maximum(k - n_main, 0), 0)),
        pl.BlockSpec((d, LANES), lambda b, i, k: (0, 0)),
    ]
    args = [x, mod, g.reshape(1, d), w, wkt, w_gate]
    if rotary:
        cos, sin = rope_tables
        in_specs += [pl.BlockSpec((tm, DK_C), lambda b, i, k: (i, 0))] * 2
        in_specs += [pl.BlockSpec((DK_C, tm), lambda b, i, k: (0, i))] * 2
        args += [cos, sin, cos.T, sin.T]
    return pl.pallas_call(
        functools.partial(_inproj_c_kernel, rotary=rotary, n_q=n_q, n_qv=n_qv, n_main=n_main),
        grid=(grp, t // tm, nt),
        in_specs=in_specs,
        out_specs=[pl.BlockSpec((1, tm, tn), lambda b, i, k: (b, i, jnp.minimum(k, n_qv - 1))),
                   pl.BlockSpec((1, tm, tn), lambda b, i, k: (b, i, jnp.clip(k - n_qv, 0, n_main - n_qv - 1))),
                   pl.BlockSpec((1, tn, tm), lambda b, i, k: (b, jnp.maximum(k - n_main, 0), i)),
                   pl.BlockSpec((1, tm, LANES), lambda b, i, k: (b, i, 0))],
        out_shape=[jax.ShapeDtypeStruct((grp, t, 2 * D_C), BF16),
                   jax.ShapeDtypeStruct((grp, t, D_C), F32),
                   jax.ShapeDtypeStruct((grp, D_C, t), BF16),
                   jax.ShapeDtypeStruct((grp, t, LANES), F32)],
        scratch_shapes=[pltpu.VMEM((tm, d), BF16)],
        compiler_params=_params("arbitrary", "arbitrary", "arbitrary"),
        name="mixer_c_in_proj",
    )(*args)


def _ctx_attn_kernel(q_ref, k_ref, v_ref, o_ref):
    s = _dot_nt(q_ref[0].astype(BF16), k_ref[0].astype(BF16))
    p = jnp.exp(s - jnp.max(s, axis=-1, keepdims=True))
    o = _dot(p.astype(BF16), v_ref[0].astype(BF16)) / jnp.sum(p, axis=-1, keepdims=True)
    o_ref[0] = o.astype(BF16)


def _ctx_attention(qkv):
    b, l, _ = qkv.shape
    blk = lambda off: pl.BlockSpec((1, l, HD_A), lambda i, h: (i, 0, off + h))
    return pl.pallas_call(
        _ctx_attn_kernel,
        grid=(b, H_A),
        in_specs=[blk(0), blk(H_A), blk(2 * H_A)],
        out_specs=pl.BlockSpec((1, l, HD_A), lambda i, h: (i, 0, h)),
        out_shape=jax.ShapeDtypeStruct((b, l, D_A), BF16),
        compiler_params=_params("arbitrary", "arbitrary"),
        name="ctx_attention",
    )(qkv, qkv, qkv)


def _na_bias_tables(rpb, rows):
    nblk = rows // NA_ROWS
    n_dr, n_dc = 2 * WIN_R - 1, 2 * WIN_C - 1
    c = np.arange(GRID_W)[:, None]
    kc = np.arange(GRID_W)[None, :]
    dc = np.clip(kc - c + WIN_C - 1, 0, n_dc - 1)
    pick_c = (dc[None] == np.arange(n_dc)[:, None, None]).astype(np.float32)
    c0 = np.clip(c - WIN_C // 2, 0, GRID_W - WIN_C)
    ok_c = (kc >= c0) & (kc < c0 + WIN_C)
    by_col = jnp.einsum("hrd,dck->hrck", rpb, pick_c, precision=lax.Precision.HIGHEST)
    tables = []
    for i in (0, 1, nblk - 1):
        ks = int(np.clip(NA_ROWS * i - WIN_R // 2, 0, rows - NA_KEY_ROWS))
        r = (NA_ROWS * i + np.arange(NA_ROWS))[:, None]
        kr = (ks + np.arange(NA_KEY_ROWS))[None, :]
        r0 = np.clip(r - WIN_R // 2, 0, rows - WIN_R)
        ok_r = (kr >= r0) & (kr < r0 + WIN_R)
        dr = np.clip(kr - r + WIN_R - 1, 0, n_dr - 1)
        pick_r = (dr[None] == np.arange(n_dr)[:, None, None]).astype(np.float32)
        bias = jnp.einsum("hrck,rqs->hqcsk", by_col, pick_r, precision=lax.Precision.HIGHEST)
        ok = ok_r[None, :, None, :, None] & ok_c[None, None, :, None, :]
        bias = jnp.where(ok, bias, -jnp.inf)
        tables.append(bias.reshape(H_A, NA_ROWS * GRID_W, NA_KEY_ROWS * GRID_W))
    return jnp.stack(tables, axis=1)


def _na_kernel(q_ref, k_ref, v_ref, ck_ref, cv_ref, bias_ref, o_ref, *, rows):
    i = pl.program_id(2)
    nblk = rows // NA_ROWS
    ks = jnp.clip(NA_ROWS * i - WIN_R // 2, 0, rows - NA_KEY_ROWS)
    start = pl.multiple_of(ks * GRID_W, GRID_W)
    nkey = NA_KEY_ROWS * GRID_W
    kind = jnp.where(i == 0, 0, jnp.where(i == nblk - 1, 2, 1))
    ones_w = jnp.ones((nkey, HD_A), BF16)
    ones_c = jnp.ones((ck_ref.shape[1], HD_A), BF16)
    for hh in range(NA_HEADS):
        sl = slice(hh * HD_A, (hh + 1) * HD_A)
        q = q_ref[0, :, sl]
        kw = k_ref[0, pl.ds(start, nkey), sl]
        vw = jnp.concatenate([v_ref[0, pl.ds(start, nkey), sl], ones_w], axis=1)
        vc = jnp.concatenate([cv_ref[0, :, sl], ones_c], axis=1)
        s_w = _dot_nt(q, kw) + bias_ref[hh, kind]
        s_c = _dot_nt(q, ck_ref[0, :, sl])
        m = jnp.maximum(jnp.max(s_w, axis=-1, keepdims=True), jnp.max(s_c, axis=-1, keepdims=True))
        p_w = jnp.exp(s_w - m).astype(BF16)
        p_c = jnp.exp(s_c - m).astype(BF16)
        o = _dot(p_w, vw) + _dot(p_c, vc)
        o_ref[0, :, sl] = (o[:, :HD_A] / o[:, HD_A:]).astype(BF16)


def _neighbourhood_attention(qkv, ck, cv, rpb):
    b, t, _ = qkv.shape
    l = ck.shape[1]
    rows = t // GRID_W
    nblk = rows // NA_ROWS
    tq = NA_ROWS * GRID_W
    bias = _na_bias_tables(rpb, rows)
    hw = NA_HEADS * HD_A
    ngrp = H_A // NA_HEADS
    full = lambda off: pl.BlockSpec((1, t, hw), lambda h, n, i: (n, 0, off + h))
    ctx = pl.BlockSpec((1, l, hw), lambda h, n, i: (n, 0, h))
    return pl.pallas_call(
        functools.partial(_na_kernel, rows=rows),
        grid=(ngrp, b, nblk),
        in_specs=[
            pl.BlockSpec((1, tq, hw), lambda h, n, i: (n, i, h)),
            full(ngrp), full(2 * ngrp), ctx, ctx,
            pl.BlockSpec((NA_HEADS, 3, tq, NA_KEY_ROWS * GRID_W), lambda h, n, i: (h, 0, 0, 0)),
        ],
        out_specs=pl.BlockSpec((1, tq, hw), lambda h, n, i: (n, i, h)),
        out_shape=jax.ShapeDtypeStruct((b, t, D_A), BF16),
        compiler_params=_params("arbitrary", "arbitrary", "arbitrary"),
        name="neighbourhood_attention",
    )(qkv, qkv, qkv, ck, cv, bias)


def _lru_kernel(xb_ref, gb_ref, cw_ref, cb_ref, w4h_ref, b4_ref, lam_ref, h0_ref, o_ref, fin_ref,
                xpad_scr, af_scr, uf_scr, ab_scr, ub_scr, *, t, chunk):
    seg = t // LRU_SEGS
    pitch = seg + SUBLANES
    pad = SUBLANES
    zeros = jnp.zeros((pad, BD_B), F32)
    xpad_scr[0:pad, :] = zeros
    xpad_scr[pad + t:2 * pad + t, :] = zeros
    xpad_scr[pad:pad + t, :] = xb_ref[0]

    c1 = (-0.5 * LRU_C) * _softplus(-lam_ref[...])
    w4h = w4h_ref[0]
    b4h = 0.5 * b4_ref[0]
    left = (CONV_W - 1) // 2
    for c in range(t // chunk):
        base = c * chunk
        xc = cb_ref[...]
        for k in range(CONV_W):
            xc = xc + xpad_scr[pl.ds(base + pad - left + k, chunk), :] * cw_ref[k:k + 1, :]
        th = jnp.tanh(_dot(xc.astype(BF16), w4h) + b4h)
        xh = 0.5 * xc
        for d, (a_scr, u_scr) in enumerate(((af_scr, uf_scr), (ab_scr, ub_scr))):
            tr = th[:, 2 * d * BD_B:(2 * d + 1) * BD_B]
            ti = th[:, (2 * d + 1) * BD_B:(2 * d + 2) * BD_B]
            log_a = c1[d:d + 1, :] * tr + c1[d:d + 1, :]
            a = jnp.exp(log_a)
            u = jnp.sqrt(-jnp.tanh(log_a) * (a * a + 1.0)) * (xh * ti + xh)
            for p in range(chunk // seg):
                s = (base + p * seg) // seg
                a_scr[s * pitch:s * pitch + seg, :] = a[p * seg:(p + 1) * seg, :]
                u_scr[s * pitch:s * pitch + seg, :] = u[p * seg:(p + 1) * seg, :]

    def scan(j, carry):
        hf, pf, hb, pb = carry
        rf = pl.ds(j, LRU_SEGS, stride=pitch)
        rb = pl.ds(seg - 1 - j, LRU_SEGS, stride=pitch)
        a = af_scr[rf, :]
        hf = a * hf + uf_scr[rf, :]
        pf = a * pf
        uf_scr[rf, :] = hf
        af_scr[rf, :] = pf
        a = ab_scr[rb, :]
        hb = a * hb + ub_scr[rb, :]
        pb = a * pb
        ub_scr[rb, :] = hb
        ab_scr[rb, :] = pb
        return hf, pf, hb, pb

    zero = jnp.zeros((LRU_SEGS, BD_B), F32)
    one = jnp.ones((LRU_SEGS, BD_B), F32)
    hf, pf, hb, pb = lax.fori_loop(0, seg, scan, (zero, one, zero, one), unroll=4)

    cf = h0_ref[0, 0:1, :]
    cin_f = []
    for s in range(LRU_SEGS):
        cin_f.append(cf)
        cf = hf[s:s + 1, :] + pf[s:s + 1, :] * cf
    cb = h0_ref[0, 1:2, :]
    cin_b = [None] * LRU_SEGS
    for s in reversed(range(LRU_SEGS)):
        cin_b[s] = cb
        cb = hb[s:s + 1, :] + pb[s:s + 1, :] * cb
    fin_ref[0, 0:1, :] = cf
    fin_ref[0, 1:2, :] = cb

    for s in range(LRU_SEGS):
        rows = slice(s * pitch, s * pitch + seg)
        hs = (uf_scr[rows, :] + af_scr[rows, :] * cin_f[s]) + (ub_scr[rows, :] + ab_scr[rows, :] * cin_b[s])
        nat = slice(s * seg, (s + 1) * seg)
        o_ref[0, nat, :] = (hs * jax.nn.gelu(gb_ref[0, nat, :])).astype(BF16)


def _rglru(xg, conv_w, conv_b, w4h, b4, lam, h0):
    b, t, _ = xg.shape
    chunk = min(t, 512)
    seg = t // LRU_SEGS
    col = lambda off: pl.BlockSpec((1, t, BD_B), lambda n, h: (n, 0, off + h))
    return pl.pallas_call(
        functools.partial(_lru_kernel, t=t, chunk=chunk),
        grid=(b, H_B),
        in_specs=[
            col(0), col(H_B),
            pl.BlockSpec((CONV_W, BD_B), lambda n, h: (0, h)),
            pl.BlockSpec((1, BD_B), lambda n, h: (0, h)),
            pl.BlockSpec((1, BD_B, 4 * BD_B), lambda n, h: (h, 0, 0)),
            pl.BlockSpec((1, 1, 4 * BD_B), lambda n, h: (h, 0, 0)),
            pl.BlockSpec((2, BD_B), lambda n, h: (0, h)),
            pl.BlockSpec((1, 2, BD_B), lambda n, h: (n, 0, h)),
        ],
        out_specs=[pl.BlockSpec((1, t, BD_B), lambda n, h: (n, 0, h)),
                   pl.BlockSpec((1, 2, BD_B), lambda n, h: (n, 0, h))],
        out_shape=[jax.ShapeDtypeStruct((b, t, D_RNN), BF16),
                   jax.ShapeDtypeStruct((b, 2, D_RNN), F32)],
        scratch_shapes=[pltpu.VMEM((t + 2 * SUBLANES, BD_B), F32)]
        + [pltpu.VMEM((LRU_SEGS * (seg + SUBLANES), BD_B), F32)] * 4,
        compiler_params=_params("arbitrary", "arbitrary"),
        name="rglru_branch",
    )(xg, xg, conv_w, conv_b.reshape(1, D_RNN), w4h, b4, lam, h0)


def _outproj_ab_kernel(x_ref, mod_ref, oa_ref, ob_ref, w_ref, o_ref):
    y = _dot(oa_ref[0], w_ref[0:D_A, :]) + _dot(ob_ref[0], w_ref[D_A:D_A + D_RNN, :])
    o_ref[0] = x_ref[0] + mod_ref[0, 5:6, :] * y


def _outproj_ab(x, mod, oa, ob, w):
    grp, t, d = x.shape
    tm = TOKEN_TILE
    tok = lambda width: pl.BlockSpec((1, tm, width), lambda b, i: (b, i, 0))
    return pl.pallas_call(
        _outproj_ab_kernel,
        grid=(grp, t // tm),
        in_specs=[
            tok(d),
            pl.BlockSpec((1, N_MOD, d), lambda b, i: (b, 0, 0)),
            tok(D_A), tok(D_RNN),
            pl.BlockSpec((D_A + D_RNN, d), lambda b, i: (0, 0)),
        ],
        out_specs=tok(d),
        out_shape=jax.ShapeDtypeStruct(x.shape, F32),
        compiler_params=_params("arbitrary", "arbitrary"),
        name="mixer_ab_out_proj",
    )(x, mod, oa, ob, w)


def _gate_prep_kernel(g_ref, b_ref, o_ref):
    t = g_ref.shape[2]
    g = g_ref[0] + b_ref[...]
    lane = lax.broadcasted_iota(jnp.int32, (H_C, t), 1) & (CHUNK - 1)
    cum = _log_sigmoid(g[H_C:2 * H_C])
    suf = _log_sigmoid(g[3 * H_C:4 * H_C])
    sh = 1
    while sh < CHUNK:
        cum = cum + jnp.where(lane >= sh, pltpu.roll(cum, sh, 1), 0.0)
        suf = suf + jnp.where(lane < CHUNK - sh, pltpu.roll(suf, t - sh, 1), 0.0)
        sh *= 2
    o_ref[0, 0:H_C] = g[0:H_C]
    o_ref[0, H_C:2 * H_C] = cum
    o_ref[0, 2 * H_C:3 * H_C] = g[2 * H_C:3 * H_C]
    o_ref[0, 3 * H_C:4 * H_C] = suf


def _gate_prep(g_rows, b_gate):
    b, r, t = g_rows.shape
    return pl.pallas_call(
        _gate_prep_kernel,
        grid=(b,),
        in_specs=[pl.BlockSpec((1, r, t), lambda n: (n, 0, 0)), pl.BlockSpec((r, 1), lambda n: (0, 0))],
        out_specs=pl.BlockSpec((1, r, t), lambda n: (n, 0, 0)),
        out_shape=jax.ShapeDtypeStruct(g_rows.shape, F32),
        compiler_params=_params("arbitrary"),
        name="mlstm_gate_prep",
    )(g_rows, b_gate.reshape(r, 1))


def _mlstm_chunk(q, kt, v_aug, ig_row, bc_row, bc_col, cm, m, reverse):
    ti = lax.broadcasted_iota(jnp.int32, (CHUNK, CHUNK), 0)
    si = lax.broadcasted_iota(jnp.int32, (CHUNK, CHUNK), 1)
    seen = (si >= ti) if reverse else (si <= ti)
    bk = bc_row[:, 0:1] if reverse else bc_row[:, CHUNK - 1:CHUNK]
    bc_rep = jnp.broadcast_to(bc_col, (CHUNK, CHUNK))
    dmat = jnp.where(seen, bc_rep - bc_row + ig_row, -jnp.inf)
    inter = bc_rep + m
    m_t = jnp.maximum(inter, jnp.broadcast_to(jnp.max(dmat, axis=1, keepdims=True), (CHUNK, CHUNK)))
    w = jnp.exp(dmat - m_t)
    g = jnp.exp(inter - m_t)
    kscale = DK_C ** -0.5
    s = _dot(q, kt) * (w * kscale)
    tot = _dot(s.astype(BF16), v_aug) + jnp.concatenate([g, g, g], axis=1) * _dot(q, cm.astype(BF16))
    inv = 1.0 / jnp.maximum(jnp.abs(tot[:, DV_C:]), jnp.exp(-m_t))
    hout = tot[:, :DV_C] * jnp.concatenate([inv, inv], axis=1)
    dend = bk - bc_row + ig_row
    m_new = jnp.maximum(bk + m, jnp.max(dend, axis=1, keepdims=True))
    we = jnp.exp(dend - m_new)
    ge = jnp.exp(bk + m - m_new)
    kwt = kt.astype(F32) * (we * kscale)
    c_new = ge * cm + _dot(kwt.astype(BF16), v_aug)
    return hout, c_new, m_new


def _mlstm_kernel(*refs, zero_state):
    if zero_state:
        qvf_ref, vvf_ref, ktf_ref, qvb_ref, vvb_ref, ktb_ref, grf_ref, gcf_ref, grb_ref, gcb_ref = refs[:10]
        rest = refs[10:]
    else:
        m0_ref, qvf_ref, vvf_ref, ktf_ref, qvb_ref, vvb_ref, ktb_ref, grf_ref, gcf_ref, grb_ref, gcb_ref = refs[:11]
        c0_ref, n0_ref = refs[11:13]
        rest = refs[13:]
    hf_ref, hb_ref, cfin_ref, nfin_ref, mfin_ref, c_scr, m_scr = rest
    n, hp, c = pl.program_id(0), pl.program_id(1), pl.program_id(2)
    eye = (lax.broadcasted_iota(jnp.int32, (DK_C, DK_C), 0) == lax.broadcasted_iota(jnp.int32, (DK_C, DK_C), 1))

    @pl.when(c == 0)
    def _():
        if zero_state:
            c_scr[...] = jnp.zeros_like(c_scr)
            m_scr[...] = jnp.zeros_like(m_scr)
            return
        for d in range(2):
            for j in range(MLSTM_HEADS):
                c_scr[d, j, :, 0:DV_C] = c0_ref[0, d, j]
                n_col = jnp.sum(jnp.where(eye, n0_ref[0, d, j], 0.0), axis=1, keepdims=True)
                c_scr[d, j, :, DV_C:DV_C + LANES] = jnp.broadcast_to(n_col, (DK_C, LANES))
                m_scr[d, j] = jnp.full(m_scr.shape[2:], m0_ref[n, d * H_C + hp * MLSTM_HEADS + j], F32)

    ones_col = jnp.ones((CHUNK, LANES), BF16)
    dirs = ((qvf_ref, vvf_ref, ktf_ref, grf_ref, gcf_ref, hf_ref), (qvb_ref, vvb_ref, ktb_ref, grb_ref, gcb_ref, hb_ref))
    chains = []
    for d, (q_ref, v_ref, kt_ref, gr_ref, gc_ref, h_ref) in enumerate(dirs):
        for j in range(MLSTM_HEADS):
            sl = slice(j * DK_C, (j + 1) * DK_C)
            v_aug = jnp.concatenate([v_ref[0, :, sl], ones_col], axis=1)
            chains.append((d, j, h_ref, sl, _mlstm_chunk(
                q_ref[0, :, sl], kt_ref[0, sl, :], v_aug,
                gr_ref[0, j, 2 * d:2 * d + 1, :], gr_ref[0, j, 2 * d + 1:2 * d + 2, :],
                gc_ref[0, 0, :, (2 * d + 1) * MLSTM_HEADS + j:(2 * d + 1) * MLSTM_HEADS + j + 1],
                c_scr[d, j], m_scr[d, j, 0:1, 0:1], reverse=(d == 1))))
    for d, j, h_ref, sl, (hout, c_new, m_new) in chains:
        h_ref[0, :, sl] = hout
        c_scr[d, j] = c_new
        m_scr[d, j] = jnp.broadcast_to(m_new, m_scr.shape[2:])

    @pl.when(c == pl.num_programs(2) - 1)
    def _():
        for d in range(2):
            for j in range(MLSTM_HEADS):
                cfin_ref[0, d, j] = c_scr[d, j, :, 0:DV_C]
                n_col = c_scr[d, j, :, DV_C:DV_C + 1]
                nfin_ref[0, d, j] = jnp.sum(jnp.where(eye, n_col, 0.0), axis=0, keepdims=True)
                mfin_ref[0, d, j] = m_scr[d, j, 0:1, :]


def _rope_tables(t):
    half = DK_C // 2
    nf = half // 2
    pos = jnp.arange(t)
    freqs = ROPE_BASE ** (-jnp.arange(nf, dtype=F32) / nf)
    cos, sin = [], []
    for p in (pos // GRID_W, pos % GRID_W):
        ang = p.astype(F32)[:, None] * freqs[None, :]
        cos += [jnp.cos(ang), jnp.cos(ang)]
        sin += [-jnp.sin(ang), jnp.sin(ang)]
    return jnp.concatenate(cos, axis=-1), jnp.concatenate(sin, axis=-1)


def _mlstm(qv, kt, gates, c0, n0, m0):
    b, t, _ = qv.shape
    nc = t // CHUNK
    hps = MLSTM_HEADS
    ngrp = H_C // hps
    g4 = gates.reshape(b, 4, H_C, t)
    g_row = jnp.transpose(g4, (0, 2, 1, 3))
    g_col = jnp.transpose(gates.reshape(b, 4, ngrp, hps, t), (0, 2, 4, 1, 3)).reshape(b, ngrp, t, 4 * hps)
    fwd = lambda c: c
    bwd = lambda c: nc - 1 - c

    def tok(off, order):
        return pl.BlockSpec((1, CHUNK, hps * DK_C), lambda n, h, c: (n, order(c), off + h))

    def kt_spec(order):
        return pl.BlockSpec((1, hps * DK_C, CHUNK), lambda n, h, c: (n, h, order(c)))

    def g_row_spec(order):
        return pl.BlockSpec((1, hps, 4, CHUNK), lambda n, h, c: (n, h, 0, order(c)))

    def g_col_spec(order):
        return pl.BlockSpec((1, 1, CHUNK, 4 * hps), lambda n, h, c: (n, h, order(c), 0))

    state = lambda *tail: pl.BlockSpec((1, 2, hps) + tail, lambda n, h, c: (n, 0, h) + (0,) * len(tail))
    hspec = lambda order: pl.BlockSpec((1, CHUNK, hps * DV_C), lambda n, h, c: (n, order(c), h))
    in_specs = [tok(0, fwd), tok(ngrp, fwd), kt_spec(fwd),
                tok(0, bwd), tok(ngrp, bwd), kt_spec(bwd),
                g_row_spec(fwd), g_col_spec(fwd), g_row_spec(bwd), g_col_spec(bwd)]
    args = [qv, qv, kt, qv, qv, kt, g_row, g_col, g_row, g_col]
    zero_state = c0 is None
    if not zero_state:
        in_specs = [pl.BlockSpec(memory_space=pltpu.SMEM)] + in_specs + [state(DK_C, DV_C), state(1, DK_C)]
        args = [m0.reshape(b, 2 * H_C)] + args + [c0, n0.reshape(b, 2, H_C, 1, DK_C)]
    hf, hb, cfin, nfin, mfin = pl.pallas_call(
        functools.partial(_mlstm_kernel, zero_state=zero_state),
        grid=(b, ngrp, nc),
        in_specs=in_specs,
        out_specs=[hspec(fwd), hspec(bwd), state(DK_C, DV_C), state(1, DK_C), state(1, LANES)],
        out_shape=[jax.ShapeDtypeStruct((b, t, D_C), F32), jax.ShapeDtypeStruct((b, t, D_C), F32),
                   jax.ShapeDtypeStruct((b, 2, H_C, DK_C, DV_C), F32),
                   jax.ShapeDtypeStruct((b, 2, H_C, 1, DK_C), F32),
                   jax.ShapeDtypeStruct((b, 2, H_C, 1, LANES), F32)],
        scratch_shapes=[pltpu.VMEM((2, hps, DK_C, DV_C + LANES), F32),
                        pltpu.VMEM((2, hps, SUBLANES, LANES), F32)],
        compiler_params=_params("arbitrary", "arbitrary", "arbitrary"),
        name="mlstm_scan",
    )(*args)
    return hf, hb, cfin, nfin.reshape(b, 2, H_C, DK_C), mfin[:, :, :, 0, 0]


def _outproj_c_kernel(x_ref, mod_ref, hf_ref, hb_ref, og_ref, g_ref, w_ref, o_ref, hn_scr):
    for h in range(H_C):
        sl = slice(h * DV_C, (h + 1) * DV_C)
        hs = hf_ref[0, :, sl] + hb_ref[0, :, sl]
        hn_scr[:, sl] = (_rms(hs, g_ref[:, sl]) * jax.nn.sigmoid(og_ref[0, :, sl])).astype(BF16)
    o_ref[0] = x_ref[0] + mod_ref[0, 5:6, :] * _dot(hn_scr[...], w_ref[...])


def _outproj_c(x, mod, hf, hb, og, mh_g, w):
    grp, t, d = x.shape
    tm = TOKEN_TILE // 2
    tok = pl.BlockSpec((1, tm, d), lambda b, i: (b, i, 0))
    return pl.pallas_call(
        _outproj_c_kernel,
        grid=(grp, t // tm),
        in_specs=[
            tok,
            pl.BlockSpec((1, N_MOD, d), lambda b, i: (b, 0, 0)),
            tok, tok,
            pl.BlockSpec((1, tm, D_C), lambda b, i: (b, i, 0)),
            pl.BlockSpec((1, D_C), lambda b, i: (0, 0)),
            pl.BlockSpec((D_C, d), lambda b, i: (0, 0)),
        ],
        out_specs=tok,
        out_shape=jax.ShapeDtypeStruct(x.shape, F32),
        scratch_shapes=[pltpu.VMEM((tm, D_C), BF16)],
        compiler_params=_params("arbitrary", "arbitrary"),
        name="mixer_c_out_proj",
    )(x, mod, hf, hb, og, mh_g.reshape(1, D_C), w)


def _mixer_ab(j, xp, xs, mp, ms, g, p):
    bp, lp, bs = p["bp"], p["lp"], xs.shape[0]
    w_in = p["w_in_ab"][j].astype(BF16)
    w_out = p["w_out_ab"][j].astype(BF16)
    gains = jnp.stack([jnp.tile(p["qn_g"][j] * (HD_A ** -0.5), H_A), jnp.tile(p["kn_g"][j], H_A),
                       jnp.ones((D_A,), F32)]).reshape(3, 1, D_A)
    wa, wx, ba, bx = p["lru_wa"][j], p["lru_wx"][j], p["lru_ba"][j], p["lru_bx"][j]
    w4 = (0.5 * jnp.concatenate([wa[0], wx[0], wa[1], wx[1]], axis=-1)).astype(BF16)
    b4 = jnp.concatenate([v.reshape(H_B, 1, BD_B) for v in (ba[0], bx[0], ba[1], bx[1])], axis=-1)
    lru = (p["conv_w"][j], p["conv_b"][j], w4, b4, p["lru_lam"][j])

    qkv, xg = _inproj_ab(xp, mp, g, w_in, gains, F32)
    qkv = qkv.reshape(bp, lp, 3 * D_A)
    oa = _ctx_attention(qkv)
    ob, h_fin = _rglru(xg.reshape(bp, lp, 2 * D_RNN), *lru, jnp.zeros((bp, 2, D_RNN), F32))
    xp = _outproj_ab(xp, mp, oa.reshape(1, bp * lp, D_A), ob.reshape(1, bp * lp, D_RNN), w_out)
    new = (qkv[:, :, D_A:2 * D_A].reshape(bp, lp, H_A, HD_A), qkv[:, :, 2 * D_A:].reshape(bp, lp, H_A, HD_A), h_fin)

    past = p["cache_k"].shape[2]
    ck = p["cache_k"][:, j].reshape(bs, past, D_A).astype(BF16)
    cv = p["cache_v"][:, j].reshape(bs, past, D_A).astype(BF16)
    qkv, xg = _inproj_ab(xs, ms, g, w_in, gains, BF16)
    oa = _neighbourhood_attention(qkv, ck, cv, p["rpb"][j])
    ob, _ = _rglru(xg, *lru, p["state_lru"][:, j])
    xs = _outproj_ab(xs, ms, oa, ob, w_out)
    return xp, xs, new


def _mixer_c(j, xp, xs, mp, ms, g, p):
    bp, lp, bs, ts = p["bp"], p["lp"], xs.shape[0], xs.shape[1]
    w = p["w_in_c"][j].astype(BF16)
    wkt = w[:, D_C:2 * D_C].T
    w_gate = jnp.pad(w[:, 4 * D_C:], ((0, 0), (0, LANES - 4 * H_C)))
    w_out = p["w_out_c"][j].astype(BF16)

    def gates_of(zg, b, t):
        g_rows = jnp.transpose(zg.reshape(b, t, LANES)[:, :, :4 * H_C], (0, 2, 1))
        return _gate_prep(g_rows, p["b_gate_c"][j])

    qv, og, kt, zg = _inproj_c(xp, mp, g, w, wkt, w_gate, None)
    kt = jnp.transpose(kt.reshape(D_C, bp, lp), (1, 0, 2))
    hf, hb, cfin, nfin, mfin = _mlstm(
        qv.reshape(bp, lp, 2 * D_C), kt, gates_of(zg, bp, lp), None, None, None)
    xp = _outproj_c(xp, mp, hf.reshape(1, bp * lp, D_C), hb.reshape(1, bp * lp, D_C), og, p["mh_norm_g"][j], w_out)

    qv, og, kt, zg = _inproj_c(xs, ms, g, w, wkt, w_gate, _rope_tables(ts))
    hf, hb, _, _, _ = _mlstm(qv, kt, gates_of(zg, bs, ts), p["state_mlstm_c"][:, j], p["state_mlstm_n"][:, j],
                             p["state_mlstm_m"][:, j])
    xs = _outproj_c(xs, ms, hf, hb, og, p["mh_norm_g"][j], w_out)
    return xp, xs, (cfin, nfin, mfin)


def kernel(x_prompt, x_sample, cache_k, cache_v, state_lru, state_mlstm_c, state_mlstm_n, state_mlstm_m, c, c_ctx, w_mod, b_mod, norm_g, w_ffn_gate, w_ffn_up, w_ffn_down, w_in_ab, w_out_ab, qn_g, kn_g, rpb, conv_w, conv_b, lru_wa, lru_ba, lru_wx, lru_bx, lru_lam, w_in_c, b_gate_c, mh_norm_g, w_out_c):
    bp, lp, d = x_prompt.shape
    bs = x_sample.shape[0]
    p = dict(bp=bp, lp=lp, cache_k=cache_k, cache_v=cache_v, state_lru=state_lru, state_mlstm_c=state_mlstm_c,
             state_mlstm_n=state_mlstm_n, state_mlstm_m=state_mlstm_m, w_in_ab=w_in_ab, w_out_ab=w_out_ab,
             qn_g=qn_g, kn_g=kn_g, rpb=rpb, conv_w=conv_w, conv_b=conv_b, lru_wa=lru_wa, lru_ba=lru_ba,
             lru_wx=lru_wx, lru_bx=lru_bx, lru_lam=lru_lam, w_in_c=w_in_c, b_gate_c=b_gate_c,
             mh_norm_g=mh_norm_g, w_out_c=w_out_c)

    cond = jnp.concatenate([c, c_ctx[None, :]], axis=0)
    cond = jnp.pad(cond, ((0, -(bs + 1) % SUBLANES), (0, 0)))
    mod = _modulation(cond, w_mod, b_mod)[:, :bs + 1].reshape(DEPTH, bs + 1, N_MOD, d)

    wg = w_ffn_gate.astype(BF16)
    wu = w_ffn_up.astype(BF16)
    wd = w_ffn_down.astype(BF16)

    xp = x_prompt.reshape(1, bp * lp, d)
    xs = x_sample
    new_ab, new_c = [], []
    for l in range(DEPTH):
        j = l // 2
        ms = mod[l, :bs]
        mp = mod[l, bs:]
        xp = _ffn(xp, mp, norm_g[l, 0], wg, wu, wd, l, 0, 0)
        xs = _ffn(xs, ms, norm_g[l, 0], wg, wu, wd, l, 0, 0)
        if l % 2 == 0:
            xp, xs, new = _mixer_ab(j, xp, xs, mp, ms, norm_g[l, 1], p)
            new_ab.append(new)
        else:
            xp, xs, new = _mixer_c(j, xp, xs, mp, ms, norm_g[l, 1], p)
            new_c.append(new)
        xp = _ffn(xp, mp, norm_g[l, 2], wg, wu, wd, l, 1, 2)
        xs = _ffn(xs, ms, norm_g[l, 2], wg, wu, wd, l, 1, 2)
    stack = lambda items, i: jnp.concatenate([jnp.expand_dims(it[i], 1) for it in items], axis=1)
    return (xp.reshape(bp, lp, d), xs, stack(new_ab, 0), stack(new_ab, 1), stack(new_ab, 2),
            stack(new_c, 0), stack(new_c, 1), stack(new_c, 2))
```

```python
import functools

import numpy as np
import jax
import jax.numpy as jnp
from jax import lax
from jax.experimental import pallas as pl
from jax.experimental.pallas import tpu as pltpu

D_MODEL = 2048
DEPTH = 2
GRID_W = 64
N_MOD = 9
D_FF = 5632
EPS = 1e-6
H_A = 8
HD_A = 128
D_A = H_A * HD_A
WIN_R = 8
WIN_C = 16
D_RNN = 1024
H_B = 8
BD_B = D_RNN // H_B
CONV_W = 4
LRU_C = 8.0
H_C = 8
DK_C = 256
DV_C = 256
D_C = H_C * DV_C
CHUNK = 128
ROPE_BASE = 10000.0

BF16 = jnp.bfloat16
F32 = jnp.float32

V7X_VMEM_BYTES = 64 * 1024 * 1024
VMEM_LIMIT = V7X_VMEM_BYTES - 8 * 1024 * 1024
FFN_VMEM_LIMIT = V7X_VMEM_BYTES - 5 * 1024 * 1024
LANES = 128
SUBLANES = 8

TOKEN_TILE = 512
FF_TILE = 512
FFN_NORM_ROWS = 128
PROJ_TOKEN_TILE = 1024
PROJ_TILE = 512
NA_ROWS = 4
NA_KEY_ROWS = NA_ROWS + WIN_R
LRU_SEGS = SUBLANES
NA_HEADS = 4
MLSTM_HEADS = 4


def _params(*sem, vmem=VMEM_LIMIT):
    return pltpu.CompilerParams(dimension_semantics=sem, vmem_limit_bytes=vmem)


def _dot(a, b):
    return jnp.dot(a, b, preferred_element_type=F32)


def _dot_nt(a, b):
    return lax.dot_general(a, b, (((1,), (1,)), ((), ())), preferred_element_type=F32)


def _dot_tn(a, b):
    return lax.dot_general(a, b, (((0,), (0,)), ((), ())), preferred_element_type=F32)


def _rms(x, g):
    return x * lax.rsqrt(jnp.mean(x * x, axis=-1, keepdims=True) + EPS) * g


def _adaln(x, g, mod_ref, j):
    return _rms(x, g) * (1.0 + mod_ref[0, 3 * j + 1:3 * j + 2, :]) + mod_ref[0, 3 * j:3 * j + 1, :]


def _softplus(x):
    return jnp.maximum(x, 0.0) + jnp.log1p(jnp.exp(-jnp.abs(x)))


def _log_sigmoid(x):
    return -_softplus(-x)


def _mod_kernel(c_ref, w_ref, b_ref, o_ref):
    c = c_ref[...]
    s = (c * jax.nn.sigmoid(c)).astype(BF16)
    o_ref[0] = _dot(s, w_ref[0].astype(BF16)) + b_ref[0]


def _modulation(cond, w_mod, b_mod):
    r = cond.shape[0]
    n = N_MOD * D_MODEL
    tn = 1024
    return pl.pallas_call(
        _mod_kernel,
        grid=(DEPTH, n // tn),
        in_specs=[
            pl.BlockSpec((r, D_MODEL), lambda l, j: (0, 0)),
            pl.BlockSpec((1, D_MODEL, tn), lambda l, j: (l, 0, j)),
            pl.BlockSpec((1, 1, tn), lambda l, j: (l, 0, j)),
        ],
        out_specs=pl.BlockSpec((1, r, tn), lambda l, j: (l, 0, j)),
        out_shape=jax.ShapeDtypeStruct((DEPTH, r, n), F32),
        compiler_params=_params("arbitrary", "arbitrary"),
        name="modulation",
    )(cond, w_mod, b_mod.reshape(DEPTH, 1, n))


def _ffn_kernel(x0_ref, xn_ref, mod_ref, modn_ref, g_ref, wga_ref, wua_ref, wda_ref, wgb_ref, wub_ref, wdb_ref,
                o_ref, ha_scr, hb_scr, xa_scr, xb_scr, acc_scr, *, j, ns, single_last):
    p, f2 = pl.program_id(0), pl.program_id(1)
    rows = xn_ref.shape[1]
    tm = ha_scr.shape[0]

    @pl.when((p == 0) & (f2 == 0))
    def _():
        xa_scr[...] = x0_ref[0]
        ha_scr[...] = _adaln(x0_ref[0], g_ref[...], mod_ref, j).astype(BF16)

    @pl.when((f2 == 0) | (f2 == ns))
    def _():
        acc_scr[...] = jnp.zeros_like(acc_scr)

    def body(h_cur, h_nxt, x_nxt, f, both):
        r0 = pl.multiple_of(jnp.minimum(f, tm // rows - 1) * rows, rows)
        xs = xn_ref[0]
        x_nxt[pl.ds(r0, rows), :] = xs
        h_nxt[pl.ds(r0, rows), :] = _adaln(xs, g_ref[...], modn_ref, j).astype(BF16)
        h = h_cur[...]
        hidden = ((wga_ref, wua_ref, wda_ref), (wgb_ref, wub_ref, wdb_ref)) if both else ((wga_ref, wua_ref, wda_ref),)
        y = None
        for wg_ref, wu_ref, wd_ref in hidden:
            a = _dot(h, wg_ref[...])
            u = _dot(h, wu_ref[...])
            act = (a * jax.nn.sigmoid(a)) * u
            yd = _dot(act.astype(BF16), wd_ref[...])
            y = yd if y is None else y + yd
        acc_scr[...] += y

    n_both = ns - 1 if single_last else ns
    for half, (h_cur, h_nxt, x_nxt) in enumerate(((ha_scr, hb_scr, xb_scr), (hb_scr, ha_scr, xa_scr))):
        @pl.when((f2 >= half * ns) & (f2 < half * ns + n_both))
        def _():
            body(h_cur, h_nxt, x_nxt, f2 - half * ns, True)

        if single_last:
            @pl.when(f2 == half * ns + ns - 1)
            def _():
                body(h_cur, h_nxt, x_nxt, ns - 1, False)

    for last, x_cur in ((ns - 1, xa_scr), (2 * ns - 1, xb_scr)):
        @pl.when(f2 == last)
        def _():
            o_ref[0] = x_cur[...] + 0.5 * mod_ref[0, 3 * j + 2:3 * j + 3, :] * acc_scr[...]


def _ffn(x, mod, g, wg, wu, wd, l, s, j):
    grp, t, d = x.shape
    tm, tf = TOKEN_TILE, FF_TILE
    nf = D_FF // tf
    ns = (nf + 1) // 2
    per_group = t // tm
    ntile = grp * per_group
    rows = FFN_NORM_ROWS
    nsl = tm // rows
    assert ntile % 2 == 0 and ns >= nsl
    tile = lambda p, f2: 2 * p + f2 // ns
    nxt = lambda p, f2: jnp.minimum(tile(p, f2) + 1, ntile - 1)
    fa = lambda f2: jnp.minimum(2 * (f2 % ns), nf - 1)
    fb = lambda f2: jnp.minimum(2 * (f2 % ns) + 1, nf - 1)
    wcol = lambda fsel: pl.BlockSpec((None, None, d, tf), lambda p, f2: (l, s, 0, fsel(f2)))
    wrow = lambda fsel: pl.BlockSpec((None, None, tf, d), lambda p, f2: (l, s, fsel(f2), 0))
    out = pl.pallas_call(
        functools.partial(_ffn_kernel, j=j, ns=ns, single_last=nf % 2 == 1),
        grid=(ntile // 2, 2 * ns),
        in_specs=[
            pl.BlockSpec((1, tm, d), lambda p, f2: (0, 0, 0), pipeline_mode=pl.Buffered(1)),
            pl.BlockSpec((1, rows, d), lambda p, f2: (0, nxt(p, f2) * nsl + jnp.minimum(f2 % ns, nsl - 1), 0)),
            pl.BlockSpec((1, N_MOD, d), lambda p, f2: (tile(p, f2) // per_group, 0, 0)),
            pl.BlockSpec((1, N_MOD, d), lambda p, f2: (nxt(p, f2) // per_group, 0, 0)),
            pl.BlockSpec((1, d), lambda p, f2: (0, 0)),
            wcol(fa), wcol(fa), wrow(fa), wcol(fb), wcol(fb), wrow(fb),
        ],
        out_specs=pl.BlockSpec((1, tm, d), lambda p, f2: (0, tile(p, f2), 0)),
        out_shape=jax.ShapeDtypeStruct((1, grp * t, d), F32),
        scratch_shapes=[pltpu.VMEM((tm, d), BF16), pltpu.VMEM((tm, d), BF16),
                        pltpu.VMEM((tm, d), F32), pltpu.VMEM((tm, d), F32), pltpu.VMEM((tm, d), F32)],
        compiler_params=_params("arbitrary", "arbitrary", vmem=FFN_VMEM_LIMIT),
        name="macaron_ffn",
    )(x.reshape(1, grp * t, d), x.reshape(1, grp * t, d), mod, mod, g.reshape(1, d), wg, wu, wd, wg, wu, wd)
    return out.reshape(x.shape)


def _rope(x, cos, sin):
    half = DK_C // 2
    out = []
    for p in range(2):
        sl = slice(p * half, (p + 1) * half)
        xs = x[:, sl]
        out.append(xs * cos[:, sl] + pltpu.roll(xs, half // 2, 1) * sin[:, sl])
    return jnp.concatenate(out, axis=-1)


def _rope_t(x, cos, sin):
    q = DK_C // 4
    swapped = jnp.concatenate([x[q:2 * q], x[0:q], x[3 * q:4 * q], x[2 * q:3 * q]], axis=0)
    return x * cos + swapped * sin


def _inproj_ab_kernel(x_ref, mod_ref, g_ref, w_ref, gain_ref, oa_ref, ob_ref, h_scr, *, n_a, n_norm):
    k = pl.program_id(2)

    @pl.when(k == 0)
    def _():
        h_scr[...] = _adaln(x_ref[0], g_ref[...], mod_ref, 1).astype(BF16)

    @pl.when(k >= n_a)
    def _():
        ob_ref[0] = _dot(h_scr[...], w_ref[...])

    @pl.when(k < n_norm)
    def _():
        for pair in range(w_ref.shape[1] // (2 * HD_A)):
            z2 = _dot(h_scr[...], w_ref[:, 2 * pair * HD_A:2 * (pair + 1) * HD_A])
            for hh in range(2):
                zs = z2[:, hh * HD_A:(hh + 1) * HD_A]
                sl = slice((2 * pair + hh) * HD_A, (2 * pair + hh + 1) * HD_A)
                r = lax.rsqrt(jnp.mean(zs * zs, axis=-1, keepdims=True) + EPS)
                oa_ref[0, :, sl] = (zs * r * gain_ref[0, :, sl]).astype(oa_ref.dtype)

    @pl.when((k >= n_norm) & (k < n_a))
    def _():
        oa_ref[0] = (_dot(h_scr[...], w_ref[...]) * gain_ref[0]).astype(oa_ref.dtype)


def _inproj_ab(x, mod, g, w, gains, dtype_a):
    grp, t, d = x.shape
    tm, tn = PROJ_TOKEN_TILE, PROJ_TILE
    n_a = 3 * D_A // tn
    nt = w.shape[1] // tn
    return pl.pallas_call(
        functools.partial(_inproj_ab_kernel, n_a=n_a, n_norm=2 * D_A // tn),
        grid=(grp, t // tm, nt),
        in_specs=[
            pl.BlockSpec((1, tm, d), lambda b, i, k: (b, i, 0)),
            pl.BlockSpec((1, N_MOD, d), lambda b, i, k: (b, 0, 0)),
            pl.BlockSpec((1, d), lambda b, i, k: (0, 0)),
            pl.BlockSpec((d, tn), lambda b, i, k: (0, k)),
            pl.BlockSpec((1, 1, tn), lambda b, i, k: (jnp.minimum(k, n_a - 1), 0, 0)),
        ],
        out_specs=[pl.BlockSpec((1, tm, tn), lambda b, i, k: (b, i, jnp.minimum(k, n_a - 1))),
                   pl.BlockSpec((1, tm, tn), lambda b, i, k: (b, i, jnp.maximum(k - n_a, 0)))],
        out_shape=[jax.ShapeDtypeStruct((grp, t, n_a * tn), dtype_a),
                   jax.ShapeDtypeStruct((grp, t, (nt - n_a) * tn), F32)],
        scratch_shapes=[pltpu.VMEM((tm, d), BF16)],
        compiler_params=_params("arbitrary", "arbitrary", "arbitrary"),
        name="mixer_ab_in_proj",
    )(x, mod, g.reshape(1, d), w, gains.reshape(n_a, 1, tn))


def _inproj_c_kernel(*refs, rotary, n_q, n_qv, n_main):
    x_ref, mod_ref, g_ref, w_ref, wkt_ref, wgate_ref = refs[:6]
    pos = 6
    if rotary:
        cos_ref, sin_ref, cost_ref, sint_ref = refs[pos:pos + 4]
        pos += 4
    oa_ref, ob_ref, okt_ref, og_ref, h_scr = refs[pos:pos + 5]
    k = pl.program_id(2)
    tn = w_ref.shape[1]

    @pl.when(k == 0)
    def _():
        h = _adaln(x_ref[0], g_ref[...], mod_ref, 1).astype(BF16)
        h_scr[...] = h
        og_ref[0] = _dot(h, wgate_ref[...])

    @pl.when(k < n_q)
    def _():
        z = _dot(h_scr[...], w_ref[...])
        if rotary:
            for hh in range(tn // DK_C):
                sl = slice(hh * DK_C, (hh + 1) * DK_C)
                oa_ref[0, :, sl] = _rope(z[:, sl], cos_ref[...], sin_ref[...]).astype(BF16)
        else:
            oa_ref[0] = z.astype(BF16)

    @pl.when((k >= n_q) & (k < n_qv))
    def _():
        oa_ref[0] = _dot(h_scr[...], w_ref[...]).astype(BF16)

    @pl.when((k >= n_qv) & (k < n_main))
    def _():
        ob_ref[0] = _dot(h_scr[...], w_ref[...])

    @pl.when(k >= n_main)
    def _():
        zt = _dot_nt(wkt_ref[...], h_scr[...])
        if rotary:
            for hh in range(tn // DK_C):
                sl = slice(hh * DK_C, (hh + 1) * DK_C)
                okt_ref[0, sl, :] = _rope_t(zt[sl, :], cost_ref[...], sint_ref[...]).astype(BF16)
        else:
            okt_ref[0] = zt.astype(BF16)


def _inproj_c(x, mod, g, w, wkt, w_gate, rope_tables):
    grp, t, d = x.shape
    tm, tn = PROJ_TOKEN_TILE, PROJ_TILE
    n_q, n_qv, n_main = D_C // tn, 2 * D_C // tn, 3 * D_C // tn
    nt = n_main + D_C // tn

    def w_col(k):
        kk = jnp.minimum(k, n_main - 1)
        return jnp.where(kk < n_q, kk, kk + n_q)

    rotary = rope_tables is not None
    in_specs = [
        pl.BlockSpec((1, tm, d), lambda b, i, k: (b, i, 0)),
        pl.BlockSpec((1, N_MOD, d), lambda b, i, k: (b, 0, 0)),
        pl.BlockSpec((1, d), lambda b, i, k: (0, 0)),
        pl.BlockSpec((d, tn), lambda b, i, k: (0, w_col(k))),
        pl.BlockSpec((tn, d), lambda b, i, k: (jnp.maximum(k - n_main, 0), 0)),
        pl.BlockSpec((d, LANES), lambda b, i, k: (0, 0)),
    ]
    args = [x, mod, g.reshape(1, d), w, wkt, w_gate]
    if rotary:
        cos, sin = rope_tables
        in_specs += [pl.BlockSpec((tm, DK_C), lambda b, i, k: (i, 0))] * 2
        in_specs += [pl.BlockSpec((DK_C, tm), lambda b, i, k: (0, i))] * 2
        args += [cos, sin, cos.T, sin.T]
    return pl.pallas_call(
        functools.partial(_inproj_c_kernel, rotary=rotary, n_q=n_q, n_qv=n_qv, n_main=n_main),
        grid=(grp, t // tm, nt),
        in_specs=in_specs,
        out_specs=[pl.BlockSpec((1, tm, tn), lambda b, i, k: (b, i, jnp.minimum(k, n_qv - 1))),
                   pl.BlockSpec((1, tm, tn), lambda b, i, k: (b, i, jnp.clip(k - n_qv, 0, n_main - n_qv - 1))),
                   pl.BlockSpec((1, tn, tm), lambda b, i, k: (b, jnp.maximum(k - n_main, 0), i)),
                   pl.BlockSpec((1, tm, LANES), lambda b, i, k: (b, i, 0))],
        out_shape=[jax.ShapeDtypeStruct((grp, t, 2 * D_C), BF16),
                   jax.ShapeDtypeStruct((grp, t, D_C), F32),
                   jax.ShapeDtypeStruct((grp, D_C, t), BF16),
                   jax.ShapeDtypeStruct((grp, t, LANES), F32)],
        scratch_shapes=[pltpu.VMEM((tm, d), BF16)],
        compiler_params=_params("arbitrary", "arbitrary", "arbitrary"),
        name="mixer_c_in_proj",
    )(*args)


def _ctx_attn_kernel(q_ref, k_ref, v_ref, o_ref):
    s = _dot_nt(q_ref[0].astype(BF16), k_ref[0].astype(BF16))
    p = jnp.exp(s - jnp.max(s, axis=-1, keepdims=True))
    o = _dot(p.astype(BF16), v_ref[0].astype(BF16)) / jnp.sum(p, axis=-1, keepdims=True)
    o_ref[0] = o.astype(BF16)


def _ctx_attention(qkv):
    b, l, _ = qkv.shape
    blk = lambda off: pl.BlockSpec((1, l, HD_A), lambda i, h: (i, 0, off + h))
    return pl.pallas_call(
        _ctx_attn_kernel,
        grid=(b, H_A),
        in_specs=[blk(0), blk(H_A), blk(2 * H_A)],
        out_specs=pl.BlockSpec((1, l, HD_A), lambda i, h: (i, 0, h)),
        out_shape=jax.ShapeDtypeStruct((b, l, D_A), BF16),
        compiler_params=_params("arbitrary", "arbitrary"),
        name="ctx_attention",
    )(qkv, qkv, qkv)


def _na_bias_tables(rpb, rows):
    nblk = rows // NA_ROWS
    n_dr, n_dc = 2 * WIN_R - 1, 2 * WIN_C - 1
    c = np.arange(GRID_W)[:, None]
    kc = np.arange(GRID_W)[None, :]
    dc = np.clip(kc - c + WIN_C - 1, 0, n_dc - 1)
    pick_c = (dc[None] == np.arange(n_dc)[:, None, None]).astype(np.float32)
    c0 = np.clip(c - WIN_C // 2, 0, GRID_W - WIN_C)
    ok_c = (kc >= c0) & (kc < c0 + WIN_C)
    by_col = jnp.einsum("hrd,dck->hrck", rpb, pick_c, precision=lax.Precision.HIGHEST)
    tables = []
    for i in (0, 1, nblk - 1):
        ks = int(np.clip(NA_ROWS * i - WIN_R // 2, 0, rows - NA_KEY_ROWS))
        r = (NA_ROWS * i + np.arange(NA_ROWS))[:, None]
        kr = (ks + np.arange(NA_KEY_ROWS))[None, :]
        r0 = np.clip(r - WIN_R // 2, 0, rows - WIN_R)
        ok_r = (kr >= r0) & (kr < r0 + WIN_R)
        dr = np.clip(kr - r + WIN_R - 1, 0, n_dr - 1)
        pick_r = (dr[None] == np.arange(n_dr)[:, None, None]).astype(np.float32)
        bias = jnp.einsum("hrck,rqs->hqcsk", by_col, pick_r, precision=lax.Precision.HIGHEST)
        ok = ok_r[None, :, None, :, None] & ok_c[None, None, :, None, :]
        bias = jnp.where(ok, bias, -jnp.inf)
        tables.append(bias.reshape(H_A, NA_ROWS * GRID_W, NA_KEY_ROWS * GRID_W))
    return jnp.stack(tables, axis=1)


def _na_kernel(q_ref, k_ref, v_ref, ck_ref, cv_ref, bias_ref, o_ref, *, rows):
    i = pl.program_id(2)
    nblk = rows // NA_ROWS
    ks = jnp.clip(NA_ROWS * i - WIN_R // 2, 0, rows - NA_KEY_ROWS)
    start = pl.multiple_of(ks * GRID_W, GRID_W)
    nkey = NA_KEY_ROWS * GRID_W
    kind = jnp.where(i == 0, 0, jnp.where(i == nblk - 1, 2, 1))
    ones_w = jnp.ones((nkey, HD_A), BF16)
    ones_c = jnp.ones((ck_ref.shape[1], HD_A), BF16)
    for hh in range(NA_HEADS):
        sl = slice(hh * HD_A, (hh + 1) * HD_A)
        q = q_ref[0, :, sl]
        kw = k_ref[0, pl.ds(start, nkey), sl]
        vw = jnp.concatenate([v_ref[0, pl.ds(start, nkey), sl], ones_w], axis=1)
        vc = jnp.concatenate([cv_ref[0, :, sl], ones_c], axis=1)
        s_w = _dot_nt(q, kw) + bias_ref[hh, kind]
        s_c = _dot_nt(q, ck_ref[0, :, sl])
        m = jnp.maximum(jnp.max(s_w, axis=-1, keepdims=True), jnp.max(s_c, axis=-1, keepdims=True))
        p_w = jnp.exp(s_w - m).astype(BF16)
        p_c = jnp.exp(s_c - m).astype(BF16)
        o = _dot(p_w, vw) + _dot(p_c, vc)
        o_ref[0, :, sl] = (o[:, :HD_A] / o[:, HD_A:]).astype(BF16)


def _neighbourhood_attention(qkv, ck, cv, rpb):
    b, t, _ = qkv.shape
    l = ck.shape[1]
    rows = t // GRID_W
    nblk = rows // NA_ROWS
    tq = NA_ROWS * GRID_W
    bias = _na_bias_tables(rpb, rows)
    hw = NA_HEADS * HD_A
    ngrp = H_A // NA_HEADS
    full = lambda off: pl.BlockSpec((1, t, hw), lambda h, n, i: (n, 0, off + h))
    ctx = pl.BlockSpec((1, l, hw), lambda h, n, i: (n, 0, h))
    return pl.pallas_call(
        functools.partial(_na_kernel, rows=rows),
        grid=(ngrp, b, nblk),
        in_specs=[
            pl.BlockSpec((1, tq, hw), lambda h, n, i: (n, i, h)),
            full(ngrp), full(2 * ngrp), ctx, ctx,
            pl.BlockSpec((NA_HEADS, 3, tq, NA_KEY_ROWS * GRID_W), lambda h, n, i: (h, 0, 0, 0)),
        ],
        out_specs=pl.BlockSpec((1, tq, hw), lambda h, n, i: (n, i, h)),
        out_shape=jax.ShapeDtypeStruct((b, t, D_A), BF16),
        compiler_params=_params("arbitrary", "arbitrary", "arbitrary"),
        name="neighbourhood_attention",
    )(qkv, qkv, qkv, ck, cv, bias)


def _lru_kernel(xb_ref, gb_ref, cw_ref, cb_ref, w4h_ref, b4_ref, lam_ref, h0_ref, o_ref, fin_ref,
                xpad_scr, af_scr, uf_scr, ab_scr, ub_scr, *, t, chunk):
    seg = t // LRU_SEGS
    pitch = seg + SUBLANES
    pad = SUBLANES
    zeros = jnp.zeros((pad, BD_B), F32)
    xpad_scr[0:pad, :] = zeros
    xpad_scr[pad + t:2 * pad + t, :] = zeros
    xpad_scr[pad:pad + t, :] = xb_ref[0]

    c1 = (-0.5 * LRU_C) * _softplus(-lam_ref[...])
    w4h = w4h_ref[0]
    b4h = 0.5 * b4_ref[0]
    left = (CONV_W - 1) // 2
    for c in range(t // chunk):
        base = c * chunk
        xc = cb_ref[...]
        for k in range(CONV_W):
            xc = xc + xpad_scr[pl.ds(base + pad - left + k, chunk), :] * cw_ref[k:k + 1, :]
        th = jnp.tanh(_dot(xc.astype(BF16), w4h) + b4h)
        xh = 0.5 * xc
        for d, (a_scr, u_scr) in enumerate(((af_scr, uf_scr), (ab_scr, ub_scr))):
            tr = th[:, 2 * d * BD_B:(2 * d + 1) * BD_B]
            ti = th[:, (2 * d + 1) * BD_B:(2 * d + 2) * BD_B]
            log_a = c1[d:d + 1, :] * tr + c1[d:d + 1, :]
            a = jnp.exp(log_a)
            u = jnp.sqrt(-jnp.tanh(log_a) * (a * a + 1.0)) * (xh * ti + xh)
            for p in range(chunk // seg):
                s = (base + p * seg) // seg
                a_scr[s * pitch:s * pitch + seg, :] = a[p * seg:(p + 1) * seg, :]
                u_scr[s * pitch:s * pitch + seg, :] = u[p * seg:(p + 1) * seg, :]

    def scan(j, carry):
        hf, pf, hb, pb = carry
        rf = pl.ds(j, LRU_SEGS, stride=pitch)
        rb = pl.ds(seg - 1 - j, LRU_SEGS, stride=pitch)
        a = af_scr[rf, :]
        hf = a * hf + uf_scr[rf, :]
        pf = a * pf
        uf_scr[rf, :] = hf
        af_scr[rf, :] = pf
        a = ab_scr[rb, :]
        hb = a * hb + ub_scr[rb, :]
        pb = a * pb
        ub_scr[rb, :] = hb
        ab_scr[rb, :] = pb
        return hf, pf, hb, pb

    zero = jnp.zeros((LRU_SEGS, BD_B), F32)
    one = jnp.ones((LRU_SEGS, BD_B), F32)
    hf, pf, hb, pb = lax.fori_loop(0, seg, scan, (zero, one, zero, one), unroll=4)

    cf = h0_ref[0, 0:1, :]
    cin_f = []
    for s in range(LRU_SEGS):
        cin_f.append(cf)
        cf = hf[s:s + 1, :] + pf[s:s + 1, :] * cf
    cb = h0_ref[0, 1:2, :]
    cin_b = [None] * LRU_SEGS
    for s in reversed(range(LRU_SEGS)):
        cin_b[s] = cb
        cb = hb[s:s + 1, :] + pb[s:s + 1, :] * cb
    fin_ref[0, 0:1, :] = cf
    fin_ref[0, 1:2, :] = cb

    for s in range(LRU_SEGS):
        rows = slice(s * pitch, s * pitch + seg)
        hs = (uf_scr[rows, :] + af_scr[rows, :] * cin_f[s]) + (ub_scr[rows, :] + ab_scr[rows, :] * cin_b[s])
        nat = slice(s * seg, (s + 1) * seg)
        o_ref[0, nat, :] = (hs * jax.nn.gelu(gb_ref[0, nat, :])).astype(BF16)


def _rglru(xg, conv_w, conv_b, w4h, b4, lam, h0):
    b, t, _ = xg.shape
    chunk = min(t, 512)
    seg = t // LRU_SEGS
    col = lambda off: pl.BlockSpec((1, t, BD_B), lambda n, h: (n, 0, off + h))
    return pl.pallas_call(
        functools.partial(_lru_kernel, t=t, chunk=chunk),
        grid=(b, H_B),
        in_specs=[
            col(0), col(H_B),
            pl.BlockSpec((CONV_W, BD_B), lambda n, h: (0, h)),
            pl.BlockSpec((1, BD_B), lambda n, h: (0, h)),
            pl.BlockSpec((1, BD_B, 4 * BD_B), lambda n, h: (h, 0, 0)),
            pl.BlockSpec((1, 1, 4 * BD_B), lambda n, h: (h, 0, 0)),
            pl.BlockSpec((2, BD_B), lambda n, h: (0, h)),
            pl.BlockSpec((1, 2, BD_B), lambda n, h: (n, 0, h)),
        ],
        out_specs=[pl.BlockSpec((1, t, BD_B), lambda n, h: (n, 0, h)),
                   pl.BlockSpec((1, 2, BD_B), lambda n, h: (n, 0, h))],
        out_shape=[jax.ShapeDtypeStruct((b, t, D_RNN), BF16),
                   jax.ShapeDtypeStruct((b, 2, D_RNN), F32)],
        scratch_shapes=[pltpu.VMEM((t + 2 * SUBLANES, BD_B), F32)]
        + [pltpu.VMEM((LRU_SEGS * (seg + SUBLANES), BD_B), F32)] * 4,
        compiler_params=_params("arbitrary", "arbitrary"),
        name="rglru_branch",
    )(xg, xg, conv_w, conv_b.reshape(1, D_RNN), w4h, b4, lam, h0)


def _outproj_ab_kernel(x_ref, mod_ref, oa_ref, ob_ref, w_ref, o_ref):
    y = _dot(oa_ref[0], w_ref[0:D_A, :]) + _dot(ob_ref[0], w_ref[D_A:D_A + D_RNN, :])
    o_ref[0] = x_ref[0] + mod_ref[0, 5:6, :] * y


def _outproj_ab(x, mod, oa, ob, w):
    grp, t, d = x.shape
    tm = TOKEN_TILE
    tok = lambda width: pl.BlockSpec((1, tm, width), lambda b, i: (b, i, 0))
    return pl.pallas_call(
        _outproj_ab_kernel,
        grid=(grp, t // tm),
        in_specs=[
            tok(d),
            pl.BlockSpec((1, N_MOD, d), lambda b, i: (b, 0, 0)),
            tok(D_A), tok(D_RNN),
            pl.BlockSpec((D_A + D_RNN, d), lambda b, i: (0, 0)),
        ],
        out_specs=tok(d),
        out_shape=jax.ShapeDtypeStruct(x.shape, F32),
        compiler_params=_params("arbitrary", "arbitrary"),
        name="mixer_ab_out_proj",
    )(x, mod, oa, ob, w)


def _gate_prep_kernel(g_ref, b_ref, o_ref):
    t = g_ref.shape[2]
    g = g_ref[0] + b_ref[...]
    lane = lax.broadcasted_iota(jnp.int32, (H_C, t), 1) & (CHUNK - 1)
    cum = _log_sigmoid(g[H_C:2 * H_C])
    suf = _log_sigmoid(g[3 * H_C:4 * H_C])
    sh = 1
    while sh < CHUNK:
        cum = cum + jnp.where(lane >= sh, pltpu.roll(cum, sh, 1), 0.0)
        suf = suf + jnp.where(lane < CHUNK - sh, pltpu.roll(suf, t - sh, 1), 0.0)
        sh *= 2
    o_ref[0, 0:H_C] = g[0:H_C]
    o_ref[0, H_C:2 * H_C] = cum
    o_ref[0, 2 * H_C:3 * H_C] = g[2 * H_C:3 * H_C]
    o_ref[0, 3 * H_C:4 * H_C] = suf


def _gate_prep(g_rows, b_gate):
    b, r, t = g_rows.shape
    return pl.pallas_call(
        _gate_prep_kernel,
        grid=(b,),
        in_specs=[pl.BlockSpec((1, r, t), lambda n: (n, 0, 0)), pl.BlockSpec((r, 1), lambda n: (0, 0))],
        out_specs=pl.BlockSpec((1, r, t), lambda n: (n, 0, 0)),
        out_shape=jax.ShapeDtypeStruct(g_rows.shape, F32),
        compiler_params=_params("arbitrary"),
        name="mlstm_gate_prep",
    )(g_rows, b_gate.reshape(r, 1))


def _mlstm_chunk(q, kt, v_aug, ig_row, bc_row, bc_col, cm, m, reverse):
    ti = lax.broadcasted_iota(jnp.int32, (CHUNK, CHUNK), 0)
    si = lax.broadcasted_iota(jnp.int32, (CHUNK, CHUNK), 1)
    seen = (si >= ti) if reverse else (si <= ti)
    bk = bc_row[:, 0:1] if reverse else bc_row[:, CHUNK - 1:CHUNK]
    bc_rep = jnp.broadcast_to(bc_col, (CHUNK, CHUNK))
    dmat = jnp.where(seen, bc_rep - bc_row + ig_row, -jnp.inf)
    inter = bc_rep + m
    m_t = jnp.maximum(inter, jnp.broadcast_to(jnp.max(dmat, axis=1, keepdims=True), (CHUNK, CHUNK)))
    w = jnp.exp(dmat - m_t)
    g = jnp.exp(inter - m_t)
    kscale = DK_C ** -0.5
    s = _dot(q, kt) * (w * kscale)
    tot = _dot(s.astype(BF16), v_aug) + jnp.concatenate([g, g, g], axis=1) * _dot(q, cm.astype(BF16))
    inv = 1.0 / jnp.maximum(jnp.abs(tot[:, DV_C:]), jnp.exp(-m_t))
    hout = tot[:, :DV_C] * jnp.concatenate([inv, inv], axis=1)
    dend = bk - bc_row + ig_row
    m_new = jnp.maximum(bk + m, jnp.max(dend, axis=1, keepdims=True))
    we = jnp.exp(dend - m_new)
    ge = jnp.exp(bk + m - m_new)
    kwt = kt.astype(F32) * (we * kscale)
    c_new = ge * cm + _dot(kwt.astype(BF16), v_aug)
    return hout, c_new, m_new


def _mlstm_kernel(*refs, zero_state):
    if zero_state:
        qvf_ref, vvf_ref, ktf_ref, qvb_ref, vvb_ref, ktb_ref, gr_ref, gc_ref = refs[:8]
        rest = refs[8:]
    else:
        m0_ref, qvf_ref, vvf_ref, ktf_ref, qvb_ref, vvb_ref, ktb_ref, gr_ref, gc_ref = refs[:9]
        c0_ref, n0_ref = refs[9:11]
        rest = refs[11:]
    hf_ref, hb_ref, cfin_ref, nfin_ref, mfin_ref, c_scr, m_scr = rest
    n, hp, c = pl.program_id(0), pl.program_id(1), pl.program_id(2)
    eye = (lax.broadcasted_iota(jnp.int32, (DK_C, DK_C), 0) == lax.broadcasted_iota(jnp.int32, (DK_C, DK_C), 1))

    @pl.when(c == 0)
    def _():
        if zero_state:
            c_scr[...] = jnp.zeros_like(c_scr)
            m_scr[...] = jnp.zeros_like(m_scr)
            return
        for d in range(2):
            for j in range(MLSTM_HEADS):
                c_scr[d, j, :, 0:DV_C] = c0_ref[0, d, j]
                n_col = jnp.sum(jnp.where(eye, n0_ref[0, d, j], 0.0), axis=1, keepdims=True)
                c_scr[d, j, :, DV_C:DV_C + LANES] = jnp.broadcast_to(n_col, (DK_C, LANES))
                m_scr[d, j] = jnp.full(m_scr.shape[2:], m0_ref[n, d * H_C + hp * MLSTM_HEADS + j], F32)

    ones_col = jnp.ones((CHUNK, LANES), BF16)
    dirs = ((qvf_ref, vvf_ref, ktf_ref, hf_ref), (qvb_ref, vvb_ref, ktb_ref, hb_ref))
    nc = pl.num_programs(2)
    chains = []
    for d, (q_ref, v_ref, kt_ref, h_ref) in enumerate(dirs):
        cc = c if d == 0 else nc - 1 - c
        rows = pl.ds(pl.multiple_of(cc * CHUNK, CHUNK), CHUNK)
        for j in range(MLSTM_HEADS):
            sl = slice(j * DK_C, (j + 1) * DK_C)
            v_aug = jnp.concatenate([v_ref[0, :, sl], ones_col], axis=1)
            col = (2 * d + 1) * MLSTM_HEADS + j
            chains.append((d, j, h_ref, sl, _mlstm_chunk(
                q_ref[0, :, sl], kt_ref[0, sl, :], v_aug,
                gr_ref[0, j, 2 * d, pl.ds(cc, 1), :], gr_ref[0, j, 2 * d + 1, pl.ds(cc, 1), :],
                gc_ref[0, 0, rows, col:col + 1],
                c_scr[d, j], m_scr[d, j, 0:1, 0:1], reverse=(d == 1))))
    for d, j, h_ref, sl, (hout, c_new, m_new) in chains:
        h_ref[0, :, sl] = hout
        c_scr[d, j] = c_new
        m_scr[d, j] = jnp.broadcast_to(m_new, m_scr.shape[2:])

    @pl.when(c == pl.num_programs(2) - 1)
    def _():
        for d in range(2):
            for j in range(MLSTM_HEADS):
                cfin_ref[0, d, j] = c_scr[d, j, :, 0:DV_C]
                n_col = c_scr[d, j, :, DV_C:DV_C + 1]
                nfin_ref[0, d, j] = jnp.sum(jnp.where(eye, n_col, 0.0), axis=0, keepdims=True)
                mfin_ref[0, d, j] = m_scr[d, j, 0:1, :]


def _rope_tables(t):
    half = DK_C // 2
    nf = half // 2
    pos = jnp.arange(t)
    freqs = ROPE_BASE ** (-jnp.arange(nf, dtype=F32) / nf)
    cos, sin = [], []
    for p in (pos // GRID_W, pos % GRID_W):
        ang = p.astype(F32)[:, None] * freqs[None, :]
        cos += [jnp.cos(ang), jnp.cos(ang)]
        sin += [-jnp.sin(ang), jnp.sin(ang)]
    return jnp.concatenate(cos, axis=-1), jnp.concatenate(sin, axis=-1)


def _mlstm(qv, kt, gates, c0, n0, m0):
    b, t, _ = qv.shape
    nc = t // CHUNK
    hps = MLSTM_HEADS
    ngrp = H_C // hps
    g4 = gates.reshape(b, 4, H_C, t)
    g_row = jnp.transpose(g4, (0, 2, 1, 3)).reshape(b, H_C, 4, nc, CHUNK)
    g_col = jnp.transpose(gates.reshape(b, 4, ngrp, hps, t), (0, 2, 4, 1, 3)).reshape(b, ngrp, t, 4 * hps)
    fwd = lambda c: c
    bwd = lambda c: nc - 1 - c

    def tok(off, order):
        return pl.BlockSpec((1, CHUNK, hps * DK_C), lambda n, h, c: (n, order(c), off + h))

    def kt_spec(order):
        return pl.BlockSpec((1, hps * DK_C, CHUNK), lambda n, h, c: (n, h, order(c)))

    g_row_spec = pl.BlockSpec((1, hps, 4, nc, CHUNK), lambda n, h, c: (n, h, 0, 0, 0))
    g_col_spec = pl.BlockSpec((1, 1, t, 4 * hps), lambda n, h, c: (n, h, 0, 0))

    state = lambda *tail: pl.BlockSpec((1, 2, hps) + tail, lambda n, h, c: (n, 0, h) + (0,) * len(tail))
    hspec = lambda order: pl.BlockSpec((1, CHUNK, hps * DV_C), lambda n, h, c: (n, order(c), h))
    in_specs = [tok(0, fwd), tok(ngrp, fwd), kt_spec(fwd),
                tok(0, bwd), tok(ngrp, bwd), kt_spec(bwd),
                g_row_spec, g_col_spec]
    args = [qv, qv, kt, qv, qv, kt, g_row, g_col]
    zero_state = c0 is None
    if not zero_state:
        in_specs = [pl.BlockSpec(memory_space=pltpu.SMEM)] + in_specs + [state(DK_C, DV_C), state(1, DK_C)]
        args = [m0.reshape(b, 2 * H_C)] + args + [c0, n0.reshape(b, 2, H_C, 1, DK_C)]
    hf, hb, cfin, nfin, mfin = pl.pallas_call(
        functools.partial(_mlstm_kernel, zero_state=zero_state),
        grid=(b, ngrp, nc),
        in_specs=in_specs,
        out_specs=[hspec(fwd), hspec(bwd), state(DK_C, DV_C), state(1, DK_C), state(1, LANES)],
        out_shape=[jax.ShapeDtypeStruct((b, t, D_C), F32), jax.ShapeDtypeStruct((b, t, D_C), F32),
                   jax.ShapeDtypeStruct((b, 2, H_C, DK_C, DV_C), F32),
                   jax.ShapeDtypeStruct((b, 2, H_C, 1, DK_C), F32),
                   jax.ShapeDtypeStruct((b, 2, H_C, 1, LANES), F32)],
        scratch_shapes=[pltpu.VMEM((2, hps, DK_C, DV_C + LANES), F32),
                        pltpu.VMEM((2, hps, SUBLANES, LANES), F32)],
        compiler_params=_params("arbitrary", "arbitrary", "arbitrary"),
        name="mlstm_scan",
    )(*args)
    return hf, hb, cfin, nfin.reshape(b, 2, H_C, DK_C), mfin[:, :, :, 0, 0]


def _outproj_c_kernel(x_ref, mod_ref, hf_ref, hb_ref, og_ref, g_ref, w_ref, o_ref, hn_scr):
    for h in range(H_C):
        sl = slice(h * DV_C, (h + 1) * DV_C)
        hs = hf_ref[0, :, sl] + hb_ref[0, :, sl]
        hn_scr[:, sl] = (_rms(hs, g_ref[:, sl]) * jax.nn.sigmoid(og_ref[0, :, sl])).astype(BF16)
    o_ref[0] = x_ref[0] + mod_ref[0, 5:6, :] * _dot(hn_scr[...], w_ref[...])


def _outproj_c(x, mod, hf, hb, og, mh_g, w):
    grp, t, d = x.shape
    tm = TOKEN_TILE // 2
    tok = pl.BlockSpec((1, tm, d), lambda b, i: (b, i, 0))
    return pl.pallas_call(
        _outproj_c_kernel,
        grid=(grp, t // tm),
        in_specs=[
            tok,
            pl.BlockSpec((1, N_MOD, d), lambda b, i: (b, 0, 0)),
            tok, tok,
            pl.BlockSpec((1, tm, D_C), lambda b, i: (b, i, 0)),
            pl.BlockSpec((1, D_C), lambda b, i: (0, 0)),
            pl.BlockSpec((D_C, d), lambda b, i: (0, 0)),
        ],
        out_specs=tok,
        out_shape=jax.ShapeDtypeStruct(x.shape, F32),
        scratch_shapes=[pltpu.VMEM((tm, D_C), BF16)],
        compiler_params=_params("arbitrary", "arbitrary"),
        name="mixer_c_out_proj",
    )(x, mod, hf, hb, og, mh_g.reshape(1, D_C), w)


def _mixer_ab(j, xp, xs, mp, ms, g, p):
    bp, lp, bs = p["bp"], p["lp"], xs.shape[0]
    w_in = p["w_in_ab"][j].astype(BF16)
    w_out = p["w_out_ab"][j].astype(BF16)
    gains = jnp.stack([jnp.tile(p["qn_g"][j] * (HD_A ** -0.5), H_A), jnp.tile(p["kn_g"][j], H_A),
                       jnp.ones((D_A,), F32)]).reshape(3, 1, D_A)
    wa, wx, ba, bx = p["lru_wa"][j], p["lru_wx"][j], p["lru_ba"][j], p["lru_bx"][j]
    w4 = (0.5 * jnp.concatenate([wa[0], wx[0], wa[1], wx[1]], axis=-1)).astype(BF16)
    b4 = jnp.concatenate([v.reshape(H_B, 1, BD_B) for v in (ba[0], bx[0], ba[1], bx[1])], axis=-1)
    lru = (p["conv_w"][j], p["conv_b"][j], w4, b4, p["lru_lam"][j])

    qkv, xg = _inproj_ab(xp, mp, g, w_in, gains, F32)
    qkv = qkv.reshape(bp, lp, 3 * D_A)
    oa = _ctx_attention(qkv)
    ob, h_fin = _rglru(xg.reshape(bp, lp, 2 * D_RNN), *lru, jnp.zeros((bp, 2, D_RNN), F32))
    xp = _outproj_ab(xp, mp, oa.reshape(1, bp * lp, D_A), ob.reshape(1, bp * lp, D_RNN), w_out)
    new = (qkv[:, :, D_A:2 * D_A].reshape(bp, lp, H_A, HD_A), qkv[:, :, 2 * D_A:].reshape(bp, lp, H_A, HD_A), h_fin)

    past = p["cache_k"].shape[2]
    ck = p["cache_k"][:, j].reshape(bs, past, D_A).astype(BF16)
    cv = p["cache_v"][:, j].reshape(bs, past, D_A).astype(BF16)
    qkv, xg = _inproj_ab(xs, ms, g, w_in, gains, BF16)
    oa = _neighbourhood_attention(qkv, ck, cv, p["rpb"][j])
    ob, _ = _rglru(xg, *lru, p["state_lru"][:, j])
    xs = _outproj_ab(xs, ms, oa, ob, w_out)
    return xp, xs, new


def _mixer_c(j, xp, xs, mp, ms, g, p):
    bp, lp, bs, ts = p["bp"], p["lp"], xs.shape[0], xs.shape[1]
    w = p["w_in_c"][j].astype(BF16)
    wkt = w[:, D_C:2 * D_C].T
    w_gate = jnp.pad(w[:, 4 * D_C:], ((0, 0), (0, LANES - 4 * H_C)))
    w_out = p["w_out_c"][j].astype(BF16)

    def gates_of(zg, b, t):
        g_rows = jnp.transpose(zg.reshape(b, t, LANES)[:, :, :4 * H_C], (0, 2, 1))
        return _gate_prep(g_rows, p["b_gate_c"][j])

    qv, og, kt, zg = _inproj_c(xp, mp, g, w, wkt, w_gate, None)
    kt = jnp.transpose(kt.reshape(D_C, bp, lp), (1, 0, 2))
    hf, hb, cfin, nfin, mfin = _mlstm(
        qv.reshape(bp, lp, 2 * D_C), kt, gates_of(zg, bp, lp), None, None, None)
    xp = _outproj_c(xp, mp, hf.reshape(1, bp * lp, D_C), hb.reshape(1, bp * lp, D_C), og, p["mh_norm_g"][j], w_out)

    qv, og, kt, zg = _inproj_c(xs, ms, g, w, wkt, w_gate, _rope_tables(ts))
    hf, hb, _, _, _ = _mlstm(qv, kt, gates_of(zg, bs, ts), p["state_mlstm_c"][:, j], p["state_mlstm_n"][:, j],
                             p["state_mlstm_m"][:, j])
    xs = _outproj_c(xs, ms, hf, hb, og, p["mh_norm_g"][j], w_out)
    return xp, xs, (cfin, nfin, mfin)


def kernel(x_prompt, x_sample, cache_k, cache_v, state_lru, state_mlstm_c, state_mlstm_n, state_mlstm_m, c, c_ctx, w_mod, b_mod, norm_g, w_ffn_gate, w_ffn_up, w_ffn_down, w_in_ab, w_out_ab, qn_g, kn_g, rpb, conv_w, conv_b, lru_wa, lru_ba, lru_wx, lru_bx, lru_lam, w_in_c, b_gate_c, mh_norm_g, w_out_c):
    bp, lp, d = x_prompt.shape
    bs = x_sample.shape[0]
    p = dict(bp=bp, lp=lp, cache_k=cache_k, cache_v=cache_v, state_lru=state_lru, state_mlstm_c=state_mlstm_c,
             state_mlstm_n=state_mlstm_n, state_mlstm_m=state_mlstm_m, w_in_ab=w_in_ab, w_out_ab=w_out_ab,
             qn_g=qn_g, kn_g=kn_g, rpb=rpb, conv_w=conv_w, conv_b=conv_b, lru_wa=lru_wa, lru_ba=lru_ba,
             lru_wx=lru_wx, lru_bx=lru_bx, lru_lam=lru_lam, w_in_c=w_in_c, b_gate_c=b_gate_c,
             mh_norm_g=mh_norm_g, w_out_c=w_out_c)

    cond = jnp.concatenate([c, c_ctx[None, :]], axis=0)
    cond = jnp.pad(cond, ((0, -(bs + 1) % SUBLANES), (0, 0)))
    mod = _modulation(cond, w_mod, b_mod)[:, :bs + 1].reshape(DEPTH, bs + 1, N_MOD, d)

    wg = w_ffn_gate.astype(BF16)
    wu = w_ffn_up.astype(BF16)
    wd = w_ffn_down.astype(BF16)

    xp = x_prompt.reshape(1, bp * lp, d)
    xs = x_sample
    new_ab, new_c = [], []
    for l in range(DEPTH):
        j = l // 2
        ms = mod[l, :bs]
        mp = mod[l, bs:]
        xp = _ffn(xp, mp, norm_g[l, 0], wg, wu, wd, l, 0, 0)
        xs = _ffn(xs, ms, norm_g[l, 0], wg, wu, wd, l, 0, 0)
        if l % 2 == 0:
            xp, xs, new = _mixer_ab(j, xp, xs, mp, ms, norm_g[l, 1], p)
            new_ab.append(new)
        else:
            xp, xs, new = _mixer_c(j, xp, xs, mp, ms, norm_g[l, 1], p)
            new_c.append(new)
        xp = _ffn(xp, mp, norm_g[l, 2], wg, wu, wd, l, 1, 2)
        xs = _ffn(xs, ms, norm_g[l, 2], wg, wu, wd, l, 1, 2)
    stack = lambda items, i: jnp.concatenate([jnp.expand_dims(it[i], 1) for it in items], axis=1)
    return (xp.reshape(bp, lp, d), xs, stack(new_ab, 0), stack(new_ab, 1), stack(new_ab, 2),
            stack(new_c, 0), stack(new_c, 1), stack(new_c, 2))
```

```python
import functools

import numpy as np
import jax
import jax.numpy as jnp
from jax import lax
from jax.experimental import pallas as pl
from jax.experimental.pallas import tpu as pltpu

D_MODEL = 2048
DEPTH = 2
GRID_W = 64
N_MOD = 9
D_FF = 5632
EPS = 1e-6
H_A = 8
HD_A = 128
D_A = H_A * HD_A
WIN_R = 8
WIN_C = 16
D_RNN = 1024
H_B = 8
BD_B = D_RNN // H_B
CONV_W = 4
LRU_C = 8.0
H_C = 8
DK_C = 256
DV_C = 256
D_C = H_C * DV_C
CHUNK = 128
ROPE_BASE = 10000.0

BF16 = jnp.bfloat16
F32 = jnp.float32

V7X_VMEM_BYTES = 64 * 1024 * 1024
VMEM_LIMIT = V7X_VMEM_BYTES - 8 * 1024 * 1024
FFN_VMEM_LIMIT = V7X_VMEM_BYTES - 5 * 1024 * 1024
LANES = 128
SUBLANES = 8

TOKEN_TILE = 512
FF_TILE = 512
FFN_NORM_ROWS = 128
PROJ_TOKEN_TILE = 1024
PROJ_TILE = 512
NA_ROWS = 4
NA_KEY_ROWS = NA_ROWS + WIN_R
LRU_SEGS = SUBLANES
NA_HEADS = 4
MLSTM_HEADS = 4


def _params(*sem, vmem=VMEM_LIMIT):
    return pltpu.CompilerParams(dimension_semantics=sem, vmem_limit_bytes=vmem)


def _dot(a, b):
    return jnp.dot(a, b, preferred_element_type=F32)


def _dot_nt(a, b):
    return lax.dot_general(a, b, (((1,), (1,)), ((), ())), preferred_element_type=F32)


def _dot_tn(a, b):
    return lax.dot_general(a, b, (((0,), (0,)), ((), ())), preferred_element_type=F32)


def _rms(x, g):
    return x * lax.rsqrt(jnp.mean(x * x, axis=-1, keepdims=True) + EPS) * g


def _adaln(x, g, mod_ref, j):
    return _rms(x, g) * (1.0 + mod_ref[0, 3 * j + 1:3 * j + 2, :]) + mod_ref[0, 3 * j:3 * j + 1, :]


def _softplus(x):
    return jnp.maximum(x, 0.0) + jnp.log1p(jnp.exp(-jnp.abs(x)))


def _log_sigmoid(x):
    return -_softplus(-x)


def _mod_kernel(c_ref, w_ref, b_ref, o_ref):
    c = c_ref[...]
    s = (c * jax.nn.sigmoid(c)).astype(BF16)
    o_ref[0] = _dot(s, w_ref[0].astype(BF16)) + b_ref[0]


def _modulation(cond, w_mod, b_mod):
    r = cond.shape[0]
    n = N_MOD * D_MODEL
    tn = 1024
    return pl.pallas_call(
        _mod_kernel,
        grid=(DEPTH, n // tn),
        in_specs=[
            pl.BlockSpec((r, D_MODEL), lambda l, j: (0, 0)),
            pl.BlockSpec((1, D_MODEL, tn), lambda l, j: (l, 0, j)),
            pl.BlockSpec((1, 1, tn), lambda l, j: (l, 0, j)),
        ],
        out_specs=pl.BlockSpec((1, r, tn), lambda l, j: (l, 0, j)),
        out_shape=jax.ShapeDtypeStruct((DEPTH, r, n), F32),
        compiler_params=_params("arbitrary", "arbitrary"),
        name="modulation",
    )(cond, w_mod, b_mod.reshape(DEPTH, 1, n))


def _ffn_kernel(x0_ref, xn_ref, mod_ref, modn_ref, g_ref, wga_ref, wua_ref, wda_ref, wgb_ref, wub_ref, wdb_ref,
                o_ref, ha_scr, hb_scr, xa_scr, xb_scr, acc_scr, *, j, ns, single_last):
    p, f2 = pl.program_id(0), pl.program_id(1)
    rows = xn_ref.shape[1]
    tm = ha_scr.shape[0]

    @pl.when((p == 0) & (f2 == 0))
    def _():
        xa_scr[...] = x0_ref[0]
        ha_scr[...] = _adaln(x0_ref[0], g_ref[...], mod_ref, j).astype(BF16)

    @pl.when((f2 == 0) | (f2 == ns))
    def _():
        acc_scr[...] = jnp.zeros_like(acc_scr)

    def body(h_cur, h_nxt, x_nxt, f, both):
        r0 = pl.multiple_of(jnp.minimum(f, tm // rows - 1) * rows, rows)
        xs = xn_ref[0]
        x_nxt[pl.ds(r0, rows), :] = xs
        h_nxt[pl.ds(r0, rows), :] = _adaln(xs, g_ref[...], modn_ref, j).astype(BF16)
        h = h_cur[...]
        hidden = ((wga_ref, wua_ref, wda_ref), (wgb_ref, wub_ref, wdb_ref)) if both else ((wga_ref, wua_ref, wda_ref),)
        y = None
        for wg_ref, wu_ref, wd_ref in hidden:
            a = _dot(h, wg_ref[...])
            u = _dot(h, wu_ref[...])
            act = (a * jax.nn.sigmoid(a)) * u
            yd = _dot(act.astype(BF16), wd_ref[...])
            y = yd if y is None else y + yd
        acc_scr[...] += y

    n_both = ns - 1 if single_last else ns
    for half, (h_cur, h_nxt, x_nxt) in enumerate(((ha_scr, hb_scr, xb_scr), (hb_scr, ha_scr, xa_scr))):
        @pl.when((f2 >= half * ns) & (f2 < half * ns + n_both))
        def _():
            body(h_cur, h_nxt, x_nxt, f2 - half * ns, True)

        if single_last:
            @pl.when(f2 == half * ns + ns - 1)
            def _():
                body(h_cur, h_nxt, x_nxt, ns - 1, False)

    for last, x_cur in ((ns - 1, xa_scr), (2 * ns - 1, xb_scr)):
        @pl.when(f2 == last)
        def _():
            o_ref[0] = x_cur[...] + 0.5 * mod_ref[0, 3 * j + 2:3 * j + 3, :] * acc_scr[...]


def _ffn(x, mod, g, wg, wu, wd, l, s, j):
    grp, t, d = x.shape
    tm, tf = TOKEN_TILE, FF_TILE
    nf = D_FF // tf
    ns = (nf + 1) // 2
    per_group = t // tm
    ntile = grp * per_group
    rows = FFN_NORM_ROWS
    nsl = tm // rows
    assert ntile % 2 == 0 and ns >= nsl
    tile = lambda p, f2: 2 * p + f2 // ns
    nxt = lambda p, f2: jnp.minimum(tile(p, f2) + 1, ntile - 1)
    fa = lambda f2: jnp.minimum(2 * (f2 % ns), nf - 1)
    fb = lambda f2: jnp.minimum(2 * (f2 % ns) + 1, nf - 1)
    wcol = lambda fsel: pl.BlockSpec((None, None, d, tf), lambda p, f2: (l, s, 0, fsel(f2)))
    wrow = lambda fsel: pl.BlockSpec((None, None, tf, d), lambda p, f2: (l, s, fsel(f2), 0))
    out = pl.pallas_call(
        functools.partial(_ffn_kernel, j=j, ns=ns, single_last=nf % 2 == 1),
        grid=(ntile // 2, 2 * ns),
        in_specs=[
            pl.BlockSpec((1, tm, d), lambda p, f2: (0, 0, 0), pipeline_mode=pl.Buffered(1)),
            pl.BlockSpec((1, rows, d), lambda p, f2: (0, nxt(p, f2) * nsl + jnp.minimum(f2 % ns, nsl - 1), 0)),
            pl.BlockSpec((1, N_MOD, d), lambda p, f2: (tile(p, f2) // per_group, 0, 0)),
            pl.BlockSpec((1, N_MOD, d), lambda p, f2: (nxt(p, f2) // per_group, 0, 0)),
            pl.BlockSpec((1, d), lambda p, f2: (0, 0)),
            wcol(fa), wcol(fa), wrow(fa), wcol(fb), wcol(fb), wrow(fb),
        ],
        out_specs=pl.BlockSpec((1, tm, d), lambda p, f2: (0, tile(p, f2), 0)),
        out_shape=jax.ShapeDtypeStruct((1, grp * t, d), F32),
        scratch_shapes=[pltpu.VMEM((tm, d), BF16), pltpu.VMEM((tm, d), BF16),
                        pltpu.VMEM((tm, d), F32), pltpu.VMEM((tm, d), F32), pltpu.VMEM((tm, d), F32)],
        compiler_params=_params("arbitrary", "arbitrary", vmem=FFN_VMEM_LIMIT),
        name="macaron_ffn",
    )(x.reshape(1, grp * t, d), x.reshape(1, grp * t, d), mod, mod, g.reshape(1, d), wg, wu, wd, wg, wu, wd)
    return out.reshape(x.shape)


def _rope(x, cos, sin):
    half = DK_C // 2
    out = []
    for p in range(2):
        sl = slice(p * half, (p + 1) * half)
        xs = x[:, sl]
        out.append(xs * cos[:, sl] + pltpu.roll(xs, half // 2, 1) * sin[:, sl])
    return jnp.concatenate(out, axis=-1)


def _rope_t(x, cos, sin):
    q = DK_C // 4
    swapped = jnp.concatenate([x[q:2 * q], x[0:q], x[3 * q:4 * q], x[2 * q:3 * q]], axis=0)
    return x * cos + swapped * sin


def _inproj_ab_kernel(x_ref, mod_ref, g_ref, w_ref, gain_ref, oa_ref, ob_ref, h_scr, *, n_a, n_norm):
    k = pl.program_id(2)

    @pl.when(k == 0)
    def _():
        h_scr[...] = _adaln(x_ref[0], g_ref[...], mod_ref, 1).astype(BF16)

    @pl.when(k >= n_a)
    def _():
        ob_ref[0] = _dot(h_scr[...], w_ref[...])

    @pl.when(k < n_norm)
    def _():
        for pair in range(w_ref.shape[1] // (2 * HD_A)):
            z2 = _dot(h_scr[...], w_ref[:, 2 * pair * HD_A:2 * (pair + 1) * HD_A])
            for hh in range(2):
                zs = z2[:, hh * HD_A:(hh + 1) * HD_A]
                sl = slice((2 * pair + hh) * HD_A, (2 * pair + hh + 1) * HD_A)
                r = lax.rsqrt(jnp.mean(zs * zs, axis=-1, keepdims=True) + EPS)
                oa_ref[0, :, sl] = (zs * r * gain_ref[0, :, sl]).astype(oa_ref.dtype)

    @pl.when((k >= n_norm) & (k < n_a))
    def _():
        oa_ref[0] = (_dot(h_scr[...], w_ref[...]) * gain_ref[0]).astype(oa_ref.dtype)


def _inproj_ab(x, mod, g, w, gains, dtype_a):
    grp, t, d = x.shape
    tm, tn = PROJ_TOKEN_TILE, PROJ_TILE
    n_a = 3 * D_A // tn
    nt = w.shape[1] // tn
    return pl.pallas_call(
        functools.partial(_inproj_ab_kernel, n_a=n_a, n_norm=2 * D_A // tn),
        grid=(grp, t // tm, nt),
        in_specs=[
            pl.BlockSpec((1, tm, d), lambda b, i, k: (b, i, 0)),
            pl.BlockSpec((1, N_MOD, d), lambda b, i, k: (b, 0, 0)),
            pl.BlockSpec((1, d), lambda b, i, k: (0, 0)),
            pl.BlockSpec((d, tn), lambda b, i, k: (0, k)),
            pl.BlockSpec((1, 1, tn), lambda b, i, k: (jnp.minimum(k, n_a - 1), 0, 0)),
        ],
        out_specs=[pl.BlockSpec((1, tm, tn), lambda b, i, k: (b, i, jnp.minimum(k, n_a - 1))),
                   pl.BlockSpec((1, tm, tn), lambda b, i, k: (b, i, jnp.maximum(k - n_a, 0)))],
        out_shape=[jax.ShapeDtypeStruct((grp, t, n_a * tn), dtype_a),
                   jax.ShapeDtypeStruct((grp, t, (nt - n_a) * tn), F32)],
        scratch_shapes=[pltpu.VMEM((tm, d), BF16)],
        compiler_params=_params("arbitrary", "arbitrary", "arbitrary"),
        name="mixer_ab_in_proj",
    )(x, mod, g.reshape(1, d), w, gains.reshape(n_a, 1, tn))


def _inproj_c_kernel(*refs, rotary, n_q, n_qv, n_main):
    x_ref, mod_ref, g_ref, w_ref, wkt_ref, wgate_ref = refs[:6]
    pos = 6
    if rotary:
        cos_ref, sin_ref, cost_ref, sint_ref = refs[pos:pos + 4]
        pos += 4
    oa_ref, ob_ref, okt_ref, og_ref, h_scr = refs[pos:pos + 5]
    k = pl.program_id(2)
    tn = w_ref.shape[1]

    @pl.when(k == 0)
    def _():
        h = _adaln(x_ref[0], g_ref[...], mod_ref, 1).astype(BF16)
        h_scr[...] = h
        og_ref[0] = _dot(h, wgate_ref[...])

    @pl.when(k < n_q)
    def _():
        z = _dot(h_scr[...], w_ref[...])
        if rotary:
            for hh in range(tn // DK_C):
                sl = slice(hh * DK_C, (hh + 1) * DK_C)
                oa_ref[0, :, sl] = _rope(z[:, sl], cos_ref[...], sin_ref[...]).astype(BF16)
        else:
            oa_ref[0] = z.astype(BF16)

    @pl.when((k >= n_q) & (k < n_qv))
    def _():
        oa_ref[0] = _dot(h_scr[...], w_ref[...]).astype(BF16)

    @pl.when((k >= n_qv) & (k < n_main))
    def _():
        ob_ref[0] = _dot(h_scr[...], w_ref[...])

    @pl.when(k >= n_main)
    def _():
        zt = _dot_nt(wkt_ref[...], h_scr[...])
        for ci in range(zt.shape[1] // CHUNK):
            cols = slice(ci * CHUNK, (ci + 1) * CHUNK)
            if rotary:
                for hh in range(tn // DK_C):
                    sl = slice(hh * DK_C, (hh + 1) * DK_C)
                    okt_ref[0, ci, sl, :] = _rope_t(zt[sl, cols], cost_ref[:, cols], sint_ref[:, cols]).astype(BF16)
            else:
                okt_ref[0, ci] = zt[:, cols].astype(BF16)


def _inproj_c(x, mod, g, w, wkt, w_gate, rope_tables):
    grp, t, d = x.shape
    tm, tn = PROJ_TOKEN_TILE, PROJ_TILE
    n_q, n_qv, n_main = D_C // tn, 2 * D_C // tn, 3 * D_C // tn
    nt = n_main + D_C // tn

    def w_col(k):
        kk = jnp.minimum(k, n_main - 1)
        return jnp.where(kk < n_q, kk, kk + n_q)

    rotary = rope_tables is not None
    in_specs = [
        pl.BlockSpec((1, tm, d), lambda b, i, k: (b, i, 0)),
        pl.BlockSpec((1, N_MOD, d), lambda b, i, k: (b, 0, 0)),
        pl.BlockSpec((1, d), lambda b, i, k: (0, 0)),
        pl.BlockSpec((d, tn), lambda b, i, k: (0, w_col(k))),
        pl.BlockSpec((tn, d), lambda b, i, k: (jnp.maximum(k - n_main, 0), 0)),
        pl.BlockSpec((d, LANES), lambda b, i, k: (0, 0)),
    ]
    args = [x, mod, g.reshape(1, d), w, wkt, w_gate]
    if rotary:
        cos, sin = rope_tables
        in_specs += [pl.BlockSpec((tm, DK_C), lambda b, i, k: (i, 0))] * 2
        in_specs += [pl.BlockSpec((DK_C, tm), lambda b, i, k: (0, i))] * 2
        args += [cos, sin, cos.T, sin.T]
    return pl.pallas_call(
        functools.partial(_inproj_c_kernel, rotary=rotary, n_q=n_q, n_qv=n_qv, n_main=n_main),
        grid=(grp, t // tm, nt),
        in_specs=in_specs,
        out_specs=[pl.BlockSpec((1, tm, tn), lambda b, i, k: (b, i, jnp.minimum(k, n_qv - 1))),
                   pl.BlockSpec((1, tm, tn), lambda b, i, k: (b, i, jnp.clip(k - n_qv, 0, n_main - n_qv - 1))),
                   pl.BlockSpec((1, tm // CHUNK, tn, CHUNK), lambda b, i, k: (b, i, jnp.maximum(k - n_main, 0), 0)),
                   pl.BlockSpec((1, tm, LANES), lambda b, i, k: (b, i, 0))],
        out_shape=[jax.ShapeDtypeStruct((grp, t, 2 * D_C), BF16),
                   jax.ShapeDtypeStruct((grp, t, D_C), F32),
                   jax.ShapeDtypeStruct((grp, t // CHUNK, D_C, CHUNK), BF16),
                   jax.ShapeDtypeStruct((grp, t, LANES), F32)],
        scratch_shapes=[pltpu.VMEM((tm, d), BF16)],
        compiler_params=_params("arbitrary", "arbitrary", "arbitrary"),
        name="mixer_c_in_proj",
    )(*args)


def _ctx_attn_kernel(q_ref, k_ref, v_ref, o_ref):
    s = _dot_nt(q_ref[0].astype(BF16), k_ref[0].astype(BF16))
    p = jnp.exp(s - jnp.max(s, axis=-1, keepdims=True))
    o = _dot(p.astype(BF16), v_ref[0].astype(BF16)) / jnp.sum(p, axis=-1, keepdims=True)
    o_ref[0] = o.astype(BF16)


def _ctx_attention(qkv):
    b, l, _ = qkv.shape
    blk = lambda off: pl.BlockSpec((1, l, HD_A), lambda i, h: (i, 0, off + h))
    return pl.pallas_call(
        _ctx_attn_kernel,
        grid=(b, H_A),
        in_specs=[blk(0), blk(H_A), blk(2 * H_A)],
        out_specs=pl.BlockSpec((1, l, HD_A), lambda i, h: (i, 0, h)),
        out_shape=jax.ShapeDtypeStruct((b, l, D_A), BF16),
        compiler_params=_params("arbitrary", "arbitrary"),
        name="ctx_attention",
    )(qkv, qkv, qkv)


def _na_bias_tables(rpb, rows):
    nblk = rows // NA_ROWS
    n_dr, n_dc = 2 * WIN_R - 1, 2 * WIN_C - 1
    c = np.arange(GRID_W)[:, None]
    kc = np.arange(GRID_W)[None, :]
    dc = np.clip(kc - c + WIN_C - 1, 0, n_dc - 1)
    pick_c = (dc[None] == np.arange(n_dc)[:, None, None]).astype(np.float32)
    c0 = np.clip(c - WIN_C // 2, 0, GRID_W - WIN_C)
    ok_c = (kc >= c0) & (kc < c0 + WIN_C)
    by_col = jnp.einsum("hrd,dck->hrck", rpb, pick_c, precision=lax.Precision.HIGHEST)
    tables = []
    for i in (0, 1, nblk - 1):
        ks = int(np.clip(NA_ROWS * i - WIN_R // 2, 0, rows - NA_KEY_ROWS))
        r = (NA_ROWS * i + np.arange(NA_ROWS))[:, None]
        kr = (ks + np.arange(NA_KEY_ROWS))[None, :]
        r0 = np.clip(r - WIN_R // 2, 0, rows - WIN_R)
        ok_r = (kr >= r0) & (kr < r0 + WIN_R)
        dr = np.clip(kr - r + WIN_R - 1, 0, n_dr - 1)
        pick_r = (dr[None] == np.arange(n_dr)[:, None, None]).astype(np.float32)
        bias = jnp.einsum("hrck,rqs->hqcsk", by_col, pick_r, precision=lax.Precision.HIGHEST)
        ok = ok_r[None, :, None, :, None] & ok_c[None, None, :, None, :]
        bias = jnp.where(ok, bias, -jnp.inf)
        tables.append(bias.reshape(H_A, NA_ROWS * GRID_W, NA_KEY_ROWS * GRID_W))
    return jnp.stack(tables, axis=1)


def _na_kernel(q_ref, k_ref, v_ref, ck_ref, cv_ref, bias_ref, o_ref, *, rows):
    i = pl.program_id(2)
    nblk = rows // NA_ROWS
    ks = jnp.clip(NA_ROWS * i - WIN_R // 2, 0, rows - NA_KEY_ROWS)
    start = pl.multiple_of(ks * GRID_W, GRID_W)
    nkey = NA_KEY_ROWS * GRID_W
    kind = jnp.where(i == 0, 0, jnp.where(i == nblk - 1, 2, 1))
    ones_w = jnp.ones((nkey, HD_A), BF16)
    ones_c = jnp.ones((ck_ref.shape[1], HD_A), BF16)
    for hh in range(NA_HEADS):
        sl = slice(hh * HD_A, (hh + 1) * HD_A)
        q = q_ref[0, :, sl]
        kw = k_ref[0, pl.ds(start, nkey), sl]
        vw = jnp.concatenate([v_ref[0, pl.ds(start, nkey), sl], ones_w], axis=1)
        vc = jnp.concatenate([cv_ref[0, :, sl], ones_c], axis=1)
        s_w = _dot_nt(q, kw) + bias_ref[hh, kind]
        s_c = _dot_nt(q, ck_ref[0, :, sl])
        m = jnp.maximum(jnp.max(s_w, axis=-1, keepdims=True), jnp.max(s_c, axis=-1, keepdims=True))
        p_w = jnp.exp(s_w - m).astype(BF16)
        p_c = jnp.exp(s_c - m).astype(BF16)
        o = _dot(p_w, vw) + _dot(p_c, vc)
        o_ref[0, :, sl] = (o[:, :HD_A] / o[:, HD_A:]).astype(BF16)


def _neighbourhood_attention(qkv, ck, cv, rpb):
    b, t, _ = qkv.shape
    l = ck.shape[1]
    rows = t // GRID_W
    nblk = rows // NA_ROWS
    tq = NA_ROWS * GRID_W
    bias = _na_bias_tables(rpb, rows)
    hw = NA_HEADS * HD_A
    ngrp = H_A // NA_HEADS
    full = lambda off: pl.BlockSpec((1, t, hw), lambda h, n, i: (n, 0, off + h))
    ctx = pl.BlockSpec((1, l, hw), lambda h, n, i: (n, 0, h))
    return pl.pallas_call(
        functools.partial(_na_kernel, rows=rows),
        grid=(ngrp, b, nblk),
        in_specs=[
            pl.BlockSpec((1, tq, hw), lambda h, n, i: (n, i, h)),
            full(ngrp), full(2 * ngrp), ctx, ctx,
            pl.BlockSpec((NA_HEADS, 3, tq, NA_KEY_ROWS * GRID_W), lambda h, n, i: (h, 0, 0, 0)),
        ],
        out_specs=pl.BlockSpec((1, tq, hw), lambda h, n, i: (n, i, h)),
        out_shape=jax.ShapeDtypeStruct((b, t, D_A), BF16),
        compiler_params=_params("arbitrary", "arbitrary", "arbitrary"),
        name="neighbourhood_attention",
    )(qkv, qkv, qkv, ck, cv, bias)


def _lru_kernel(xb_ref, gb_ref, cw_ref, cb_ref, w4h_ref, b4_ref, lam_ref, h0_ref, o_ref, fin_ref,
                xpad_scr, af_scr, uf_scr, ab_scr, ub_scr, *, t, chunk):
    seg = t // LRU_SEGS
    pitch = seg + SUBLANES
    pad = SUBLANES
    zeros = jnp.zeros((pad, BD_B), F32)
    xpad_scr[0:pad, :] = zeros
    xpad_scr[pad + t:2 * pad + t, :] = zeros
    xpad_scr[pad:pad + t, :] = xb_ref[0]

    c1 = (-0.5 * LRU_C) * _softplus(-lam_ref[...])
    w4h = w4h_ref[0]
    b4h = 0.5 * b4_ref[0]
    left = (CONV_W - 1) // 2
    for c in range(t // chunk):
        base = c * chunk
        xc = cb_ref[...]
        for k in range(CONV_W):
            xc = xc + xpad_scr[pl.ds(base + pad - left + k, chunk), :] * cw_ref[k:k + 1, :]
        th = jnp.tanh(_dot(xc.astype(BF16), w4h) + b4h)
        xh = 0.5 * xc
        for d, (a_scr, u_scr) in enumerate(((af_scr, uf_scr), (ab_scr, ub_scr))):
            tr = th[:, 2 * d * BD_B:(2 * d + 1) * BD_B]
            ti = th[:, (2 * d + 1) * BD_B:(2 * d + 2) * BD_B]
            log_a = c1[d:d + 1, :] * tr + c1[d:d + 1, :]
            a = jnp.exp(log_a)
            u = jnp.sqrt(-jnp.tanh(log_a) * (a * a + 1.0)) * (xh * ti + xh)
            for p in range(chunk // seg):
                s = (base + p * seg) // seg
                a_scr[s * pitch:s * pitch + seg, :] = a[p * seg:(p + 1) * seg, :]
                u_scr[s * pitch:s * pitch + seg, :] = u[p * seg:(p + 1) * seg, :]

    def scan(j, carry):
        hf, pf, hb, pb = carry
        rf = pl.ds(j, LRU_SEGS, stride=pitch)
        rb = pl.ds(seg - 1 - j, LRU_SEGS, stride=pitch)
        a = af_scr[rf, :]
        hf = a * hf + uf_scr[rf, :]
        pf = a * pf
        uf_scr[rf, :] = hf
        af_scr[rf, :] = pf
        a = ab_scr[rb, :]
        hb = a * hb + ub_scr[rb, :]
        pb = a * pb
        ub_scr[rb, :] = hb
        ab_scr[rb, :] = pb
        return hf, pf, hb, pb

    zero = jnp.zeros((LRU_SEGS, BD_B), F32)
    one = jnp.ones((LRU_SEGS, BD_B), F32)
    hf, pf, hb, pb = lax.fori_loop(0, seg, scan, (zero, one, zero, one), unroll=4)

    cf = h0_ref[0, 0:1, :]
    cin_f = []
    for s in range(LRU_SEGS):
        cin_f.append(cf)
        cf = hf[s:s + 1, :] + pf[s:s + 1, :] * cf
    cb = h0_ref[0, 1:2, :]
    cin_b = [None] * LRU_SEGS
    for s in reversed(range(LRU_SEGS)):
        cin_b[s] = cb
        cb = hb[s:s + 1, :] + pb[s:s + 1, :] * cb
    fin_ref[0, 0:1, :] = cf
    fin_ref[0, 1:2, :] = cb

    for s in range(LRU_SEGS):
        rows = slice(s * pitch, s * pitch + seg)
        hs = (uf_scr[rows, :] + af_scr[rows, :] * cin_f[s]) + (ub_scr[rows, :] + ab_scr[rows, :] * cin_b[s])
        nat = slice(s * seg, (s + 1) * seg)
        o_ref[0, nat, :] = (hs * jax.nn.gelu(gb_ref[0, nat, :])).astype(BF16)


def _rglru(xg, conv_w, conv_b, w4h, b4, lam, h0):
    b, t, _ = xg.shape
    chunk = min(t, 512)
    seg = t // LRU_SEGS
    col = lambda off: pl.BlockSpec((1, t, BD_B), lambda n, h: (n, 0, off + h))
    return pl.pallas_call(
        functools.partial(_lru_kernel, t=t, chunk=chunk),
        grid=(b, H_B),
        in_specs=[
            col(0), col(H_B),
            pl.BlockSpec((CONV_W, BD_B), lambda n, h: (0, h)),
            pl.BlockSpec((1, BD_B), lambda n, h: (0, h)),
            pl.BlockSpec((1, BD_B, 4 * BD_B), lambda n, h: (h, 0, 0)),
            pl.BlockSpec((1, 1, 4 * BD_B), lambda n, h: (h, 0, 0)),
            pl.BlockSpec((2, BD_B), lambda n, h: (0, h)),
            pl.BlockSpec((1, 2, BD_B), lambda n, h: (n, 0, h)),
        ],
        out_specs=[pl.BlockSpec((1, t, BD_B), lambda n, h: (n, 0, h)),
                   pl.BlockSpec((1, 2, BD_B), lambda n, h: (n, 0, h))],
        out_shape=[jax.ShapeDtypeStruct((b, t, D_RNN), BF16),
                   jax.ShapeDtypeStruct((b, 2, D_RNN), F32)],
        scratch_shapes=[pltpu.VMEM((t + 2 * SUBLANES, BD_B), F32)]
        + [pltpu.VMEM((LRU_SEGS * (seg + SUBLANES), BD_B), F32)] * 4,
        compiler_params=_params("arbitrary", "arbitrary"),
        name="rglru_branch",
    )(xg, xg, conv_w, conv_b.reshape(1, D_RNN), w4h, b4, lam, h0)


def _outproj_ab_kernel(x_ref, mod_ref, oa_ref, ob_ref, w_ref, o_ref):
    y = _dot(oa_ref[0], w_ref[0:D_A, :]) + _dot(ob_ref[0], w_ref[D_A:D_A + D_RNN, :])
    o_ref[0] = x_ref[0] + mod_ref[0, 5:6, :] * y


def _outproj_ab(x, mod, oa, ob, w):
    grp, t, d = x.shape
    tm = TOKEN_TILE
    tok = lambda width: pl.BlockSpec((1, tm, width), lambda b, i: (b, i, 0))
    return pl.pallas_call(
        _outproj_ab_kernel,
        grid=(grp, t // tm),
        in_specs=[
            tok(d),
            pl.BlockSpec((1, N_MOD, d), lambda b, i: (b, 0, 0)),
            tok(D_A), tok(D_RNN),
            pl.BlockSpec((D_A + D_RNN, d), lambda b, i: (0, 0)),
        ],
        out_specs=tok(d),
        out_shape=jax.ShapeDtypeStruct(x.shape, F32),
        compiler_params=_params("arbitrary", "arbitrary"),
        name="mixer_ab_out_proj",
    )(x, mod, oa, ob, w)


def _gate_prep_kernel(g_ref, b_ref, o_ref):
    t = g_ref.shape[2]
    g = g_ref[0] + b_ref[...]
    lane = lax.broadcasted_iota(jnp.int32, (H_C, t), 1) & (CHUNK - 1)
    cum = _log_sigmoid(g[H_C:2 * H_C])
    suf = _log_sigmoid(g[3 * H_C:4 * H_C])
    sh = 1
    while sh < CHUNK:
        cum = cum + jnp.where(lane >= sh, pltpu.roll(cum, sh, 1), 0.0)
        suf = suf + jnp.where(lane < CHUNK - sh, pltpu.roll(suf, t - sh, 1), 0.0)
        sh *= 2
    o_ref[0, 0:H_C] = g[0:H_C]
    o_ref[0, H_C:2 * H_C] = cum
    o_ref[0, 2 * H_C:3 * H_C] = g[2 * H_C:3 * H_C]
    o_ref[0, 3 * H_C:4 * H_C] = suf


def _gate_prep(g_rows, b_gate):
    b, r, t = g_rows.shape
    return pl.pallas_call(
        _gate_prep_kernel,
        grid=(b,),
        in_specs=[pl.BlockSpec((1, r, t), lambda n: (n, 0, 0)), pl.BlockSpec((r, 1), lambda n: (0, 0))],
        out_specs=pl.BlockSpec((1, r, t), lambda n: (n, 0, 0)),
        out_shape=jax.ShapeDtypeStruct(g_rows.shape, F32),
        compiler_params=_params("arbitrary"),
        name="mlstm_gate_prep",
    )(g_rows, b_gate.reshape(r, 1))


def _mlstm_chunk(q, kt, v_aug, ig_row, bc_row, bc_col, cm, m, reverse):
    ti = lax.broadcasted_iota(jnp.int32, (CHUNK, CHUNK), 0)
    si = lax.broadcasted_iota(jnp.int32, (CHUNK, CHUNK), 1)
    seen = (si >= ti) if reverse else (si <= ti)
    bk = bc_row[:, 0:1] if reverse else bc_row[:, CHUNK - 1:CHUNK]
    bc_rep = jnp.broadcast_to(bc_col, (CHUNK, CHUNK))
    dmat = jnp.where(seen, bc_rep - bc_row + ig_row, -jnp.inf)
    inter = bc_rep + m
    m_t = jnp.maximum(inter, jnp.broadcast_to(jnp.max(dmat, axis=1, keepdims=True), (CHUNK, CHUNK)))
    w = jnp.exp(dmat - m_t)
    g = jnp.exp(inter - m_t)
    kscale = DK_C ** -0.5
    s = _dot(q, kt) * (w * kscale)
    tot = _dot(s.astype(BF16), v_aug) + jnp.concatenate([g, g, g], axis=1) * _dot(q, cm.astype(BF16))
    inv = 1.0 / jnp.maximum(jnp.abs(tot[:, DV_C:]), jnp.exp(-m_t))
    hout = tot[:, :DV_C] * jnp.concatenate([inv, inv], axis=1)
    dend = bk - bc_row + ig_row
    m_new = jnp.maximum(bk + m, jnp.max(dend, axis=1, keepdims=True))
    we = jnp.exp(dend - m_new)
    ge = jnp.exp(bk + m - m_new)
    kwt = kt.astype(F32) * (we * kscale)
    c_new = ge * cm + _dot(kwt.astype(BF16), v_aug)
    return hout, c_new, m_new


def _mlstm_kernel(*refs, zero_state):
    if zero_state:
        qvf_ref, vvf_ref, ktf_ref, qvb_ref, vvb_ref, ktb_ref, gr_ref, gc_ref = refs[:8]
        rest = refs[8:]
    else:
        m0_ref, qvf_ref, vvf_ref, ktf_ref, qvb_ref, vvb_ref, ktb_ref, gr_ref, gc_ref = refs[:9]
        c0_ref, n0_ref = refs[9:11]
        rest = refs[11:]
    hf_ref, hb_ref, cfin_ref, nfin_ref, mfin_ref, c_scr, m_scr = rest
    n, hp, c = pl.program_id(0), pl.program_id(1), pl.program_id(2)
    eye = (lax.broadcasted_iota(jnp.int32, (DK_C, DK_C), 0) == lax.broadcasted_iota(jnp.int32, (DK_C, DK_C), 1))

    @pl.when(c == 0)
    def _():
        if zero_state:
            c_scr[...] = jnp.zeros_like(c_scr)
            m_scr[...] = jnp.zeros_like(m_scr)
            return
        for d in range(2):
            for j in range(MLSTM_HEADS):
                c_scr[d, j, :, 0:DV_C] = c0_ref[0, d, j]
                n_col = jnp.sum(jnp.where(eye, n0_ref[0, d, j], 0.0), axis=1, keepdims=True)
                c_scr[d, j, :, DV_C:DV_C + LANES] = jnp.broadcast_to(n_col, (DK_C, LANES))
                m_scr[d, j] = jnp.full(m_scr.shape[2:], m0_ref[n, d * H_C + hp * MLSTM_HEADS + j], F32)

    ones_col = jnp.ones((CHUNK, LANES), BF16)
    dirs = ((qvf_ref, vvf_ref, ktf_ref, hf_ref), (qvb_ref, vvb_ref, ktb_ref, hb_ref))
    nc = pl.num_programs(2)
    chains = []
    for d, (q_ref, v_ref, kt_ref, h_ref) in enumerate(dirs):
        cc = c if d == 0 else nc - 1 - c
        rows = pl.ds(pl.multiple_of(cc * CHUNK, CHUNK), CHUNK)
        for j in range(MLSTM_HEADS):
            sl = slice(j * DK_C, (j + 1) * DK_C)
            v_aug = jnp.concatenate([v_ref[0, :, sl], ones_col], axis=1)
            col = (2 * d + 1) * MLSTM_HEADS + j
            chains.append((d, j, h_ref, sl, _mlstm_chunk(
                q_ref[0, :, sl], kt_ref[0, 0, sl, :], v_aug,
                gr_ref[0, j, 2 * d, pl.ds(cc, 1), :], gr_ref[0, j, 2 * d + 1, pl.ds(cc, 1), :],
                gc_ref[0, 0, rows, col:col + 1],
                c_scr[d, j], m_scr[d, j, 0:1, 0:1], reverse=(d == 1))))
    for d, j, h_ref, sl, (hout, c_new, m_new) in chains:
        h_ref[0, :, sl] = hout
        c_scr[d, j] = c_new
        m_scr[d, j] = jnp.broadcast_to(m_new, m_scr.shape[2:])

    @pl.when(c == pl.num_programs(2) - 1)
    def _():
        for d in range(2):
            for j in range(MLSTM_HEADS):
                cfin_ref[0, d, j] = c_scr[d, j, :, 0:DV_C]
                n_col = c_scr[d, j, :, DV_C:DV_C + 1]
                nfin_ref[0, d, j] = jnp.sum(jnp.where(eye, n_col, 0.0), axis=0, keepdims=True)
                mfin_ref[0, d, j] = m_scr[d, j, 0:1, :]


def _rope_tables(t):
    half = DK_C // 2
    nf = half // 2
    pos = jnp.arange(t)
    freqs = ROPE_BASE ** (-jnp.arange(nf, dtype=F32) / nf)
    cos, sin = [], []
    for p in (pos // GRID_W, pos % GRID_W):
        ang = p.astype(F32)[:, None] * freqs[None, :]
        cos += [jnp.cos(ang), jnp.cos(ang)]
        sin += [-jnp.sin(ang), jnp.sin(ang)]
    return jnp.concatenate(cos, axis=-1), jnp.concatenate(sin, axis=-1)


def _mlstm(qv, kt, gates, c0, n0, m0):
    b, t, _ = qv.shape
    nc = t // CHUNK
    hps = MLSTM_HEADS
    ngrp = H_C // hps
    g4 = gates.reshape(b, 4, H_C, t)
    g_row = jnp.transpose(g4, (0, 2, 1, 3)).reshape(b, H_C, 4, nc, CHUNK)
    g_col = jnp.transpose(gates.reshape(b, 4, ngrp, hps, t), (0, 2, 4, 1, 3)).reshape(b, ngrp, t, 4 * hps)
    fwd = lambda c: c
    bwd = lambda c: nc - 1 - c

    def tok(off, order):
        return pl.BlockSpec((1, CHUNK, hps * DK_C), lambda n, h, c: (n, order(c), off + h))

    def kt_spec(order):
        return pl.BlockSpec((1, 1, hps * DK_C, CHUNK), lambda n, h, c: (n, order(c), h, 0))

    g_row_spec = pl.BlockSpec((1, hps, 4, nc, CHUNK), lambda n, h, c: (n, h, 0, 0, 0))
    g_col_spec = pl.BlockSpec((1, 1, t, 4 * hps), lambda n, h, c: (n, h, 0, 0))

    state = lambda *tail: pl.BlockSpec((1, 2, hps) + tail, lambda n, h, c: (n, 0, h) + (0,) * len(tail))
    hspec = lambda order: pl.BlockSpec((1, CHUNK, hps * DV_C), lambda n, h, c: (n, order(c), h))
    in_specs = [tok(0, fwd), tok(ngrp, fwd), kt_spec(fwd),
                tok(0, bwd), tok(ngrp, bwd), kt_spec(bwd),
                g_row_spec, g_col_spec]
    args = [qv, qv, kt, qv, qv, kt, g_row, g_col]
    zero_state = c0 is None
    if not zero_state:
        in_specs = [pl.BlockSpec(memory_space=pltpu.SMEM)] + in_specs + [state(DK_C, DV_C), state(1, DK_C)]
        args = [m0.reshape(b, 2 * H_C)] + args + [c0, n0.reshape(b, 2, H_C, 1, DK_C)]
    hf, hb, cfin, nfin, mfin = pl.pallas_call(
        functools.partial(_mlstm_kernel, zero_state=zero_state),
        grid=(b, ngrp, nc),
        in_specs=in_specs,
        out_specs=[hspec(fwd), hspec(bwd), state(DK_C, DV_C), state(1, DK_C), state(1, LANES)],
        out_shape=[jax.ShapeDtypeStruct((b, t, D_C), F32), jax.ShapeDtypeStruct((b, t, D_C), F32),
                   jax.ShapeDtypeStruct((b, 2, H_C, DK_C, DV_C), F32),
                   jax.ShapeDtypeStruct((b, 2, H_C, 1, DK_C), F32),
                   jax.ShapeDtypeStruct((b, 2, H_C, 1, LANES), F32)],
        scratch_shapes=[pltpu.VMEM((2, hps, DK_C, DV_C + LANES), F32),
                        pltpu.VMEM((2, hps, SUBLANES, LANES), F32)],
        compiler_params=_params("arbitrary", "arbitrary", "arbitrary"),
        name="mlstm_scan",
    )(*args)
    return hf, hb, cfin, nfin.reshape(b, 2, H_C, DK_C), mfin[:, :, :, 0, 0]


def _outproj_c_kernel(x_ref, mod_ref, hf_ref, hb_ref, og_ref, g_ref, w_ref, o_ref, hn_scr):
    for h in range(H_C):
        sl = slice(h * DV_C, (h + 1) * DV_C)
        hs = hf_ref[0, :, sl] + hb_ref[0, :, sl]
        hn_scr[:, sl] = (_rms(hs, g_ref[:, sl]) * jax.nn.sigmoid(og_ref[0, :, sl])).astype(BF16)
    o_ref[0] = x_ref[0] + mod_ref[0, 5:6, :] * _dot(hn_scr[...], w_ref[...])


def _outproj_c(x, mod, hf, hb, og, mh_g, w):
    grp, t, d = x.shape
    tm = TOKEN_TILE // 2
    tok = pl.BlockSpec((1, tm, d), lambda b, i: (b, i, 0))
    return pl.pallas_call(
        _outproj_c_kernel,
        grid=(grp, t // tm),
        in_specs=[
            tok,
            pl.BlockSpec((1, N_MOD, d), lambda b, i: (b, 0, 0)),
            tok, tok,
            pl.BlockSpec((1, tm, D_C), lambda b, i: (b, i, 0)),
            pl.BlockSpec((1, D_C), lambda b, i: (0, 0)),
            pl.BlockSpec((D_C, d), lambda b, i: (0, 0)),
        ],
        out_specs=tok,
        out_shape=jax.ShapeDtypeStruct(x.shape, F32),
        scratch_shapes=[pltpu.VMEM((tm, D_C), BF16)],
        compiler_params=_params("arbitrary", "arbitrary"),
        name="mixer_c_out_proj",
    )(x, mod, hf, hb, og, mh_g.reshape(1, D_C), w)


def _mixer_ab(j, xp, xs, mp, ms, g, p):
    bp, lp, bs = p["bp"], p["lp"], xs.shape[0]
    w_in = p["w_in_ab"][j].astype(BF16)
    w_out = p["w_out_ab"][j].astype(BF16)
    gains = jnp.stack([jnp.tile(p["qn_g"][j] * (HD_A ** -0.5), H_A), jnp.tile(p["kn_g"][j], H_A),
                       jnp.ones((D_A,), F32)]).reshape(3, 1, D_A)
    wa, wx, ba, bx = p["lru_wa"][j], p["lru_wx"][j], p["lru_ba"][j], p["lru_bx"][j]
    w4 = (0.5 * jnp.concatenate([wa[0], wx[0], wa[1], wx[1]], axis=-1)).astype(BF16)
    b4 = jnp.concatenate([v.reshape(H_B, 1, BD_B) for v in (ba[0], bx[0], ba[1], bx[1])], axis=-1)
    lru = (p["conv_w"][j], p["conv_b"][j], w4, b4, p["lru_lam"][j])

    qkv, xg = _inproj_ab(xp, mp, g, w_in, gains, F32)
    qkv = qkv.reshape(bp, lp, 3 * D_A)
    oa = _ctx_attention(qkv)
    ob, h_fin = _rglru(xg.reshape(bp, lp, 2 * D_RNN), *lru, jnp.zeros((bp, 2, D_RNN), F32))
    xp = _outproj_ab(xp, mp, oa.reshape(1, bp * lp, D_A), ob.reshape(1, bp * lp, D_RNN), w_out)
    new = (qkv[:, :, D_A:2 * D_A].reshape(bp, lp, H_A, HD_A), qkv[:, :, 2 * D_A:].reshape(bp, lp, H_A, HD_A), h_fin)

    past = p["cache_k"].shape[2]
    ck = p["cache_k"][:, j].reshape(bs, past, D_A).astype(BF16)
    cv = p["cache_v"][:, j].reshape(bs, past, D_A).astype(BF16)
    qkv, xg = _inproj_ab(xs, ms, g, w_in, gains, BF16)
    oa = _neighbourhood_attention(qkv, ck, cv, p["rpb"][j])
    ob, _ = _rglru(xg, *lru, p["state_lru"][:, j])
    xs = _outproj_ab(xs, ms, oa, ob, w_out)
    return xp, xs, new


def _mixer_c(j, xp, xs, mp, ms, g, p):
    bp, lp, bs, ts = p["bp"], p["lp"], xs.shape[0], xs.shape[1]
    w = p["w_in_c"][j].astype(BF16)
    wkt = w[:, D_C:2 * D_C].T
    w_gate = jnp.pad(w[:, 4 * D_C:], ((0, 0), (0, LANES - 4 * H_C)))
    w_out = p["w_out_c"][j].astype(BF16)

    def gates_of(zg, b, t):
        g_rows = jnp.transpose(zg.reshape(b, t, LANES)[:, :, :4 * H_C], (0, 2, 1))
        return _gate_prep(g_rows, p["b_gate_c"][j])

    qv, og, kt, zg = _inproj_c(xp, mp, g, w, wkt, w_gate, None)
    hf, hb, cfin, nfin, mfin = _mlstm(
        qv.reshape(bp, lp, 2 * D_C), kt.reshape(bp, lp // CHUNK, D_C, CHUNK), gates_of(zg, bp, lp), None, None, None)
    xp = _outproj_c(xp, mp, hf.reshape(1, bp * lp, D_C), hb.reshape(1, bp * lp, D_C), og, p["mh_norm_g"][j], w_out)

    qv, og, kt, zg = _inproj_c(xs, ms, g, w, wkt, w_gate, _rope_tables(ts))
    hf, hb, _, _, _ = _mlstm(qv, kt, gates_of(zg, bs, ts), p["state_mlstm_c"][:, j], p["state_mlstm_n"][:, j],
                             p["state_mlstm_m"][:, j])
    xs = _outproj_c(xs, ms, hf, hb, og, p["mh_norm_g"][j], w_out)
    return xp, xs, (cfin, nfin, mfin)


def kernel(x_prompt, x_sample, cache_k, cache_v, state_lru, state_mlstm_c, state_mlstm_n, state_mlstm_m, c, c_ctx, w_mod, b_mod, norm_g, w_ffn_gate, w_ffn_up, w_ffn_down, w_in_ab, w_out_ab, qn_g, kn_g, rpb, conv_w, conv_b, lru_wa, lru_ba, lru_wx, lru_bx, lru_lam, w_in_c, b_gate_c, mh_norm_g, w_out_c):
    bp, lp, d = x_prompt.shape
    bs = x_sample.shape[0]
    p = dict(bp=bp, lp=lp, cache_k=cache_k, cache_v=cache_v, state_lru=state_lru, state_mlstm_c=state_mlstm_c,
             state_mlstm_n=state_mlstm_n, state_mlstm_m=state_mlstm_m, w_in_ab=w_in_ab, w_out_ab=w_out_ab,
             qn_g=qn_g, kn_g=kn_g, rpb=rpb, conv_w=conv_w, conv_b=conv_b, lru_wa=lru_wa, lru_ba=lru_ba,
             lru_wx=lru_wx, lru_bx=lru_bx, lru_lam=lru_lam, w_in_c=w_in_c, b_gate_c=b_gate_c,
             mh_norm_g=mh_norm_g, w_out_c=w_out_c)

    cond = jnp.concatenate([c, c_ctx[None, :]], axis=0)
    cond = jnp.pad(cond, ((0, -(bs + 1) % SUBLANES), (0, 0)))
    mod = _modulation(cond, w_mod, b_mod)[:, :bs + 1].reshape(DEPTH, bs + 1, N_MOD, d)

    wg = w_ffn_gate.astype(BF16)
    wu = w_ffn_up.astype(BF16)
    wd = w_ffn_down.astype(BF16)

    xp = x_prompt.reshape(1, bp * lp, d)
    xs = x_sample
    new_ab, new_c = [], []
    for l in range(DEPTH):
        j = l // 2
        ms = mod[l, :bs]
        mp = mod[l, bs:]
        xp = _ffn(xp, mp, norm_g[l, 0], wg, wu, wd, l, 0, 0)
        xs = _ffn(xs, ms, norm_g[l, 0], wg, wu, wd, l, 0, 0)
        if l % 2 == 0:
            xp, xs, new = _mixer_ab(j, xp, xs, mp, ms, norm_g[l, 1], p)
            new_ab.append(new)
        else:
            xp, xs, new = _mixer_c(j, xp, xs, mp, ms, norm_g[l, 1], p)
            new_c.append(new)
        xp = _ffn(xp, mp, norm_g[l, 2], wg, wu, wd, l, 1, 2)
        xs = _ffn(xs, ms, norm_g[l, 2], wg, wu, wd, l, 1, 2)
    stack = lambda items, i: jnp.concatenate([jnp.expand_dims(it[i], 1) for it in items], axis=1)
    return (xp.reshape(bp, lp, d), xs, stack(new_ab, 0), stack(new_ab, 1), stack(new_ab, 2),
            stack(new_c, 0), stack(new_c, 1), stack(new_c, 2))
```

```python
import functools

import numpy as np
import jax
import jax.numpy as jnp
from jax import lax
from jax.experimental import pallas as pl
from jax.experimental.pallas import tpu as pltpu

D_MODEL = 2048
DEPTH = 2
GRID_W = 64
N_MOD = 9
D_FF = 5632
EPS = 1e-6
H_A = 8
HD_A = 128
D_A = H_A * HD_A
WIN_R = 8
WIN_C = 16
D_RNN = 1024
H_B = 8
BD_B = D_RNN // H_B
CONV_W = 4
LRU_C = 8.0
H_C = 8
DK_C = 256
DV_C = 256
D_C = H_C * DV_C
CHUNK = 128
ROPE_BASE = 10000.0

BF16 = jnp.bfloat16
F32 = jnp.float32

V7X_VMEM_BYTES = 64 * 1024 * 1024
VMEM_LIMIT = V7X_VMEM_BYTES - 8 * 1024 * 1024
FFN_VMEM_LIMIT = V7X_VMEM_BYTES - 5 * 1024 * 1024
LANES = 128
SUBLANES = 8

TOKEN_TILE = 512
FF_TILE = 512
FFN_NORM_ROWS = 128
PROJ_TOKEN_TILE = 1024
PROJ_TILE = 512
NA_ROWS = 4
NA_KEY_ROWS = NA_ROWS + WIN_R
LRU_SEGS = SUBLANES
NA_HEADS = 4
MLSTM_HEADS = 4


def _params(*sem, vmem=VMEM_LIMIT):
    return pltpu.CompilerParams(dimension_semantics=sem, vmem_limit_bytes=vmem)


def _dot(a, b):
    return jnp.dot(a, b, preferred_element_type=F32)


def _dot_nt(a, b):
    return lax.dot_general(a, b, (((1,), (1,)), ((), ())), preferred_element_type=F32)


def _dot_tn(a, b):
    return lax.dot_general(a, b, (((0,), (0,)), ((), ())), preferred_element_type=F32)


def _rms(x, g):
    return x * lax.rsqrt(jnp.mean(x * x, axis=-1, keepdims=True) + EPS) * g


def _adaln(x, g, mod_ref, j):
    return _rms(x, g) * (1.0 + mod_ref[0, 3 * j + 1:3 * j + 2, :]) + mod_ref[0, 3 * j:3 * j + 1, :]


def _softplus(x):
    return jnp.maximum(x, 0.0) + jnp.log1p(jnp.exp(-jnp.abs(x)))


def _log_sigmoid(x):
    return -_softplus(-x)


def _mod_kernel(c_ref, w_ref, b_ref, o_ref):
    c = c_ref[...]
    s = (c * jax.nn.sigmoid(c)).astype(BF16)
    o_ref[0] = _dot(s, w_ref[0].astype(BF16)) + b_ref[0]


def _modulation(cond, w_mod, b_mod):
    r = cond.shape[0]
    n = N_MOD * D_MODEL
    tn = 1024
    return pl.pallas_call(
        _mod_kernel,
        grid=(DEPTH, n // tn),
        in_specs=[
            pl.BlockSpec((r, D_MODEL), lambda l, j: (0, 0)),
            pl.BlockSpec((1, D_MODEL, tn), lambda l, j: (l, 0, j)),
            pl.BlockSpec((1, 1, tn), lambda l, j: (l, 0, j)),
        ],
        out_specs=pl.BlockSpec((1, r, tn), lambda l, j: (l, 0, j)),
        out_shape=jax.ShapeDtypeStruct((DEPTH, r, n), F32),
        compiler_params=_params("arbitrary", "arbitrary"),
        name="modulation",
    )(cond, w_mod, b_mod.reshape(DEPTH, 1, n))


def _ffn_kernel(x0_ref, xn_ref, mod_ref, modn_ref, g_ref, wga_ref, wua_ref, wda_ref, wgb_ref, wub_ref, wdb_ref,
                o_ref, ha_scr, hb_scr, xa_scr, xb_scr, acc_scr, *, j, ns, single_last):
    p, f2 = pl.program_id(0), pl.program_id(1)
    rows = xn_ref.shape[1]
    tm = ha_scr.shape[0]

    @pl.when((p == 0) & (f2 == 0))
    def _():
        xa_scr[...] = x0_ref[0]
        ha_scr[...] = _adaln(x0_ref[0], g_ref[...], mod_ref, j).astype(BF16)

    @pl.when((f2 == 0) | (f2 == ns))
    def _():
        acc_scr[...] = jnp.zeros_like(acc_scr)

    def body(h_cur, h_nxt, x_nxt, f, both):
        r0 = pl.multiple_of(jnp.minimum(f, tm // rows - 1) * rows, rows)
        xs = xn_ref[0]
        x_nxt[pl.ds(r0, rows), :] = xs
        h_nxt[pl.ds(r0, rows), :] = _adaln(xs, g_ref[...], modn_ref, j).astype(BF16)
        h = h_cur[...]
        hidden = ((wga_ref, wua_ref, wda_ref), (wgb_ref, wub_ref, wdb_ref)) if both else ((wga_ref, wua_ref, wda_ref),)
        y = None
        for wg_ref, wu_ref, wd_ref in hidden:
            a = _dot(h, wg_ref[...])
            u = _dot(h, wu_ref[...])
            act = (a * jax.nn.sigmoid(a)) * u
            yd = _dot(act.astype(BF16), wd_ref[...])
            y = yd if y is None else y + yd
        acc_scr[...] += y

    n_both = ns - 1 if single_last else ns
    for half, (h_cur, h_nxt, x_nxt) in enumerate(((ha_scr, hb_scr, xb_scr), (hb_scr, ha_scr, xa_scr))):
        @pl.when((f2 >= half * ns) & (f2 < half * ns + n_both))
        def _():
            body(h_cur, h_nxt, x_nxt, f2 - half * ns, True)

        if single_last:
            @pl.when(f2 == half * ns + ns - 1)
            def _():
                body(h_cur, h_nxt, x_nxt, ns - 1, False)

    for last, x_cur in ((ns - 1, xa_scr), (2 * ns - 1, xb_scr)):
        @pl.when(f2 == last)
        def _():
            o_ref[0] = x_cur[...] + 0.5 * mod_ref[0, 3 * j + 2:3 * j + 3, :] * acc_scr[...]


def _ffn(x, mod, g, wg, wu, wd, l, s, j):
    grp, t, d = x.shape
    tm, tf = TOKEN_TILE, FF_TILE
    nf = D_FF // tf
    ns = (nf + 1) // 2
    per_group = t // tm
    ntile = grp * per_group
    rows = FFN_NORM_ROWS
    nsl = tm // rows
    assert ntile % 2 == 0 and ns >= nsl
    tile = lambda p, f2: 2 * p + f2 // ns
    nxt = lambda p, f2: jnp.minimum(tile(p, f2) + 1, ntile - 1)
    fa = lambda f2: jnp.minimum(2 * (f2 % ns), nf - 1)
    fb = lambda f2: jnp.minimum(2 * (f2 % ns) + 1, nf - 1)
    wcol = lambda fsel: pl.BlockSpec((None, None, None, d, tf), lambda p, f2: (l, s, fsel(f2), 0, 0))
    wrow = lambda fsel: pl.BlockSpec((None, None, tf, d), lambda p, f2: (l, s, fsel(f2), 0))
    out = pl.pallas_call(
        functools.partial(_ffn_kernel, j=j, ns=ns, single_last=nf % 2 == 1),
        grid=(ntile // 2, 2 * ns),
        in_specs=[
            pl.BlockSpec((1, tm, d), lambda p, f2: (0, 0, 0), pipeline_mode=pl.Buffered(1)),
            pl.BlockSpec((1, rows, d), lambda p, f2: (0, nxt(p, f2) * nsl + jnp.minimum(f2 % ns, nsl - 1), 0)),
            pl.BlockSpec((1, N_MOD, d), lambda p, f2: (tile(p, f2) // per_group, 0, 0)),
            pl.BlockSpec((1, N_MOD, d), lambda p, f2: (nxt(p, f2) // per_group, 0, 0)),
            pl.BlockSpec((1, d), lambda p, f2: (0, 0)),
            wcol(fa), wcol(fa), wrow(fa), wcol(fb), wcol(fb), wrow(fb),
        ],
        out_specs=pl.BlockSpec((1, tm, d), lambda p, f2: (0, tile(p, f2), 0)),
        out_shape=jax.ShapeDtypeStruct((1, grp * t, d), F32),
        scratch_shapes=[pltpu.VMEM((tm, d), BF16), pltpu.VMEM((tm, d), BF16),
                        pltpu.VMEM((tm, d), F32), pltpu.VMEM((tm, d), F32), pltpu.VMEM((tm, d), F32)],
        compiler_params=_params("arbitrary", "arbitrary", vmem=FFN_VMEM_LIMIT),
        name="macaron_ffn",
    )(x.reshape(1, grp * t, d), x.reshape(1, grp * t, d), mod, mod, g.reshape(1, d), wg, wu, wd, wg, wu, wd)
    return out.reshape(x.shape)


def _tile_major(w):
    d, n = w.shape
    return jnp.swapaxes(w.reshape(d, n // PROJ_TILE, PROJ_TILE), 0, 1)


def _rope(x, cos, sin):
    half = DK_C // 2
    out = []
    for p in range(2):
        sl = slice(p * half, (p + 1) * half)
        xs = x[:, sl]
        out.append(xs * cos[:, sl] + pltpu.roll(xs, half // 2, 1) * sin[:, sl])
    return jnp.concatenate(out, axis=-1)


def _rope_t(x, cos, sin):
    q = DK_C // 4
    swapped = jnp.concatenate([x[q:2 * q], x[0:q], x[3 * q:4 * q], x[2 * q:3 * q]], axis=0)
    return x * cos + swapped * sin


def _inproj_ab_kernel(x_ref, mod_ref, g_ref, w_ref, gain_ref, oa_ref, ob_ref, h_scr, *, n_a, n_norm):
    k = pl.program_id(2)

    @pl.when(k == 0)
    def _():
        h_scr[...] = _adaln(x_ref[0], g_ref[...], mod_ref, 1).astype(BF16)

    @pl.when(k >= n_a)
    def _():
        ob_ref[0] = _dot(h_scr[...], w_ref[...])

    @pl.when(k < n_norm)
    def _():
        for pair in range(w_ref.shape[1] // (2 * HD_A)):
            z2 = _dot(h_scr[...], w_ref[:, 2 * pair * HD_A:2 * (pair + 1) * HD_A])
            for hh in range(2):
                zs = z2[:, hh * HD_A:(hh + 1) * HD_A]
                sl = slice((2 * pair + hh) * HD_A, (2 * pair + hh + 1) * HD_A)
                r = lax.rsqrt(jnp.mean(zs * zs, axis=-1, keepdims=True) + EPS)
                oa_ref[0, :, sl] = (zs * r * gain_ref[0, :, sl]).astype(oa_ref.dtype)

    @pl.when((k >= n_norm) & (k < n_a))
    def _():
        oa_ref[0] = (_dot(h_scr[...], w_ref[...]) * gain_ref[0]).astype(oa_ref.dtype)


def _inproj_ab(x, mod, g, w, gains, dtype_a):
    grp, t, d = x.shape
    tm, tn = PROJ_TOKEN_TILE, PROJ_TILE
    n_a = 3 * D_A // tn
    nt = w.shape[0]
    return pl.pallas_call(
        functools.partial(_inproj_ab_kernel, n_a=n_a, n_norm=2 * D_A // tn),
        grid=(grp, t // tm, nt),
        in_specs=[
            pl.BlockSpec((1, tm, d), lambda b, i, k: (b, i, 0)),
            pl.BlockSpec((1, N_MOD, d), lambda b, i, k: (b, 0, 0)),
            pl.BlockSpec((1, d), lambda b, i, k: (0, 0)),
            pl.BlockSpec((None, d, tn), lambda b, i, k: (k, 0, 0)),
            pl.BlockSpec((1, 1, tn), lambda b, i, k: (jnp.minimum(k, n_a - 1), 0, 0)),
        ],
        out_specs=[pl.BlockSpec((1, tm, tn), lambda b, i, k: (b, i, jnp.minimum(k, n_a - 1))),
                   pl.BlockSpec((1, tm, tn), lambda b, i, k: (b, i, jnp.maximum(k - n_a, 0)))],
        out_shape=[jax.ShapeDtypeStruct((grp, t, n_a * tn), dtype_a),
                   jax.ShapeDtypeStruct((grp, t, (nt - n_a) * tn), F32)],
        scratch_shapes=[pltpu.VMEM((tm, d), BF16)],
        compiler_params=_params("arbitrary", "arbitrary", "arbitrary"),
        name="mixer_ab_in_proj",
    )(x, mod, g.reshape(1, d), w, gains.reshape(n_a, 1, tn))


def _inproj_c_kernel(*refs, rotary, n_q, n_qv, n_main):
    x_ref, mod_ref, g_ref, w_ref, wkt_ref, wgate_ref = refs[:6]
    pos = 6
    if rotary:
        cos_ref, sin_ref, cost_ref, sint_ref = refs[pos:pos + 4]
        pos += 4
    oa_ref, ob_ref, okt_ref, og_ref, h_scr = refs[pos:pos + 5]
    k = pl.program_id(2)
    tn = w_ref.shape[1]

    @pl.when(k == 0)
    def _():
        h = _adaln(x_ref[0], g_ref[...], mod_ref, 1).astype(BF16)
        h_scr[...] = h
        og_ref[0] = _dot(h, wgate_ref[...])

    @pl.when(k < n_q)
    def _():
        z = _dot(h_scr[...], w_ref[...])
        if rotary:
            for hh in range(tn // DK_C):
                sl = slice(hh * DK_C, (hh + 1) * DK_C)
                oa_ref[0, :, sl] = _rope(z[:, sl], cos_ref[...], sin_ref[...]).astype(BF16)
        else:
            oa_ref[0] = z.astype(BF16)

    @pl.when((k >= n_q) & (k < n_qv))
    def _():
        oa_ref[0] = _dot(h_scr[...], w_ref[...]).astype(BF16)

    @pl.when((k >= n_qv) & (k < n_main))
    def _():
        ob_ref[0] = _dot(h_scr[...], w_ref[...])

    @pl.when(k >= n_main)
    def _():
        zt = _dot_nt(wkt_ref[...], h_scr[...])
        for ci in range(zt.shape[1] // CHUNK):
            cols = slice(ci * CHUNK, (ci + 1) * CHUNK)
            if rotary:
                for hh in range(tn // DK_C):
                    sl = slice(hh * DK_C, (hh + 1) * DK_C)
                    okt_ref[0, ci, sl, :] = _rope_t(zt[sl, cols], cost_ref[:, cols], sint_ref[:, cols]).astype(BF16)
            else:
                okt_ref[0, ci] = zt[:, cols].astype(BF16)


def _inproj_c(x, mod, g, w, wkt, w_gate, rope_tables):
    grp, t, d = x.shape
    tm, tn = PROJ_TOKEN_TILE, PROJ_TILE
    n_q, n_qv, n_main = D_C // tn, 2 * D_C // tn, 3 * D_C // tn
    nt = n_main + D_C // tn

    def w_col(k):
        kk = jnp.minimum(k, n_main - 1)
        return jnp.where(kk < n_q, kk, kk + n_q)

    rotary = rope_tables is not None
    in_specs = [
        pl.BlockSpec((1, tm, d), lambda b, i, k: (b, i, 0)),
        pl.BlockSpec((1, N_MOD, d), lambda b, i, k: (b, 0, 0)),
        pl.BlockSpec((1, d), lambda b, i, k: (0, 0)),
        pl.BlockSpec((None, d, tn), lambda b, i, k: (w_col(k), 0, 0)),
        pl.BlockSpec((tn, d), lambda b, i, k: (jnp.maximum(k - n_main, 0), 0)),
        pl.BlockSpec((d, LANES), lambda b, i, k: (0, 0)),
    ]
    args = [x, mod, g.reshape(1, d), w, wkt, w_gate]
    if rotary:
        cos, sin = rope_tables
        in_specs += [pl.BlockSpec((tm, DK_C), lambda b, i, k: (i, 0))] * 2
        in_specs += [pl.BlockSpec((DK_C, tm), lambda b, i, k: (0, i))] * 2
        args += [cos, sin, cos.T, sin.T]
    return pl.pallas_call(
        functools.partial(_inproj_c_kernel, rotary=rotary, n_q=n_q, n_qv=n_qv, n_main=n_main),
        grid=(grp, t // tm, nt),
        in_specs=in_specs,
        out_specs=[pl.BlockSpec((1, tm, tn), lambda b, i, k: (b, i, jnp.minimum(k, n_qv - 1))),
                   pl.BlockSpec((1, tm, tn), lambda b, i, k: (b, i, jnp.clip(k - n_qv, 0, n_main - n_qv - 1))),
                   pl.BlockSpec((1, tm // CHUNK, tn, CHUNK), lambda b, i, k: (b, i, jnp.maximum(k - n_main, 0), 0)),
                   pl.BlockSpec((1, tm, LANES), lambda b, i, k: (b, i, 0))],
        out_shape=[jax.ShapeDtypeStruct((grp, t, 2 * D_C), BF16),
                   jax.ShapeDtypeStruct((grp, t, D_C), F32),
                   jax.ShapeDtypeStruct((grp, t // CHUNK, D_C, CHUNK), BF16),
                   jax.ShapeDtypeStruct((grp, t, LANES), F32)],
        scratch_shapes=[pltpu.VMEM((tm, d), BF16)],
        compiler_params=_params("arbitrary", "arbitrary", "arbitrary"),
        name="mixer_c_in_proj",
    )(*args)


def _ctx_attn_kernel(q_ref, k_ref, v_ref, o_ref):
    s = _dot_nt(q_ref[0].astype(BF16), k_ref[0].astype(BF16))
    p = jnp.exp(s - jnp.max(s, axis=-1, keepdims=True))
    o = _dot(p.astype(BF16), v_ref[0].astype(BF16)) / jnp.sum(p, axis=-1, keepdims=True)
    o_ref[0] = o.astype(BF16)


def _ctx_attention(qkv):
    b, l, _ = qkv.shape
    blk = lambda off: pl.BlockSpec((1, l, HD_A), lambda i, h: (i, 0, off + h))
    return pl.pallas_call(
        _ctx_attn_kernel,
        grid=(b, H_A),
        in_specs=[blk(0), blk(H_A), blk(2 * H_A)],
        out_specs=pl.BlockSpec((1, l, HD_A), lambda i, h: (i, 0, h)),
        out_shape=jax.ShapeDtypeStruct((b, l, D_A), BF16),
        compiler_params=_params("arbitrary", "arbitrary"),
        name="ctx_attention",
    )(qkv, qkv, qkv)


def _na_bias_tables(rpb, rows):
    nblk = rows // NA_ROWS
    n_dr, n_dc = 2 * WIN_R - 1, 2 * WIN_C - 1
    c = np.arange(GRID_W)[:, None]
    kc = np.arange(GRID_W)[None, :]
    dc = np.clip(kc - c + WIN_C - 1, 0, n_dc - 1)
    pick_c = (dc[None] == np.arange(n_dc)[:, None, None]).astype(np.float32)
    c0 = np.clip(c - WIN_C // 2, 0, GRID_W - WIN_C)
    ok_c = (kc >= c0) & (kc < c0 + WIN_C)
    by_col = jnp.einsum("hrd,dck->hrck", rpb, pick_c, precision=lax.Precision.HIGHEST)
    tables = []
    for i in (0, 1, nblk - 1):
        ks = int(np.clip(NA_ROWS * i - WIN_R // 2, 0, rows - NA_KEY_ROWS))
        r = (NA_ROWS * i + np.arange(NA_ROWS))[:, None]
        kr = (ks + np.arange(NA_KEY_ROWS))[None, :]
        r0 = np.clip(r - WIN_R // 2, 0, rows - WIN_R)
        ok_r = (kr >= r0) & (kr < r0 + WIN_R)
        dr = np.clip(kr - r + WIN_R - 1, 0, n_dr - 1)
        pick_r = (dr[None] == np.arange(n_dr)[:, None, None]).astype(np.float32)
        bias = jnp.einsum("hrck,rqs->hqcsk", by_col, pick_r, precision=lax.Precision.HIGHEST)
        ok = ok_r[None, :, None, :, None] & ok_c[None, None, :, None, :]
        bias = jnp.where(ok, bias, -jnp.inf)
        tables.append(bias.reshape(H_A, NA_ROWS * GRID_W, NA_KEY_ROWS * GRID_W))
    return jnp.stack(tables, axis=1)


def _na_kernel(q_ref, k_ref, v_ref, ck_ref, cv_ref, bias_ref, o_ref, *, rows):
    i = pl.program_id(2)
    nblk = rows // NA_ROWS
    ks = jnp.clip(NA_ROWS * i - WIN_R // 2, 0, rows - NA_KEY_ROWS)
    start = pl.multiple_of(ks * GRID_W, GRID_W)
    nkey = NA_KEY_ROWS * GRID_W
    kind = jnp.where(i == 0, 0, jnp.where(i == nblk - 1, 2, 1))
    ones_w = jnp.ones((nkey, HD_A), BF16)
    ones_c = jnp.ones((ck_ref.shape[1], HD_A), BF16)
    for hh in range(NA_HEADS):
        sl = slice(hh * HD_A, (hh + 1) * HD_A)
        q = q_ref[0, :, sl]
        kw = k_ref[0, pl.ds(start, nkey), sl]
        vw = jnp.concatenate([v_ref[0, pl.ds(start, nkey), sl], ones_w], axis=1)
        vc = jnp.concatenate([cv_ref[0, :, sl], ones_c], axis=1)
        s_w = _dot_nt(q, kw) + bias_ref[hh, kind]
        s_c = _dot_nt(q, ck_ref[0, :, sl])
        m = jnp.maximum(jnp.max(s_w, axis=-1, keepdims=True), jnp.max(s_c, axis=-1, keepdims=True))
        p_w = jnp.exp(s_w - m).astype(BF16)
        p_c = jnp.exp(s_c - m).astype(BF16)
        o = _dot(p_w, vw) + _dot(p_c, vc)
        o_ref[0, :, sl] = (o[:, :HD_A] / o[:, HD_A:]).astype(BF16)


def _neighbourhood_attention(qkv, ck, cv, rpb):
    b, t, _ = qkv.shape
    l = ck.shape[1]
    rows = t // GRID_W
    nblk = rows // NA_ROWS
    tq = NA_ROWS * GRID_W
    bias = _na_bias_tables(rpb, rows)
    hw = NA_HEADS * HD_A
    ngrp = H_A // NA_HEADS
    full = lambda off: pl.BlockSpec((1, t, hw), lambda h, n, i: (n, 0, off + h))
    ctx = pl.BlockSpec((1, l, hw), lambda h, n, i: (n, 0, h))
    return pl.pallas_call(
        functools.partial(_na_kernel, rows=rows),
        grid=(ngrp, b, nblk),
        in_specs=[
            pl.BlockSpec((1, tq, hw), lambda h, n, i: (n, i, h)),
            full(ngrp), full(2 * ngrp), ctx, ctx,
            pl.BlockSpec((NA_HEADS, 3, tq, NA_KEY_ROWS * GRID_W), lambda h, n, i: (h, 0, 0, 0)),
        ],
        out_specs=pl.BlockSpec((1, tq, hw), lambda h, n, i: (n, i, h)),
        out_shape=jax.ShapeDtypeStruct((b, t, D_A), BF16),
        compiler_params=_params("arbitrary", "arbitrary", "arbitrary"),
        name="neighbourhood_attention",
    )(qkv, qkv, qkv, ck, cv, bias)


def _lru_kernel(xb_ref, gb_ref, cw_ref, cb_ref, w4h_ref, b4_ref, lam_ref, h0_ref, o_ref, fin_ref,
                xpad_scr, af_scr, uf_scr, ab_scr, ub_scr, *, t, chunk):
    seg = t // LRU_SEGS
    pitch = seg + SUBLANES
    pad = SUBLANES
    zeros = jnp.zeros((pad, BD_B), F32)
    xpad_scr[0:pad, :] = zeros
    xpad_scr[pad + t:2 * pad + t, :] = zeros
    xpad_scr[pad:pad + t, :] = xb_ref[0]

    c1 = (-0.5 * LRU_C) * _softplus(-lam_ref[...])
    w4h = w4h_ref[0]
    b4h = 0.5 * b4_ref[0]
    left = (CONV_W - 1) // 2
    for c in range(t // chunk):
        base = c * chunk
        xc = cb_ref[...]
        for k in range(CONV_W):
            xc = xc + xpad_scr[pl.ds(base + pad - left + k, chunk), :] * cw_ref[k:k + 1, :]
        th = jnp.tanh(_dot(xc.astype(BF16), w4h) + b4h)
        xh = 0.5 * xc
        for d, (a_scr, u_scr) in enumerate(((af_scr, uf_scr), (ab_scr, ub_scr))):
            tr = th[:, 2 * d * BD_B:(2 * d + 1) * BD_B]
            ti = th[:, (2 * d + 1) * BD_B:(2 * d + 2) * BD_B]
            log_a = c1[d:d + 1, :] * tr + c1[d:d + 1, :]
            a = jnp.exp(log_a)
            u = jnp.sqrt(-jnp.tanh(log_a) * (a * a + 1.0)) * (xh * ti + xh)
            for p in range(chunk // seg):
                s = (base + p * seg) // seg
                a_scr[s * pitch:s * pitch + seg, :] = a[p * seg:(p + 1) * seg, :]
                u_scr[s * pitch:s * pitch + seg, :] = u[p * seg:(p + 1) * seg, :]

    def scan(j, carry):
        hf, pf, hb, pb = carry
        rf = pl.ds(j, LRU_SEGS, stride=pitch)
        rb = pl.ds(seg - 1 - j, LRU_SEGS, stride=pitch)
        a = af_scr[rf, :]
        hf = a * hf + uf_scr[rf, :]
        pf = a * pf
        uf_scr[rf, :] = hf
        af_scr[rf, :] = pf
        a = ab_scr[rb, :]
        hb = a * hb + ub_scr[rb, :]
        pb = a * pb
        ub_scr[rb, :] = hb
        ab_scr[rb, :] = pb
        return hf, pf, hb, pb

    zero = jnp.zeros((LRU_SEGS, BD_B), F32)
    one = jnp.ones((LRU_SEGS, BD_B), F32)
    hf, pf, hb, pb = lax.fori_loop(0, seg, scan, (zero, one, zero, one), unroll=4)

    cf = h0_ref[0, 0:1, :]
    cin_f = []
    for s in range(LRU_SEGS):
        cin_f.append(cf)
        cf = hf[s:s + 1, :] + pf[s:s + 1, :] * cf
    cb = h0_ref[0, 1:2, :]
    cin_b = [None] * LRU_SEGS
    for s in reversed(range(LRU_SEGS)):
        cin_b[s] = cb
        cb = hb[s:s + 1, :] + pb[s:s + 1, :] * cb
    fin_ref[0, 0:1, :] = cf
    fin_ref[0, 1:2, :] = cb

    for s in range(LRU_SEGS):
        rows = slice(s * pitch, s * pitch + seg)
        hs = (uf_scr[rows, :] + af_scr[rows, :] * cin_f[s]) + (ub_scr[rows, :] + ab_scr[rows, :] * cin_b[s])
        nat = slice(s * seg, (s + 1) * seg)
        o_ref[0, nat, :] = (hs * jax.nn.gelu(gb_ref[0, nat, :])).astype(BF16)


def _rglru(xg, conv_w, conv_b, w4h, b4, lam, h0):
    b, t, _ = xg.shape
    chunk = min(t, 512)
    seg = t // LRU_SEGS
    col = lambda off: pl.BlockSpec((1, t, BD_B), lambda n, h: (n, 0, off + h))
    return pl.pallas_call(
        functools.partial(_lru_kernel, t=t, chunk=chunk),
        grid=(b, H_B),
        in_specs=[
            col(0), col(H_B),
            pl.BlockSpec((CONV_W, BD_B), lambda n, h: (0, h)),
            pl.BlockSpec((1, BD_B), lambda n, h: (0, h)),
            pl.BlockSpec((1, BD_B, 4 * BD_B), lambda n, h: (h, 0, 0)),
            pl.BlockSpec((1, 1, 4 * BD_B), lambda n, h: (h, 0, 0)),
            pl.BlockSpec((2, BD_B), lambda n, h: (0, h)),
            pl.BlockSpec((1, 2, BD_B), lambda n, h: (n, 0, h)),
        ],
        out_specs=[pl.BlockSpec((1, t, BD_B), lambda n, h: (n, 0, h)),
                   pl.BlockSpec((1, 2, BD_B), lambda n, h: (n, 0, h))],
        out_shape=[jax.ShapeDtypeStruct((b, t, D_RNN), BF16),
                   jax.ShapeDtypeStruct((b, 2, D_RNN), F32)],
        scratch_shapes=[pltpu.VMEM((t + 2 * SUBLANES, BD_B), F32)]
        + [pltpu.VMEM((LRU_SEGS * (seg + SUBLANES), BD_B), F32)] * 4,
        compiler_params=_params("arbitrary", "arbitrary"),
        name="rglru_branch",
    )(xg, xg, conv_w, conv_b.reshape(1, D_RNN), w4h, b4, lam, h0)


def _outproj_ab_kernel(x_ref, mod_ref, oa_ref, ob_ref, w_ref, o_ref):
    y = _dot(oa_ref[0], w_ref[0:D_A, :]) + _dot(ob_ref[0], w_ref[D_A:D_A + D_RNN, :])
    o_ref[0] = x_ref[0] + mod_ref[0, 5:6, :] * y


def _outproj_ab(x, mod, oa, ob, w):
    grp, t, d = x.shape
    tm = TOKEN_TILE
    tok = lambda width: pl.BlockSpec((1, tm, width), lambda b, i: (b, i, 0))
    return pl.pallas_call(
        _outproj_ab_kernel,
        grid=(grp, t // tm),
        in_specs=[
            tok(d),
            pl.BlockSpec((1, N_MOD, d), lambda b, i: (b, 0, 0)),
            tok(D_A), tok(D_RNN),
            pl.BlockSpec((D_A + D_RNN, d), lambda b, i: (0, 0)),
        ],
        out_specs=tok(d),
        out_shape=jax.ShapeDtypeStruct(x.shape, F32),
        compiler_params=_params("arbitrary", "arbitrary"),
        name="mixer_ab_out_proj",
    )(x, mod, oa, ob, w)


def _gate_prep_kernel(g_ref, b_ref, o_ref):
    t = g_ref.shape[2]
    g = g_ref[0] + b_ref[...]
    lane = lax.broadcasted_iota(jnp.int32, (H_C, t), 1) & (CHUNK - 1)
    cum = _log_sigmoid(g[H_C:2 * H_C])
    suf = _log_sigmoid(g[3 * H_C:4 * H_C])
    sh = 1
    while sh < CHUNK:
        cum = cum + jnp.where(lane >= sh, pltpu.roll(cum, sh, 1), 0.0)
        suf = suf + jnp.where(lane < CHUNK - sh, pltpu.roll(suf, t - sh, 1), 0.0)
        sh *= 2
    o_ref[0, 0:H_C] = g[0:H_C]
    o_ref[0, H_C:2 * H_C] = cum
    o_ref[0, 2 * H_C:3 * H_C] = g[2 * H_C:3 * H_C]
    o_ref[0, 3 * H_C:4 * H_C] = suf


def _gate_prep(g_rows, b_gate):
    b, r, t = g_rows.shape
    return pl.pallas_call(
        _gate_prep_kernel,
        grid=(b,),
        in_specs=[pl.BlockSpec((1, r, t), lambda n: (n, 0, 0)), pl.BlockSpec((r, 1), lambda n: (0, 0))],
        out_specs=pl.BlockSpec((1, r, t), lambda n: (n, 0, 0)),
        out_shape=jax.ShapeDtypeStruct(g_rows.shape, F32),
        compiler_params=_params("arbitrary"),
        name="mlstm_gate_prep",
    )(g_rows, b_gate.reshape(r, 1))


def _mlstm_chunk(q, kt, v_aug, ig_row, bc_row, bc_col, cm, m, reverse):
    ti = lax.broadcasted_iota(jnp.int32, (CHUNK, CHUNK), 0)
    si = lax.broadcasted_iota(jnp.int32, (CHUNK, CHUNK), 1)
    seen = (si >= ti) if reverse else (si <= ti)
    bk = bc_row[:, 0:1] if reverse else bc_row[:, CHUNK - 1:CHUNK]
    bc_rep = jnp.broadcast_to(bc_col, (CHUNK, CHUNK))
    dmat = jnp.where(seen, bc_rep - bc_row + ig_row, -jnp.inf)
    inter = bc_rep + m
    m_t = jnp.maximum(inter, jnp.broadcast_to(jnp.max(dmat, axis=1, keepdims=True), (CHUNK, CHUNK)))
    w = jnp.exp(dmat - m_t)
    g = jnp.exp(inter - m_t)
    kscale = DK_C ** -0.5
    s = _dot(q, kt) * (w * kscale)
    tot = _dot(s.astype(BF16), v_aug) + jnp.concatenate([g, g, g], axis=1) * _dot(q, cm.astype(BF16))
    inv = 1.0 / jnp.maximum(jnp.abs(tot[:, DV_C:]), jnp.exp(-m_t))
    hout = tot[:, :DV_C] * jnp.concatenate([inv, inv], axis=1)
    dend = bk - bc_row + ig_row
    m_new = jnp.maximum(bk + m, jnp.max(dend, axis=1, keepdims=True))
    we = jnp.exp(dend - m_new)
    ge = jnp.exp(bk + m - m_new)
    kwt = kt.astype(F32) * (we * kscale)
    c_new = ge * cm + _dot(kwt.astype(BF16), v_aug)
    return hout, c_new, m_new


def _mlstm_kernel(*refs, zero_state):
    if zero_state:
        qvf_ref, vvf_ref, ktf_ref, qvb_ref, vvb_ref, ktb_ref, gr_ref, gc_ref = refs[:8]
        rest = refs[8:]
    else:
        m0_ref, qvf_ref, vvf_ref, ktf_ref, qvb_ref, vvb_ref, ktb_ref, gr_ref, gc_ref = refs[:9]
        c0_ref, n0_ref = refs[9:11]
        rest = refs[11:]
    hf_ref, hb_ref, cfin_ref, nfin_ref, mfin_ref, c_scr, m_scr = rest
    n, hp, c = pl.program_id(0), pl.program_id(1), pl.program_id(2)
    eye = (lax.broadcasted_iota(jnp.int32, (DK_C, DK_C), 0) == lax.broadcasted_iota(jnp.int32, (DK_C, DK_C), 1))

    @pl.when(c == 0)
    def _():
        if zero_state:
            c_scr[...] = jnp.zeros_like(c_scr)
            m_scr[...] = jnp.zeros_like(m_scr)
            return
        for d in range(2):
            for j in range(MLSTM_HEADS):
                c_scr[d, j, :, 0:DV_C] = c0_ref[0, d, j]
                n_col = jnp.sum(jnp.where(eye, n0_ref[0, d, j], 0.0), axis=1, keepdims=True)
                c_scr[d, j, :, DV_C:DV_C + LANES] = jnp.broadcast_to(n_col, (DK_C, LANES))
                m_scr[d, j] = jnp.full(m_scr.shape[2:], m0_ref[n, d * H_C + hp * MLSTM_HEADS + j], F32)

    ones_col = jnp.ones((CHUNK, LANES), BF16)
    dirs = ((qvf_ref, vvf_ref, ktf_ref, hf_ref), (qvb_ref, vvb_ref, ktb_ref, hb_ref))
    nc = pl.num_programs(2)
    chains = []
    for d, (q_ref, v_ref, kt_ref, h_ref) in enumerate(dirs):
        cc = c if d == 0 else nc - 1 - c
        rows = pl.ds(pl.multiple_of(cc * CHUNK, CHUNK), CHUNK)
        for j in range(MLSTM_HEADS):
            sl = slice(j * DK_C, (j + 1) * DK_C)
            v_aug = jnp.concatenate([v_ref[0, :, sl], ones_col], axis=1)
            col = (2 * d + 1) * MLSTM_HEADS + j
            chains.append((d, j, h_ref, sl, _mlstm_chunk(
                q_ref[0, :, sl], kt_ref[0, 0, sl, :], v_aug,
                gr_ref[0, j, 2 * d, pl.ds(cc, 1), :], gr_ref[0, j, 2 * d + 1, pl.ds(cc, 1), :],
                gc_ref[0, 0, rows, col:col + 1],
                c_scr[d, j], m_scr[d, j, 0:1, 0:1], reverse=(d == 1))))
    for d, j, h_ref, sl, (hout, c_new, m_new) in chains:
        h_ref[0, :, sl] = hout
        c_scr[d, j] = c_new
        m_scr[d, j] = jnp.broadcast_to(m_new, m_scr.shape[2:])

    @pl.when(c == pl.num_programs(2) - 1)
    def _():
        for d in range(2):
            for j in range(MLSTM_HEADS):
                cfin_ref[0, d, j] = c_scr[d, j, :, 0:DV_C]
                n_col = c_scr[d, j, :, DV_C:DV_C + 1]
                nfin_ref[0, d, j] = jnp.sum(jnp.where(eye, n_col, 0.0), axis=0, keepdims=True)
                mfin_ref[0, d, j] = m_scr[d, j, 0:1, :]


def _rope_tables(t):
    half = DK_C // 2
    nf = half // 2
    pos = jnp.arange(t)
    freqs = ROPE_BASE ** (-jnp.arange(nf, dtype=F32) / nf)
    cos, sin = [], []
    for p in (pos // GRID_W, pos % GRID_W):
        ang = p.astype(F32)[:, None] * freqs[None, :]
        cos += [jnp.cos(ang), jnp.cos(ang)]
        sin += [-jnp.sin(ang), jnp.sin(ang)]
    return jnp.concatenate(cos, axis=-1), jnp.concatenate(sin, axis=-1)


def _mlstm(qv, kt, gates, c0, n0, m0):
    b, t, _ = qv.shape
    nc = t // CHUNK
    hps = MLSTM_HEADS
    ngrp = H_C // hps
    g4 = gates.reshape(b, 4, H_C, t)
    g_row = jnp.transpose(g4, (0, 2, 1, 3)).reshape(b, H_C, 4, nc, CHUNK)
    g_col = jnp.transpose(gates.reshape(b, 4, ngrp, hps, t), (0, 2, 4, 1, 3)).reshape(b, ngrp, t, 4 * hps)
    fwd = lambda c: c
    bwd = lambda c: nc - 1 - c

    def tok(off, order):
        return pl.BlockSpec((1, CHUNK, hps * DK_C), lambda n, h, c: (n, order(c), off + h))

    def kt_spec(order):
        return pl.BlockSpec((1, 1, hps * DK_C, CHUNK), lambda n, h, c: (n, order(c), h, 0))

    g_row_spec = pl.BlockSpec((1, hps, 4, nc, CHUNK), lambda n, h, c: (n, h, 0, 0, 0))
    g_col_spec = pl.BlockSpec((1, 1, t, 4 * hps), lambda n, h, c: (n, h, 0, 0))

    state = lambda *tail: pl.BlockSpec((1, 2, hps) + tail, lambda n, h, c: (n, 0, h) + (0,) * len(tail))
    hspec = lambda order: pl.BlockSpec((1, CHUNK, hps * DV_C), lambda n, h, c: (n, order(c), h))
    in_specs = [tok(0, fwd), tok(ngrp, fwd), kt_spec(fwd),
                tok(0, bwd), tok(ngrp, bwd), kt_spec(bwd),
                g_row_spec, g_col_spec]
    args = [qv, qv, kt, qv, qv, kt, g_row, g_col]
    zero_state = c0 is None
    if not zero_state:
        in_specs = [pl.BlockSpec(memory_space=pltpu.SMEM)] + in_specs + [state(DK_C, DV_C), state(1, DK_C)]
        args = [m0.reshape(b, 2 * H_C)] + args + [c0, n0.reshape(b, 2, H_C, 1, DK_C)]
    hf, hb, cfin, nfin, mfin = pl.pallas_call(
        functools.partial(_mlstm_kernel, zero_state=zero_state),
        grid=(b, ngrp, nc),
        in_specs=in_specs,
        out_specs=[hspec(fwd), hspec(bwd), state(DK_C, DV_C), state(1, DK_C), state(1, LANES)],
        out_shape=[jax.ShapeDtypeStruct((b, t, D_C), F32), jax.ShapeDtypeStruct((b, t, D_C), F32),
                   jax.ShapeDtypeStruct((b, 2, H_C, DK_C, DV_C), F32),
                   jax.ShapeDtypeStruct((b, 2, H_C, 1, DK_C), F32),
                   jax.ShapeDtypeStruct((b, 2, H_C, 1, LANES), F32)],
        scratch_shapes=[pltpu.VMEM((2, hps, DK_C, DV_C + LANES), F32),
                        pltpu.VMEM((2, hps, SUBLANES, LANES), F32)],
        compiler_params=_params("arbitrary", "arbitrary", "arbitrary"),
        name="mlstm_scan",
    )(*args)
    return hf, hb, cfin, nfin.reshape(b, 2, H_C, DK_C), mfin[:, :, :, 0, 0]


def _outproj_c_kernel(x_ref, mod_ref, hf_ref, hb_ref, og_ref, g_ref, w_ref, o_ref, hn_scr):
    for h in range(H_C):
        sl = slice(h * DV_C, (h + 1) * DV_C)
        hs = hf_ref[0, :, sl] + hb_ref[0, :, sl]
        hn_scr[:, sl] = (_rms(hs, g_ref[:, sl]) * jax.nn.sigmoid(og_ref[0, :, sl])).astype(BF16)
    o_ref[0] = x_ref[0] + mod_ref[0, 5:6, :] * _dot(hn_scr[...], w_ref[...])


def _outproj_c(x, mod, hf, hb, og, mh_g, w):
    grp, t, d = x.shape
    tm = TOKEN_TILE // 2
    tok = pl.BlockSpec((1, tm, d), lambda b, i: (b, i, 0))
    return pl.pallas_call(
        _outproj_c_kernel,
        grid=(grp, t // tm),
        in_specs=[
            tok,
            pl.BlockSpec((1, N_MOD, d), lambda b, i: (b, 0, 0)),
            tok, tok,
            pl.BlockSpec((1, tm, D_C), lambda b, i: (b, i, 0)),
            pl.BlockSpec((1, D_C), lambda b, i: (0, 0)),
            pl.BlockSpec((D_C, d), lambda b, i: (0, 0)),
        ],
        out_specs=tok,
        out_shape=jax.ShapeDtypeStruct(x.shape, F32),
        scratch_shapes=[pltpu.VMEM((tm, D_C), BF16)],
        compiler_params=_params("arbitrary", "arbitrary"),
        name="mixer_c_out_proj",
    )(x, mod, hf, hb, og, mh_g.reshape(1, D_C), w)


def _mixer_ab(j, xp, xs, mp, ms, g, p):
    bp, lp, bs = p["bp"], p["lp"], xs.shape[0]
    w_in = _tile_major(p["w_in_ab"][j].astype(BF16))
    w_out = p["w_out_ab"][j].astype(BF16)
    gains = jnp.stack([jnp.tile(p["qn_g"][j] * (HD_A ** -0.5), H_A), jnp.tile(p["kn_g"][j], H_A),
                       jnp.ones((D_A,), F32)]).reshape(3, 1, D_A)
    wa, wx, ba, bx = p["lru_wa"][j], p["lru_wx"][j], p["lru_ba"][j], p["lru_bx"][j]
    w4 = (0.5 * jnp.concatenate([wa[0], wx[0], wa[1], wx[1]], axis=-1)).astype(BF16)
    b4 = jnp.concatenate([v.reshape(H_B, 1, BD_B) for v in (ba[0], bx[0], ba[1], bx[1])], axis=-1)
    lru = (p["conv_w"][j], p["conv_b"][j], w4, b4, p["lru_lam"][j])

    qkv, xg = _inproj_ab(xp, mp, g, w_in, gains, F32)
    qkv = qkv.reshape(bp, lp, 3 * D_A)
    oa = _ctx_attention(qkv)
    ob, h_fin = _rglru(xg.reshape(bp, lp, 2 * D_RNN), *lru, jnp.zeros((bp, 2, D_RNN), F32))
    xp = _outproj_ab(xp, mp, oa.reshape(1, bp * lp, D_A), ob.reshape(1, bp * lp, D_RNN), w_out)
    new = (qkv[:, :, D_A:2 * D_A].reshape(bp, lp, H_A, HD_A), qkv[:, :, 2 * D_A:].reshape(bp, lp, H_A, HD_A), h_fin)

    past = p["cache_k"].shape[2]
    ck = p["cache_k"][:, j].reshape(bs, past, D_A).astype(BF16)
    cv = p["cache_v"][:, j].reshape(bs, past, D_A).astype(BF16)
    qkv, xg = _inproj_ab(xs, ms, g, w_in, gains, BF16)
    oa = _neighbourhood_attention(qkv, ck, cv, p["rpb"][j])
    ob, _ = _rglru(xg, *lru, p["state_lru"][:, j])
    xs = _outproj_ab(xs, ms, oa, ob, w_out)
    return xp, xs, new


def _mixer_c(j, xp, xs, mp, ms, g, p):
    bp, lp, bs, ts = p["bp"], p["lp"], xs.shape[0], xs.shape[1]
    w = p["w_in_c"][j].astype(BF16)
    wkt = w[:, D_C:2 * D_C].T
    w_gate = jnp.pad(w[:, 4 * D_C:], ((0, 0), (0, LANES - 4 * H_C)))
    w = _tile_major(w[:, :4 * D_C])
    w_out = p["w_out_c"][j].astype(BF16)

    def gates_of(zg, b, t):
        g_rows = jnp.transpose(zg.reshape(b, t, LANES)[:, :, :4 * H_C], (0, 2, 1))
        return _gate_prep(g_rows, p["b_gate_c"][j])

    qv, og, kt, zg = _inproj_c(xp, mp, g, w, wkt, w_gate, None)
    hf, hb, cfin, nfin, mfin = _mlstm(
        qv.reshape(bp, lp, 2 * D_C), kt.reshape(bp, lp // CHUNK, D_C, CHUNK), gates_of(zg, bp, lp), None, None, None)
    xp = _outproj_c(xp, mp, hf.reshape(1, bp * lp, D_C), hb.reshape(1, bp * lp, D_C), og, p["mh_norm_g"][j], w_out)

    qv, og, kt, zg = _inproj_c(xs, ms, g, w, wkt, w_gate, _rope_tables(ts))
    hf, hb, _, _, _ = _mlstm(qv, kt, gates_of(zg, bs, ts), p["state_mlstm_c"][:, j], p["state_mlstm_n"][:, j],
                             p["state_mlstm_m"][:, j])
    xs = _outproj_c(xs, ms, hf, hb, og, p["mh_norm_g"][j], w_out)
    return xp, xs, (cfin, nfin, mfin)


def kernel(x_prompt, x_sample, cache_k, cache_v, state_lru, state_mlstm_c, state_mlstm_n, state_mlstm_m, c, c_ctx, w_mod, b_mod, norm_g, w_ffn_gate, w_ffn_up, w_ffn_down, w_in_ab, w_out_ab, qn_g, kn_g, rpb, conv_w, conv_b, lru_wa, lru_ba, lru_wx, lru_bx, lru_lam, w_in_c, b_gate_c, mh_norm_g, w_out_c):
    bp, lp, d = x_prompt.shape
    bs = x_sample.shape[0]
    p = dict(bp=bp, lp=lp, cache_k=cache_k, cache_v=cache_v, state_lru=state_lru, state_mlstm_c=state_mlstm_c,
             state_mlstm_n=state_mlstm_n, state_mlstm_m=state_mlstm_m, w_in_ab=w_in_ab, w_out_ab=w_out_ab,
             qn_g=qn_g, kn_g=kn_g, rpb=rpb, conv_w=conv_w, conv_b=conv_b, lru_wa=lru_wa, lru_ba=lru_ba,
             lru_wx=lru_wx, lru_bx=lru_bx, lru_lam=lru_lam, w_in_c=w_in_c, b_gate_c=b_gate_c,
             mh_norm_g=mh_norm_g, w_out_c=w_out_c)

    cond = jnp.concatenate([c, c_ctx[None, :]], axis=0)
    cond = jnp.pad(cond, ((0, -(bs + 1) % SUBLANES), (0, 0)))
    mod = _modulation(cond, w_mod, b_mod)[:, :bs + 1].reshape(DEPTH, bs + 1, N_MOD, d)

    tiled = lambda w: jnp.swapaxes(w.astype(BF16).reshape(DEPTH, 2, d, D_FF // FF_TILE, FF_TILE), 2, 3)
    wg = tiled(w_ffn_gate)
    wu = tiled(w_ffn_up)
    wd = w_ffn_down.astype(BF16)

    xp = x_prompt.reshape(1, bp * lp, d)
    xs = x_sample
    new_ab, new_c = [], []
    for l in range(DEPTH):
        j = l // 2
        ms = mod[l, :bs]
        mp = mod[l, bs:]
        xp = _ffn(xp, mp, norm_g[l, 0], wg, wu, wd, l, 0, 0)
        xs = _ffn(xs, ms, norm_g[l, 0], wg, wu, wd, l, 0, 0)
        if l % 2 == 0:
            xp, xs, new = _mixer_ab(j, xp, xs, mp, ms, norm_g[l, 1], p)
            new_ab.append(new)
        else:
            xp, xs, new = _mixer_c(j, xp, xs, mp, ms, norm_g[l, 1], p)
            new_c.append(new)
        xp = _ffn(xp, mp, norm_g[l, 2], wg, wu, wd, l, 1, 2)
        xs = _ffn(xs, ms, norm_g[l, 2], wg, wu, wd, l, 1, 2)
    stack = lambda items, i: jnp.concatenate([jnp.expand_dims(it[i], 1) for it in items], axis=1)
    return (xp.reshape(bp, lp, d), xs, stack(new_ab, 0), stack(new_ab, 1), stack(new_ab, 2),
            stack(new_c, 0), stack(new_c, 1), stack(new_c, 2))
```

```python
import functools

import numpy as np
import jax
import jax.numpy as jnp
from jax import lax
from jax.experimental import pallas as pl
from jax.experimental.pallas import tpu as pltpu

D_MODEL = 2048
DEPTH = 2
GRID_W = 64
N_MOD = 9
D_FF = 5632
EPS = 1e-6
H_A = 8
HD_A = 128
D_A = H_A * HD_A
WIN_R = 8
WIN_C = 16
D_RNN = 1024
H_B = 8
BD_B = D_RNN // H_B
CONV_W = 4
LRU_C = 8.0
H_C = 8
DK_C = 256
DV_C = 256
D_C = H_C * DV_C
CHUNK = 128
ROPE_BASE = 10000.0

BF16 = jnp.bfloat16
F32 = jnp.float32

V7X_VMEM_BYTES = 64 * 1024 * 1024
VMEM_LIMIT = V7X_VMEM_BYTES - 8 * 1024 * 1024
FFN_VMEM_LIMIT = V7X_VMEM_BYTES - 5 * 1024 * 1024
LANES = 128
SUBLANES = 8

TOKEN_TILE = 512
FF_TILE = 512
FFN_NORM_ROWS = 128
PROJ_TOKEN_TILE = 1024
PROJ_TILE_C = 1024
PROJ_TILE_AB = 1024
NA_ROWS = 4
NA_KEY_ROWS = NA_ROWS + WIN_R
LRU_SEGS = SUBLANES
NA_HEADS = 4
MLSTM_HEADS = 8


def _params(*sem, vmem=VMEM_LIMIT):
    return pltpu.CompilerParams(dimension_semantics=sem, vmem_limit_bytes=vmem)


def _dot(a, b):
    return jnp.dot(a, b, preferred_element_type=F32)


def _dot_nt(a, b):
    return lax.dot_general(a, b, (((1,), (1,)), ((), ())), preferred_element_type=F32)


def _dot_tn(a, b):
    return lax.dot_general(a, b, (((0,), (0,)), ((), ())), preferred_element_type=F32)


def _rms(x, g):
    return x * lax.rsqrt(jnp.mean(x * x, axis=-1, keepdims=True) + EPS) * g


def _adaln(x, g, mod_ref, j):
    return _rms(x, g) * (1.0 + mod_ref[0, 3 * j + 1:3 * j + 2, :]) + mod_ref[0, 3 * j:3 * j + 1, :]


def _softplus(x):
    return jnp.maximum(x, 0.0) + jnp.log1p(jnp.exp(-jnp.abs(x)))


def _log_sigmoid(x):
    return -_softplus(-x)


def _mod_kernel(c_ref, w_ref, b_ref, o_ref):
    c = c_ref[...]
    s = (c * jax.nn.sigmoid(c)).astype(BF16)
    o_ref[0] = _dot(s, w_ref[0].astype(BF16)) + b_ref[0]


def _modulation(cond, w_mod, b_mod):
    r = cond.shape[0]
    n = N_MOD * D_MODEL
    tn = 1024
    return pl.pallas_call(
        _mod_kernel,
        grid=(DEPTH, n // tn),
        in_specs=[
            pl.BlockSpec((r, D_MODEL), lambda l, j: (0, 0)),
            pl.BlockSpec((1, D_MODEL, tn), lambda l, j: (l, 0, j)),
            pl.BlockSpec((1, 1, tn), lambda l, j: (l, 0, j)),
        ],
        out_specs=pl.BlockSpec((1, r, tn), lambda l, j: (l, 0, j)),
        out_shape=jax.ShapeDtypeStruct((DEPTH, r, n), F32),
        compiler_params=_params("arbitrary", "arbitrary"),
        name="modulation",
    )(cond, w_mod, b_mod.reshape(DEPTH, 1, n))


def _ffn_kernel(x0_ref, xn_ref, mod_ref, modn_ref, g_ref, wga_ref, wua_ref, wda_ref, wgb_ref, wub_ref, wdb_ref,
                o_ref, ha_scr, hb_scr, xa_scr, xb_scr, acc_scr, *, j, ns, single_last):
    p, f2 = pl.program_id(0), pl.program_id(1)
    rows = xn_ref.shape[1]
    tm = ha_scr.shape[0]

    @pl.when((p == 0) & (f2 == 0))
    def _():
        xa_scr[...] = x0_ref[0]
        ha_scr[...] = _adaln(x0_ref[0], g_ref[...], mod_ref, j).astype(BF16)

    @pl.when((f2 == 0) | (f2 == ns))
    def _():
        acc_scr[...] = jnp.zeros_like(acc_scr)

    def body(h_cur, h_nxt, x_nxt, f, both):
        r0 = pl.multiple_of(jnp.minimum(f, tm // rows - 1) * rows, rows)
        xs = xn_ref[0]
        x_nxt[pl.ds(r0, rows), :] = xs
        h_nxt[pl.ds(r0, rows), :] = _adaln(xs, g_ref[...], modn_ref, j).astype(BF16)
        h = h_cur[...]
        hidden = ((wga_ref, wua_ref, wda_ref), (wgb_ref, wub_ref, wdb_ref)) if both else ((wga_ref, wua_ref, wda_ref),)
        y = None
        for wg_ref, wu_ref, wd_ref in hidden:
            a = _dot(h, wg_ref[...])
            u = _dot(h, wu_ref[...])
            act = (a * jax.nn.sigmoid(a)) * u
            yd = _dot(act.astype(BF16), wd_ref[...])
            y = yd if y is None else y + yd
        acc_scr[...] += y

    n_both = ns - 1 if single_last else ns
    for half, (h_cur, h_nxt, x_nxt) in enumerate(((ha_scr, hb_scr, xb_scr), (hb_scr, ha_scr, xa_scr))):
        @pl.when((f2 >= half * ns) & (f2 < half * ns + n_both))
        def _():
            body(h_cur, h_nxt, x_nxt, f2 - half * ns, True)

        if single_last:
            @pl.when(f2 == half * ns + ns - 1)
            def _():
                body(h_cur, h_nxt, x_nxt, ns - 1, False)

    for last, x_cur in ((ns - 1, xa_scr), (2 * ns - 1, xb_scr)):
        @pl.when(f2 == last)
        def _():
            o_ref[0] = x_cur[...] + 0.5 * mod_ref[0, 3 * j + 2:3 * j + 3, :] * acc_scr[...]


def _ffn(x, mod, g, wg, wu, wd, l, s, j):
    grp, t, d = x.shape
    tm, tf = TOKEN_TILE, FF_TILE
    nf = D_FF // tf
    ns = (nf + 1) // 2
    per_group = t // tm
    ntile = grp * per_group
    rows = FFN_NORM_ROWS
    nsl = tm // rows
    assert ntile % 2 == 0 and ns >= nsl
    tile = lambda p, f2: 2 * p + f2 // ns
    nxt = lambda p, f2: jnp.minimum(tile(p, f2) + 1, ntile - 1)
    fa = lambda f2: jnp.minimum(2 * (f2 % ns), nf - 1)
    fb = lambda f2: jnp.minimum(2 * (f2 % ns) + 1, nf - 1)
    wcol = lambda fsel: pl.BlockSpec((None, None, d, tf), lambda p, f2: (l, s, 0, fsel(f2)))
    wrow = lambda fsel: pl.BlockSpec((None, None, tf, d), lambda p, f2: (l, s, fsel(f2), 0))
    out = pl.pallas_call(
        functools.partial(_ffn_kernel, j=j, ns=ns, single_last=nf % 2 == 1),
        grid=(ntile // 2, 2 * ns),
        in_specs=[
            pl.BlockSpec((1, tm, d), lambda p, f2: (0, 0, 0), pipeline_mode=pl.Buffered(1)),
            pl.BlockSpec((1, rows, d), lambda p, f2: (0, nxt(p, f2) * nsl + jnp.minimum(f2 % ns, nsl - 1), 0)),
            pl.BlockSpec((1, N_MOD, d), lambda p, f2: (tile(p, f2) // per_group, 0, 0)),
            pl.BlockSpec((1, N_MOD, d), lambda p, f2: (nxt(p, f2) // per_group, 0, 0)),
            pl.BlockSpec((1, d), lambda p, f2: (0, 0)),
            wcol(fa), wcol(fa), wrow(fa), wcol(fb), wcol(fb), wrow(fb),
        ],
        out_specs=pl.BlockSpec((1, tm, d), lambda p, f2: (0, tile(p, f2), 0)),
        out_shape=jax.ShapeDtypeStruct((1, grp * t, d), F32),
        scratch_shapes=[pltpu.VMEM((tm, d), BF16), pltpu.VMEM((tm, d), BF16),
                        pltpu.VMEM((tm, d), F32), pltpu.VMEM((tm, d), F32), pltpu.VMEM((tm, d), F32)],
        compiler_params=_params("arbitrary", "arbitrary", vmem=FFN_VMEM_LIMIT),
        name="macaron_ffn",
    )(x.reshape(1, grp * t, d), x.reshape(1, grp * t, d), mod, mod, g.reshape(1, d), wg, wu, wd, wg, wu, wd)
    return out.reshape(x.shape)


def _rope(x, cos, sin):
    half = DK_C // 2
    out = []
    for p in range(2):
        sl = slice(p * half, (p + 1) * half)
        xs = x[:, sl]
        out.append(xs * cos[:, sl] + pltpu.roll(xs, half // 2, 1) * sin[:, sl])
    return jnp.concatenate(out, axis=-1)


def _rope_t(x, cos, sin):
    q = DK_C // 4
    swapped = jnp.concatenate([x[q:2 * q], x[0:q], x[3 * q:4 * q], x[2 * q:3 * q]], axis=0)
    return x * cos + swapped * sin


def _inproj_ab_kernel(x_ref, mod_ref, g_ref, w_ref, gain_ref, oa_ref, ob_ref, h_scr, *, n_a, n_norm):
    k = pl.program_id(2)

    @pl.when(k == 0)
    def _():
        h_scr[...] = _adaln(x_ref[0], g_ref[...], mod_ref, 1).astype(BF16)

    @pl.when(k >= n_a)
    def _():
        ob_ref[0] = _dot(h_scr[...], w_ref[...])

    @pl.when(k < n_norm)
    def _():
        for pair in range(w_ref.shape[1] // (2 * HD_A)):
            z2 = _dot(h_scr[...], w_ref[:, 2 * pair * HD_A:2 * (pair + 1) * HD_A])
            for hh in range(2):
                zs = z2[:, hh * HD_A:(hh + 1) * HD_A]
                sl = slice((2 * pair + hh) * HD_A, (2 * pair + hh + 1) * HD_A)
                r = lax.rsqrt(jnp.mean(zs * zs, axis=-1, keepdims=True) + EPS)
                oa_ref[0, :, sl] = (zs * r * gain_ref[0, :, sl]).astype(oa_ref.dtype)

    @pl.when((k >= n_norm) & (k < n_a))
    def _():
        oa_ref[0] = (_dot(h_scr[...], w_ref[...]) * gain_ref[0]).astype(oa_ref.dtype)


def _inproj_ab(x, mod, g, w, gains, dtype_a):
    grp, t, d = x.shape
    tm, tn = PROJ_TOKEN_TILE, PROJ_TILE_AB
    n_a = 3 * D_A // tn
    nt = w.shape[1] // tn
    return pl.pallas_call(
        functools.partial(_inproj_ab_kernel, n_a=n_a, n_norm=2 * D_A // tn),
        grid=(grp, t // tm, nt),
        in_specs=[
            pl.BlockSpec((1, tm, d), lambda b, i, k: (b, i, 0)),
            pl.BlockSpec((1, N_MOD, d), lambda b, i, k: (b, 0, 0)),
            pl.BlockSpec((1, d), lambda b, i, k: (0, 0)),
            pl.BlockSpec((d, tn), lambda b, i, k: (0, k)),
            pl.BlockSpec((1, 1, tn), lambda b, i, k: (jnp.minimum(k, n_a - 1), 0, 0)),
        ],
        out_specs=[pl.BlockSpec((1, tm, tn), lambda b, i, k: (b, i, jnp.minimum(k, n_a - 1))),
                   pl.BlockSpec((1, tm, tn), lambda b, i, k: (b, i, jnp.maximum(k - n_a, 0)))],
        out_shape=[jax.ShapeDtypeStruct((grp, t, n_a * tn), dtype_a),
                   jax.ShapeDtypeStruct((grp, t, (nt - n_a) * tn), F32)],
        scratch_shapes=[pltpu.VMEM((tm, d), BF16)],
        compiler_params=_params("arbitrary", "arbitrary", "arbitrary"),
        name="mixer_ab_in_proj",
    )(x, mod, g.reshape(1, d), w, gains.reshape(n_a, 1, tn))


def _inproj_c_kernel(*refs, rotary, n_q, n_qv):
    x_ref, mod_ref, g_ref, w_ref, wgate_ref = refs[:5]
    pos = 5
    if rotary:
        cos_ref, sin_ref = refs[pos:pos + 2]
        pos += 2
    oa_ref, ob_ref, og_ref, h_scr = refs[pos:pos + 4]
    k = pl.program_id(2)
    tn = w_ref.shape[1]

    @pl.when(k == 0)
    def _():
        h = _adaln(x_ref[0], g_ref[...], mod_ref, 1).astype(BF16)
        h_scr[...] = h
        og_ref[0] = _dot(h, wgate_ref[...])

    @pl.when(k < n_q)
    def _():
        z = _dot(h_scr[...], w_ref[...])
        if rotary:
            for hh in range(tn // DK_C):
                sl = slice(hh * DK_C, (hh + 1) * DK_C)
                oa_ref[0, :, sl] = _rope(z[:, sl], cos_ref[...], sin_ref[...]).astype(BF16)
        else:
            oa_ref[0] = z.astype(BF16)

    @pl.when((k >= n_q) & (k < n_qv))
    def _():
        oa_ref[0] = _dot(h_scr[...], w_ref[...]).astype(BF16)

    @pl.when(k >= n_qv)
    def _():
        ob_ref[0] = _dot(h_scr[...], w_ref[...])


def _inproj_kt_kernel(*refs, rotary):
    x_ref, mod_ref, g_ref, wkt_ref = refs[:4]
    pos = 4
    if rotary:
        cost_ref, sint_ref = refs[pos:pos + 2]
        pos += 2
    okt_ref, h_scr = refs[pos:pos + 2]
    tn = wkt_ref.shape[0]

    @pl.when(pl.program_id(2) == 0)
    def _():
        h_scr[...] = _adaln(x_ref[0], g_ref[...], mod_ref, 1).astype(BF16)

    zt = _dot_nt(wkt_ref[...], h_scr[...])
    for ci in range(zt.shape[1] // CHUNK):
        cols = slice(ci * CHUNK, (ci + 1) * CHUNK)
        if rotary:
            for hh in range(tn // DK_C):
                sl = slice(hh * DK_C, (hh + 1) * DK_C)
                okt_ref[0, ci, sl, :] = _rope_t(zt[sl, cols], cost_ref[:, cols], sint_ref[:, cols]).astype(BF16)
        else:
            okt_ref[0, ci] = zt[:, cols].astype(BF16)


def _inproj_c(x, mod, g, w, wkt, w_gate, rope_tables):
    grp, t, d = x.shape
    tm, tn = PROJ_TOKEN_TILE, PROJ_TILE_C
    n_q, n_qv, n_main = D_C // tn, 2 * D_C // tn, 3 * D_C // tn

    def w_col(k):
        return jnp.where(k < n_q, k, k + n_q)

    rotary = rope_tables is not None
    common = [
        pl.BlockSpec((1, tm, d), lambda b, i, k: (b, i, 0)),
        pl.BlockSpec((1, N_MOD, d), lambda b, i, k: (b, 0, 0)),
        pl.BlockSpec((1, d), lambda b, i, k: (0, 0)),
    ]
    in_specs = common + [pl.BlockSpec((d, tn), lambda b, i, k: (0, w_col(k))),
                         pl.BlockSpec((d, LANES), lambda b, i, k: (0, 0))]
    args = [x, mod, g.reshape(1, d), w, w_gate]
    kt_specs = common + [pl.BlockSpec((tn, d), lambda b, i, k: (k, 0))]
    kt_args = [x, mod, g.reshape(1, d), wkt]
    if rotary:
        cos, sin = rope_tables
        in_specs += [pl.BlockSpec((tm, DK_C), lambda b, i, k: (i, 0))] * 2
        args += [cos, sin]
        kt_specs += [pl.BlockSpec((DK_C, tm), lambda b, i, k: (0, i))] * 2
        kt_args += [cos.T, sin.T]
    qv, og, zg = pl.pallas_call(
        functools.partial(_inproj_c_kernel, rotary=rotary, n_q=n_q, n_qv=n_qv),
        grid=(grp, t // tm, n_main),
        in_specs=in_specs,
        out_specs=[pl.BlockSpec((1, tm, tn), lambda b, i, k: (b, i, jnp.minimum(k, n_qv - 1))),
                   pl.BlockSpec((1, tm, tn), lambda b, i, k: (b, i, jnp.maximum(k - n_qv, 0))),
                   pl.BlockSpec((1, tm, LANES), lambda b, i, k: (b, i, 0))],
        out_shape=[jax.ShapeDtypeStruct((grp, t, 2 * D_C), BF16),
                   jax.ShapeDtypeStruct((grp, t, D_C), F32),
                   jax.ShapeDtypeStruct((grp, t, LANES), F32)],
        scratch_shapes=[pltpu.VMEM((tm, d), BF16)],
        compiler_params=_params("arbitrary", "arbitrary", "arbitrary"),
        name="mixer_c_in_proj",
    )(*args)
    kt = pl.pallas_call(
        functools.partial(_inproj_kt_kernel, rotary=rotary),
        grid=(grp, t // tm, D_C // tn),
        in_specs=kt_specs,
        out_specs=pl.BlockSpec((1, tm // CHUNK, tn, CHUNK), lambda b, i, k: (b, i, k, 0)),
        out_shape=jax.ShapeDtypeStruct((grp, t // CHUNK, D_C, CHUNK), BF16),
        scratch_shapes=[pltpu.VMEM((tm, d), BF16)],
        compiler_params=_params("arbitrary", "arbitrary", "arbitrary"),
        name="mixer_c_in_proj_kt",
    )(*kt_args)
    return qv, og, kt, zg


def _ctx_attn_kernel(q_ref, k_ref, v_ref, o_ref):
    s = _dot_nt(q_ref[0].astype(BF16), k_ref[0].astype(BF16))
    p = jnp.exp(s - jnp.max(s, axis=-1, keepdims=True))
    o = _dot(p.astype(BF16), v_ref[0].astype(BF16)) / jnp.sum(p, axis=-1, keepdims=True)
    o_ref[0] = o.astype(BF16)


def _ctx_attention(qkv):
    b, l, _ = qkv.shape
    blk = lambda off: pl.BlockSpec((1, l, HD_A), lambda i, h: (i, 0, off + h))
    return pl.pallas_call(
        _ctx_attn_kernel,
        grid=(b, H_A),
        in_specs=[blk(0), blk(H_A), blk(2 * H_A)],
        out_specs=pl.BlockSpec((1, l, HD_A), lambda i, h: (i, 0, h)),
        out_shape=jax.ShapeDtypeStruct((b, l, D_A), BF16),
        compiler_params=_params("arbitrary", "arbitrary"),
        name="ctx_attention",
    )(qkv, qkv, qkv)


def _na_bias_tables(rpb, rows):
    nblk = rows // NA_ROWS
    n_dr, n_dc = 2 * WIN_R - 1, 2 * WIN_C - 1
    c = np.arange(GRID_W)[:, None]
    kc = np.arange(GRID_W)[None, :]
    dc = np.clip(kc - c + WIN_C - 1, 0, n_dc - 1)
    pick_c = (dc[None] == np.arange(n_dc)[:, None, None]).astype(np.float32)
    c0 = np.clip(c - WIN_C // 2, 0, GRID_W - WIN_C)
    ok_c = (kc >= c0) & (kc < c0 + WIN_C)
    by_col = jnp.einsum("hrd,dck->hrck", rpb, pick_c, precision=lax.Precision.HIGHEST)
    tables = []
    for i in (0, 1, nblk - 1):
        ks = int(np.clip(NA_ROWS * i - WIN_R // 2, 0, rows - NA_KEY_ROWS))
        r = (NA_ROWS * i + np.arange(NA_ROWS))[:, None]
        kr = (ks + np.arange(NA_KEY_ROWS))[None, :]
        r0 = np.clip(r - WIN_R // 2, 0, rows - WIN_R)
        ok_r = (kr >= r0) & (kr < r0 + WIN_R)
        dr = np.clip(kr - r + WIN_R - 1, 0, n_dr - 1)
        pick_r = (dr[None] == np.arange(n_dr)[:, None, None]).astype(np.float32)
        bias = jnp.einsum("hrck,rqs->hqcsk", by_col, pick_r, precision=lax.Precision.HIGHEST)
        ok = ok_r[None, :, None, :, None] & ok_c[None, None, :, None, :]
        bias = jnp.where(ok, bias, -jnp.inf)
        tables.append(bias.reshape(H_A, NA_ROWS * GRID_W, NA_KEY_ROWS * GRID_W))
    return jnp.stack(tables, axis=1)


def _na_kernel(q_ref, k_ref, v_ref, ck_ref, cv_ref, bias_ref, o_ref, *, rows):
    i = pl.program_id(2)
    nblk = rows // NA_ROWS
    ks = jnp.clip(NA_ROWS * i - WIN_R // 2, 0, rows - NA_KEY_ROWS)
    start = pl.multiple_of(ks * GRID_W, GRID_W)
    nkey = NA_KEY_ROWS * GRID_W
    kind = jnp.where(i == 0, 0, jnp.where(i == nblk - 1, 2, 1))
    ones_w = jnp.ones((nkey, HD_A), BF16)
    ones_c = jnp.ones((ck_ref.shape[1], HD_A), BF16)
    for hh in range(NA_HEADS):
        sl = slice(hh * HD_A, (hh + 1) * HD_A)
        q = q_ref[0, :, sl]
        kw = k_ref[0, pl.ds(start, nkey), sl]
        vw = jnp.concatenate([v_ref[0, pl.ds(start, nkey), sl], ones_w], axis=1)
        vc = jnp.concatenate([cv_ref[0, :, sl], ones_c], axis=1)
        s_w = _dot_nt(q, kw) + bias_ref[hh, kind]
        s_c = _dot_nt(q, ck_ref[0, :, sl])
        m = jnp.maximum(jnp.max(s_w, axis=-1, keepdims=True), jnp.max(s_c, axis=-1, keepdims=True))
        p_w = jnp.exp(s_w - m).astype(BF16)
        p_c = jnp.exp(s_c - m).astype(BF16)
        o = _dot(p_w, vw) + _dot(p_c, vc)
        o_ref[0, :, sl] = (o[:, :HD_A] / o[:, HD_A:]).astype(BF16)


def _neighbourhood_attention(qkv, ck, cv, rpb):
    b, t, _ = qkv.shape
    l = ck.shape[1]
    rows = t // GRID_W
    nblk = rows // NA_ROWS
    tq = NA_ROWS * GRID_W
    bias = _na_bias_tables(rpb, rows)
    hw = NA_HEADS * HD_A
    ngrp = H_A // NA_HEADS
    full = lambda off: pl.BlockSpec((1, t, hw), lambda h, n, i: (n, 0, off + h))
    ctx = pl.BlockSpec((1, l, hw), lambda h, n, i: (n, 0, h))
    return pl.pallas_call(
        functools.partial(_na_kernel, rows=rows),
        grid=(ngrp, b, nblk),
        in_specs=[
            pl.BlockSpec((1, tq, hw), lambda h, n, i: (n, i, h)),
            full(ngrp), full(2 * ngrp), ctx, ctx,
            pl.BlockSpec((NA_HEADS, 3, tq, NA_KEY_ROWS * GRID_W), lambda h, n, i: (h, 0, 0, 0)),
        ],
        out_specs=pl.BlockSpec((1, tq, hw), lambda h, n, i: (n, i, h)),
        out_shape=jax.ShapeDtypeStruct((b, t, D_A), BF16),
        compiler_params=_params("arbitrary", "arbitrary", "arbitrary"),
        name="neighbourhood_attention",
    )(qkv, qkv, qkv, ck, cv, bias)


def _lru_kernel(xb_ref, gb_ref, cw_ref, cb_ref, w4h_ref, b4_ref, lam_ref, h0_ref, o_ref, fin_ref,
                xpad_scr, af_scr, uf_scr, ab_scr, ub_scr, *, t, chunk):
    seg = t // LRU_SEGS
    pitch = seg + SUBLANES
    pad = SUBLANES
    zeros = jnp.zeros((pad, BD_B), F32)
    xpad_scr[0:pad, :] = zeros
    xpad_scr[pad + t:2 * pad + t, :] = zeros
    xpad_scr[pad:pad + t, :] = xb_ref[0]

    c1 = (-0.5 * LRU_C) * _softplus(-lam_ref[...])
    w4h = w4h_ref[0]
    b4h = 0.5 * b4_ref[0]
    left = (CONV_W - 1) // 2
    for c in range(t // chunk):
        base = c * chunk
        xc = cb_ref[...]
        for k in range(CONV_W):
            xc = xc + xpad_scr[pl.ds(base + pad - left + k, chunk), :] * cw_ref[k:k + 1, :]
        th = jnp.tanh(_dot(xc.astype(BF16), w4h) + b4h)
        xh = 0.5 * xc
        for d, (a_scr, u_scr) in enumerate(((af_scr, uf_scr), (ab_scr, ub_scr))):
            tr = th[:, 2 * d * BD_B:(2 * d + 1) * BD_B]
            ti = th[:, (2 * d + 1) * BD_B:(2 * d + 2) * BD_B]
            log_a = c1[d:d + 1, :] * tr + c1[d:d + 1, :]
            a = jnp.exp(log_a)
            u = jnp.sqrt(-jnp.tanh(log_a) * (a * a + 1.0)) * (xh * ti + xh)
            for p in range(chunk // seg):
                s = (base + p * seg) // seg
                a_scr[s * pitch:s * pitch + seg, :] = a[p * seg:(p + 1) * seg, :]
                u_scr[s * pitch:s * pitch + seg, :] = u[p * seg:(p + 1) * seg, :]

    def scan(j, carry):
        hf, pf, hb, pb = carry
        rf = pl.ds(j, LRU_SEGS, stride=pitch)
        rb = pl.ds(seg - 1 - j, LRU_SEGS, stride=pitch)
        a = af_scr[rf, :]
        hf = a * hf + uf_scr[rf, :]
        pf = a * pf
        uf_scr[rf, :] = hf
        af_scr[rf, :] = pf
        a = ab_scr[rb, :]
        hb = a * hb + ub_scr[rb, :]
        pb = a * pb
        ub_scr[rb, :] = hb
        ab_scr[rb, :] = pb
        return hf, pf, hb, pb

    zero = jnp.zeros((LRU_SEGS, BD_B), F32)
    one = jnp.ones((LRU_SEGS, BD_B), F32)
    hf, pf, hb, pb = lax.fori_loop(0, seg, scan, (zero, one, zero, one), unroll=4)

    cf = h0_ref[0, 0:1, :]
    cin_f = []
    for s in range(LRU_SEGS):
        cin_f.append(cf)
        cf = hf[s:s + 1, :] + pf[s:s + 1, :] * cf
    cb = h0_ref[0, 1:2, :]
    cin_b = [None] * LRU_SEGS
    for s in reversed(range(LRU_SEGS)):
        cin_b[s] = cb
        cb = hb[s:s + 1, :] + pb[s:s + 1, :] * cb
    fin_ref[0, 0:1, :] = cf
    fin_ref[0, 1:2, :] = cb

    for s in range(LRU_SEGS):
        rows = slice(s * pitch, s * pitch + seg)
        hs = (uf_scr[rows, :] + af_scr[rows, :] * cin_f[s]) + (ub_scr[rows, :] + ab_scr[rows, :] * cin_b[s])
        nat = slice(s * seg, (s + 1) * seg)
        o_ref[0, nat, :] = (hs * jax.nn.gelu(gb_ref[0, nat, :])).astype(BF16)


def _rglru(xg, conv_w, conv_b, w4h, b4, lam, h0):
    b, t, _ = xg.shape
    chunk = min(t, 512)
    seg = t // LRU_SEGS
    col = lambda off: pl.BlockSpec((1, t, BD_B), lambda n, h: (n, 0, off + h))
    return pl.pallas_call(
        functools.partial(_lru_kernel, t=t, chunk=chunk),
        grid=(b, H_B),
        in_specs=[
            col(0), col(H_B),
            pl.BlockSpec((CONV_W, BD_B), lambda n, h: (0, h)),
            pl.BlockSpec((1, BD_B), lambda n, h: (0, h)),
            pl.BlockSpec((1, BD_B, 4 * BD_B), lambda n, h: (h, 0, 0)),
            pl.BlockSpec((1, 1, 4 * BD_B), lambda n, h: (h, 0, 0)),
            pl.BlockSpec((2, BD_B), lambda n, h: (0, h)),
            pl.BlockSpec((1, 2, BD_B), lambda n, h: (n, 0, h)),
        ],
        out_specs=[pl.BlockSpec((1, t, BD_B), lambda n, h: (n, 0, h)),
                   pl.BlockSpec((1, 2, BD_B), lambda n, h: (n, 0, h))],
        out_shape=[jax.ShapeDtypeStruct((b, t, D_RNN), BF16),
                   jax.ShapeDtypeStruct((b, 2, D_RNN), F32)],
        scratch_shapes=[pltpu.VMEM((t + 2 * SUBLANES, BD_B), F32)]
        + [pltpu.VMEM((LRU_SEGS * (seg + SUBLANES), BD_B), F32)] * 4,
        compiler_params=_params("arbitrary", "arbitrary"),
        name="rglru_branch",
    )(xg, xg, conv_w, conv_b.reshape(1, D_RNN), w4h, b4, lam, h0)


def _outproj_ab_kernel(x_ref, mod_ref, oa_ref, ob_ref, w_ref, o_ref):
    y = _dot(oa_ref[0], w_ref[0:D_A, :]) + _dot(ob_ref[0], w_ref[D_A:D_A + D_RNN, :])
    o_ref[0] = x_ref[0] + mod_ref[0, 5:6, :] * y


def _outproj_ab(x, mod, oa, ob, w):
    grp, t, d = x.shape
    tm = TOKEN_TILE
    tok = lambda width: pl.BlockSpec((1, tm, width), lambda b, i: (b, i, 0))
    return pl.pallas_call(
        _outproj_ab_kernel,
        grid=(grp, t // tm),
        in_specs=[
            tok(d),
            pl.BlockSpec((1, N_MOD, d), lambda b, i: (b, 0, 0)),
            tok(D_A), tok(D_RNN),
            pl.BlockSpec((D_A + D_RNN, d), lambda b, i: (0, 0)),
        ],
        out_specs=tok(d),
        out_shape=jax.ShapeDtypeStruct(x.shape, F32),
        compiler_params=_params("arbitrary", "arbitrary"),
        name="mixer_ab_out_proj",
    )(x, mod, oa, ob, w)


def _gate_prep_kernel(g_ref, b_ref, o_ref):
    t = g_ref.shape[2]
    g = g_ref[0] + b_ref[...]
    lane = lax.broadcasted_iota(jnp.int32, (H_C, t), 1) & (CHUNK - 1)
    cum = _log_sigmoid(g[H_C:2 * H_C])
    suf = _log_sigmoid(g[3 * H_C:4 * H_C])
    sh = 1
    while sh < CHUNK:
        cum = cum + jnp.where(lane >= sh, pltpu.roll(cum, sh, 1), 0.0)
        suf = suf + jnp.where(lane < CHUNK - sh, pltpu.roll(suf, t - sh, 1), 0.0)
        sh *= 2
    o_ref[0, 0:H_C] = g[0:H_C]
    o_ref[0, H_C:2 * H_C] = cum
    o_ref[0, 2 * H_C:3 * H_C] = g[2 * H_C:3 * H_C]
    o_ref[0, 3 * H_C:4 * H_C] = suf


def _gate_prep(g_rows, b_gate):
    b, r, t = g_rows.shape
    return pl.pallas_call(
        _gate_prep_kernel,
        grid=(b,),
        in_specs=[pl.BlockSpec((1, r, t), lambda n: (n, 0, 0)), pl.BlockSpec((r, 1), lambda n: (0, 0))],
        out_specs=pl.BlockSpec((1, r, t), lambda n: (n, 0, 0)),
        out_shape=jax.ShapeDtypeStruct(g_rows.shape, F32),
        compiler_params=_params("arbitrary"),
        name="mlstm_gate_prep",
    )(g_rows, b_gate.reshape(r, 1))


def _mlstm_chunk(q, kt, v_aug, ig_row, bc_row, bc_col, cm, m, reverse):
    ti = lax.broadcasted_iota(jnp.int32, (CHUNK, CHUNK), 0)
    si = lax.broadcasted_iota(jnp.int32, (CHUNK, CHUNK), 1)
    seen = (si >= ti) if reverse else (si <= ti)
    bk = bc_row[:, 0:1] if reverse else bc_row[:, CHUNK - 1:CHUNK]
    bc_rep = jnp.broadcast_to(bc_col, (CHUNK, CHUNK))
    dmat = jnp.where(seen, bc_rep - bc_row + ig_row, -jnp.inf)
    inter = bc_rep + m
    m_t = jnp.maximum(inter, jnp.broadcast_to(jnp.max(dmat, axis=1, keepdims=True), (CHUNK, CHUNK)))
    w = jnp.exp(dmat - m_t)
    g = jnp.exp(inter - m_t)
    kscale = DK_C ** -0.5
    s = _dot(q, kt) * (w * kscale)
    tot = _dot(s.astype(BF16), v_aug) + jnp.concatenate([g, g, g], axis=1) * _dot(q, cm.astype(BF16))
    inv = 1.0 / jnp.maximum(jnp.abs(tot[:, DV_C:]), jnp.exp(-m_t))
    hout = tot[:, :DV_C] * jnp.concatenate([inv, inv], axis=1)
    dend = bk - bc_row + ig_row
    m_new = jnp.maximum(bk + m, jnp.max(dend, axis=1, keepdims=True))
    we = jnp.exp(dend - m_new)
    ge = jnp.exp(bk + m - m_new)
    kwt = kt.astype(F32) * (we * kscale)
    c_new = ge * cm + _dot(kwt.astype(BF16), v_aug)
    return hout, c_new, m_new


def _mlstm_kernel(*refs, zero_state):
    if zero_state:
        qvf_ref, vvf_ref, ktf_ref, qvb_ref, vvb_ref, ktb_ref, gr_ref, gc_ref = refs[:8]
        rest = refs[8:]
    else:
        m0_ref, qvf_ref, vvf_ref, ktf_ref, qvb_ref, vvb_ref, ktb_ref, gr_ref, gc_ref = refs[:9]
        c0_ref, n0_ref = refs[9:11]
        rest = refs[11:]
    hf_ref, hb_ref, cfin_ref, nfin_ref, mfin_ref, c_scr, m_scr = rest
    n, hp, c = pl.program_id(0), pl.program_id(1), pl.program_id(2)
    eye = (lax.broadcasted_iota(jnp.int32, (DK_C, DK_C), 0) == lax.broadcasted_iota(jnp.int32, (DK_C, DK_C), 1))

    @pl.when(c == 0)
    def _():
        if zero_state:
            c_scr[...] = jnp.zeros_like(c_scr)
            m_scr[...] = jnp.zeros_like(m_scr)
            return
        for d in range(2):
            for j in range(MLSTM_HEADS):
                c_scr[d, j, :, 0:DV_C] = c0_ref[0, d, j]
                n_col = jnp.sum(jnp.where(eye, n0_ref[0, d, j], 0.0), axis=1, keepdims=True)
                c_scr[d, j, :, DV_C:DV_C + LANES] = jnp.broadcast_to(n_col, (DK_C, LANES))
                m_scr[d, j] = jnp.full(m_scr.shape[2:], m0_ref[n, d * H_C + hp * MLSTM_HEADS + j], F32)

    ones_col = jnp.ones((CHUNK, LANES), BF16)
    dirs = ((qvf_ref, vvf_ref, ktf_ref, hf_ref), (qvb_ref, vvb_ref, ktb_ref, hb_ref))
    nc = pl.num_programs(2)
    chains = []
    for d, (q_ref, v_ref, kt_ref, h_ref) in enumerate(dirs):
        cc = c if d == 0 else nc - 1 - c
        rows = pl.ds(pl.multiple_of(cc * CHUNK, CHUNK), CHUNK)
        for j in range(MLSTM_HEADS):
            sl = slice(j * DK_C, (j + 1) * DK_C)
            v_aug = jnp.concatenate([v_ref[0, :, sl], ones_col], axis=1)
            col = (2 * d + 1) * MLSTM_HEADS + j
            chains.append((d, j, h_ref, sl, _mlstm_chunk(
                q_ref[0, :, sl], kt_ref[0, 0, sl, :], v_aug,
                gr_ref[0, j, 2 * d, pl.ds(cc, 1), :], gr_ref[0, j, 2 * d + 1, pl.ds(cc, 1), :],
                gc_ref[0, 0, rows, col:col + 1],
                c_scr[d, j], m_scr[d, j, 0:1, 0:1], reverse=(d == 1))))
    for d, j, h_ref, sl, (hout, c_new, m_new) in chains:
        h_ref[0, :, sl] = hout
        c_scr[d, j] = c_new
        m_scr[d, j] = jnp.broadcast_to(m_new, m_scr.shape[2:])

    @pl.when(c == pl.num_programs(2) - 1)
    def _():
        for d in range(2):
            for j in range(MLSTM_HEADS):
                cfin_ref[0, d, j] = c_scr[d, j, :, 0:DV_C]
                n_col = c_scr[d, j, :, DV_C:DV_C + 1]
                nfin_ref[0, d, j] = jnp.sum(jnp.where(eye, n_col, 0.0), axis=0, keepdims=True)
                mfin_ref[0, d, j] = m_scr[d, j, 0:1, :]


def _rope_tables(t):
    half = DK_C // 2
    nf = half // 2
    pos = jnp.arange(t)
    freqs = ROPE_BASE ** (-jnp.arange(nf, dtype=F32) / nf)
    cos, sin = [], []
    for p in (pos // GRID_W, pos % GRID_W):
        ang = p.astype(F32)[:, None] * freqs[None, :]
        cos += [jnp.cos(ang), jnp.cos(ang)]
        sin += [-jnp.sin(ang), jnp.sin(ang)]
    return jnp.concatenate(cos, axis=-1), jnp.concatenate(sin, axis=-1)


def _mlstm(qv, kt, gates, c0, n0, m0):
    b, t, _ = qv.shape
    nc = t // CHUNK
    hps = MLSTM_HEADS
    ngrp = H_C // hps
    g4 = gates.reshape(b, 4, H_C, t)
    g_row = jnp.transpose(g4, (0, 2, 1, 3)).reshape(b, H_C, 4, nc, CHUNK)
    g_col = jnp.transpose(gates.reshape(b, 4, ngrp, hps, t), (0, 2, 4, 1, 3)).reshape(b, ngrp, t, 4 * hps)
    fwd = lambda c: c
    bwd = lambda c: nc - 1 - c

    def tok(off, order):
        return pl.BlockSpec((1, CHUNK, hps * DK_C), lambda n, h, c: (n, order(c), off + h))

    def kt_spec(order):
        return pl.BlockSpec((1, 1, hps * DK_C, CHUNK), lambda n, h, c: (n, order(c), h, 0))

    g_row_spec = pl.BlockSpec((1, hps, 4, nc, CHUNK), lambda n, h, c: (n, h, 0, 0, 0))
    g_col_spec = pl.BlockSpec((1, 1, t, 4 * hps), lambda n, h, c: (n, h, 0, 0))

    state = lambda *tail: pl.BlockSpec((1, 2, hps) + tail, lambda n, h, c: (n, 0, h) + (0,) * len(tail))
    hspec = lambda order: pl.BlockSpec((1, CHUNK, hps * DV_C), lambda n, h, c: (n, order(c), h))
    in_specs = [tok(0, fwd), tok(ngrp, fwd), kt_spec(fwd),
                tok(0, bwd), tok(ngrp, bwd), kt_spec(bwd),
                g_row_spec, g_col_spec]
    args = [qv, qv, kt, qv, qv, kt, g_row, g_col]
    zero_state = c0 is None
    if not zero_state:
        in_specs = [pl.BlockSpec(memory_space=pltpu.SMEM)] + in_specs + [state(DK_C, DV_C), state(1, DK_C)]
        args = [m0.reshape(b, 2 * H_C)] + args + [c0, n0.reshape(b, 2, H_C, 1, DK_C)]
    hf, hb, cfin, nfin, mfin = pl.pallas_call(
        functools.partial(_mlstm_kernel, zero_state=zero_state),
        grid=(b, ngrp, nc),
        in_specs=in_specs,
        out_specs=[hspec(fwd), hspec(bwd), state(DK_C, DV_C), state(1, DK_C), state(1, LANES)],
        out_shape=[jax.ShapeDtypeStruct((b, t, D_C), F32), jax.ShapeDtypeStruct((b, t, D_C), F32),
                   jax.ShapeDtypeStruct((b, 2, H_C, DK_C, DV_C), F32),
                   jax.ShapeDtypeStruct((b, 2, H_C, 1, DK_C), F32),
                   jax.ShapeDtypeStruct((b, 2, H_C, 1, LANES), F32)],
        scratch_shapes=[pltpu.VMEM((2, hps, DK_C, DV_C + LANES), F32),
                        pltpu.VMEM((2, hps, SUBLANES, LANES), F32)],
        compiler_params=_params("arbitrary", "arbitrary", "arbitrary"),
        name="mlstm_scan",
    )(*args)
    return hf, hb, cfin, nfin.reshape(b, 2, H_C, DK_C), mfin[:, :, :, 0, 0]


def _outproj_c_kernel(x_ref, mod_ref, hf_ref, hb_ref, og_ref, g_ref, w_ref, o_ref, hn_scr):
    for h in range(H_C):
        sl = slice(h * DV_C, (h + 1) * DV_C)
        hs = hf_ref[0, :, sl] + hb_ref[0, :, sl]
        hn_scr[:, sl] = (_rms(hs, g_ref[:, sl]) * jax.nn.sigmoid(og_ref[0, :, sl])).astype(BF16)
    o_ref[0] = x_ref[0] + mod_ref[0, 5:6, :] * _dot(hn_scr[...], w_ref[...])


def _outproj_c(x, mod, hf, hb, og, mh_g, w):
    grp, t, d = x.shape
    tm = TOKEN_TILE
    tok = pl.BlockSpec((1, tm, d), lambda b, i: (b, i, 0))
    return pl.pallas_call(
        _outproj_c_kernel,
        grid=(grp, t // tm),
        in_specs=[
            tok,
            pl.BlockSpec((1, N_MOD, d), lambda b, i: (b, 0, 0)),
            tok, tok,
            pl.BlockSpec((1, tm, D_C), lambda b, i: (b, i, 0)),
            pl.BlockSpec((1, D_C), lambda b, i: (0, 0)),
            pl.BlockSpec((D_C, d), lambda b, i: (0, 0), pipeline_mode=pl.Buffered(1)),
        ],
        out_specs=tok,
        out_shape=jax.ShapeDtypeStruct(x.shape, F32),
        scratch_shapes=[pltpu.VMEM((tm, D_C), BF16)],
        compiler_params=_params("arbitrary", "arbitrary"),
        name="mixer_c_out_proj",
    )(x, mod, hf, hb, og, mh_g.reshape(1, D_C), w)


def _mixer_ab(j, xp, xs, mp, ms, g, p):
    bp, lp, bs = p["bp"], p["lp"], xs.shape[0]
    w_in = p["w_in_ab"][j].astype(BF16)
    w_out = p["w_out_ab"][j].astype(BF16)
    gains = jnp.stack([jnp.tile(p["qn_g"][j] * (HD_A ** -0.5), H_A), jnp.tile(p["kn_g"][j], H_A),
                       jnp.ones((D_A,), F32)]).reshape(3, 1, D_A)
    wa, wx, ba, bx = p["lru_wa"][j], p["lru_wx"][j], p["lru_ba"][j], p["lru_bx"][j]
    w4 = (0.5 * jnp.concatenate([wa[0], wx[0], wa[1], wx[1]], axis=-1)).astype(BF16)
    b4 = jnp.concatenate([v.reshape(H_B, 1, BD_B) for v in (ba[0], bx[0], ba[1], bx[1])], axis=-1)
    lru = (p["conv_w"][j], p["conv_b"][j], w4, b4, p["lru_lam"][j])

    qkv, xg = _inproj_ab(xp, mp, g, w_in, gains, F32)
    qkv = qkv.reshape(bp, lp, 3 * D_A)
    oa = _ctx_attention(qkv)
    ob, h_fin = _rglru(xg.reshape(bp, lp, 2 * D_RNN), *lru, jnp.zeros((bp, 2, D_RNN), F32))
    xp = _outproj_ab(xp, mp, oa.reshape(1, bp * lp, D_A), ob.reshape(1, bp * lp, D_RNN), w_out)
    new = (qkv[:, :, D_A:2 * D_A].reshape(bp, lp, H_A, HD_A), qkv[:, :, 2 * D_A:].reshape(bp, lp, H_A, HD_A), h_fin)

    past = p["cache_k"].shape[2]
    ck = p["cache_k"][:, j].reshape(bs, past, D_A).astype(BF16)
    cv = p["cache_v"][:, j].reshape(bs, past, D_A).astype(BF16)
    qkv, xg = _inproj_ab(xs, ms, g, w_in, gains, BF16)
    oa = _neighbourhood_attention(qkv, ck, cv, p["rpb"][j])
    ob, _ = _rglru(xg, *lru, p["state_lru"][:, j])
    xs = _outproj_ab(xs, ms, oa, ob, w_out)
    return xp, xs, new


def _mixer_c(j, xp, xs, mp, ms, g, p):
    bp, lp, bs, ts = p["bp"], p["lp"], xs.shape[0], xs.shape[1]
    w = p["w_in_c"][j].astype(BF16)
    wkt = w[:, D_C:2 * D_C].T
    w_gate = jnp.pad(w[:, 4 * D_C:], ((0, 0), (0, LANES - 4 * H_C)))
    w_out = p["w_out_c"][j].astype(BF16)

    def gates_of(zg, b, t):
        g_rows = jnp.transpose(zg.reshape(b, t, LANES)[:, :, :4 * H_C], (0, 2, 1))
        return _gate_prep(g_rows, p["b_gate_c"][j])

    qv, og, kt, zg = _inproj_c(xp, mp, g, w, wkt, w_gate, None)
    hf, hb, cfin, nfin, mfin = _mlstm(
        qv.reshape(bp, lp, 2 * D_C), kt.reshape(bp, lp // CHUNK, D_C, CHUNK), gates_of(zg, bp, lp), None, None, None)
    xp = _outproj_c(xp, mp, hf.reshape(1, bp * lp, D_C), hb.reshape(1, bp * lp, D_C), og, p["mh_norm_g"][j], w_out)

    qv, og, kt, zg = _inproj_c(xs, ms, g, w, wkt, w_gate, _rope_tables(ts))
    hf, hb, _, _, _ = _mlstm(qv, kt, gates_of(zg, bs, ts), p["state_mlstm_c"][:, j], p["state_mlstm_n"][:, j],
                             p["state_mlstm_m"][:, j])
    xs = _outproj_c(xs, ms, hf, hb, og, p["mh_norm_g"][j], w_out)
    return xp, xs, (cfin, nfin, mfin)


def kernel(x_prompt, x_sample, cache_k, cache_v, state_lru, state_mlstm_c, state_mlstm_n, state_mlstm_m, c, c_ctx, w_mod, b_mod, norm_g, w_ffn_gate, w_ffn_up, w_ffn_down, w_in_ab, w_out_ab, qn_g, kn_g, rpb, conv_w, conv_b, lru_wa, lru_ba, lru_wx, lru_bx, lru_lam, w_in_c, b_gate_c, mh_norm_g, w_out_c):
    bp, lp, d = x_prompt.shape
    bs = x_sample.shape[0]
    p = dict(bp=bp, lp=lp, cache_k=cache_k, cache_v=cache_v, state_lru=state_lru, state_mlstm_c=state_mlstm_c,
             state_mlstm_n=state_mlstm_n, state_mlstm_m=state_mlstm_m, w_in_ab=w_in_ab, w_out_ab=w_out_ab,
             qn_g=qn_g, kn_g=kn_g, rpb=rpb, conv_w=conv_w, conv_b=conv_b, lru_wa=lru_wa, lru_ba=lru_ba,
             lru_wx=lru_wx, lru_bx=lru_bx, lru_lam=lru_lam, w_in_c=w_in_c, b_gate_c=b_gate_c,
             mh_norm_g=mh_norm_g, w_out_c=w_out_c)

    cond = jnp.concatenate([c, c_ctx[None, :]], axis=0)
    cond = jnp.pad(cond, ((0, -(bs + 1) % SUBLANES), (0, 0)))
    mod = _modulation(cond, w_mod, b_mod)[:, :bs + 1].reshape(DEPTH, bs + 1, N_MOD, d)

    wg = w_ffn_gate.astype(BF16)
    wu = w_ffn_up.astype(BF16)
    wd = w_ffn_down.astype(BF16)

    xp = x_prompt.reshape(1, bp * lp, d)
    xs = x_sample
    new_ab, new_c = [], []
    for l in range(DEPTH):
        j = l // 2
        ms = mod[l, :bs]
        mp = mod[l, bs:]
        xp = _ffn(xp, mp, norm_g[l, 0], wg, wu, wd, l, 0, 0)
        xs = _ffn(xs, ms, norm_g[l, 0], wg, wu, wd, l, 0, 0)
        if l % 2 == 0:
            xp, xs, new = _mixer_ab(j, xp, xs, mp, ms, norm_g[l, 1], p)
            new_ab.append(new)
        else:
            xp, xs, new = _mixer_c(j, xp, xs, mp, ms, norm_g[l, 1], p)
            new_c.append(new)
        xp = _ffn(xp, mp, norm_g[l, 2], wg, wu, wd, l, 1, 2)
        xs = _ffn(xs, ms, norm_g[l, 2], wg, wu, wd, l, 1, 2)
    stack = lambda items, i: jnp.concatenate([jnp.expand_dims(it[i], 1) for it in items], axis=1)
    return (xp.reshape(bp, lp, d), xs, stack(new_ab, 0), stack(new_ab, 1), stack(new_ab, 2),
            stack(new_c, 0), stack(new_c, 1), stack(new_c, 2))
```

```python
import functools

import numpy as np
import jax
import jax.numpy as jnp
from jax import lax
from jax.experimental import pallas as pl
from jax.experimental.pallas import tpu as pltpu

D_MODEL = 2048
DEPTH = 2
GRID_W = 64
N_MOD = 9
D_FF = 5632
EPS = 1e-6
H_A = 8
HD_A = 128
D_A = H_A * HD_A
WIN_R = 8
WIN_C = 16
D_RNN = 1024
H_B = 8
BD_B = D_RNN // H_B
CONV_W = 4
LRU_C = 8.0
H_C = 8
DK_C = 256
DV_C = 256
D_C = H_C * DV_C
CHUNK = 128
ROPE_BASE = 10000.0

BF16 = jnp.bfloat16
F32 = jnp.float32

V7X_VMEM_BYTES = 64 * 1024 * 1024
VMEM_LIMIT = V7X_VMEM_BYTES - 8 * 1024 * 1024
FFN_VMEM_LIMIT = V7X_VMEM_BYTES - 5 * 1024 * 1024
LANES = 128
SUBLANES = 8

TOKEN_TILE = 512
FF_TILE = 512
FFN_NORM_ROWS = 128
PROJ_TOKEN_TILE = 1024
PROJ_TILE_C = 1024
PROJ_TILE_AB = 1024
NA_ROWS = 4
NA_KEY_ROWS = NA_ROWS + WIN_R
LRU_SEGS = SUBLANES
NA_HEADS = 4
MLSTM_HEADS = 8


def _params(*sem, vmem=VMEM_LIMIT):
    return pltpu.CompilerParams(dimension_semantics=sem, vmem_limit_bytes=vmem)


def _dot(a, b):
    return jnp.dot(a, b, preferred_element_type=F32)


def _dot_nt(a, b):
    return lax.dot_general(a, b, (((1,), (1,)), ((), ())), preferred_element_type=F32)


def _dot_tn(a, b):
    return lax.dot_general(a, b, (((0,), (0,)), ((), ())), preferred_element_type=F32)


def _rms(x, g):
    return x * lax.rsqrt(jnp.mean(x * x, axis=-1, keepdims=True) + EPS) * g


def _adaln(x, g, mod_ref, j):
    return _rms(x, g) * (1.0 + mod_ref[0, 3 * j + 1:3 * j + 2, :]) + mod_ref[0, 3 * j:3 * j + 1, :]


def _softplus(x):
    return jnp.maximum(x, 0.0) + jnp.log1p(jnp.exp(-jnp.abs(x)))


def _log_sigmoid(x):
    return -_softplus(-x)


def _mod_kernel(c_ref, w_ref, b_ref, o_ref):
    c = c_ref[...]
    s = (c * jax.nn.sigmoid(c)).astype(BF16)
    o_ref[0] = _dot(s, w_ref[0].astype(BF16)) + b_ref[0]


def _modulation(cond, w_mod, b_mod):
    r = cond.shape[0]
    n = N_MOD * D_MODEL
    tn = 1024
    return pl.pallas_call(
        _mod_kernel,
        grid=(DEPTH, n // tn),
        in_specs=[
            pl.BlockSpec((r, D_MODEL), lambda l, j: (0, 0)),
            pl.BlockSpec((1, D_MODEL, tn), lambda l, j: (l, 0, j)),
            pl.BlockSpec((1, 1, tn), lambda l, j: (l, 0, j)),
        ],
        out_specs=pl.BlockSpec((1, r, tn), lambda l, j: (l, 0, j)),
        out_shape=jax.ShapeDtypeStruct((DEPTH, r, n), F32),
        compiler_params=_params("arbitrary", "arbitrary"),
        name="modulation",
    )(cond, w_mod, b_mod.reshape(DEPTH, 1, n))


def _ffn_kernel(x0_ref, xn_ref, mod_ref, modn_ref, g_ref, wga_ref, wua_ref, wda_ref, wgb_ref, wub_ref, wdb_ref,
                o_ref, ha_scr, hb_scr, xa_scr, xb_scr, acc_scr, *, j, ns, single_last):
    p, f2 = pl.program_id(0), pl.program_id(1)
    rows = xn_ref.shape[1]
    tm = ha_scr.shape[0]

    @pl.when((p == 0) & (f2 == 0))
    def _():
        xa_scr[...] = x0_ref[0]
        ha_scr[...] = _adaln(x0_ref[0], g_ref[...], mod_ref, j).astype(BF16)

    @pl.when((f2 == 0) | (f2 == ns))
    def _():
        acc_scr[...] = jnp.zeros_like(acc_scr)

    def body(h_cur, h_nxt, x_nxt, f, both):
        r0 = pl.multiple_of(jnp.minimum(f, tm // rows - 1) * rows, rows)
        xs = xn_ref[0]
        x_nxt[pl.ds(r0, rows), :] = xs
        h_nxt[pl.ds(r0, rows), :] = _adaln(xs, g_ref[...], modn_ref, j).astype(BF16)
        h = h_cur[...]
        hidden = ((wga_ref, wua_ref, wda_ref), (wgb_ref, wub_ref, wdb_ref)) if both else ((wga_ref, wua_ref, wda_ref),)
        y = None
        for wg_ref, wu_ref, wd_ref in hidden:
            a = _dot(h, wg_ref[...])
            u = _dot(h, wu_ref[...])
            act = (a * jax.nn.sigmoid(a)) * u
            yd = _dot(act.astype(BF16), wd_ref[...])
            y = yd if y is None else y + yd
        acc_scr[...] += y

    n_both = ns - 1 if single_last else ns
    for half, (h_cur, h_nxt, x_nxt) in enumerate(((ha_scr, hb_scr, xb_scr), (hb_scr, ha_scr, xa_scr))):
        @pl.when((f2 >= half * ns) & (f2 < half * ns + n_both))
        def _():
            body(h_cur, h_nxt, x_nxt, f2 - half * ns, True)

        if single_last:
            @pl.when(f2 == half * ns + ns - 1)
            def _():
                body(h_cur, h_nxt, x_nxt, ns - 1, False)

    for last, x_cur in ((ns - 1, xa_scr), (2 * ns - 1, xb_scr)):
        @pl.when(f2 == last)
        def _():
            o_ref[0] = x_cur[...] + 0.5 * mod_ref[0, 3 * j + 2:3 * j + 3, :] * acc_scr[...]


def _ffn(x, mod, g, wg, wu, wd, l, s, j):
    grp, t, d = x.shape
    tm, tf = TOKEN_TILE, FF_TILE
    nf = D_FF // tf
    ns = (nf + 1) // 2
    per_group = t // tm
    ntile = grp * per_group
    rows = FFN_NORM_ROWS
    nsl = tm // rows
    assert ntile % 2 == 0 and ns >= nsl
    tile = lambda p, f2: 2 * p + f2 // ns
    nxt = lambda p, f2: jnp.minimum(tile(p, f2) + 1, ntile - 1)
    fa = lambda f2: jnp.minimum(2 * (f2 % ns), nf - 1)
    fb = lambda f2: jnp.minimum(2 * (f2 % ns) + 1, nf - 1)
    wcol = lambda fsel: pl.BlockSpec((None, None, d, tf), lambda p, f2: (l, s, 0, fsel(f2)))
    wrow = lambda fsel: pl.BlockSpec((None, None, tf, d), lambda p, f2: (l, s, fsel(f2), 0))
    out = pl.pallas_call(
        functools.partial(_ffn_kernel, j=j, ns=ns, single_last=nf % 2 == 1),
        grid=(ntile // 2, 2 * ns),
        in_specs=[
            pl.BlockSpec((1, tm, d), lambda p, f2: (0, 0, 0), pipeline_mode=pl.Buffered(1)),
            pl.BlockSpec((1, rows, d), lambda p, f2: (0, nxt(p, f2) * nsl + jnp.minimum(f2 % ns, nsl - 1), 0)),
            pl.BlockSpec((1, N_MOD, d), lambda p, f2: (tile(p, f2) // per_group, 0, 0)),
            pl.BlockSpec((1, N_MOD, d), lambda p, f2: (nxt(p, f2) // per_group, 0, 0)),
            pl.BlockSpec((1, d), lambda p, f2: (0, 0)),
            wcol(fa), wcol(fa), wrow(fa), wcol(fb), wcol(fb), wrow(fb),
        ],
        out_specs=pl.BlockSpec((1, tm, d), lambda p, f2: (0, tile(p, f2), 0)),
        out_shape=jax.ShapeDtypeStruct((1, grp * t, d), F32),
        scratch_shapes=[pltpu.VMEM((tm, d), BF16), pltpu.VMEM((tm, d), BF16),
                        pltpu.VMEM((tm, d), F32), pltpu.VMEM((tm, d), F32), pltpu.VMEM((tm, d), F32)],
        compiler_params=_params("arbitrary", "arbitrary", vmem=FFN_VMEM_LIMIT),
        name="macaron_ffn",
    )(x.reshape(1, grp * t, d), x.reshape(1, grp * t, d), mod, mod, g.reshape(1, d), wg, wu, wd, wg, wu, wd)
    return out.reshape(x.shape)


def _rope(x, cos, sin):
    half = DK_C // 2
    out = []
    for p in range(2):
        sl = slice(p * half, (p + 1) * half)
        xs = x[:, sl]
        out.append(xs * cos[:, sl] + pltpu.roll(xs, half // 2, 1) * sin[:, sl])
    return jnp.concatenate(out, axis=-1)


def _rope_t(x, cos, sin):
    q = DK_C // 4
    swapped = jnp.concatenate([x[q:2 * q], x[0:q], x[3 * q:4 * q], x[2 * q:3 * q]], axis=0)
    return x * cos + swapped * sin


def _inproj_ab_kernel(x_ref, mod_ref, g_ref, w_ref, gain_ref, oa_ref, ob_ref, h_scr, *, n_a, n_norm):
    k = pl.program_id(2)

    @pl.when(k == 0)
    def _():
        h_scr[...] = _adaln(x_ref[0], g_ref[...], mod_ref, 1).astype(BF16)

    @pl.when(k >= n_a)
    def _():
        ob_ref[0] = _dot(h_scr[...], w_ref[...])

    @pl.when(k < n_norm)
    def _():
        for pair in range(w_ref.shape[1] // (2 * HD_A)):
            z2 = _dot(h_scr[...], w_ref[:, 2 * pair * HD_A:2 * (pair + 1) * HD_A])
            for hh in range(2):
                zs = z2[:, hh * HD_A:(hh + 1) * HD_A]
                sl = slice((2 * pair + hh) * HD_A, (2 * pair + hh + 1) * HD_A)
                r = lax.rsqrt(jnp.mean(zs * zs, axis=-1, keepdims=True) + EPS)
                oa_ref[0, :, sl] = (zs * r * gain_ref[0, :, sl]).astype(oa_ref.dtype)

    @pl.when((k >= n_norm) & (k < n_a))
    def _():
        oa_ref[0] = (_dot(h_scr[...], w_ref[...]) * gain_ref[0]).astype(oa_ref.dtype)


def _inproj_ab(x, mod, g, w, gains, dtype_a):
    grp, t, d = x.shape
    tm, tn = PROJ_TOKEN_TILE, PROJ_TILE_AB
    n_a = 3 * D_A // tn
    nt = w.shape[1] // tn
    return pl.pallas_call(
        functools.partial(_inproj_ab_kernel, n_a=n_a, n_norm=2 * D_A // tn),
        grid=(grp, t // tm, nt),
        in_specs=[
            pl.BlockSpec((1, tm, d), lambda b, i, k: (b, i, 0)),
            pl.BlockSpec((1, N_MOD, d), lambda b, i, k: (b, 0, 0)),
            pl.BlockSpec((1, d), lambda b, i, k: (0, 0)),
            pl.BlockSpec((d, tn), lambda b, i, k: (0, k)),
            pl.BlockSpec((1, 1, tn), lambda b, i, k: (jnp.minimum(k, n_a - 1), 0, 0)),
        ],
        out_specs=[pl.BlockSpec((1, tm, tn), lambda b, i, k: (b, i, jnp.minimum(k, n_a - 1))),
                   pl.BlockSpec((1, tm, tn), lambda b, i, k: (b, i, jnp.maximum(k - n_a, 0)))],
        out_shape=[jax.ShapeDtypeStruct((grp, t, n_a * tn), dtype_a),
                   jax.ShapeDtypeStruct((grp, t, (nt - n_a) * tn), F32)],
        scratch_shapes=[pltpu.VMEM((tm, d), BF16)],
        compiler_params=_params("arbitrary", "arbitrary", "arbitrary"),
        name="mixer_ab_in_proj",
    )(x, mod, g.reshape(1, d), w, gains.reshape(n_a, 1, tn))


def _inproj_c_kernel(*refs, rotary, n_q, n_qv):
    x_ref, mod_ref, g_ref, w_ref, wgate_ref = refs[:5]
    pos = 5
    if rotary:
        cos_ref, sin_ref = refs[pos:pos + 2]
        pos += 2
    oa_ref, ob_ref, og_ref, h_scr = refs[pos:pos + 4]
    k = pl.program_id(2)
    tn = w_ref.shape[1]

    @pl.when(k == 0)
    def _():
        h = _adaln(x_ref[0], g_ref[...], mod_ref, 1).astype(BF16)
        h_scr[...] = h
        og_ref[0] = _dot(h, wgate_ref[...])

    @pl.when(k < n_q)
    def _():
        z = _dot(h_scr[...], w_ref[...])
        if rotary:
            for hh in range(tn // DK_C):
                sl = slice(hh * DK_C, (hh + 1) * DK_C)
                oa_ref[0, :, sl] = _rope(z[:, sl], cos_ref[...], sin_ref[...]).astype(BF16)
        else:
            oa_ref[0] = z.astype(BF16)

    @pl.when((k >= n_q) & (k < n_qv))
    def _():
        oa_ref[0] = _dot(h_scr[...], w_ref[...]).astype(BF16)

    @pl.when(k >= n_qv)
    def _():
        ob_ref[0] = _dot(h_scr[...], w_ref[...])


def _inproj_kt_kernel(*refs, rotary):
    x_ref, mod_ref, g_ref, wkt_ref = refs[:4]
    pos = 4
    if rotary:
        cost_ref, sint_ref = refs[pos:pos + 2]
        pos += 2
    okt_ref, h_scr = refs[pos:pos + 2]
    tn = wkt_ref.shape[0]

    @pl.when(pl.program_id(2) == 0)
    def _():
        h_scr[...] = _adaln(x_ref[0], g_ref[...], mod_ref, 1).astype(BF16)

    zt = _dot_nt(wkt_ref[...], h_scr[...])
    for ci in range(zt.shape[1] // CHUNK):
        cols = slice(ci * CHUNK, (ci + 1) * CHUNK)
        if rotary:
            for hh in range(tn // DK_C):
                sl = slice(hh * DK_C, (hh + 1) * DK_C)
                okt_ref[0, ci, sl, :] = _rope_t(zt[sl, cols], cost_ref[:, cols], sint_ref[:, cols]).astype(BF16)
        else:
            okt_ref[0, ci] = zt[:, cols].astype(BF16)


def _inproj_c(x, mod, g, w, wkt, w_gate, rope_tables):
    grp, t, d = x.shape
    tm, tn = PROJ_TOKEN_TILE, PROJ_TILE_C
    n_q, n_qv, n_main = D_C // tn, 2 * D_C // tn, 3 * D_C // tn

    def w_col(k):
        return jnp.where(k < n_q, k, k + n_q)

    rotary = rope_tables is not None
    common = [
        pl.BlockSpec((1, tm, d), lambda b, i, k: (b, i, 0)),
        pl.BlockSpec((1, N_MOD, d), lambda b, i, k: (b, 0, 0)),
        pl.BlockSpec((1, d), lambda b, i, k: (0, 0)),
    ]
    in_specs = common + [pl.BlockSpec((d, tn), lambda b, i, k: (0, w_col(k))),
                         pl.BlockSpec((d, LANES), lambda b, i, k: (0, 0))]
    args = [x, mod, g.reshape(1, d), w, w_gate]
    kt_specs = common + [pl.BlockSpec((tn, d), lambda b, i, k: (k, 0))]
    kt_args = [x, mod, g.reshape(1, d), wkt]
    if rotary:
        cos, sin = rope_tables
        in_specs += [pl.BlockSpec((tm, DK_C), lambda b, i, k: (i, 0))] * 2
        args += [cos, sin]
        kt_specs += [pl.BlockSpec((DK_C, tm), lambda b, i, k: (0, i))] * 2
        kt_args += [cos.T, sin.T]
    qv, og, zg = pl.pallas_call(
        functools.partial(_inproj_c_kernel, rotary=rotary, n_q=n_q, n_qv=n_qv),
        grid=(grp, t // tm, n_main),
        in_specs=in_specs,
        out_specs=[pl.BlockSpec((1, tm, tn), lambda b, i, k: (b, i, jnp.minimum(k, n_qv - 1))),
                   pl.BlockSpec((1, tm, tn), lambda b, i, k: (b, i, jnp.maximum(k - n_qv, 0))),
                   pl.BlockSpec((1, tm, LANES), lambda b, i, k: (b, i, 0))],
        out_shape=[jax.ShapeDtypeStruct((grp, t, 2 * D_C), BF16),
                   jax.ShapeDtypeStruct((grp, t, D_C), F32),
                   jax.ShapeDtypeStruct((grp, t, LANES), F32)],
        scratch_shapes=[pltpu.VMEM((tm, d), BF16)],
        compiler_params=_params("arbitrary", "arbitrary", "arbitrary"),
        name="mixer_c_in_proj",
    )(*args)
    kt = pl.pallas_call(
        functools.partial(_inproj_kt_kernel, rotary=rotary),
        grid=(grp, t // tm, D_C // tn),
        in_specs=kt_specs,
        out_specs=pl.BlockSpec((1, tm // CHUNK, tn, CHUNK), lambda b, i, k: (b, i, k, 0)),
        out_shape=jax.ShapeDtypeStruct((grp, t // CHUNK, D_C, CHUNK), BF16),
        scratch_shapes=[pltpu.VMEM((tm, d), BF16)],
        compiler_params=_params("arbitrary", "arbitrary", "arbitrary"),
        name="mixer_c_in_proj_kt",
    )(*kt_args)
    return qv, og, kt, zg


def _ctx_attn_kernel(q_ref, k_ref, v_ref, o_ref):
    for h in range(H_A):
        sl = slice(h * HD_A, (h + 1) * HD_A)
        s = _dot_nt(q_ref[0, :, sl].astype(BF16), k_ref[0, :, sl].astype(BF16))
        p = jnp.exp(s - jnp.max(s, axis=-1, keepdims=True))
        o = _dot(p.astype(BF16), v_ref[0, :, sl].astype(BF16)) / jnp.sum(p, axis=-1, keepdims=True)
        o_ref[0, :, sl] = o.astype(BF16)


def _ctx_attention(qkv):
    b, l, _ = qkv.shape
    blk = lambda off: pl.BlockSpec((1, l, D_A), lambda i: (i, 0, off))
    return pl.pallas_call(
        _ctx_attn_kernel,
        grid=(b,),
        in_specs=[blk(0), blk(1), blk(2)],
        out_specs=pl.BlockSpec((1, l, D_A), lambda i: (i, 0, 0)),
        out_shape=jax.ShapeDtypeStruct((b, l, D_A), BF16),
        compiler_params=_params("arbitrary"),
        name="ctx_attention",
    )(qkv, qkv, qkv)


def _na_bias_tables(rpb, rows):
    nblk = rows // NA_ROWS
    n_dr, n_dc = 2 * WIN_R - 1, 2 * WIN_C - 1
    c = np.arange(GRID_W)[:, None]
    kc = np.arange(GRID_W)[None, :]
    dc = np.clip(kc - c + WIN_C - 1, 0, n_dc - 1)
    pick_c = (dc[None] == np.arange(n_dc)[:, None, None]).astype(np.float32)
    c0 = np.clip(c - WIN_C // 2, 0, GRID_W - WIN_C)
    ok_c = (kc >= c0) & (kc < c0 + WIN_C)
    by_col = jnp.einsum("hrd,dck->hrck", rpb, pick_c, precision=lax.Precision.HIGHEST)
    tables = []
    for i in (0, 1, nblk - 1):
        ks = int(np.clip(NA_ROWS * i - WIN_R // 2, 0, rows - NA_KEY_ROWS))
        r = (NA_ROWS * i + np.arange(NA_ROWS))[:, None]
        kr = (ks + np.arange(NA_KEY_ROWS))[None, :]
        r0 = np.clip(r - WIN_R // 2, 0, rows - WIN_R)
        ok_r = (kr >= r0) & (kr < r0 + WIN_R)
        dr = np.clip(kr - r + WIN_R - 1, 0, n_dr - 1)
        pick_r = (dr[None] == np.arange(n_dr)[:, None, None]).astype(np.float32)
        bias = jnp.einsum("hrck,rqs->hqcsk", by_col, pick_r, precision=lax.Precision.HIGHEST)
        ok = ok_r[None, :, None, :, None] & ok_c[None, None, :, None, :]
        bias = jnp.where(ok, bias, -jnp.inf)
        tables.append(bias.reshape(H_A, NA_ROWS * GRID_W, NA_KEY_ROWS * GRID_W))
    return jnp.stack(tables, axis=1)


def _na_kernel(q_ref, k_ref, v_ref, ck_ref, cv_ref, bias_ref, o_ref, *, rows):
    i = pl.program_id(2)
    nblk = rows // NA_ROWS
    ks = jnp.clip(NA_ROWS * i - WIN_R // 2, 0, rows - NA_KEY_ROWS)
    start = pl.multiple_of(ks * GRID_W, GRID_W)
    nkey = NA_KEY_ROWS * GRID_W
    kind = jnp.where(i == 0, 0, jnp.where(i == nblk - 1, 2, 1))
    ones_w = jnp.ones((nkey, HD_A), BF16)
    ones_c = jnp.ones((ck_ref.shape[1], HD_A), BF16)
    for hh in range(NA_HEADS):
        sl = slice(hh * HD_A, (hh + 1) * HD_A)
        q = q_ref[0, :, sl]
        kw = k_ref[0, pl.ds(start, nkey), sl]
        vw = jnp.concatenate([v_ref[0, pl.ds(start, nkey), sl], ones_w], axis=1)
        vc = jnp.concatenate([cv_ref[0, :, sl], ones_c], axis=1)
        s_w = _dot_nt(q, kw) + bias_ref[hh, kind]
        s_c = _dot_nt(q, ck_ref[0, :, sl])
        m = jnp.maximum(jnp.max(s_w, axis=-1, keepdims=True), jnp.max(s_c, axis=-1, keepdims=True))
        p_w = jnp.exp(s_w - m).astype(BF16)
        p_c = jnp.exp(s_c - m).astype(BF16)
        o = _dot(p_w, vw) + _dot(p_c, vc)
        o_ref[0, :, sl] = (o[:, :HD_A] / o[:, HD_A:]).astype(BF16)


def _neighbourhood_attention(qkv, ck, cv, rpb):
    b, t, _ = qkv.shape
    l = ck.shape[1]
    rows = t // GRID_W
    nblk = rows // NA_ROWS
    tq = NA_ROWS * GRID_W
    bias = _na_bias_tables(rpb, rows)
    hw = NA_HEADS * HD_A
    ngrp = H_A // NA_HEADS
    full = lambda off: pl.BlockSpec((1, t, hw), lambda h, n, i: (n, 0, off + h))
    ctx = pl.BlockSpec((1, l, hw), lambda h, n, i: (n, 0, h))
    return pl.pallas_call(
        functools.partial(_na_kernel, rows=rows),
        grid=(ngrp, b, nblk),
        in_specs=[
            pl.BlockSpec((1, tq, hw), lambda h, n, i: (n, i, h)),
            full(ngrp), full(2 * ngrp), ctx, ctx,
            pl.BlockSpec((NA_HEADS, 3, tq, NA_KEY_ROWS * GRID_W), lambda h, n, i: (h, 0, 0, 0)),
        ],
        out_specs=pl.BlockSpec((1, tq, hw), lambda h, n, i: (n, i, h)),
        out_shape=jax.ShapeDtypeStruct((b, t, D_A), BF16),
        compiler_params=_params("arbitrary", "arbitrary", "arbitrary"),
        name="neighbourhood_attention",
    )(qkv, qkv, qkv, ck, cv, bias)


def _lru_kernel(xb_ref, gb_ref, cw_ref, cb_ref, w4h_ref, b4_ref, lam_ref, h0_ref, o_ref, fin_ref,
                xpad_scr, af_scr, uf_scr, ab_scr, ub_scr, *, t, chunk):
    seg = t // LRU_SEGS
    pitch = seg + SUBLANES
    pad = SUBLANES
    zeros = jnp.zeros((pad, BD_B), F32)
    xpad_scr[0:pad, :] = zeros
    xpad_scr[pad + t:2 * pad + t, :] = zeros
    xpad_scr[pad:pad + t, :] = xb_ref[0]

    c1 = (-0.5 * LRU_C) * _softplus(-lam_ref[...])
    w4h = w4h_ref[0]
    b4h = 0.5 * b4_ref[0]
    left = (CONV_W - 1) // 2
    for c in range(t // chunk):
        base = c * chunk
        xc = cb_ref[...]
        for k in range(CONV_W):
            xc = xc + xpad_scr[pl.ds(base + pad - left + k, chunk), :] * cw_ref[k:k + 1, :]
        th = jnp.tanh(_dot(xc.astype(BF16), w4h) + b4h)
        xh = 0.5 * xc
        for d, (a_scr, u_scr) in enumerate(((af_scr, uf_scr), (ab_scr, ub_scr))):
            tr = th[:, 2 * d * BD_B:(2 * d + 1) * BD_B]
            ti = th[:, (2 * d + 1) * BD_B:(2 * d + 2) * BD_B]
            log_a = c1[d:d + 1, :] * tr + c1[d:d + 1, :]
            a = jnp.exp(log_a)
            u = jnp.sqrt(-jnp.tanh(log_a) * (a * a + 1.0)) * (xh * ti + xh)
            for p in range(chunk // seg):
                s = (base + p * seg) // seg
                a_scr[s * pitch:s * pitch + seg, :] = a[p * seg:(p + 1) * seg, :]
                u_scr[s * pitch:s * pitch + seg, :] = u[p * seg:(p + 1) * seg, :]

    def scan(j, carry):
        hf, pf, hb, pb = carry
        rf = pl.ds(j, LRU_SEGS, stride=pitch)
        rb = pl.ds(seg - 1 - j, LRU_SEGS, stride=pitch)
        a = af_scr[rf, :]
        hf = a * hf + uf_scr[rf, :]
        pf = a * pf
        uf_scr[rf, :] = hf
        af_scr[rf, :] = pf
        a = ab_scr[rb, :]
        hb = a * hb + ub_scr[rb, :]
        pb = a * pb
        ub_scr[rb, :] = hb
        ab_scr[rb, :] = pb
        return hf, pf, hb, pb

    zero = jnp.zeros((LRU_SEGS, BD_B), F32)
    one = jnp.ones((LRU_SEGS, BD_B), F32)
    hf, pf, hb, pb = lax.fori_loop(0, seg, scan, (zero, one, zero, one), unroll=4)

    cf = h0_ref[0, 0:1, :]
    cin_f = []
    for s in range(LRU_SEGS):
        cin_f.append(cf)
        cf = hf[s:s + 1, :] + pf[s:s + 1, :] * cf
    cb = h0_ref[0, 1:2, :]
    cin_b = [None] * LRU_SEGS
    for s in reversed(range(LRU_SEGS)):
        cin_b[s] = cb
        cb = hb[s:s + 1, :] + pb[s:s + 1, :] * cb
    fin_ref[0, 0:1, :] = cf
    fin_ref[0, 1:2, :] = cb

    for s in range(LRU_SEGS):
        rows = slice(s * pitch, s * pitch + seg)
        hs = (uf_scr[rows, :] + af_scr[rows, :] * cin_f[s]) + (ub_scr[rows, :] + ab_scr[rows, :] * cin_b[s])
        nat = slice(s * seg, (s + 1) * seg)
        o_ref[0, nat, :] = (hs * jax.nn.gelu(gb_ref[0, nat, :])).astype(BF16)


def _rglru(xg, conv_w, conv_b, w4h, b4, lam, h0):
    b, t, _ = xg.shape
    chunk = min(t, 512)
    seg = t // LRU_SEGS
    col = lambda off: pl.BlockSpec((1, t, BD_B), lambda n, h: (n, 0, off + h))
    return pl.pallas_call(
        functools.partial(_lru_kernel, t=t, chunk=chunk),
        grid=(b, H_B),
        in_specs=[
            col(0), col(H_B),
            pl.BlockSpec((CONV_W, BD_B), lambda n, h: (0, h)),
            pl.BlockSpec((1, BD_B), lambda n, h: (0, h)),
            pl.BlockSpec((1, BD_B, 4 * BD_B), lambda n, h: (h, 0, 0)),
            pl.BlockSpec((1, 1, 4 * BD_B), lambda n, h: (h, 0, 0)),
            pl.BlockSpec((2, BD_B), lambda n, h: (0, h)),
            pl.BlockSpec((1, 2, BD_B), lambda n, h: (n, 0, h)),
        ],
        out_specs=[pl.BlockSpec((1, t, BD_B), lambda n, h: (n, 0, h)),
                   pl.BlockSpec((1, 2, BD_B), lambda n, h: (n, 0, h))],
        out_shape=[jax.ShapeDtypeStruct((b, t, D_RNN), BF16),
                   jax.ShapeDtypeStruct((b, 2, D_RNN), F32)],
        scratch_shapes=[pltpu.VMEM((t + 2 * SUBLANES, BD_B), F32)]
        + [pltpu.VMEM((LRU_SEGS * (seg + SUBLANES), BD_B), F32)] * 4,
        compiler_params=_params("arbitrary", "arbitrary"),
        name="rglru_branch",
    )(xg, xg, conv_w, conv_b.reshape(1, D_RNN), w4h, b4, lam, h0)


def _outproj_ab_kernel(x_ref, mod_ref, oa_ref, ob_ref, w_ref, o_ref):
    y = _dot(oa_ref[0], w_ref[0:D_A, :]) + _dot(ob_ref[0], w_ref[D_A:D_A + D_RNN, :])
    o_ref[0] = x_ref[0] + mod_ref[0, 5:6, :] * y


def _outproj_ab(x, mod, oa, ob, w):
    grp, t, d = x.shape
    tm = TOKEN_TILE
    tok = lambda width: pl.BlockSpec((1, tm, width), lambda b, i: (b, i, 0))
    return pl.pallas_call(
        _outproj_ab_kernel,
        grid=(grp, t // tm),
        in_specs=[
            tok(d),
            pl.BlockSpec((1, N_MOD, d), lambda b, i: (b, 0, 0)),
            tok(D_A), tok(D_RNN),
            pl.BlockSpec((D_A + D_RNN, d), lambda b, i: (0, 0)),
        ],
        out_specs=tok(d),
        out_shape=jax.ShapeDtypeStruct(x.shape, F32),
        compiler_params=_params("arbitrary", "arbitrary"),
        name="mixer_ab_out_proj",
    )(x, mod, oa, ob, w)


def _gate_prep_kernel(g_ref, b_ref, o_ref):
    t = g_ref.shape[2]
    g = g_ref[0] + b_ref[...]
    lane = lax.broadcasted_iota(jnp.int32, (H_C, t), 1) & (CHUNK - 1)
    cum = _log_sigmoid(g[H_C:2 * H_C])
    suf = _log_sigmoid(g[3 * H_C:4 * H_C])
    sh = 1
    while sh < CHUNK:
        cum = cum + jnp.where(lane >= sh, pltpu.roll(cum, sh, 1), 0.0)
        suf = suf + jnp.where(lane < CHUNK - sh, pltpu.roll(suf, t - sh, 1), 0.0)
        sh *= 2
    o_ref[0, 0:H_C] = g[0:H_C]
    o_ref[0, H_C:2 * H_C] = cum
    o_ref[0, 2 * H_C:3 * H_C] = g[2 * H_C:3 * H_C]
    o_ref[0, 3 * H_C:4 * H_C] = suf


def _gate_prep(g_rows, b_gate):
    b, r, t = g_rows.shape
    return pl.pallas_call(
        _gate_prep_kernel,
        grid=(b,),
        in_specs=[pl.BlockSpec((1, r, t), lambda n: (n, 0, 0)), pl.BlockSpec((r, 1), lambda n: (0, 0))],
        out_specs=pl.BlockSpec((1, r, t), lambda n: (n, 0, 0)),
        out_shape=jax.ShapeDtypeStruct(g_rows.shape, F32),
        compiler_params=_params("arbitrary"),
        name="mlstm_gate_prep",
    )(g_rows, b_gate.reshape(r, 1))


def _mlstm_chunk(q, kt, v_aug, ig_row, bc_row, bc_col, cm, m, reverse):
    ti = lax.broadcasted_iota(jnp.int32, (CHUNK, CHUNK), 0)
    si = lax.broadcasted_iota(jnp.int32, (CHUNK, CHUNK), 1)
    seen = (si >= ti) if reverse else (si <= ti)
    bk = bc_row[:, 0:1] if reverse else bc_row[:, CHUNK - 1:CHUNK]
    bc_rep = jnp.broadcast_to(bc_col, (CHUNK, CHUNK))
    dmat = jnp.where(seen, bc_rep - bc_row + ig_row, -jnp.inf)
    inter = bc_rep + m
    m_t = jnp.maximum(inter, jnp.broadcast_to(jnp.max(dmat, axis=1, keepdims=True), (CHUNK, CHUNK)))
    w = jnp.exp(dmat - m_t)
    g = jnp.exp(inter - m_t)
    kscale = DK_C ** -0.5
    s = _dot(q, kt) * (w * kscale)
    tot = _dot(s.astype(BF16), v_aug) + jnp.concatenate([g, g, g], axis=1) * _dot(q, cm.astype(BF16))
    inv = 1.0 / jnp.maximum(jnp.abs(tot[:, DV_C:]), jnp.exp(-m_t))
    hout = tot[:, :DV_C] * jnp.concatenate([inv, inv], axis=1)
    dend = bk - bc_row + ig_row
    m_new = jnp.maximum(bk + m, jnp.max(dend, axis=1, keepdims=True))
    we = jnp.exp(dend - m_new)
    ge = jnp.exp(bk + m - m_new)
    kwt = kt.astype(F32) * (we * kscale)
    c_new = ge * cm + _dot(kwt.astype(BF16), v_aug)
    return hout, c_new, m_new


def _mlstm_kernel(*refs, zero_state):
    if zero_state:
        qvf_ref, vvf_ref, ktf_ref, qvb_ref, vvb_ref, ktb_ref, gr_ref, gc_ref = refs[:8]
        rest = refs[8:]
    else:
        m0_ref, qvf_ref, vvf_ref, ktf_ref, qvb_ref, vvb_ref, ktb_ref, gr_ref, gc_ref = refs[:9]
        c0_ref, n0_ref = refs[9:11]
        rest = refs[11:]
    hf_ref, hb_ref, cfin_ref, nfin_ref, mfin_ref, c_scr, m_scr = rest
    n, hp, c = pl.program_id(0), pl.program_id(1), pl.program_id(2)
    eye = (lax.broadcasted_iota(jnp.int32, (DK_C, DK_C), 0) == lax.broadcasted_iota(jnp.int32, (DK_C, DK_C), 1))

    @pl.when(c == 0)
    def _():
        if zero_state:
            c_scr[...] = jnp.zeros_like(c_scr)
            m_scr[...] = jnp.zeros_like(m_scr)
            return
        for d in range(2):
            for j in range(MLSTM_HEADS):
                c_scr[d, j, :, 0:DV_C] = c0_ref[0, d, j]
                n_col = jnp.sum(jnp.where(eye, n0_ref[0, d, j], 0.0), axis=1, keepdims=True)
                c_scr[d, j, :, DV_C:DV_C + LANES] = jnp.broadcast_to(n_col, (DK_C, LANES))
                m_scr[d, j] = jnp.full(m_scr.shape[2:], m0_ref[n, d * H_C + hp * MLSTM_HEADS + j], F32)

    ones_col = jnp.ones((CHUNK, LANES), BF16)
    dirs = ((qvf_ref, vvf_ref, ktf_ref, hf_ref), (qvb_ref, vvb_ref, ktb_ref, hb_ref))
    nc = pl.num_programs(2)
    chains = []
    for d, (q_ref, v_ref, kt_ref, h_ref) in enumerate(dirs):
        cc = c if d == 0 else nc - 1 - c
        rows = pl.ds(pl.multiple_of(cc * CHUNK, CHUNK), CHUNK)
        for j in range(MLSTM_HEADS):
            sl = slice(j * DK_C, (j + 1) * DK_C)
            v_aug = jnp.concatenate([v_ref[0, :, sl], ones_col], axis=1)
            col = (2 * d + 1) * MLSTM_HEADS + j
            chains.append((d, j, h_ref, sl, _mlstm_chunk(
                q_ref[0, :, sl], kt_ref[0, 0, sl, :], v_aug,
                gr_ref[0, j, 2 * d, pl.ds(cc, 1), :], gr_ref[0, j, 2 * d + 1, pl.ds(cc, 1), :],
                gc_ref[0, 0, rows, col:col + 1],
                c_scr[d, j], m_scr[d, j, 0:1, 0:1], reverse=(d == 1))))
    for d, j, h_ref, sl, (hout, c_new, m_new) in chains:
        h_ref[0, :, sl] = hout
        c_scr[d, j] = c_new
        m_scr[d, j] = jnp.broadcast_to(m_new, m_scr.shape[2:])

    @pl.when(c == pl.num_programs(2) - 1)
    def _():
        for d in range(2):
            for j in range(MLSTM_HEADS):
                cfin_ref[0, d, j] = c_scr[d, j, :, 0:DV_C]
                n_col = c_scr[d, j, :, DV_C:DV_C + 1]
                nfin_ref[0, d, j] = jnp.sum(jnp.where(eye, n_col, 0.0), axis=0, keepdims=True)
                mfin_ref[0, d, j] = m_scr[d, j, 0:1, :]


def _rope_tables(t):
    half = DK_C // 2
    nf = half // 2
    pos = jnp.arange(t)
    freqs = ROPE_BASE ** (-jnp.arange(nf, dtype=F32) / nf)
    cos, sin = [], []
    for p in (pos // GRID_W, pos % GRID_W):
        ang = p.astype(F32)[:, None] * freqs[None, :]
        cos += [jnp.cos(ang), jnp.cos(ang)]
        sin += [-jnp.sin(ang), jnp.sin(ang)]
    return jnp.concatenate(cos, axis=-1), jnp.concatenate(sin, axis=-1)


def _mlstm(qv, kt, gates, c0, n0, m0):
    b, t, _ = qv.shape
    nc = t // CHUNK
    hps = MLSTM_HEADS
    ngrp = H_C // hps
    g4 = gates.reshape(b, 4, H_C, t)
    g_row = jnp.transpose(g4, (0, 2, 1, 3)).reshape(b, H_C, 4, nc, CHUNK)
    g_col = jnp.transpose(gates.reshape(b, 4, ngrp, hps, t), (0, 2, 4, 1, 3)).reshape(b, ngrp, t, 4 * hps)
    fwd = lambda c: c
    bwd = lambda c: nc - 1 - c

    def tok(off, order):
        return pl.BlockSpec((1, CHUNK, hps * DK_C), lambda n, h, c: (n, order(c), off + h))

    def kt_spec(order):
        return pl.BlockSpec((1, 1, hps * DK_C, CHUNK), lambda n, h, c: (n, order(c), h, 0))

    g_row_spec = pl.BlockSpec((1, hps, 4, nc, CHUNK), lambda n, h, c: (n, h, 0, 0, 0))
    g_col_spec = pl.BlockSpec((1, 1, t, 4 * hps), lambda n, h, c: (n, h, 0, 0))

    state = lambda *tail: pl.BlockSpec((1, 2, hps) + tail, lambda n, h, c: (n, 0, h) + (0,) * len(tail))
    hspec = lambda order: pl.BlockSpec((1, CHUNK, hps * DV_C), lambda n, h, c: (n, order(c), h))
    in_specs = [tok(0, fwd), tok(ngrp, fwd), kt_spec(fwd),
                tok(0, bwd), tok(ngrp, bwd), kt_spec(bwd),
                g_row_spec, g_col_spec]
    args = [qv, qv, kt, qv, qv, kt, g_row, g_col]
    zero_state = c0 is None
    if not zero_state:
        in_specs = [pl.BlockSpec(memory_space=pltpu.SMEM)] + in_specs + [state(DK_C, DV_C), state(1, DK_C)]
        args = [m0.reshape(b, 2 * H_C)] + args + [c0, n0.reshape(b, 2, H_C, 1, DK_C)]
    hf, hb, cfin, nfin, mfin = pl.pallas_call(
        functools.partial(_mlstm_kernel, zero_state=zero_state),
        grid=(b, ngrp, nc),
        in_specs=in_specs,
        out_specs=[hspec(fwd), hspec(bwd), state(DK_C, DV_C), state(1, DK_C), state(1, LANES)],
        out_shape=[jax.ShapeDtypeStruct((b, t, D_C), F32), jax.ShapeDtypeStruct((b, t, D_C), F32),
                   jax.ShapeDtypeStruct((b, 2, H_C, DK_C, DV_C), F32),
                   jax.ShapeDtypeStruct((b, 2, H_C, 1, DK_C), F32),
                   jax.ShapeDtypeStruct((b, 2, H_C, 1, LANES), F32)],
        scratch_shapes=[pltpu.VMEM((2, hps, DK_C, DV_C + LANES), F32),
                        pltpu.VMEM((2, hps, SUBLANES, LANES), F32)],
        compiler_params=_params("arbitrary", "arbitrary", "arbitrary"),
        name="mlstm_scan",
    )(*args)
    return hf, hb, cfin, nfin.reshape(b, 2, H_C, DK_C), mfin[:, :, :, 0, 0]


def _outproj_c_kernel(x_ref, mod_ref, hf_ref, hb_ref, og_ref, g_ref, w_ref, o_ref, hn_scr):
    for h in range(H_C):
        sl = slice(h * DV_C, (h + 1) * DV_C)
        hs = hf_ref[0, :, sl] + hb_ref[0, :, sl]
        hn_scr[:, sl] = (_rms(hs, g_ref[:, sl]) * jax.nn.sigmoid(og_ref[0, :, sl])).astype(BF16)
    o_ref[0] = x_ref[0] + mod_ref[0, 5:6, :] * _dot(hn_scr[...], w_ref[...])


def _outproj_c(x, mod, hf, hb, og, mh_g, w):
    grp, t, d = x.shape
    tm = TOKEN_TILE
    tok = pl.BlockSpec((1, tm, d), lambda b, i: (b, i, 0))
    return pl.pallas_call(
        _outproj_c_kernel,
        grid=(grp, t // tm),
        in_specs=[
            tok,
            pl.BlockSpec((1, N_MOD, d), lambda b, i: (b, 0, 0)),
            tok, tok,
            pl.BlockSpec((1, tm, D_C), lambda b, i: (b, i, 0)),
            pl.BlockSpec((1, D_C), lambda b, i: (0, 0)),
            pl.BlockSpec((D_C, d), lambda b, i: (0, 0), pipeline_mode=pl.Buffered(1)),
        ],
        out_specs=tok,
        out_shape=jax.ShapeDtypeStruct(x.shape, F32),
        scratch_shapes=[pltpu.VMEM((tm, D_C), BF16)],
        compiler_params=_params("arbitrary", "arbitrary"),
        name="mixer_c_out_proj",
    )(x, mod, hf, hb, og, mh_g.reshape(1, D_C), w)


def _mixer_ab(j, xp, xs, mp, ms, g, p):
    bp, lp, bs = p["bp"], p["lp"], xs.shape[0]
    w_in = p["w_in_ab"][j].astype(BF16)
    w_out = p["w_out_ab"][j].astype(BF16)
    gains = jnp.stack([jnp.tile(p["qn_g"][j] * (HD_A ** -0.5), H_A), jnp.tile(p["kn_g"][j], H_A),
                       jnp.ones((D_A,), F32)]).reshape(3, 1, D_A)
    wa, wx, ba, bx = p["lru_wa"][j], p["lru_wx"][j], p["lru_ba"][j], p["lru_bx"][j]
    w4 = (0.5 * jnp.concatenate([wa[0], wx[0], wa[1], wx[1]], axis=-1)).astype(BF16)
    b4 = jnp.concatenate([v.reshape(H_B, 1, BD_B) for v in (ba[0], bx[0], ba[1], bx[1])], axis=-1)
    lru = (p["conv_w"][j], p["conv_b"][j], w4, b4, p["lru_lam"][j])

    qkv, xg = _inproj_ab(xp, mp, g, w_in, gains, F32)
    qkv = qkv.reshape(bp, lp, 3 * D_A)
    oa = _ctx_attention(qkv)
    ob, h_fin = _rglru(xg.reshape(bp, lp, 2 * D_RNN), *lru, jnp.zeros((bp, 2, D_RNN), F32))
    xp = _outproj_ab(xp, mp, oa.reshape(1, bp * lp, D_A), ob.reshape(1, bp * lp, D_RNN), w_out)
    new = (qkv[:, :, D_A:2 * D_A].reshape(bp, lp, H_A, HD_A), qkv[:, :, 2 * D_A:].reshape(bp, lp, H_A, HD_A), h_fin)

    past = p["cache_k"].shape[2]
    ck = p["cache_k"][:, j].reshape(bs, past, D_A).astype(BF16)
    cv = p["cache_v"][:, j].reshape(bs, past, D_A).astype(BF16)
    qkv, xg = _inproj_ab(xs, ms, g, w_in, gains, BF16)
    oa = _neighbourhood_attention(qkv, ck, cv, p["rpb"][j])
    ob, _ = _rglru(xg, *lru, p["state_lru"][:, j])
    xs = _outproj_ab(xs, ms, oa, ob, w_out)
    return xp, xs, new


def _mixer_c(j, xp, xs, mp, ms, g, p):
    bp, lp, bs, ts = p["bp"], p["lp"], xs.shape[0], xs.shape[1]
    w = p["w_in_c"][j].astype(BF16)
    wkt = w[:, D_C:2 * D_C].T
    w_gate = jnp.pad(w[:, 4 * D_C:], ((0, 0), (0, LANES - 4 * H_C)))
    w_out = p["w_out_c"][j].astype(BF16)

    def gates_of(zg, b, t):
        g_rows = jnp.transpose(zg.reshape(b, t, LANES)[:, :, :4 * H_C], (0, 2, 1))
        return _gate_prep(g_rows, p["b_gate_c"][j])

    qv, og, kt, zg = _inproj_c(xp, mp, g, w, wkt, w_gate, None)
    hf, hb, cfin, nfin, mfin = _mlstm(
        qv.reshape(bp, lp, 2 * D_C), kt.reshape(bp, lp // CHUNK, D_C, CHUNK), gates_of(zg, bp, lp), None, None, None)
    xp = _outproj_c(xp, mp, hf.reshape(1, bp * lp, D_C), hb.reshape(1, bp * lp, D_C), og, p["mh_norm_g"][j], w_out)

    qv, og, kt, zg = _inproj_c(xs, ms, g, w, wkt, w_gate, _rope_tables(ts))
    hf, hb, _, _, _ = _mlstm(qv, kt, gates_of(zg, bs, ts), p["state_mlstm_c"][:, j], p["state_mlstm_n"][:, j],
                             p["state_mlstm_m"][:, j])
    xs = _outproj_c(xs, ms, hf, hb, og, p["mh_norm_g"][j], w_out)
    return xp, xs, (cfin, nfin, mfin)


def kernel(x_prompt, x_sample, cache_k, cache_v, state_lru, state_mlstm_c, state_mlstm_n, state_mlstm_m, c, c_ctx, w_mod, b_mod, norm_g, w_ffn_gate, w_ffn_up, w_ffn_down, w_in_ab, w_out_ab, qn_g, kn_g, rpb, conv_w, conv_b, lru_wa, lru_ba, lru_wx, lru_bx, lru_lam, w_in_c, b_gate_c, mh_norm_g, w_out_c):
    bp, lp, d = x_prompt.shape
    bs = x_sample.shape[0]
    p = dict(bp=bp, lp=lp, cache_k=cache_k, cache_v=cache_v, state_lru=state_lru, state_mlstm_c=state_mlstm_c,
             state_mlstm_n=state_mlstm_n, state_mlstm_m=state_mlstm_m, w_in_ab=w_in_ab, w_out_ab=w_out_ab,
             qn_g=qn_g, kn_g=kn_g, rpb=rpb, conv_w=conv_w, conv_b=conv_b, lru_wa=lru_wa, lru_ba=lru_ba,
             lru_wx=lru_wx, lru_bx=lru_bx, lru_lam=lru_lam, w_in_c=w_in_c, b_gate_c=b_gate_c,
             mh_norm_g=mh_norm_g, w_out_c=w_out_c)

    cond = jnp.concatenate([c, c_ctx[None, :]], axis=0)
    cond = jnp.pad(cond, ((0, -(bs + 1) % SUBLANES), (0, 0)))
    mod = _modulation(cond, w_mod, b_mod)[:, :bs + 1].reshape(DEPTH, bs + 1, N_MOD, d)

    wg = w_ffn_gate.astype(BF16)
    wu = w_ffn_up.astype(BF16)
    wd = w_ffn_down.astype(BF16)

    xp = x_prompt.reshape(1, bp * lp, d)
    xs = x_sample
    new_ab, new_c = [], []
    for l in range(DEPTH):
        j = l // 2
        ms = mod[l, :bs]
        mp = mod[l, bs:]
        xp = _ffn(xp, mp, norm_g[l, 0], wg, wu, wd, l, 0, 0)
        xs = _ffn(xs, ms, norm_g[l, 0], wg, wu, wd, l, 0, 0)
        if l % 2 == 0:
            xp, xs, new = _mixer_ab(j, xp, xs, mp, ms, norm_g[l, 1], p)
            new_ab.append(new)
        else:
            xp, xs, new = _mixer_c(j, xp, xs, mp, ms, norm_g[l, 1], p)
            new_c.append(new)
        xp = _ffn(xp, mp, norm_g[l, 2], wg, wu, wd, l, 1, 2)
        xs = _ffn(xs, ms, norm_g[l, 2], wg, wu, wd, l, 1, 2)
    stack = lambda items, i: jnp.concatenate([jnp.expand_dims(it[i], 1) for it in items], axis=1)
    return (xp.reshape(bp, lp, d), xs, stack(new_ab, 0), stack(new_ab, 1), stack(new_ab, 2),
            stack(new_c, 0), stack(new_c, 1), stack(new_c, 2))
```
